```python
import jax, jax.numpy as jnp
from jax import lax
import numpy as np

D_MODEL = 1024
BATCH = 2
SEQ = 8192
DEPTH = 1
DEC_BATCH = 128
DEC_SEQ = 4
PAST_LEN = 8192
PAGE_SIZE = 128

M_HEADS = 4
M_DK = 128
M_DV = 128
M_WIDTH = M_HEADS * M_DV
MLSTM_CHUNK = 64
A_HEADS = 8
A_KV_HEADS = 2
A_HD = 64
A_GROUP = A_HEADS // A_KV_HEADS
A_WIDTH = A_HEADS * A_HD
WINDOW = 128
D_MIX = M_WIDTH + A_WIDTH
SPLITS = (M_HEADS * M_DK, M_HEADS * M_DK, M_WIDTH, M_WIDTH, M_HEADS, M_HEADS,
          A_WIDTH, A_KV_HEADS * A_HD, A_KV_HEADS * A_HD)
D_IN = 2 * M_HEADS * M_DK + 2 * M_WIDTH + 2 * M_HEADS + A_WIDTH + 2 * A_KV_HEADS * A_HD
D_FF = 2816
RMS_EPS = 1e-6

kernel_name = "hymba_mlstm_swa_sink_alibi_macaron_step"


def rmsnorm(x, g):
    xf = x.astype(jnp.float32)
    y = xf * lax.rsqrt(jnp.mean(xf * xf, axis=-1, keepdims=True) + RMS_EPS)
    return (y * g.astype(jnp.float32)).astype(x.dtype)


def swiglu(x, w_gate, w_up, w_down):
    return (jax.nn.silu(x @ w_gate) * (x @ w_up)) @ w_down


def split_proj(h, w_in, b_gate):
    B, T, _ = h.shape
    z = h @ w_in
    idx = np.cumsum(SPLITS)[:-1].tolist()
    qm, km, vm, om, ip, fp, qa, ka, va = jnp.split(z, idx, axis=-1)
    bg = b_gate.astype(z.dtype)
    m_parts = (qm.reshape(B, T, M_HEADS, M_DK),
               km.reshape(B, T, M_HEADS, M_DK) * (M_DK ** -0.5),
               vm.reshape(B, T, M_HEADS, M_DV),
               jax.nn.sigmoid(om.astype(jnp.float32)),
               ip + bg[:M_HEADS],
               fp + bg[M_HEADS:])
    a_parts = (qa.reshape(B, T, A_HEADS, A_HD),
               ka.reshape(B, T, A_KV_HEADS, A_HD),
               va.reshape(B, T, A_KV_HEADS, A_HD))
    return m_parts, a_parts


def mlstm_chunk(carry, xs):
    C, n, m = carry
    q, k, v, ig, fg = xs
    L = q.shape[2]
    b = jnp.cumsum(jax.nn.log_sigmoid(fg), axis=-1)
    causal = jnp.tril(jnp.ones((L, L), dtype=bool))
    D = jnp.where(causal, b[..., :, None] - b[..., None, :] + ig[..., None, :], -jnp.inf)
    inter = b + m[..., None]
    m_t = jnp.maximum(inter, jnp.max(D, axis=-1))
    W = jnp.exp(D - m_t[..., None])
    a = jnp.exp(inter - m_t)
    S = jnp.einsum('bhtd,bhsd->bhts', q, k) * W
    num = a[..., None] * jnp.einsum('bhtk,bhkv->bhtv', q, C) + jnp.einsum('bhts,bhsv->bhtv', S, v)
    den = a * jnp.einsum('bhtk,bhk->bht', q, n) + jnp.sum(S, axis=-1)
    h = num / jnp.maximum(jnp.abs(den), jnp.exp(-m_t))[..., None]
    bL = b[..., -1]
    g = bL[..., None] - b + ig
    m_new = jnp.maximum(bL + m, jnp.max(g, axis=-1))
    decay = jnp.exp(bL + m - m_new)
    wk = jnp.exp(g - m_new[..., None])
    C_new = decay[..., None, None] * C + jnp.einsum('bhs,bhsk,bhsv->bhkv', wk, k, v)
    n_new = decay[..., None] * n + jnp.einsum('bhs,bhsk->bhk', wk, k)
    return (C_new, n_new, m_new), h


def mlstm_heads(q, k, v, o, ig, fg, state, norm_gain, out_dtype):
    B, T = q.shape[:2]
    L = MLSTM_CHUNK if T % MLSTM_CHUNK == 0 else T
    nc = T // L

    def chunks(a):
        a = a.astype(jnp.float32).reshape((B, nc, L) + a.shape[2:])
        return jnp.moveaxis(a, (1, 3), (0, 2))

    C0, n0, m0 = state
    init = (C0.astype(jnp.float32), n0.astype(jnp.float32), m0.astype(jnp.float32))
    (C1, n1, m1), h = lax.scan(mlstm_chunk, init,
                               (chunks(q), chunks(k), chunks(v), chunks(ig), chunks(fg)))
    h = jnp.moveaxis(h, (0, 2), (1, 3)).reshape(B, T, M_HEADS, M_DV)
    h = h * lax.rsqrt(jnp.mean(h * h, axis=-1, keepdims=True) + RMS_EPS)
    h = h * norm_gain.astype(jnp.float32).reshape(M_HEADS, M_DV)
    h = o.reshape(B, T, M_HEADS, M_DV) * h
    return h.reshape(B, T, M_WIDTH).astype(out_dtype), (C1, n1, m1)


def alibi(dist):
    slopes = jnp.exp2(-8.0 * jnp.arange(1, A_HEADS + 1, dtype=jnp.float32) / A_HEADS)
    slopes = slopes.reshape(A_KV_HEADS, A_GROUP)
    return -slopes[:, :, None, None] * dist.astype(jnp.float32)


def sink_softmax(s, sinks):
    sk = sinks.astype(jnp.float32).reshape(A_KV_HEADS, A_GROUP, 1)
    mx = jnp.maximum(jnp.max(s, axis=-1), sk)
    p = jnp.exp(s - mx[..., None])
    den = jnp.sum(p, axis=-1) + jnp.exp(sk - mx)
    return p / den[..., None]


def swa_prompt(q, k, v, sinks):
    B, T = q.shape[:2]
    nb = T // WINDOW
    qb = q.reshape(B, nb, WINDOW, A_KV_HEADS, A_GROUP, A_HD)
    kb = k.reshape(B, nb, WINDOW, A_KV_HEADS, A_HD)
    vb = v.reshape(B, nb, WINDOW, A_KV_HEADS, A_HD)
    shift = lambda a: jnp.concatenate([jnp.zeros_like(a[:, :1]), a[:, :-1]], axis=1)
    kk = jnp.concatenate([shift(kb), kb], axis=2)
    vv = jnp.concatenate([shift(vb), vb], axis=2)
    qi = jnp.arange(WINDOW)[:, None]
    kj = jnp.arange(2 * WINDOW)[None, :]
    dist = WINDOW + qi - kj
    exists = (jnp.arange(nb)[:, None, None] * WINDOW + kj - WINDOW) >= 0
    valid = (dist >= 0) & (dist < WINDOW) & exists
    s = jnp.einsum('bnqhgd,bnkhd->bnhgqk', qb, kk).astype(jnp.float32) * (A_HD ** -0.5) + alibi(dist)
    s = jnp.where(valid[None, :, None, None], s, -jnp.inf)
    p = sink_softmax(s, sinks)
    o = jnp.einsum('bnhgqk,bnkhd->bnqhgd', p.astype(v.dtype), vv).reshape(B, T, A_WIDTH)
    keep = min(WINDOW, T)
    return o, (k[:, T - keep:], v[:, T - keep:])


def swa_sample(q, k, v, sinks, buf_k, buf_v):
    Bd, S = q.shape[:2]
    Wc = buf_k.shape[1]
    kk = jnp.concatenate([buf_k.astype(k.dtype), k], axis=1)
    vv = jnp.concatenate([buf_v.astype(v.dtype), v], axis=1)
    dist = Wc + jnp.arange(S)[:, None] - jnp.arange(Wc + S)[None, :]
    valid = (dist >= 0) & (dist < WINDOW)
    qg = q.reshape(Bd, S, A_KV_HEADS, A_GROUP, A_HD)
    s = jnp.einsum('bqhgd,bkhd->bhgqk', qg, kk).astype(jnp.float32) * (A_HD ** -0.5) + alibi(dist)
    s = jnp.where(valid, s, -jnp.inf)
    p = sink_softmax(s, sinks)
    o = jnp.einsum('bhgqk,bkhd->bqhgd', p.astype(v.dtype), vv).reshape(Bd, S, A_WIDTH)
    return o, (kk[:, S:], vv[:, S:])


def trunk_layer(x, l, mstate, kv_buf, norm_gains, ffn_w_gate, ffn_w_up, ffn_w_down,
                w_in, b_gate, mlstm_norm_gain, attn_sinks, w_out):
    g = norm_gains[l]

    def half_ffn(x, j, gpre, gpost):
        y = swiglu(rmsnorm(x, g[gpre]), ffn_w_gate[l, j], ffn_w_up[l, j], ffn_w_down[l, j])
        return x + 0.5 * rmsnorm(y, g[gpost])

    x = half_ffn(x, 0, 0, 1)
    h = rmsnorm(x, g[2])
    (qm, km, vm, om, ip, fp), (qa, ka, va) = split_proj(h, w_in[l], b_gate[l])
    u_m, new_m = mlstm_heads(qm, km, vm, om, ip, fp, mstate, mlstm_norm_gain[l], h.dtype)
    if kv_buf is None:
        u_a, new_kv = swa_prompt(qa, ka, va, attn_sinks[l])
    else:
        u_a, new_kv = swa_sample(qa, ka, va, attn_sinks[l], kv_buf[0], kv_buf[1])
    u = jnp.concatenate([u_m, u_a], axis=-1)
    x = x + rmsnorm(u @ w_out[l], g[3])
    x = half_ffn(x, 1, 4, 5)
    return x, (new_kv[0], new_kv[1], new_m[0], new_m[1], new_m[2])


def setup_inputs(seed: int = 0) -> dict:
    key = jax.random.key(seed)
    ks = jax.random.split(key, 20)
    f32 = jnp.float32
    nrm = lambda k, shape, scale: scale * jax.random.normal(k, shape, f32)
    w_cache = min(WINDOW, PAST_LEN)
    b_gate = jnp.concatenate(
        [nrm(ks[0], (DEPTH, M_HEADS), 0.1),
         jnp.linspace(3.0, 6.0, M_HEADS, dtype=f32)[None, :] + nrm(ks[1], (DEPTH, M_HEADS), 0.1)], axis=-1)
    return {
        "x_prompt": nrm(ks[2], (BATCH, SEQ, D_MODEL), 1.0),
        "x_sample": nrm(ks[3], (DEC_BATCH, DEC_SEQ, D_MODEL), 1.0),
        "cache_swa_k": nrm(ks[4], (DEPTH, DEC_BATCH, w_cache, A_KV_HEADS, A_HD), 1.0),
        "cache_swa_v": nrm(ks[5], (DEPTH, DEC_BATCH, w_cache, A_KV_HEADS, A_HD), 1.0),
        "state_mlstm_C": nrm(ks[6], (DEPTH, DEC_BATCH, M_HEADS, M_DK, M_DV), 0.3),
        "state_mlstm_n": nrm(ks[7], (DEPTH, DEC_BATCH, M_HEADS, M_DK), 0.3),
        "state_mlstm_m": nrm(ks[8], (DEPTH, DEC_BATCH, M_HEADS), 1.0),
        "norm_gains": 1.0 + nrm(ks[9], (DEPTH, 6, D_MODEL), 0.05),
        "ffn_w_gate": nrm(ks[10], (DEPTH, 2, D_MODEL, D_FF), D_MODEL ** -0.5),
        "ffn_w_up": nrm(ks[11], (DEPTH, 2, D_MODEL, D_FF), D_MODEL ** -0.5),
        "ffn_w_down": nrm(ks[12], (DEPTH, 2, D_FF, D_MODEL), D_FF ** -0.5),
        "w_in": nrm(ks[13], (DEPTH, D_MODEL, D_IN), D_MODEL ** -0.5),
        "b_gate": b_gate,
        "mlstm_norm_gain": 1.0 + nrm(ks[14], (DEPTH, M_WIDTH), 0.05),
        "attn_sinks": nrm(ks[15], (DEPTH, A_HEADS), 1.0),
        "w_out": nrm(ks[16], (DEPTH, D_MIX, D_MODEL), D_MIX ** -0.5),
    }


def reference(x_prompt, x_sample, cache_swa_k, cache_swa_v, state_mlstm_C, state_mlstm_n,
              state_mlstm_m, norm_gains, ffn_w_gate, ffn_w_up, ffn_w_down, w_in, b_gate,
              mlstm_norm_gain, attn_sinks, w_out):
    B = x_prompt.shape[0]
    zero_state = (jnp.zeros((B, M_HEADS, M_DK, M_DV), jnp.float32),
                  jnp.zeros((B, M_HEADS, M_DK), jnp.float32),
                  jnp.zeros((B, M_HEADS), jnp.float32))
    yp, ys = x_prompt, x_sample
    p_new, s_new = [], []
    for l in range(DEPTH):
        yp, st_p = trunk_layer(yp, l, zero_state, None, norm_gains, ffn_w_gate, ffn_w_up,
                               ffn_w_down, w_in, b_gate, mlstm_norm_gain, attn_sinks, w_out)
        p_new.append(st_p)
        ys, st_s = trunk_layer(ys, l, (state_mlstm_C[l], state_mlstm_n[l], state_mlstm_m[l]),
                               (cache_swa_k[l], cache_swa_v[l]), norm_gains, ffn_w_gate, ffn_w_up,
                               ffn_w_down, w_in, b_gate, mlstm_norm_gain, attn_sinks, w_out)
        s_new.append(st_s)
    stk = lambda lst, i: jnp.stack([st[i] for st in lst], axis=0)
    return (yp, ys,
            stk(p_new, 0), stk(p_new, 1), stk(p_new, 2), stk(p_new, 3), stk(p_new, 4),
            stk(s_new, 0), stk(s_new, 1), stk(s_new, 2), stk(s_new, 3), stk(s_new, 4))
```

```python
import functools

import jax
import jax.numpy as jnp
from jax import lax
from jax.experimental import pallas as pl
from jax.experimental.pallas import tpu as pltpu

F32 = jnp.float32
BF16 = jnp.bfloat16

D_MODEL = 1024
D_FF = 2816
M_HEADS = 4
M_DK = 128
M_DV = 128
M_WIDTH = M_HEADS * M_DV
A_HEADS = 8
A_KV_HEADS = 2
A_GROUP = A_HEADS // A_KV_HEADS
A_HD = 64
A_WIDTH = A_HEADS * A_HD
WINDOW = 128
RMS_EPS = 1e-6

LANES = 128
FFN_CHUNK = 256
N_FFN_CHUNKS = D_FF // FFN_CHUNK
TOKEN_TILE = 512
MLSTM_CHUNK = 256
ZM_WIDTH = 4 * M_WIDTH
KV_WIDTH = 2 * A_KV_HEADS * A_HD
PROJ_WIDTH = ZM_WIDTH + A_WIDTH + KV_WIDTH + LANES
VMEM_LIMIT = 56 * 1024 * 1024

NT_DIMS = (((1,), (1,)), ((), ()))
TN_DIMS = (((0,), (0,)), ((), ()))


def _rms(x, g):
    ms = jnp.mean(x * x, axis=-1, keepdims=True)
    return x * lax.rsqrt(ms + RMS_EPS) * g


def _log_sigmoid(x):
    return jnp.minimum(x, 0.0) - jnp.log1p(jnp.exp(-jnp.abs(x)))


def _dot(a, b):
    return jnp.dot(a, b, preferred_element_type=F32)


def _dot_nt(a, b):
    return lax.dot_general(a, b, NT_DIMS, preferred_element_type=F32)


def _resident(shape):
    nd = len(shape)
    return pl.BlockSpec(shape, lambda *_: (0,) * nd, pipeline_mode=pl.Buffered(1))


def _swiglu_residual(x, gpre, gpost, wgu_ref, wd_ref, h_ref, acc_ref):
    h_ref[...] = _rms(x, gpre).astype(BF16)
    acc_ref[...] = jnp.zeros_like(acc_ref)

    def step(f, carry):
        gu = _dot(h_ref[...], wgu_ref[f])
        g = gu[:, :FFN_CHUNK]
        u = gu[:, FFN_CHUNK:]
        a = (g * jax.nn.sigmoid(g) * u).astype(BF16)
        acc_ref[...] += _dot(a, wd_ref[f])
        return carry

    lax.fori_loop(0, N_FFN_CHUNKS, step, 0)
    return x + 0.5 * _rms(acc_ref[...], gpost)


def _ffn_kernel(x_ref, gains_ref, wgu_ref, wd_ref, o_ref, h_ref, acc_ref, *, pre, post):
    o_ref[...] = _swiglu_residual(x_ref[...], gains_ref[pre:pre + 1, :], gains_ref[post:post + 1, :],
                                  wgu_ref, wd_ref, h_ref, acc_ref)


def _mix_ffn_kernel(x_ref, um_ref, ua_ref, wo_ref, gains_ref, wgu_ref, wd_ref, o_ref, h_ref, acc_ref,
                    *, mix, pre, post):
    u = jnp.concatenate([um_ref[...], ua_ref[...]], axis=-1)
    x = x_ref[...] + _rms(_dot(u, wo_ref[...]), gains_ref[mix:mix + 1, :])
    o_ref[...] = _swiglu_residual(x, gains_ref[pre:pre + 1, :], gains_ref[post:post + 1, :],
                                  wgu_ref, wd_ref, h_ref, acc_ref)


def _ffn_scratch(tm):
    return [pltpu.VMEM((tm, D_MODEL), BF16), pltpu.VMEM((tm, D_MODEL), F32)]


def _dense_params():
    return pltpu.CompilerParams(dimension_semantics=("parallel",), vmem_limit_bytes=VMEM_LIMIT)


def _ffn(x, gains, wgu, wd, pre, post):
    n = x.shape[0]
    tm = min(TOKEN_TILE, n)
    row = pl.BlockSpec((tm, D_MODEL), lambda i: (i, 0))
    return pl.pallas_call(
        functools.partial(_ffn_kernel, pre=pre, post=post),
        grid=(n // tm,),
        in_specs=[row, _resident(gains.shape), _resident(wgu.shape), _resident(wd.shape)],
        out_specs=row,
        out_shape=jax.ShapeDtypeStruct((n, D_MODEL), F32),
        scratch_shapes=_ffn_scratch(tm),
        compiler_params=_dense_params(),
        name="ffn_half_step",
    )(x, gains, wgu, wd)


def _mix_ffn(x, um, ua, wo, gains, wgu, wd, mix, pre, post):
    n = x.shape[0]
    tm = min(TOKEN_TILE, n)
    row = pl.BlockSpec((tm, D_MODEL), lambda i: (i, 0))
    half = pl.BlockSpec((tm, M_WIDTH), lambda i: (i, 0))
    return pl.pallas_call(
        functools.partial(_mix_ffn_kernel, mix=mix, pre=pre, post=post),
        grid=(n // tm,),
        in_specs=[row, half, half, _resident(wo.shape), _resident(gains.shape),
                  _resident(wgu.shape), _resident(wd.shape)],
        out_specs=row,
        out_shape=jax.ShapeDtypeStruct((n, D_MODEL), F32),
        scratch_shapes=_ffn_scratch(tm),
        compiler_params=_dense_params(),
        name="out_proj_ffn_half_step",
    )(x, um, ua, wo, gains, wgu, wd)


def _proj_kernel(x_ref, gains_ref, win_ref, bias_ref, zm_ref, qa_ref, kva_ref, gates_ref, *, pre):
    h = _rms(x_ref[...], gains_ref[pre:pre + 1, :]).astype(BF16)

    def cols(lo, width):
        return _dot(h, win_ref[:, lo:lo + width])

    zm_ref[:, 0:M_WIDTH] = cols(0, M_WIDTH).astype(BF16)
    zm_ref[:, M_WIDTH:2 * M_WIDTH] = (cols(M_WIDTH, M_WIDTH) * (M_DK ** -0.5)).astype(BF16)
    zm_ref[:, 2 * M_WIDTH:3 * M_WIDTH] = cols(2 * M_WIDTH, M_WIDTH).astype(BF16)
    zm_ref[:, 3 * M_WIDTH:] = jax.nn.sigmoid(cols(3 * M_WIDTH, M_WIDTH)).astype(BF16)
    qa_ref[...] = (cols(ZM_WIDTH, A_WIDTH) * (A_HD ** -0.5)).astype(BF16)
    kva_ref[...] = cols(ZM_WIDTH + A_WIDTH, KV_WIDTH)
    gates_ref[...] = cols(ZM_WIDTH + A_WIDTH + KV_WIDTH, LANES) + bias_ref[...]


def _proj(x, gains, win, bias, pre):
    n = x.shape[0]
    tm = min(TOKEN_TILE, n)

    def rows(width):
        return pl.BlockSpec((tm, width), lambda i: (i, 0))

    return pl.pallas_call(
        functools.partial(_proj_kernel, pre=pre),
        grid=(n // tm,),
        in_specs=[rows(D_MODEL), _resident(gains.shape), _resident(win.shape), _resident(bias.shape)],
        out_specs=[rows(ZM_WIDTH), rows(A_WIDTH), rows(KV_WIDTH), rows(LANES)],
        out_shape=[jax.ShapeDtypeStruct((n, ZM_WIDTH), BF16),
                   jax.ShapeDtypeStruct((n, A_WIDTH), BF16),
                   jax.ShapeDtypeStruct((n, KV_WIDTH), F32),
                   jax.ShapeDtypeStruct((n, LANES), F32)],
        compiler_params=_dense_params(),
        name="input_projection",
    )(x, gains, win, bias)


def _ones_column(rows):
    lane = lax.broadcasted_iota(jnp.int32, (rows, LANES), 1)
    return jnp.where(lane == 0, 1.0, 0.0).astype(BF16)


def _head_norm_gate(hh, o, gain):
    hn = hh * lax.rsqrt(jnp.mean(hh * hh, axis=-1, keepdims=True) + RMS_EPS) * gain
    return (o.astype(F32) * hn).astype(BF16)


def _mlstm_prompt_kernel(zm_ref, gates_ref, gain_ref, um_ref, cn_ref, m_ref):
    chunk = zm_ref.shape[0]

    @pl.when(pl.program_id(1) == 0)
    def _():
        cn_ref[...] = jnp.zeros_like(cn_ref)
        m_ref[...] = jnp.zeros_like(m_ref)

    row = lax.broadcasted_iota(jnp.int32, (chunk, chunk), 0)
    col = lax.broadcasted_iota(jnp.int32, (chunk, chunk), 1)
    causal = col <= row
    gates = gates_ref[...]
    gates_t = gates.T
    hi = lax.Precision.HIGHEST
    b_cols = jnp.dot(causal.astype(F32), _log_sigmoid(gates), precision=hi, preferred_element_type=F32)
    b_rows = jnp.dot(_log_sigmoid(gates_t), (row <= col).astype(F32), precision=hi,
                     preferred_element_type=F32)
    ones_col = _ones_column(chunk)

    for h in range(M_HEADS):
        lanes = slice(h * M_DK, (h + 1) * M_DK)
        q = zm_ref[:, lanes]
        k = zm_ref[:, M_WIDTH + h * M_DK:M_WIDTH + (h + 1) * M_DK]
        v = zm_ref[:, 2 * M_WIDTH + h * M_DV:2 * M_WIDTH + (h + 1) * M_DV]
        o = zm_ref[:, 3 * M_WIDTH + h * M_DV:3 * M_WIDTH + (h + 1) * M_DV]
        ig_col = gates[:, h:h + 1]
        ig_row = gates_t[h:h + 1, :]
        b_col = b_cols[:, M_HEADS + h:M_HEADS + h + 1]
        b_row = b_rows[M_HEADS + h:M_HEADS + h + 1, :]
        m_prev = m_ref[0, h][:, 0:1]
        cn = cn_ref[0, h]

        d = jnp.where(causal, b_col - b_row + ig_row, -jnp.inf)
        inter = b_col + m_prev
        m_t = jnp.maximum(inter, jnp.max(d, axis=-1, keepdims=True))
        w = jnp.exp(d - m_t)
        a = jnp.exp(inter - m_t)
        s = _dot_nt(q, k) * w
        v_ext = jnp.concatenate([v, ones_col], axis=-1)
        nd = a * _dot(q, cn.astype(BF16)) + _dot(s.astype(BF16), v_ext)
        den = nd[:, M_DV:M_DV + 1]
        hh = nd[:, :M_DV] / jnp.maximum(jnp.abs(den), jnp.exp(-m_t))
        um_ref[:, lanes] = _head_norm_gate(hh, o, gain_ref[:, lanes])

        b_last = b_col[chunk - 1:chunk, :]
        g = b_last - b_col + ig_col
        m_new = jnp.maximum(b_last + m_prev, jnp.max(g, axis=0, keepdims=True))
        decay = jnp.exp(b_last + m_prev - m_new)
        wv = (jnp.exp(g - m_new) * v_ext.astype(F32)).astype(BF16)
        upd = lax.dot_general(k, wv, TN_DIMS, preferred_element_type=F32)
        cn_ref[0, h] = decay * cn + upd
        m_ref[0, h] = jnp.broadcast_to(m_new, (1, LANES))


def _mlstm_prompt(zm, gates, gain, batch):
    n = zm.shape[0]
    chunk = min(MLSTM_CHUNK, n // batch)
    nc = n // batch // chunk

    def rows(width):
        return pl.BlockSpec((chunk, width), lambda b, c: (b * nc + c, 0))

    return pl.pallas_call(
        _mlstm_prompt_kernel,
        grid=(batch, nc),
        in_specs=[rows(ZM_WIDTH), rows(LANES), pl.BlockSpec((1, M_WIDTH), lambda b, c: (0, 0))],
        out_specs=[rows(M_WIDTH),
                   pl.BlockSpec((1, M_HEADS, M_DK, 2 * LANES), lambda b, c: (b, 0, 0, 0)),
                   pl.BlockSpec((1, M_HEADS, 1, LANES), lambda b, c: (b, 0, 0, 0))],
        out_shape=[jax.ShapeDtypeStruct((n, M_WIDTH), BF16),
                   jax.ShapeDtypeStruct((batch, M_HEADS, M_DK, 2 * LANES), F32),
                   jax.ShapeDtypeStruct((batch, M_HEADS, 1, LANES), F32)],
        compiler_params=pltpu.CompilerParams(dimension_semantics=("parallel", "arbitrary"),
                                             vmem_limit_bytes=VMEM_LIMIT),
        name="mlstm_prompt",
    )(zm, gates, gain)


def _alibi_slope(head):
    return 2.0 ** (-8.0 * (head + 1) / A_HEADS)


def _swa_prompt_kernel(sinks_ref, qa_ref, kvc_ref, kvp_ref, ua_ref):
    blk = pl.program_id(1)
    kvp = kvp_ref[...]
    kvc = kvc_ref[...]
    hw = A_KV_HEADS * A_HD
    kk = jnp.concatenate([kvp[:, :hw], kvc[:, :hw]], axis=0).astype(BF16)
    vv = jnp.concatenate([kvp[:, hw:], kvc[:, hw:]], axis=0).astype(BF16)
    qi = lax.broadcasted_iota(jnp.int32, (WINDOW, 2 * WINDOW), 0)
    kj = lax.broadcasted_iota(jnp.int32, (WINDOW, 2 * WINDOW), 1)
    dist = WINDOW + qi - kj
    valid = (dist >= 0) & (dist < WINDOW) & ((kj >= WINDOW) | (blk > 0))
    distf = dist.astype(F32)
    low = lax.broadcasted_iota(jnp.int32, (WINDOW, LANES), 1) < A_HD

    for p in range(A_GROUP):
        qp = qa_ref[:, p * LANES:(p + 1) * LANES]
        outs = []
        for kvh in range(A_KV_HEADS):
            head = kvh * A_GROUP + p
            keep = low if kvh == 0 else jnp.logical_not(low)
            s = _dot_nt(jnp.where(keep, qp, jnp.zeros_like(qp)), kk)
            s = jnp.where(valid, s - _alibi_slope(head) * distf, -jnp.inf)
            sink = sinks_ref[head]
            mx = jnp.maximum(jnp.max(s, axis=-1, keepdims=True), sink)
            e = jnp.exp(s - mx)
            den = jnp.sum(e, axis=-1, keepdims=True) + jnp.exp(sink - mx)
            outs.append(_dot(e.astype(BF16), vv) / den)
        ua_ref[:, p * LANES:(p + 1) * LANES] = jnp.where(low, outs[0], outs[1]).astype(BF16)


def _swa_prompt(sinks, qa, kva, batch):
    n = qa.shape[0]
    nb = n // batch // WINDOW
    return pl.pallas_call(
        _swa_prompt_kernel,
        grid=(batch, nb),
        in_specs=[pl.BlockSpec(memory_space=pltpu.SMEM),
                  pl.BlockSpec((WINDOW, A_WIDTH), lambda b, i: (b * nb + i, 0)),
                  pl.BlockSpec((WINDOW, KV_WIDTH), lambda b, i: (b * nb + i, 0)),
                  pl.BlockSpec((WINDOW, KV_WIDTH), lambda b, i: (b * nb + jnp.maximum(i - 1, 0), 0))],
        out_specs=pl.BlockSpec((WINDOW, A_WIDTH), lambda b, i: (b * nb + i, 0)),
        out_shape=jax.ShapeDtypeStruct((n, A_WIDTH), BF16),
        compiler_params=pltpu.CompilerParams(dimension_semantics=("parallel", "parallel"),
                                             vmem_limit_bytes=VMEM_LIMIT),
        name="swa_prompt",
    )(sinks, qa, kva, kva)


SAMPLE_ROWS = 128


def _mlstm_sample_kernel(q_ref, k_ref, v_ref, o_ref, gates_ref, gain_ref, c_ref, n_ref, m_ref,
                         um_ref, co_ref, no_ref, mo_ref,
                         qf_ref, qc_ref, nexp_ref, wkk_ref, dec_ref, *, steps):
    rows = SAMPLE_ROWS
    nb = rows // steps
    head = pl.program_id(1)
    row = lax.broadcasted_iota(jnp.int32, (rows, rows), 0)
    col = lax.broadcasted_iota(jnp.int32, (rows, rows), 1)
    same = (row // steps) == (col // steps)
    causal = same & (col <= row)
    eye = row == col

    def to_row(x_col):
        return jnp.sum(jnp.where(eye, x_col, 0.0), axis=0, keepdims=True)

    def pick(mask, x_row, fill, reduce):
        return reduce(jnp.where(mask, x_row, fill), axis=-1, keepdims=True)

    gates = gates_ref[...]
    lane = lax.broadcasted_iota(jnp.int32, (rows, LANES), 1)
    ig_col = jnp.sum(jnp.where(lane == head, gates, 0.0), axis=-1, keepdims=True)
    fg_col = jnp.sum(jnp.where(lane == head + M_HEADS, gates, 0.0), axis=-1, keepdims=True)
    lf_row = to_row(_log_sigmoid(fg_col))
    b_col = pick(causal, lf_row, 0.0, jnp.sum)
    b_row = to_row(b_col)
    ig_row = to_row(ig_col)
    m_prev = m_ref[0]

    d = jnp.where(causal, b_col - b_row + ig_row, -jnp.inf)
    inter = b_col + m_prev
    m_t = jnp.maximum(inter, jnp.max(d, axis=-1, keepdims=True))
    w = jnp.exp(d - m_t)
    a = jnp.exp(inter - m_t)

    q = q_ref[...]
    k = k_ref[...]
    v = v_ref[...]
    qf_ref[...] = q.astype(F32)
    for t in range(steps):
        nexp_ref[pl.ds(t, nb, stride=steps), :] = n_ref[...]

    sub = lax.broadcasted_iota(jnp.int32, (2 * steps, M_DV), 0)

    def qc_pair(j, carry):
        q8 = qf_ref[pl.ds(pl.multiple_of(j * 2 * steps, 2 * steps), 2 * steps), :].astype(BF16)
        r0 = _dot(q8, c_ref[2 * j, 0].astype(BF16))
        r1 = _dot(q8, c_ref[2 * j + 1, 0].astype(BF16))
        qc_ref[pl.ds(pl.multiple_of(j * 2 * steps, 2 * steps), 2 * steps), :] = jnp.where(sub < steps, r0, r1)
        return carry

    lax.fori_loop(0, nb // 2, qc_pair, 0)

    s = _dot_nt(q, k) * w
    qn = jnp.sum(qf_ref[...] * nexp_ref[...], axis=-1, keepdims=True)
    num = a * qc_ref[...] + _dot(s.astype(BF16), v)
    den = a * qn + jnp.sum(s, axis=-1, keepdims=True)
    hh = num / jnp.maximum(jnp.abs(den), jnp.exp(-m_t))
    um_ref[...] = _head_norm_gate(hh, o_ref[...], gain_ref[...])

    last = same & ((col % steps) == steps - 1)
    b_last = pick(last, b_row, 0.0, jnp.sum)
    g_col = b_last - b_col + ig_col
    g_max = pick(same, to_row(g_col), -jnp.inf, jnp.max)
    m_new = jnp.maximum(b_last + m_prev, g_max)
    decay = jnp.exp(b_last + m_prev - m_new)
    wk = jnp.exp(g_col - m_new)
    mo_ref[0] = m_new

    kf = k.astype(F32)
    wkk_ref[...] = wk * kf
    dec_ref[...] = jnp.broadcast_to(decay, (rows, M_DV))
    n_sum = wkk_ref[pl.ds(0, nb, stride=steps), :]
    for t in range(1, steps):
        n_sum = n_sum + wkk_ref[pl.ds(t, nb, stride=steps), :]
    no_ref[...] = dec_ref[pl.ds(0, nb, stride=steps), :] * n_ref[...] + n_sum

    k_t = kf.T
    wv = (wk * v.astype(F32)).astype(BF16)
    tok_batch = lax.broadcasted_iota(jnp.int32, (M_DK, rows), 1) // steps

    def update(b, carry):
        upd = _dot(jnp.where(tok_batch == b, k_t, 0.0).astype(BF16), wv)
        co_ref[b, 0] = dec_ref[pl.ds(b * steps, 1), :] * c_ref[b, 0] + upd
        return carry

    lax.fori_loop(0, nb, update, 0)


def _mlstm_sample(zm, gates, gain, c, n2, m_rows, steps):
    rows = zm.shape[0]
    batch = c.shape[0]
    nb = SAMPLE_ROWS // steps
    groups = rows // SAMPLE_ROWS

    def zcol(section):
        return pl.BlockSpec((SAMPLE_ROWS, M_DK), lambda g, h: (g, section * M_HEADS + h))

    c_spec = pl.BlockSpec((nb, 1, M_DK, M_DV), lambda g, h: (g, h, 0, 0))
    n_spec = pl.BlockSpec((nb, M_DK), lambda g, h: (g, h))
    m_spec = pl.BlockSpec((1, SAMPLE_ROWS, 1), lambda g, h: (h, g, 0))
    return pl.pallas_call(
        functools.partial(_mlstm_sample_kernel, steps=steps),
        grid=(groups, M_HEADS),
        in_specs=[zcol(0), zcol(1), zcol(2), zcol(3),
                  pl.BlockSpec((SAMPLE_ROWS, LANES), lambda g, h: (g, 0)),
                  pl.BlockSpec((1, M_DV), lambda g, h: (0, h)),
                  c_spec, n_spec, m_spec],
        out_specs=[pl.BlockSpec((SAMPLE_ROWS, M_DV), lambda g, h: (g, h)), c_spec, n_spec, m_spec],
        out_shape=[jax.ShapeDtypeStruct((rows, M_WIDTH), BF16),
                   jax.ShapeDtypeStruct(c.shape, F32),
                   jax.ShapeDtypeStruct(n2.shape, F32),
                   jax.ShapeDtypeStruct(m_rows.shape, F32)],
        scratch_shapes=[pltpu.VMEM((SAMPLE_ROWS, M_DK), F32),
                        pltpu.VMEM((SAMPLE_ROWS, M_DV), F32),
                        pltpu.VMEM((SAMPLE_ROWS, M_DK), F32),
                        pltpu.VMEM((SAMPLE_ROWS, M_DK), F32),
                        pltpu.VMEM((SAMPLE_ROWS, M_DV), F32)],
        compiler_params=pltpu.CompilerParams(dimension_semantics=("parallel", "parallel"),
                                             vmem_limit_bytes=VMEM_LIMIT),
        name="mlstm_sample",
    )(zm, zm, zm, zm, gates, gain, c, n2, m_rows)


SWA_SAMPLE_BATCHES = 16


def _swa_sample_kernel(sinks_ref, qa_ref, kva_ref, ck_ref, cv_ref, ua_ref, nk_ref, nv_ref, qf_ref, uo_ref,
                       *, steps):
    pair_rows = 2 * steps
    n_pairs = SWA_SAMPLE_BATCHES // 2
    hw = A_KV_HEADS * A_HD
    srows = A_HEADS * pair_rows
    qf_ref[...] = qa_ref[...].astype(F32)

    r = lax.broadcasted_iota(jnp.int32, (srows, 1), 0)
    tok = r % pair_rows
    t_col = tok % steps
    row_batch = tok // steps
    blk = r // pair_rows
    head = (blk % A_KV_HEADS) * A_GROUP + blk // A_KV_HEADS
    slope = jnp.zeros((srows, 1), F32)
    sink = jnp.zeros((srows, 1), F32)
    for hd in range(A_HEADS):
        slope = jnp.where(head == hd, _alibi_slope(hd), slope)
        sink = jnp.where(head == hd, sinks_ref[hd], sink)
    row_low = (blk % A_KV_HEADS) == 0

    jj = lax.broadcasted_iota(jnp.int32, (srows, WINDOW), 1)
    dist_c = WINDOW + t_col - jj
    valid_c = dist_c < WINDOW
    cc = lax.broadcasted_iota(jnp.int32, (srows, pair_rows), 1)
    dist_n = t_col - cc % steps
    low = lax.broadcasted_iota(jnp.int32, (pair_rows, LANES), 1) < A_HD
    low_s = lax.broadcasted_iota(jnp.int32, (srows, LANES), 1) < A_HD
    sub = lax.broadcasted_iota(jnp.int32, (pair_rows, LANES), 0)

    def pair(j, carry):
        r0 = pl.multiple_of(j * pair_rows, pair_rows)
        q8 = qf_ref[pl.ds(r0, pair_rows), :]
        parts = []
        for p in range(A_GROUP):
            qp = q8[:, p * LANES:(p + 1) * LANES]
            parts.append(jnp.where(low, qp, 0.0))
            parts.append(jnp.where(low, 0.0, qp))
        qs = jnp.concatenate(parts, axis=0).astype(BF16)
        kv_new = kva_ref[pl.ds(r0, pair_rows), :]
        k_new = kv_new[:, :hw]
        v_new = kv_new[:, hw:]
        s_new = _dot_nt(qs, k_new.astype(BF16))
        out8 = [jnp.zeros((pair_rows, LANES), F32) for _ in range(A_GROUP)]
        for bb in range(2):
            b = 2 * j + bb
            kc = ck_ref[b]
            vc = cv_ref[b]
            s_c = jnp.where(valid_c, _dot_nt(qs, kc.astype(BF16)) - slope * dist_c.astype(F32), -jnp.inf)
            valid_n = (cc // steps == bb) & (dist_n >= 0)
            s_n = jnp.where(valid_n, s_new - slope * dist_n.astype(F32), -jnp.inf)
            mx = jnp.maximum(jnp.maximum(jnp.max(s_c, axis=-1, keepdims=True),
                                         jnp.max(s_n, axis=-1, keepdims=True)), sink)
            e_c = jnp.exp(s_c - mx)
            e_n = jnp.exp(s_n - mx)
            den = (jnp.sum(e_c, axis=-1, keepdims=True) + jnp.sum(e_n, axis=-1, keepdims=True)
                   + jnp.exp(sink - mx))
            pv = (_dot(e_c.astype(BF16), vc.astype(BF16)) + _dot(e_n.astype(BF16), v_new.astype(BF16))) / den
            pv = jnp.where(low_s == row_low, pv, 0.0)
            for p in range(A_GROUP):
                both = (pv[(2 * p) * pair_rows:(2 * p + 1) * pair_rows]
                        + pv[(2 * p + 1) * pair_rows:(2 * p + 2) * pair_rows])
                out8[p] = jnp.where(sub // steps == bb, both, out8[p])
            tail_k = k_new if bb == 1 else pltpu.roll(k_new, steps, 0)
            tail_v = v_new if bb == 1 else pltpu.roll(v_new, steps, 0)
            rk = pltpu.roll(kc, WINDOW - steps, 0)
            rv = pltpu.roll(vc, WINDOW - steps, 0)
            nk_ref[b] = rk
            nv_ref[b] = rv
            nk_ref[b, WINDOW - pair_rows:, :] = jnp.where(sub >= steps, tail_k, rk[WINDOW - pair_rows:])
            nv_ref[b, WINDOW - pair_rows:, :] = jnp.where(sub >= steps, tail_v, rv[WINDOW - pair_rows:])
        uo_ref[pl.ds(r0, pair_rows), :] = jnp.concatenate(out8, axis=-1)
        return carry

    lax.fori_loop(0, n_pairs, pair, 0)
    ua_ref[...] = uo_ref[...].astype(BF16)


def _swa_sample(sinks, qa, kva, ck, cv, steps):
    batch = ck.shape[0]
    nbt = SWA_SAMPLE_BATCHES
    rows = nbt * steps
    cache = pl.BlockSpec((nbt, WINDOW, LANES), lambda g: (g, 0, 0))
    return pl.pallas_call(
        functools.partial(_swa_sample_kernel, steps=steps),
        grid=(batch // nbt,),
        in_specs=[pl.BlockSpec(memory_space=pltpu.SMEM),
                  pl.BlockSpec((rows, A_WIDTH), lambda g: (g, 0)),
                  pl.BlockSpec((rows, KV_WIDTH), lambda g: (g, 0)),
                  cache, cache],
        out_specs=[pl.BlockSpec((rows, A_WIDTH), lambda g: (g, 0)), cache, cache],
        out_shape=[jax.ShapeDtypeStruct((batch * steps, A_WIDTH), BF16),
                   jax.ShapeDtypeStruct(ck.shape, F32),
                   jax.ShapeDtypeStruct(cv.shape, F32)],
        scratch_shapes=[pltpu.VMEM((rows, A_WIDTH), F32), pltpu.VMEM((rows, A_WIDTH), F32)],
        compiler_params=pltpu.CompilerParams(dimension_semantics=("parallel",),
                                             vmem_limit_bytes=VMEM_LIMIT),
        name="swa_sample",
    )(sinks, qa, kva, ck, cv)


def _pair_order():
    return [h for p in range(A_GROUP) for h in (p, p + A_GROUP)]


def _prep_ffn(w_gate, w_up, w_down):
    wg = w_gate.astype(BF16).reshape(D_MODEL, N_FFN_CHUNKS, FFN_CHUNK)
    wu = w_up.astype(BF16).reshape(D_MODEL, N_FFN_CHUNKS, FFN_CHUNK)
    wgu = jnp.concatenate([wg, wu], axis=-1).transpose(1, 0, 2)
    wd = w_down.astype(BF16).reshape(N_FFN_CHUNKS, FFN_CHUNK, D_MODEL)
    return wgu, wd


def _prep_proj(w_in, b_gate):
    m_end = 4 * M_WIDTH
    g_end = m_end + 2 * M_HEADS
    qa = w_in[:, g_end:g_end + A_WIDTH].reshape(D_MODEL, A_HEADS, A_HD)[:, jnp.array(_pair_order())]
    gates = jnp.pad(w_in[:, m_end:g_end], ((0, 0), (0, LANES - 2 * M_HEADS)))
    win = jnp.concatenate([w_in[:, :m_end], qa.reshape(D_MODEL, A_WIDTH), w_in[:, g_end + A_WIDTH:], gates],
                          axis=-1).astype(BF16)
    bias = jnp.pad(b_gate, (0, LANES - 2 * M_HEADS)).reshape(1, LANES)
    return win, bias


def _prep_out(w_out):
    wa = w_out[M_WIDTH:].reshape(A_HEADS, A_HD, D_MODEL)[jnp.array(_pair_order())]
    return jnp.concatenate([w_out[:M_WIDTH], wa.reshape(A_WIDTH, D_MODEL)], axis=0).astype(BF16)


def kernel(x_prompt, x_sample, cache_swa_k, cache_swa_v, state_mlstm_C, state_mlstm_n, state_mlstm_m,
           norm_gains, ffn_w_gate, ffn_w_up, ffn_w_down, w_in, b_gate, mlstm_norm_gain, attn_sinks, w_out):
    assert norm_gains.shape[0] == 1, "single layer"
    batch, seq, _ = x_prompt.shape
    dec_batch, steps, _ = x_sample.shape
    gains = norm_gains[0]
    wgu1, wd1 = _prep_ffn(ffn_w_gate[0, 0], ffn_w_up[0, 0], ffn_w_down[0, 0])
    wgu2, wd2 = _prep_ffn(ffn_w_gate[0, 1], ffn_w_up[0, 1], ffn_w_down[0, 1])
    win, bias = _prep_proj(w_in[0], b_gate[0])
    wo = _prep_out(w_out[0])
    gain = mlstm_norm_gain[0].reshape(1, M_WIDTH)
    sinks = attn_sinks[0]
    kv_lanes = A_KV_HEADS * A_HD

    xp = _ffn(x_prompt.reshape(batch * seq, D_MODEL), gains, wgu1, wd1, 0, 1)
    zm, qa, kva, gates = _proj(xp, gains, win, bias, 2)
    um, cn, mm = _mlstm_prompt(zm, gates, gain, batch)
    ua = _swa_prompt(sinks, qa, kva, batch)
    yp = _mix_ffn(xp, um, ua, wo, gains, wgu2, wd2, 3, 4, 5).reshape(batch, seq, D_MODEL)
    keep = min(WINDOW, seq)
    kv_tail = kva.reshape(batch, seq, KV_WIDTH)[:, seq - keep:]
    p_k = kv_tail[..., :kv_lanes].reshape(1, batch, keep, A_KV_HEADS, A_HD)
    p_v = kv_tail[..., kv_lanes:].reshape(1, batch, keep, A_KV_HEADS, A_HD)
    p_c = cn[None, :, :, :, :M_DV]
    p_n = cn[None, :, :, :, M_DV]
    p_m = mm[None, :, :, 0, 0]

    xs = _ffn(x_sample.reshape(dec_batch * steps, D_MODEL), gains, wgu1, wd1, 0, 1)
    zm_s, qa_s, kva_s, gates_s = _proj(xs, gains, win, bias, 2)
    m_rows = jnp.repeat(state_mlstm_m[0].T, steps, axis=1)[..., None]
    um_s, s_c, s_n2, s_m_rows = _mlstm_sample(zm_s, gates_s, gain, state_mlstm_C[0],
                                              state_mlstm_n[0].reshape(dec_batch, M_HEADS * M_DK),
                                              m_rows, steps)
    wc = cache_swa_k.shape[2]
    ua_s, s_k, s_v = _swa_sample(sinks, qa_s, kva_s, cache_swa_k[0].reshape(dec_batch, wc, kv_lanes),
                                 cache_swa_v[0].reshape(dec_batch, wc, kv_lanes), steps)
    ys = _mix_ffn(xs, um_s, ua_s, wo, gains, wgu2, wd2, 3, 4, 5).reshape(dec_batch, steps, D_MODEL)
    s_n = s_n2.reshape(1, dec_batch, M_HEADS, M_DK)
    s_m = s_m_rows[:, ::steps, 0].T[None]
    s_k = s_k.reshape(1, dec_batch, wc, A_KV_HEADS, A_HD)
    s_v = s_v.reshape(1, dec_batch, wc, A_KV_HEADS, A_HD)
    return (yp, ys, p_k, p_v, p_c, p_n, p_m, s_k, s_v, s_c[None], s_n, s_m)
```

```python
import functools

import jax
import jax.numpy as jnp
from jax import lax
from jax.experimental import pallas as pl
from jax.experimental.pallas import tpu as pltpu

F32 = jnp.float32
BF16 = jnp.bfloat16

D_MODEL = 1024
D_FF = 2816
M_HEADS = 4
M_DK = 128
M_DV = 128
M_WIDTH = M_HEADS * M_DV
A_HEADS = 8
A_KV_HEADS = 2
A_GROUP = A_HEADS // A_KV_HEADS
A_HD = 64
A_WIDTH = A_HEADS * A_HD
WINDOW = 128
RMS_EPS = 1e-6

LANES = 128
FFN_CHUNK = 256
N_FFN_CHUNKS = D_FF // FFN_CHUNK
TOKEN_TILE = 512
MLSTM_CHUNK = 256
ZM_WIDTH = 4 * M_WIDTH
KV_WIDTH = 2 * A_KV_HEADS * A_HD
PROJ_WIDTH = ZM_WIDTH + A_WIDTH + KV_WIDTH + LANES
VMEM_LIMIT = 56 * 1024 * 1024

NT_DIMS = (((1,), (1,)), ((), ()))
TN_DIMS = (((0,), (0,)), ((), ()))


def _rms(x, g):
    ms = jnp.mean(x * x, axis=-1, keepdims=True)
    return x * lax.rsqrt(ms + RMS_EPS) * g


def _log_sigmoid(x):
    return jnp.minimum(x, 0.0) - jnp.log1p(jnp.exp(-jnp.abs(x)))


def _dot(a, b):
    return jnp.dot(a, b, preferred_element_type=F32)


def _dot_nt(a, b):
    return lax.dot_general(a, b, NT_DIMS, preferred_element_type=F32)


def _resident(shape):
    nd = len(shape)
    return pl.BlockSpec(shape, lambda *_: (0,) * nd, pipeline_mode=pl.Buffered(1))


def _swiglu_residual(x, gpre, gpost, wg_ref, wu_ref, wd_ref, h_ref, acc_ref):
    h_ref[...] = _rms(x, gpre).astype(BF16)
    for f in range(N_FFN_CHUNKS):
        cols = slice(f * FFN_CHUNK, (f + 1) * FFN_CHUNK)
        h = h_ref[...]
        g = _dot(h, wg_ref[:, cols])
        u = _dot(h, wu_ref[:, cols])
        part = _dot((g * jax.nn.sigmoid(g) * u).astype(BF16), wd_ref[cols, :])
        if f == 0:
            acc_ref[...] = part
        else:
            acc_ref[...] += part
    return x + _rms(acc_ref[...], 0.5 * gpost)


def _ffn_kernel(x_ref, gains_ref, wg_ref, wu_ref, wd_ref, o_ref, h_ref, acc_ref, *, pre, post):
    o_ref[...] = _swiglu_residual(x_ref[...], gains_ref[pre:pre + 1, :], gains_ref[post:post + 1, :],
                                  wg_ref, wu_ref, wd_ref, h_ref, acc_ref)


def _mix_ffn_kernel(x_ref, um_ref, ua_ref, wo_ref, gains_ref, wg_ref, wu_ref, wd_ref, o_ref, h_ref, acc_ref,
                    *, mix, pre, post):
    u = jnp.concatenate([um_ref[...], ua_ref[...]], axis=-1)
    x = x_ref[...] + _rms(_dot(u, wo_ref[...]), gains_ref[mix:mix + 1, :])
    o_ref[...] = _swiglu_residual(x, gains_ref[pre:pre + 1, :], gains_ref[post:post + 1, :],
                                  wg_ref, wu_ref, wd_ref, h_ref, acc_ref)


def _ffn_scratch(tm):
    return [pltpu.VMEM((tm, D_MODEL), BF16), pltpu.VMEM((tm, D_MODEL), F32)]


def _dense_params():
    return pltpu.CompilerParams(dimension_semantics=("parallel",), vmem_limit_bytes=VMEM_LIMIT)


def _ffn(x, gains, wg, wu, wd, pre, post):
    n = x.shape[0]
    tm = min(TOKEN_TILE, n)
    row = pl.BlockSpec((tm, D_MODEL), lambda i: (i, 0))
    return pl.pallas_call(
        functools.partial(_ffn_kernel, pre=pre, post=post),
        grid=(n // tm,),
        in_specs=[row, _resident(gains.shape), _resident(wg.shape), _resident(wu.shape), _resident(wd.shape)],
        out_specs=row,
        out_shape=jax.ShapeDtypeStruct((n, D_MODEL), F32),
        scratch_shapes=_ffn_scratch(tm),
        compiler_params=_dense_params(),
        name="ffn_half_step",
    )(x, gains, wg, wu, wd)


def _mix_ffn(x, um, ua, wo, gains, wg, wu, wd, mix, pre, post):
    n = x.shape[0]
    tm = min(TOKEN_TILE, n)
    row = pl.BlockSpec((tm, D_MODEL), lambda i: (i, 0))
    half = pl.BlockSpec((tm, M_WIDTH), lambda i: (i, 0))
    return pl.pallas_call(
        functools.partial(_mix_ffn_kernel, mix=mix, pre=pre, post=post),
        grid=(n // tm,),
        in_specs=[row, half, half, _resident(wo.shape), _resident(gains.shape),
                  _resident(wg.shape), _resident(wu.shape), _resident(wd.shape)],
        out_specs=row,
        out_shape=jax.ShapeDtypeStruct((n, D_MODEL), F32),
        scratch_shapes=_ffn_scratch(tm),
        compiler_params=_dense_params(),
        name="out_proj_ffn_half_step",
    )(x, um, ua, wo, gains, wg, wu, wd)


def _proj_kernel(x_ref, gains_ref, win_ref, bias_ref, zm_ref, qa_ref, kva_ref, gates_ref, *, pre):
    h = _rms(x_ref[...], gains_ref[pre:pre + 1, :]).astype(BF16)

    def cols(lo, width):
        return _dot(h, win_ref[:, lo:lo + width])

    zm_ref[:, 0:M_WIDTH] = cols(0, M_WIDTH).astype(BF16)
    zm_ref[:, M_WIDTH:2 * M_WIDTH] = (cols(M_WIDTH, M_WIDTH) * (M_DK ** -0.5)).astype(BF16)
    zm_ref[:, 2 * M_WIDTH:3 * M_WIDTH] = cols(2 * M_WIDTH, M_WIDTH).astype(BF16)
    zm_ref[:, 3 * M_WIDTH:] = jax.nn.sigmoid(cols(3 * M_WIDTH, M_WIDTH)).astype(BF16)
    qa_ref[...] = (cols(ZM_WIDTH, A_WIDTH) * (A_HD ** -0.5)).astype(BF16)
    kva_ref[...] = cols(ZM_WIDTH + A_WIDTH, KV_WIDTH)
    gates_ref[...] = cols(ZM_WIDTH + A_WIDTH + KV_WIDTH, LANES) + bias_ref[...]


def _proj(x, gains, win, bias, pre):
    n = x.shape[0]
    tm = min(TOKEN_TILE, n)

    def rows(width):
        return pl.BlockSpec((tm, width), lambda i: (i, 0))

    return pl.pallas_call(
        functools.partial(_proj_kernel, pre=pre),
        grid=(n // tm,),
        in_specs=[rows(D_MODEL), _resident(gains.shape), _resident(win.shape), _resident(bias.shape)],
        out_specs=[rows(ZM_WIDTH), rows(A_WIDTH), rows(KV_WIDTH), rows(LANES)],
        out_shape=[jax.ShapeDtypeStruct((n, ZM_WIDTH), BF16),
                   jax.ShapeDtypeStruct((n, A_WIDTH), BF16),
                   jax.ShapeDtypeStruct((n, KV_WIDTH), F32),
                   jax.ShapeDtypeStruct((n, LANES), F32)],
        compiler_params=_dense_params(),
        name="input_projection",
    )(x, gains, win, bias)


def _ones_column(rows):
    lane = lax.broadcasted_iota(jnp.int32, (rows, LANES), 1)
    return jnp.where(lane == 0, 1.0, 0.0).astype(BF16)


def _head_norm_gate(hh, o, gain):
    hn = hh * lax.rsqrt(jnp.mean(hh * hh, axis=-1, keepdims=True) + RMS_EPS) * gain
    return (o.astype(F32) * hn).astype(BF16)


def _mlstm_prompt_kernel(zm_ref, gates_ref, gain_ref, um_ref, cn_ref, m_ref):
    chunk = zm_ref.shape[0]

    @pl.when(pl.program_id(1) == 0)
    def _():
        cn_ref[...] = jnp.zeros_like(cn_ref)
        m_ref[...] = jnp.zeros_like(m_ref)

    row = lax.broadcasted_iota(jnp.int32, (chunk, chunk), 0)
    col = lax.broadcasted_iota(jnp.int32, (chunk, chunk), 1)
    causal = col <= row
    gates = gates_ref[...]
    gates_t = gates.T
    hi = lax.Precision.HIGHEST
    b_cols = jnp.dot(causal.astype(F32), _log_sigmoid(gates), precision=hi, preferred_element_type=F32)
    b_rows = jnp.dot(_log_sigmoid(gates_t), (row <= col).astype(F32), precision=hi,
                     preferred_element_type=F32)
    ones_col = _ones_column(chunk)

    for h in range(M_HEADS):
        lanes = slice(h * M_DK, (h + 1) * M_DK)
        q = zm_ref[:, lanes]
        k = zm_ref[:, M_WIDTH + h * M_DK:M_WIDTH + (h + 1) * M_DK]
        v = zm_ref[:, 2 * M_WIDTH + h * M_DV:2 * M_WIDTH + (h + 1) * M_DV]
        o = zm_ref[:, 3 * M_WIDTH + h * M_DV:3 * M_WIDTH + (h + 1) * M_DV]
        ig_col = gates[:, h:h + 1]
        ig_row = gates_t[h:h + 1, :]
        b_col = b_cols[:, M_HEADS + h:M_HEADS + h + 1]
        b_row = b_rows[M_HEADS + h:M_HEADS + h + 1, :]
        m_prev = m_ref[0, h][:, 0:1]
        cn = cn_ref[0, h]

        d = jnp.where(causal, b_col - b_row + ig_row, -jnp.inf)
        inter = b_col + m_prev
        m_t = jnp.maximum(inter, jnp.max(d, axis=-1, keepdims=True))
        w = jnp.exp(d - m_t)
        a = jnp.exp(inter - m_t)
        s = _dot_nt(q, k) * w
        v_ext = jnp.concatenate([v, ones_col], axis=-1)
        nd = a * _dot(q, cn.astype(BF16)) + _dot(s.astype(BF16), v_ext)
        den = nd[:, M_DV:M_DV + 1]
        hh = nd[:, :M_DV] / jnp.maximum(jnp.abs(den), jnp.exp(-m_t))
        um_ref[:, lanes] = _head_norm_gate(hh, o, gain_ref[:, lanes])

        b_last = b_col[chunk - 1:chunk, :]
        g = b_last - b_col + ig_col
        m_new = jnp.maximum(b_last + m_prev, jnp.max(g, axis=0, keepdims=True))
        decay = jnp.exp(b_last + m_prev - m_new)
        wv = (jnp.exp(g - m_new) * v_ext.astype(F32)).astype(BF16)
        upd = lax.dot_general(k, wv, TN_DIMS, preferred_element_type=F32)
        cn_ref[0, h] = decay * cn + upd
        m_ref[0, h] = jnp.broadcast_to(m_new, (1, LANES))


def _mlstm_prompt(zm, gates, gain, batch):
    n = zm.shape[0]
    chunk = min(MLSTM_CHUNK, n // batch)
    nc = n // batch // chunk

    def rows(width):
        return pl.BlockSpec((chunk, width), lambda b, c: (b * nc + c, 0))

    return pl.pallas_call(
        _mlstm_prompt_kernel,
        grid=(batch, nc),
        in_specs=[rows(ZM_WIDTH), rows(LANES), pl.BlockSpec((1, M_WIDTH), lambda b, c: (0, 0))],
        out_specs=[rows(M_WIDTH),
                   pl.BlockSpec((1, M_HEADS, M_DK, 2 * LANES), lambda b, c: (b, 0, 0, 0)),
                   pl.BlockSpec((1, M_HEADS, 1, LANES), lambda b, c: (b, 0, 0, 0))],
        out_shape=[jax.ShapeDtypeStruct((n, M_WIDTH), BF16),
                   jax.ShapeDtypeStruct((batch, M_HEADS, M_DK, 2 * LANES), F32),
                   jax.ShapeDtypeStruct((batch, M_HEADS, 1, LANES), F32)],
        compiler_params=pltpu.CompilerParams(dimension_semantics=("parallel", "arbitrary"),
                                             vmem_limit_bytes=VMEM_LIMIT),
        name="mlstm_prompt",
    )(zm, gates, gain)


def _alibi_slope(head):
    return 2.0 ** (-8.0 * (head + 1) / A_HEADS)


def _swa_prompt_kernel(sinks_ref, qa_ref, kvc_ref, kvp_ref, ua_ref):
    blk = pl.program_id(1)
    kvp = kvp_ref[...]
    kvc = kvc_ref[...]
    hw = A_KV_HEADS * A_HD
    kk = jnp.concatenate([kvp[:, :hw], kvc[:, :hw]], axis=0).astype(BF16)
    vv = jnp.concatenate([kvp[:, hw:], kvc[:, hw:]], axis=0).astype(BF16)
    qi = lax.broadcasted_iota(jnp.int32, (WINDOW, 2 * WINDOW), 0)
    kj = lax.broadcasted_iota(jnp.int32, (WINDOW, 2 * WINDOW), 1)
    dist = WINDOW + qi - kj
    valid = (dist >= 0) & (dist < WINDOW) & ((kj >= WINDOW) | (blk > 0))
    distf = dist.astype(F32)
    low = lax.broadcasted_iota(jnp.int32, (WINDOW, LANES), 1) < A_HD

    for p in range(A_GROUP):
        qp = qa_ref[:, p * LANES:(p + 1) * LANES]
        outs = []
        for kvh in range(A_KV_HEADS):
            head = kvh * A_GROUP + p
            keep = low if kvh == 0 else jnp.logical_not(low)
            s = _dot_nt(jnp.where(keep, qp, jnp.zeros_like(qp)), kk)
            s = jnp.where(valid, s - _alibi_slope(head) * distf, -jnp.inf)
            sink = sinks_ref[head]
            mx = jnp.maximum(jnp.max(s, axis=-1, keepdims=True), sink)
            e = jnp.exp(s - mx)
            den = jnp.sum(e, axis=-1, keepdims=True) + jnp.exp(sink - mx)
            outs.append(_dot(e.astype(BF16), vv) / den)
        ua_ref[:, p * LANES:(p + 1) * LANES] = jnp.where(low, outs[0], outs[1]).astype(BF16)


def _swa_prompt(sinks, qa, kva, batch):
    n = qa.shape[0]
    nb = n // batch // WINDOW
    return pl.pallas_call(
        _swa_prompt_kernel,
        grid=(batch, nb),
        in_specs=[pl.BlockSpec(memory_space=pltpu.SMEM),
                  pl.BlockSpec((WINDOW, A_WIDTH), lambda b, i: (b * nb + i, 0)),
                  pl.BlockSpec((WINDOW, KV_WIDTH), lambda b, i: (b * nb + i, 0)),
                  pl.BlockSpec((WINDOW, KV_WIDTH), lambda b, i: (b * nb + jnp.maximum(i - 1, 0), 0))],
        out_specs=pl.BlockSpec((WINDOW, A_WIDTH), lambda b, i: (b * nb + i, 0)),
        out_shape=jax.ShapeDtypeStruct((n, A_WIDTH), BF16),
        compiler_params=pltpu.CompilerParams(dimension_semantics=("parallel", "parallel"),
                                             vmem_limit_bytes=VMEM_LIMIT),
        name="swa_prompt",
    )(sinks, qa, kva, kva)


SAMPLE_ROWS = 128


def _mlstm_sample_kernel(q_ref, k_ref, v_ref, o_ref, gates_ref, gain_ref, c_ref, n_ref, m_ref,
                         um_ref, co_ref, no_ref, mo_ref,
                         qf_ref, qc_ref, nexp_ref, wkk_ref, dec_ref, *, steps):
    rows = SAMPLE_ROWS
    nb = rows // steps
    head = pl.program_id(1)
    row = lax.broadcasted_iota(jnp.int32, (rows, rows), 0)
    col = lax.broadcasted_iota(jnp.int32, (rows, rows), 1)
    same = (row // steps) == (col // steps)
    causal = same & (col <= row)
    eye = row == col

    def to_row(x_col):
        return jnp.sum(jnp.where(eye, x_col, 0.0), axis=0, keepdims=True)

    def pick(mask, x_row, fill, reduce):
        return reduce(jnp.where(mask, x_row, fill), axis=-1, keepdims=True)

    gates = gates_ref[...]
    lane = lax.broadcasted_iota(jnp.int32, (rows, LANES), 1)
    ig_col = jnp.sum(jnp.where(lane == head, gates, 0.0), axis=-1, keepdims=True)
    fg_col = jnp.sum(jnp.where(lane == head + M_HEADS, gates, 0.0), axis=-1, keepdims=True)
    lf_row = to_row(_log_sigmoid(fg_col))
    b_col = pick(causal, lf_row, 0.0, jnp.sum)
    b_row = to_row(b_col)
    ig_row = to_row(ig_col)
    m_prev = m_ref[0]

    d = jnp.where(causal, b_col - b_row + ig_row, -jnp.inf)
    inter = b_col + m_prev
    m_t = jnp.maximum(inter, jnp.max(d, axis=-1, keepdims=True))
    w = jnp.exp(d - m_t)
    a = jnp.exp(inter - m_t)

    q = q_ref[...]
    k = k_ref[...]
    v = v_ref[...]
    qf_ref[...] = q.astype(F32)
    for t in range(steps):
        nexp_ref[pl.ds(t, nb, stride=steps), :] = n_ref[...]

    sub = lax.broadcasted_iota(jnp.int32, (2 * steps, M_DV), 0)

    def qc_pair(j, carry):
        q8 = qf_ref[pl.ds(pl.multiple_of(j * 2 * steps, 2 * steps), 2 * steps), :].astype(BF16)
        r0 = _dot(q8, c_ref[2 * j, 0].astype(BF16))
        r1 = _dot(q8, c_ref[2 * j + 1, 0].astype(BF16))
        qc_ref[pl.ds(pl.multiple_of(j * 2 * steps, 2 * steps), 2 * steps), :] = jnp.where(sub < steps, r0, r1)
        return carry

    lax.fori_loop(0, nb // 2, qc_pair, 0)

    s = _dot_nt(q, k) * w
    qn = jnp.sum(qf_ref[...] * nexp_ref[...], axis=-1, keepdims=True)
    num = a * qc_ref[...] + _dot(s.astype(BF16), v)
    den = a * qn + jnp.sum(s, axis=-1, keepdims=True)
    hh = num / jnp.maximum(jnp.abs(den), jnp.exp(-m_t))
    um_ref[...] = _head_norm_gate(hh, o_ref[...], gain_ref[...])

    last = same & ((col % steps) == steps - 1)
    b_last = pick(last, b_row, 0.0, jnp.sum)
    g_col = b_last - b_col + ig_col
    g_max = pick(same, to_row(g_col), -jnp.inf, jnp.max)
    m_new = jnp.maximum(b_last + m_prev, g_max)
    decay = jnp.exp(b_last + m_prev - m_new)
    wk = jnp.exp(g_col - m_new)
    mo_ref[0] = m_new

    kf = k.astype(F32)
    wkk_ref[...] = wk * kf
    dec_ref[...] = jnp.broadcast_to(decay, (rows, M_DV))
    n_sum = wkk_ref[pl.ds(0, nb, stride=steps), :]
    for t in range(1, steps):
        n_sum = n_sum + wkk_ref[pl.ds(t, nb, stride=steps), :]
    no_ref[...] = dec_ref[pl.ds(0, nb, stride=steps), :] * n_ref[...] + n_sum

    k_t = kf.T
    wv = (wk * v.astype(F32)).astype(BF16)
    tok_batch = lax.broadcasted_iota(jnp.int32, (M_DK, rows), 1) // steps

    def update(b, carry):
        upd = _dot(jnp.where(tok_batch == b, k_t, 0.0).astype(BF16), wv)
        co_ref[b, 0] = dec_ref[pl.ds(b * steps, 1), :] * c_ref[b, 0] + upd
        return carry

    lax.fori_loop(0, nb, update, 0)


def _mlstm_sample(zm, gates, gain, c, n2, m_rows, steps):
    rows = zm.shape[0]
    batch = c.shape[0]
    nb = SAMPLE_ROWS // steps
    groups = rows // SAMPLE_ROWS

    def zcol(section):
        return pl.BlockSpec((SAMPLE_ROWS, M_DK), lambda g, h: (g, section * M_HEADS + h))

    c_spec = pl.BlockSpec((nb, 1, M_DK, M_DV), lambda g, h: (g, h, 0, 0))
    n_spec = pl.BlockSpec((nb, M_DK), lambda g, h: (g, h))
    m_spec = pl.BlockSpec((1, SAMPLE_ROWS, 1), lambda g, h: (h, g, 0))
    return pl.pallas_call(
        functools.partial(_mlstm_sample_kernel, steps=steps),
        grid=(groups, M_HEADS),
        in_specs=[zcol(0), zcol(1), zcol(2), zcol(3),
                  pl.BlockSpec((SAMPLE_ROWS, LANES), lambda g, h: (g, 0)),
                  pl.BlockSpec((1, M_DV), lambda g, h: (0, h)),
                  c_spec, n_spec, m_spec],
        out_specs=[pl.BlockSpec((SAMPLE_ROWS, M_DV), lambda g, h: (g, h)), c_spec, n_spec, m_spec],
        out_shape=[jax.ShapeDtypeStruct((rows, M_WIDTH), BF16),
                   jax.ShapeDtypeStruct(c.shape, F32),
                   jax.ShapeDtypeStruct(n2.shape, F32),
                   jax.ShapeDtypeStruct(m_rows.shape, F32)],
        scratch_shapes=[pltpu.VMEM((SAMPLE_ROWS, M_DK), F32),
                        pltpu.VMEM((SAMPLE_ROWS, M_DV), F32),
                        pltpu.VMEM((SAMPLE_ROWS, M_DK), F32),
                        pltpu.VMEM((SAMPLE_ROWS, M_DK), F32),
                        pltpu.VMEM((SAMPLE_ROWS, M_DV), F32)],
        compiler_params=pltpu.CompilerParams(dimension_semantics=("parallel", "parallel"),
                                             vmem_limit_bytes=VMEM_LIMIT),
        name="mlstm_sample",
    )(zm, zm, zm, zm, gates, gain, c, n2, m_rows)


SWA_SAMPLE_BATCHES = 16


def _swa_sample_kernel(sinks_ref, qa_ref, kva_ref, ck_ref, cv_ref, ua_ref, nk_ref, nv_ref, qf_ref, uo_ref,
                       *, steps):
    pair_rows = 2 * steps
    n_pairs = SWA_SAMPLE_BATCHES // 2
    hw = A_KV_HEADS * A_HD
    srows = A_HEADS * pair_rows
    qf_ref[...] = qa_ref[...].astype(F32)

    r = lax.broadcasted_iota(jnp.int32, (srows, 1), 0)
    tok = r % pair_rows
    t_col = tok % steps
    row_batch = tok // steps
    blk = r // pair_rows
    head = (blk % A_KV_HEADS) * A_GROUP + blk // A_KV_HEADS
    slope = jnp.zeros((srows, 1), F32)
    sink = jnp.zeros((srows, 1), F32)
    for hd in range(A_HEADS):
        slope = jnp.where(head == hd, _alibi_slope(hd), slope)
        sink = jnp.where(head == hd, sinks_ref[hd], sink)
    row_low = (blk % A_KV_HEADS) == 0

    jj = lax.broadcasted_iota(jnp.int32, (srows, WINDOW), 1)
    dist_c = WINDOW + t_col - jj
    valid_c = dist_c < WINDOW
    cc = lax.broadcasted_iota(jnp.int32, (srows, pair_rows), 1)
    dist_n = t_col - cc % steps
    low = lax.broadcasted_iota(jnp.int32, (pair_rows, LANES), 1) < A_HD
    low_s = lax.broadcasted_iota(jnp.int32, (srows, LANES), 1) < A_HD
    sub = lax.broadcasted_iota(jnp.int32, (pair_rows, LANES), 0)

    def pair(j, carry):
        r0 = pl.multiple_of(j * pair_rows, pair_rows)
        q8 = qf_ref[pl.ds(r0, pair_rows), :]
        parts = []
        for p in range(A_GROUP):
            qp = q8[:, p * LANES:(p + 1) * LANES]
            parts.append(jnp.where(low, qp, 0.0))
            parts.append(jnp.where(low, 0.0, qp))
        qs = jnp.concatenate(parts, axis=0).astype(BF16)
        kv_new = kva_ref[pl.ds(r0, pair_rows), :]
        k_new = kv_new[:, :hw]
        v_new = kv_new[:, hw:]
        s_new = _dot_nt(qs, k_new.astype(BF16))
        out8 = [jnp.zeros((pair_rows, LANES), F32) for _ in range(A_GROUP)]
        for bb in range(2):
            b = 2 * j + bb
            kc = ck_ref[b]
            vc = cv_ref[b]
            s_c = jnp.where(valid_c, _dot_nt(qs, kc.astype(BF16)) - slope * dist_c.astype(F32), -jnp.inf)
            valid_n = (cc // steps == bb) & (dist_n >= 0)
            s_n = jnp.where(valid_n, s_new - slope * dist_n.astype(F32), -jnp.inf)
            mx = jnp.maximum(jnp.maximum(jnp.max(s_c, axis=-1, keepdims=True),
                                         jnp.max(s_n, axis=-1, keepdims=True)), sink)
            e_c = jnp.exp(s_c - mx)
            e_n = jnp.exp(s_n - mx)
            den = (jnp.sum(e_c, axis=-1, keepdims=True) + jnp.sum(e_n, axis=-1, keepdims=True)
                   + jnp.exp(sink - mx))
            pv = (_dot(e_c.astype(BF16), vc.astype(BF16)) + _dot(e_n.astype(BF16), v_new.astype(BF16))) / den
            pv = jnp.where(low_s == row_low, pv, 0.0)
            for p in range(A_GROUP):
                both = (pv[(2 * p) * pair_rows:(2 * p + 1) * pair_rows]
                        + pv[(2 * p + 1) * pair_rows:(2 * p + 2) * pair_rows])
                out8[p] = jnp.where(sub // steps == bb, both, out8[p])
            tail_k = k_new if bb == 1 else pltpu.roll(k_new, steps, 0)
            tail_v = v_new if bb == 1 else pltpu.roll(v_new, steps, 0)
            rk = pltpu.roll(kc, WINDOW - steps, 0)
            rv = pltpu.roll(vc, WINDOW - steps, 0)
            nk_ref[b] = rk
            nv_ref[b] = rv
            nk_ref[b, WINDOW - pair_rows:, :] = jnp.where(sub >= steps, tail_k, rk[WINDOW - pair_rows:])
            nv_ref[b, WINDOW - pair_rows:, :] = jnp.where(sub >= steps, tail_v, rv[WINDOW - pair_rows:])
        uo_ref[pl.ds(r0, pair_rows), :] = jnp.concatenate(out8, axis=-1)
        return carry

    lax.fori_loop(0, n_pairs, pair, 0)
    ua_ref[...] = uo_ref[...].astype(BF16)


def _swa_sample(sinks, qa, kva, ck, cv, steps):
    batch = ck.shape[0]
    nbt = SWA_SAMPLE_BATCHES
    rows = nbt * steps
    cache = pl.BlockSpec((nbt, WINDOW, LANES), lambda g: (g, 0, 0))
    return pl.pallas_call(
        functools.partial(_swa_sample_kernel, steps=steps),
        grid=(batch // nbt,),
        in_specs=[pl.BlockSpec(memory_space=pltpu.SMEM),
                  pl.BlockSpec((rows, A_WIDTH), lambda g: (g, 0)),
                  pl.BlockSpec((rows, KV_WIDTH), lambda g: (g, 0)),
                  cache, cache],
        out_specs=[pl.BlockSpec((rows, A_WIDTH), lambda g: (g, 0)), cache, cache],
        out_shape=[jax.ShapeDtypeStruct((batch * steps, A_WIDTH), BF16),
                   jax.ShapeDtypeStruct(ck.shape, F32),
                   jax.ShapeDtypeStruct(cv.shape, F32)],
        scratch_shapes=[pltpu.VMEM((rows, A_WIDTH), F32), pltpu.VMEM((rows, A_WIDTH), F32)],
        compiler_params=pltpu.CompilerParams(dimension_semantics=("parallel",),
                                             vmem_limit_bytes=VMEM_LIMIT),
        name="swa_sample",
    )(sinks, qa, kva, ck, cv)


def _prep_proj(w_in, b_gate):
    m_end = 4 * M_WIDTH
    g_end = m_end + 2 * M_HEADS
    qa = w_in[:, g_end:g_end + A_WIDTH].reshape(D_MODEL, A_KV_HEADS, A_GROUP, A_HD).transpose(0, 2, 1, 3)
    gates = jnp.pad(w_in[:, m_end:g_end], ((0, 0), (0, LANES - 2 * M_HEADS)))
    win = jnp.concatenate([w_in[:, :m_end], qa.reshape(D_MODEL, A_WIDTH), w_in[:, g_end + A_WIDTH:], gates],
                          axis=-1).astype(BF16)
    bias = jnp.pad(b_gate, (0, LANES - 2 * M_HEADS)).reshape(1, LANES)
    return win, bias


def _prep_out(w_out):
    wa = w_out[M_WIDTH:].reshape(A_KV_HEADS, A_GROUP, A_HD, D_MODEL).transpose(1, 0, 2, 3)
    return jnp.concatenate([w_out[:M_WIDTH], wa.reshape(A_WIDTH, D_MODEL)], axis=0).astype(BF16)


def kernel(x_prompt, x_sample, cache_swa_k, cache_swa_v, state_mlstm_C, state_mlstm_n, state_mlstm_m,
           norm_gains, ffn_w_gate, ffn_w_up, ffn_w_down, w_in, b_gate, mlstm_norm_gain, attn_sinks, w_out):
    assert norm_gains.shape[0] == 1, "single layer"
    batch, seq, _ = x_prompt.shape
    dec_batch, steps, _ = x_sample.shape
    gains = norm_gains[0]
    ffn1 = tuple(w[0, 0].astype(BF16) for w in (ffn_w_gate, ffn_w_up, ffn_w_down))
    ffn2 = tuple(w[0, 1].astype(BF16) for w in (ffn_w_gate, ffn_w_up, ffn_w_down))
    win, bias = _prep_proj(w_in[0], b_gate[0])
    wo = _prep_out(w_out[0])
    gain = mlstm_norm_gain[0].reshape(1, M_WIDTH)
    sinks = attn_sinks[0]
    kv_lanes = A_KV_HEADS * A_HD

    xp = _ffn(x_prompt.reshape(batch * seq, D_MODEL), gains, *ffn1, 0, 1)
    zm, qa, kva, gates = _proj(xp, gains, win, bias, 2)
    um, cn, mm = _mlstm_prompt(zm, gates, gain, batch)
    ua = _swa_prompt(sinks, qa, kva, batch)
    yp = _mix_ffn(xp, um, ua, wo, gains, *ffn2, 3, 4, 5).reshape(batch, seq, D_MODEL)
    keep = min(WINDOW, seq)
    kv_tail = kva.reshape(batch, seq, KV_WIDTH)[:, seq - keep:]
    p_k = kv_tail[..., :kv_lanes].reshape(1, batch, keep, A_KV_HEADS, A_HD)
    p_v = kv_tail[..., kv_lanes:].reshape(1, batch, keep, A_KV_HEADS, A_HD)
    p_c = cn[None, :, :, :, :M_DV]
    p_n = cn[None, :, :, :, M_DV]
    p_m = mm[None, :, :, 0, 0]

    xs = _ffn(x_sample.reshape(dec_batch * steps, D_MODEL), gains, *ffn1, 0, 1)
    zm_s, qa_s, kva_s, gates_s = _proj(xs, gains, win, bias, 2)
    m_rows = jnp.repeat(state_mlstm_m[0].T, steps, axis=1)[..., None]
    um_s, s_c, s_n2, s_m_rows = _mlstm_sample(zm_s, gates_s, gain, state_mlstm_C[0],
                                              state_mlstm_n[0].reshape(dec_batch, M_HEADS * M_DK),
                                              m_rows, steps)
    wc = cache_swa_k.shape[2]
    ua_s, s_k, s_v = _swa_sample(sinks, qa_s, kva_s, cache_swa_k[0].reshape(dec_batch, wc, kv_lanes),
                                 cache_swa_v[0].reshape(dec_batch, wc, kv_lanes), steps)
    ys = _mix_ffn(xs, um_s, ua_s, wo, gains, *ffn2, 3, 4, 5).reshape(dec_batch, steps, D_MODEL)
    s_n = s_n2.reshape(1, dec_batch, M_HEADS, M_DK)
    s_m = s_m_rows[:, ::steps, 0].T[None]
    s_k = s_k.reshape(1, dec_batch, wc, A_KV_HEADS, A_HD)
    s_v = s_v.reshape(1, dec_batch, wc, A_KV_HEADS, A_HD)
    return (yp, ys, p_k, p_v, p_c, p_n, p_m, s_k, s_v, s_c[None], s_n, s_m)
```

```python
import functools

import jax
import jax.numpy as jnp
from jax import lax
from jax.experimental import pallas as pl
from jax.experimental.pallas import tpu as pltpu

F32 = jnp.float32
BF16 = jnp.bfloat16

D_MODEL = 1024
D_FF = 2816
M_HEADS = 4
M_DK = 128
M_DV = 128
M_WIDTH = M_HEADS * M_DV
A_HEADS = 8
A_KV_HEADS = 2
A_GROUP = A_HEADS // A_KV_HEADS
A_HD = 64
A_WIDTH = A_HEADS * A_HD
WINDOW = 128
RMS_EPS = 1e-6

LANES = 128
FFN_CHUNK = 256
N_FFN_CHUNKS = D_FF // FFN_CHUNK
TOKEN_TILE = 512
MLSTM_CHUNK = 256
ZM_WIDTH = 4 * M_WIDTH
KV_WIDTH = 2 * A_KV_HEADS * A_HD
PROJ_WIDTH = ZM_WIDTH + A_WIDTH + KV_WIDTH + LANES
VMEM_LIMIT = 56 * 1024 * 1024

NT_DIMS = (((1,), (1,)), ((), ()))
TN_DIMS = (((0,), (0,)), ((), ()))


def _rms(x, g):
    ms = jnp.mean(x * x, axis=-1, keepdims=True)
    return x * lax.rsqrt(ms + RMS_EPS) * g


def _log_sigmoid(x):
    return jnp.minimum(x, 0.0) - jnp.log1p(jnp.exp(-jnp.abs(x)))


def _dot(a, b):
    return jnp.dot(a, b, preferred_element_type=F32)


def _dot_nt(a, b):
    return lax.dot_general(a, b, NT_DIMS, preferred_element_type=F32)


def _resident(shape):
    nd = len(shape)
    return pl.BlockSpec(shape, lambda *_: (0,) * nd, pipeline_mode=pl.Buffered(1))


def _resident_ffn_weight(w, j):
    return pl.BlockSpec((None, None) + w.shape[2:], lambda *_: (0, j, 0, 0), pipeline_mode=pl.Buffered(1))


def _swiglu_residual(x, gpre, gpost, wg_ref, wu_ref, wd_ref, h_ref, acc_ref):
    h_ref[...] = _rms(x, gpre).astype(BF16)
    for f in range(N_FFN_CHUNKS):
        cols = slice(f * FFN_CHUNK, (f + 1) * FFN_CHUNK)
        h = h_ref[...]
        g = _dot(h, wg_ref[:, cols])
        u = _dot(h, wu_ref[:, cols])
        part = _dot((g * jax.nn.sigmoid(g) * u).astype(BF16), wd_ref[cols, :])
        if f == 0:
            acc_ref[...] = part
        else:
            acc_ref[...] += part
    return x + _rms(acc_ref[...], 0.5 * gpost)


def _ffn_kernel(x_ref, gains_ref, wg_ref, wu_ref, wd_ref, o_ref, h_ref, acc_ref, *, pre, post):
    o_ref[...] = _swiglu_residual(x_ref[...], gains_ref[pre:pre + 1, :], gains_ref[post:post + 1, :],
                                  wg_ref, wu_ref, wd_ref, h_ref, acc_ref)


def _mix_ffn_kernel(x_ref, um_ref, ua_ref, wo_ref, gains_ref, wg_ref, wu_ref, wd_ref, o_ref, h_ref, acc_ref,
                    *, mix, pre, post):
    u = jnp.concatenate([um_ref[...], ua_ref[...]], axis=-1)
    x = x_ref[...] + _rms(_dot(u, wo_ref[...]), gains_ref[mix:mix + 1, :])
    o_ref[...] = _swiglu_residual(x, gains_ref[pre:pre + 1, :], gains_ref[post:post + 1, :],
                                  wg_ref, wu_ref, wd_ref, h_ref, acc_ref)


def _ffn_scratch(tm):
    return [pltpu.VMEM((tm, D_MODEL), BF16), pltpu.VMEM((tm, D_MODEL), F32)]


def _dense_params():
    return pltpu.CompilerParams(dimension_semantics=("parallel",), vmem_limit_bytes=VMEM_LIMIT)


def _ffn(x, gains, wg, wu, wd, j, pre, post):
    n = x.shape[0]
    tm = min(TOKEN_TILE, n)
    row = pl.BlockSpec((tm, D_MODEL), lambda i: (i, 0))
    return pl.pallas_call(
        functools.partial(_ffn_kernel, pre=pre, post=post),
        grid=(n // tm,),
        in_specs=[row, _resident(gains.shape)] + [_resident_ffn_weight(w, j) for w in (wg, wu, wd)],
        out_specs=row,
        out_shape=jax.ShapeDtypeStruct((n, D_MODEL), F32),
        scratch_shapes=_ffn_scratch(tm),
        compiler_params=_dense_params(),
        name="ffn_half_step",
    )(x, gains, wg, wu, wd)


def _mix_ffn(x, um, ua, wo, gains, wg, wu, wd, j, mix, pre, post):
    n = x.shape[0]
    tm = min(TOKEN_TILE, n)
    row = pl.BlockSpec((tm, D_MODEL), lambda i: (i, 0))
    half = pl.BlockSpec((tm, M_WIDTH), lambda i: (i, 0))
    return pl.pallas_call(
        functools.partial(_mix_ffn_kernel, mix=mix, pre=pre, post=post),
        grid=(n // tm,),
        in_specs=[row, half, half, _resident(wo.shape), _resident(gains.shape)]
        + [_resident_ffn_weight(w, j) for w in (wg, wu, wd)],
        out_specs=row,
        out_shape=jax.ShapeDtypeStruct((n, D_MODEL), F32),
        scratch_shapes=_ffn_scratch(tm),
        compiler_params=_dense_params(),
        name="out_proj_ffn_half_step",
    )(x, um, ua, wo, gains, wg, wu, wd)


def _proj_kernel(x_ref, gains_ref, win_ref, bias_ref, zm_ref, qa_ref, kva_ref, gates_ref, *, pre):
    h = _rms(x_ref[...], gains_ref[pre:pre + 1, :]).astype(BF16)

    def cols(lo, width):
        return _dot(h, win_ref[:, lo:lo + width])

    zm_ref[:, 0:M_WIDTH] = cols(0, M_WIDTH).astype(BF16)
    zm_ref[:, M_WIDTH:2 * M_WIDTH] = (cols(M_WIDTH, M_WIDTH) * (M_DK ** -0.5)).astype(BF16)
    zm_ref[:, 2 * M_WIDTH:3 * M_WIDTH] = cols(2 * M_WIDTH, M_WIDTH).astype(BF16)
    zm_ref[:, 3 * M_WIDTH:] = jax.nn.sigmoid(cols(3 * M_WIDTH, M_WIDTH)).astype(BF16)
    qa_ref[...] = (cols(ZM_WIDTH, A_WIDTH) * (A_HD ** -0.5)).astype(BF16)
    kva_ref[...] = cols(ZM_WIDTH + A_WIDTH, KV_WIDTH)
    gates_ref[...] = cols(ZM_WIDTH + A_WIDTH + KV_WIDTH, LANES) + bias_ref[...]


def _proj(x, gains, win, bias, pre):
    n = x.shape[0]
    tm = min(TOKEN_TILE, n)

    def rows(width):
        return pl.BlockSpec((tm, width), lambda i: (i, 0))

    return pl.pallas_call(
        functools.partial(_proj_kernel, pre=pre),
        grid=(n // tm,),
        in_specs=[rows(D_MODEL), _resident(gains.shape), _resident(win.shape), _resident(bias.shape)],
        out_specs=[rows(ZM_WIDTH), rows(A_WIDTH), rows(KV_WIDTH), rows(LANES)],
        out_shape=[jax.ShapeDtypeStruct((n, ZM_WIDTH), BF16),
                   jax.ShapeDtypeStruct((n, A_WIDTH), BF16),
                   jax.ShapeDtypeStruct((n, KV_WIDTH), F32),
                   jax.ShapeDtypeStruct((n, LANES), F32)],
        compiler_params=_dense_params(),
        name="input_projection",
    )(x, gains, win, bias)


def _head_norm_gate(hh, o, gain):
    hn = hh * lax.rsqrt(jnp.mean(hh * hh, axis=-1, keepdims=True) + RMS_EPS) * gain
    return (o.astype(F32) * hn).astype(BF16)


def _mlstm_prompt_kernel(zm_ref, gates_ref, gain_ref, um_ref, c_ref, n_ref, m_ref, cnt_ref, ms_ref):
    chunk = zm_ref.shape[0]
    step = pl.program_id(1)

    @pl.when(step == 0)
    def _():
        cnt_ref[...] = jnp.zeros_like(cnt_ref)
        ms_ref[...] = jnp.zeros_like(ms_ref)

    si = lax.broadcasted_iota(jnp.int32, (chunk, chunk), 0)
    ti = lax.broadcasted_iota(jnp.int32, (chunk, chunk), 1)
    causal = si <= ti
    gates = gates_ref[...]
    gates_t = gates.T
    hi = lax.Precision.HIGHEST
    b_cols = jnp.dot((ti <= si).astype(F32), _log_sigmoid(gates), precision=hi, preferred_element_type=F32)
    b_rows = jnp.dot(_log_sigmoid(gates_t), causal.astype(F32), precision=hi, preferred_element_type=F32)
    ones_rows = jnp.where(lax.broadcasted_iota(jnp.int32, (LANES, chunk), 0) == 0, 1.0, 0.0).astype(BF16)

    for h in range(M_HEADS):
        lanes = slice(h * M_DK, (h + 1) * M_DK)
        q = zm_ref[:, lanes]
        k = zm_ref[:, M_WIDTH + h * M_DK:M_WIDTH + (h + 1) * M_DK]
        v = zm_ref[:, 2 * M_WIDTH + h * M_DV:2 * M_WIDTH + (h + 1) * M_DV]
        o = zm_ref[:, 3 * M_WIDTH + h * M_DV:3 * M_WIDTH + (h + 1) * M_DV]
        x_col = gates[:, h:h + 1] - b_cols[:, M_HEADS + h:M_HEADS + h + 1]
        ig_row = gates_t[h:h + 1, :]
        b_row = b_rows[M_HEADS + h:M_HEADS + h + 1, :]
        m_prev = ms_ref[h][:, 0:1]
        cnt = cnt_ref[h]

        xm = jnp.where(causal, x_col, -jnp.inf)
        mu = jnp.maximum(m_prev, jnp.max(xm, axis=0, keepdims=True))
        a = jnp.exp(m_prev - mu)
        s_t = _dot_nt(k, q) * jnp.exp(xm - mu)
        v_ext = jnp.concatenate([v.astype(F32).T.astype(BF16), ones_rows], axis=0)
        nd = a * _dot_nt(cnt.astype(BF16), q) + _dot(v_ext, s_t.astype(BF16))
        den = nd[M_DV:M_DV + 1, :]
        hh = nd[:M_DV, :] * (1.0 / jnp.maximum(jnp.abs(den), jnp.exp(-(b_row + mu))))
        hn = hh * lax.rsqrt(jnp.mean(hh * hh, axis=0, keepdims=True) + RMS_EPS)
        um_ref[:, lanes] = (hn.T * gain_ref[:, lanes] * o.astype(F32)).astype(BF16)

        b_last = b_row[:, chunk - 1:chunk]
        g = b_last - b_row + ig_row
        m_new = jnp.maximum(b_last + m_prev, jnp.max(g, axis=-1, keepdims=True))
        decay = jnp.exp(b_last + m_prev - m_new)
        wv = (jnp.exp(g - m_new) * v_ext.astype(F32)).astype(BF16)
        cnt_ref[h] = decay * cnt + _dot(wv, k)
        ms_ref[h] = jnp.broadcast_to(m_new, (1, LANES))

    @pl.when(step == pl.num_programs(1) - 1)
    def _():
        for h in range(M_HEADS):
            c_ref[0, h] = cnt_ref[h][:M_DV, :].T
            n_ref[0, h] = cnt_ref[h][M_DV:M_DV + 1, :]
            m_ref[0, h] = ms_ref[h]


def _mlstm_prompt(zm, gates, gain, batch):
    n = zm.shape[0]
    chunk = min(MLSTM_CHUNK, n // batch)
    nc = n // batch // chunk

    def rows(width):
        return pl.BlockSpec((chunk, width), lambda b, c: (b * nc + c, 0))

    def state(*dims):
        return pl.BlockSpec((1, M_HEADS) + dims, lambda b, c: (b, 0, 0, 0))

    return pl.pallas_call(
        _mlstm_prompt_kernel,
        grid=(batch, nc),
        in_specs=[rows(ZM_WIDTH), rows(LANES), pl.BlockSpec((1, M_WIDTH), lambda b, c: (0, 0))],
        out_specs=[rows(M_WIDTH), state(M_DK, M_DV), state(1, M_DK), state(1, LANES)],
        out_shape=[jax.ShapeDtypeStruct((n, M_WIDTH), BF16),
                   jax.ShapeDtypeStruct((batch, M_HEADS, M_DK, M_DV), F32),
                   jax.ShapeDtypeStruct((batch, M_HEADS, 1, M_DK), F32),
                   jax.ShapeDtypeStruct((batch, M_HEADS, 1, LANES), F32)],
        scratch_shapes=[pltpu.VMEM((M_HEADS, 2 * LANES, M_DK), F32), pltpu.VMEM((M_HEADS, 1, LANES), F32)],
        compiler_params=pltpu.CompilerParams(dimension_semantics=("parallel", "arbitrary"),
                                             vmem_limit_bytes=VMEM_LIMIT),
        name="mlstm_prompt",
    )(zm, gates, gain)


def _alibi_slope(head):
    return 2.0 ** (-8.0 * (head + 1) / A_HEADS)


def _swa_prompt_kernel(sinks_ref, qa_ref, kvc_ref, kvp_ref, ua_ref):
    blk = pl.program_id(1)
    kvp = kvp_ref[...]
    kvc = kvc_ref[...]
    hw = A_KV_HEADS * A_HD
    kk = jnp.concatenate([kvp[:, :hw], kvc[:, :hw]], axis=0).astype(BF16)
    vv = jnp.concatenate([kvp[:, hw:], kvc[:, hw:]], axis=0).astype(BF16)
    qi = lax.broadcasted_iota(jnp.int32, (WINDOW, 2 * WINDOW), 0)
    kj = lax.broadcasted_iota(jnp.int32, (WINDOW, 2 * WINDOW), 1)
    dist = WINDOW + qi - kj
    valid = (dist >= 0) & (dist < WINDOW) & ((kj >= WINDOW) | (blk > 0))
    distf = dist.astype(F32)
    low = lax.broadcasted_iota(jnp.int32, (WINDOW, LANES), 1) < A_HD

    for p in range(A_GROUP):
        qp = qa_ref[:, p * LANES:(p + 1) * LANES]
        outs = []
        for kvh in range(A_KV_HEADS):
            head = kvh * A_GROUP + p
            keep = low if kvh == 0 else jnp.logical_not(low)
            s = _dot_nt(jnp.where(keep, qp, jnp.zeros_like(qp)), kk)
            s = jnp.where(valid, s - _alibi_slope(head) * distf, -jnp.inf)
            sink = sinks_ref[head]
            mx = jnp.maximum(jnp.max(s, axis=-1, keepdims=True), sink)
            e = jnp.exp(s - mx)
            den = jnp.sum(e, axis=-1, keepdims=True) + jnp.exp(sink - mx)
            outs.append(_dot(e.astype(BF16), vv) / den)
        ua_ref[:, p * LANES:(p + 1) * LANES] = jnp.where(low, outs[0], outs[1]).astype(BF16)


def _swa_prompt(sinks, qa, kva, batch):
    n = qa.shape[0]
    nb = n // batch // WINDOW
    return pl.pallas_call(
        _swa_prompt_kernel,
        grid=(batch, nb),
        in_specs=[pl.BlockSpec(memory_space=pltpu.SMEM),
                  pl.BlockSpec((WINDOW, A_WIDTH), lambda b, i: (b * nb + i, 0)),
                  pl.BlockSpec((WINDOW, KV_WIDTH), lambda b, i: (b * nb + i, 0)),
                  pl.BlockSpec((WINDOW, KV_WIDTH), lambda b, i: (b * nb + jnp.maximum(i - 1, 0), 0))],
        out_specs=pl.BlockSpec((WINDOW, A_WIDTH), lambda b, i: (b * nb + i, 0)),
        out_shape=jax.ShapeDtypeStruct((n, A_WIDTH), BF16),
        compiler_params=pltpu.CompilerParams(dimension_semantics=("parallel", "parallel"),
                                             vmem_limit_bytes=VMEM_LIMIT),
        name="swa_prompt",
    )(sinks, qa, kva, kva)


SAMPLE_ROWS = 128


def _mlstm_sample_kernel(q_ref, k_ref, v_ref, o_ref, gates_ref, gain_ref, c_ref, n_ref, m_ref,
                         um_ref, co_ref, no_ref, mo_ref,
                         qf_ref, qc_ref, nexp_ref, wkk_ref, dec_ref, *, steps):
    rows = SAMPLE_ROWS
    nb = rows // steps
    head = pl.program_id(1)
    row = lax.broadcasted_iota(jnp.int32, (rows, rows), 0)
    col = lax.broadcasted_iota(jnp.int32, (rows, rows), 1)
    same = (row // steps) == (col // steps)
    causal = same & (col <= row)
    eye = row == col

    def to_row(x_col):
        return jnp.sum(jnp.where(eye, x_col, 0.0), axis=0, keepdims=True)

    def pick(mask, x_row, fill, reduce):
        return reduce(jnp.where(mask, x_row, fill), axis=-1, keepdims=True)

    gates = gates_ref[...]
    lane = lax.broadcasted_iota(jnp.int32, (rows, LANES), 1)
    ig_col = jnp.sum(jnp.where(lane == head, gates, 0.0), axis=-1, keepdims=True)
    fg_col = jnp.sum(jnp.where(lane == head + M_HEADS, gates, 0.0), axis=-1, keepdims=True)
    lf_row = to_row(_log_sigmoid(fg_col))
    b_col = pick(causal, lf_row, 0.0, jnp.sum)
    b_row = to_row(b_col)
    ig_row = to_row(ig_col)
    m_prev = m_ref[0]

    d = jnp.where(causal, b_col - b_row + ig_row, -jnp.inf)
    inter = b_col + m_prev
    m_t = jnp.maximum(inter, jnp.max(d, axis=-1, keepdims=True))
    w = jnp.exp(d - m_t)
    a = jnp.exp(inter - m_t)

    q = q_ref[...]
    k = k_ref[...]
    v = v_ref[...]
    qf_ref[...] = q.astype(F32)
    for t in range(steps):
        nexp_ref[pl.ds(t, nb, stride=steps), :] = n_ref[...]

    sub = lax.broadcasted_iota(jnp.int32, (2 * steps, M_DV), 0)

    def qc_pair(j, carry):
        q8 = qf_ref[pl.ds(pl.multiple_of(j * 2 * steps, 2 * steps), 2 * steps), :].astype(BF16)
        r0 = _dot(q8, c_ref[2 * j, 0].astype(BF16))
        r1 = _dot(q8, c_ref[2 * j + 1, 0].astype(BF16))
        qc_ref[pl.ds(pl.multiple_of(j * 2 * steps, 2 * steps), 2 * steps), :] = jnp.where(sub < steps, r0, r1)
        return carry

    lax.fori_loop(0, nb // 2, qc_pair, 0, unroll=4)

    s = _dot_nt(q, k) * w
    qn = jnp.sum(qf_ref[...] * nexp_ref[...], axis=-1, keepdims=True)
    num = a * qc_ref[...] + _dot(s.astype(BF16), v)
    den = a * qn + jnp.sum(s, axis=-1, keepdims=True)
    hh = num / jnp.maximum(jnp.abs(den), jnp.exp(-m_t))
    um_ref[...] = _head_norm_gate(hh, o_ref[...], gain_ref[...])

    last = same & ((col % steps) == steps - 1)
    b_last = pick(last, b_row, 0.0, jnp.sum)
    g_col = b_last - b_col + ig_col
    g_max = pick(same, to_row(g_col), -jnp.inf, jnp.max)
    m_new = jnp.maximum(b_last + m_prev, g_max)
    decay = jnp.exp(b_last + m_prev - m_new)
    wk = jnp.exp(g_col - m_new)
    mo_ref[0] = m_new

    kf = k.astype(F32)
    wkk_ref[...] = wk * kf
    dec_ref[...] = jnp.broadcast_to(decay, (rows, M_DV))
    n_sum = wkk_ref[pl.ds(0, nb, stride=steps), :]
    for t in range(1, steps):
        n_sum = n_sum + wkk_ref[pl.ds(t, nb, stride=steps), :]
    no_ref[...] = dec_ref[pl.ds(0, nb, stride=steps), :] * n_ref[...] + n_sum

    k_t = kf.T
    wv = (wk * v.astype(F32)).astype(BF16)
    tok_batch = lax.broadcasted_iota(jnp.int32, (M_DK, rows), 1) // steps

    def update(b, carry):
        upd = _dot(jnp.where(tok_batch == b, k_t, 0.0).astype(BF16), wv)
        co_ref[b, 0] = dec_ref[pl.ds(b * steps, 1), :] * c_ref[b, 0] + upd
        return carry

    lax.fori_loop(0, nb, update, 0, unroll=8)


def _mlstm_sample(zm, gates, gain, c, n2, m_rows, steps):
    rows = zm.shape[0]
    batch = c.shape[0]
    nb = SAMPLE_ROWS // steps
    groups = rows // SAMPLE_ROWS

    def zcol(section):
        return pl.BlockSpec((SAMPLE_ROWS, M_DK), lambda g, h: (g, section * M_HEADS + h))

    c_spec = pl.BlockSpec((nb, 1, M_DK, M_DV), lambda g, h: (g, h, 0, 0))
    n_spec = pl.BlockSpec((nb, M_DK), lambda g, h: (g, h))
    m_spec = pl.BlockSpec((1, SAMPLE_ROWS, 1), lambda g, h: (h, g, 0))
    return pl.pallas_call(
        functools.partial(_mlstm_sample_kernel, steps=steps),
        grid=(groups, M_HEADS),
        in_specs=[zcol(0), zcol(1), zcol(2), zcol(3),
                  pl.BlockSpec((SAMPLE_ROWS, LANES), lambda g, h: (g, 0)),
                  pl.BlockSpec((1, M_DV), lambda g, h: (0, h)),
                  c_spec, n_spec, m_spec],
        out_specs=[pl.BlockSpec((SAMPLE_ROWS, M_DV), lambda g, h: (g, h)), c_spec, n_spec, m_spec],
        out_shape=[jax.ShapeDtypeStruct((rows, M_WIDTH), BF16),
                   jax.ShapeDtypeStruct(c.shape, F32),
                   jax.ShapeDtypeStruct(n2.shape, F32),
                   jax.ShapeDtypeStruct(m_rows.shape, F32)],
        scratch_shapes=[pltpu.VMEM((SAMPLE_ROWS, M_DK), F32),
                        pltpu.VMEM((SAMPLE_ROWS, M_DV), F32),
                        pltpu.VMEM((SAMPLE_ROWS, M_DK), F32),
                        pltpu.VMEM((SAMPLE_ROWS, M_DK), F32),
                        pltpu.VMEM((SAMPLE_ROWS, M_DV), F32)],
        compiler_params=pltpu.CompilerParams(dimension_semantics=("parallel", "parallel"),
                                             vmem_limit_bytes=VMEM_LIMIT),
        name="mlstm_sample",
    )(zm, zm, zm, zm, gates, gain, c, n2, m_rows)


SWA_SAMPLE_BATCHES = 16


def _swa_sample_kernel(sinks_ref, qa_ref, kva_ref, ck_ref, cv_ref, ua_ref, nk_ref, nv_ref, qf_ref, uo_ref,
                       *, steps):
    pair_rows = 2 * steps
    n_pairs = SWA_SAMPLE_BATCHES // 2
    hw = A_KV_HEADS * A_HD
    srows = A_HEADS * pair_rows
    qf_ref[...] = qa_ref[...].astype(F32)

    r = lax.broadcasted_iota(jnp.int32, (srows, 1), 0)
    tok = r % pair_rows
    t_col = tok % steps
    row_batch = tok // steps
    blk = r // pair_rows
    head = (blk % A_KV_HEADS) * A_GROUP + blk // A_KV_HEADS
    slope = jnp.zeros((srows, 1), F32)
    sink = jnp.zeros((srows, 1), F32)
    for hd in range(A_HEADS):
        slope = jnp.where(head == hd, _alibi_slope(hd), slope)
        sink = jnp.where(head == hd, sinks_ref[hd], sink)
    row_low = (blk % A_KV_HEADS) == 0

    jj = lax.broadcasted_iota(jnp.int32, (srows, WINDOW), 1)
    dist_c = WINDOW + t_col - jj
    valid_c = dist_c < WINDOW
    cc = lax.broadcasted_iota(jnp.int32, (srows, pair_rows), 1)
    dist_n = t_col - cc % steps
    low = lax.broadcasted_iota(jnp.int32, (pair_rows, LANES), 1) < A_HD
    low_s = lax.broadcasted_iota(jnp.int32, (srows, LANES), 1) < A_HD
    sub = lax.broadcasted_iota(jnp.int32, (pair_rows, LANES), 0)

    def pair(j, carry):
        r0 = pl.multiple_of(j * pair_rows, pair_rows)
        q8 = qf_ref[pl.ds(r0, pair_rows), :]
        parts = []
        for p in range(A_GROUP):
            qp = q8[:, p * LANES:(p + 1) * LANES]
            parts.append(jnp.where(low, qp, 0.0))
            parts.append(jnp.where(low, 0.0, qp))
        qs = jnp.concatenate(parts, axis=0).astype(BF16)
        kv_new = kva_ref[pl.ds(r0, pair_rows), :]
        k_new = kv_new[:, :hw]
        v_new = kv_new[:, hw:]
        s_new = _dot_nt(qs, k_new.astype(BF16))
        out8 = [jnp.zeros((pair_rows, LANES), F32) for _ in range(A_GROUP)]
        for bb in range(2):
            b = 2 * j + bb
            kc = ck_ref[b]
            vc = cv_ref[b]
            s_c = jnp.where(valid_c, _dot_nt(qs, kc.astype(BF16)) - slope * dist_c.astype(F32), -jnp.inf)
            valid_n = (cc // steps == bb) & (dist_n >= 0)
            s_n = jnp.where(valid_n, s_new - slope * dist_n.astype(F32), -jnp.inf)
            mx = jnp.maximum(jnp.maximum(jnp.max(s_c, axis=-1, keepdims=True),
                                         jnp.max(s_n, axis=-1, keepdims=True)), sink)
            e_c = jnp.exp(s_c - mx)
            e_n = jnp.exp(s_n - mx)
            den = (jnp.sum(e_c, axis=-1, keepdims=True) + jnp.sum(e_n, axis=-1, keepdims=True)
                   + jnp.exp(sink - mx))
            pv = (_dot(e_c.astype(BF16), vc.astype(BF16)) + _dot(e_n.astype(BF16), v_new.astype(BF16))) / den
            pv = jnp.where(low_s == row_low, pv, 0.0)
            for p in range(A_GROUP):
                both = (pv[(2 * p) * pair_rows:(2 * p + 1) * pair_rows]
                        + pv[(2 * p + 1) * pair_rows:(2 * p + 2) * pair_rows])
                out8[p] = jnp.where(sub // steps == bb, both, out8[p])
            tail_k = k_new if bb == 1 else pltpu.roll(k_new, steps, 0)
            tail_v = v_new if bb == 1 else pltpu.roll(v_new, steps, 0)
            rk = pltpu.roll(kc, WINDOW - steps, 0)
            rv = pltpu.roll(vc, WINDOW - steps, 0)
            nk_ref[b] = rk
            nv_ref[b] = rv
            nk_ref[b, WINDOW - pair_rows:, :] = jnp.where(sub >= steps, tail_k, rk[WINDOW - pair_rows:])
            nv_ref[b, WINDOW - pair_rows:, :] = jnp.where(sub >= steps, tail_v, rv[WINDOW - pair_rows:])
        uo_ref[pl.ds(r0, pair_rows), :] = jnp.concatenate(out8, axis=-1)
        return carry

    lax.fori_loop(0, n_pairs, pair, 0, unroll=2)
    ua_ref[...] = uo_ref[...].astype(BF16)


def _swa_sample(sinks, qa, kva, ck, cv, steps):
    batch = ck.shape[0]
    nbt = SWA_SAMPLE_BATCHES
    rows = nbt * steps
    cache = pl.BlockSpec((nbt, WINDOW, LANES), lambda g: (g, 0, 0))
    return pl.pallas_call(
        functools.partial(_swa_sample_kernel, steps=steps),
        grid=(batch // nbt,),
        in_specs=[pl.BlockSpec(memory_space=pltpu.SMEM),
                  pl.BlockSpec((rows, A_WIDTH), lambda g: (g, 0)),
                  pl.BlockSpec((rows, KV_WIDTH), lambda g: (g, 0)),
                  cache, cache],
        out_specs=[pl.BlockSpec((rows, A_WIDTH), lambda g: (g, 0)), cache, cache],
        out_shape=[jax.ShapeDtypeStruct((batch * steps, A_WIDTH), BF16),
                   jax.ShapeDtypeStruct(ck.shape, F32),
                   jax.ShapeDtypeStruct(cv.shape, F32)],
        scratch_shapes=[pltpu.VMEM((rows, A_WIDTH), F32), pltpu.VMEM((rows, A_WIDTH), F32)],
        compiler_params=pltpu.CompilerParams(dimension_semantics=("parallel",),
                                             vmem_limit_bytes=VMEM_LIMIT),
        name="swa_sample",
    )(sinks, qa, kva, ck, cv)


def _prep_proj(w_in, b_gate):
    m_end = 4 * M_WIDTH
    g_end = m_end + 2 * M_HEADS
    qa = w_in[:, g_end:g_end + A_WIDTH].reshape(D_MODEL, A_KV_HEADS, A_GROUP, A_HD).transpose(0, 2, 1, 3)
    gates = jnp.pad(w_in[:, m_end:g_end], ((0, 0), (0, LANES - 2 * M_HEADS)))
    win = jnp.concatenate([w_in[:, :m_end], qa.reshape(D_MODEL, A_WIDTH), w_in[:, g_end + A_WIDTH:], gates],
                          axis=-1).astype(BF16)
    bias = jnp.pad(b_gate, (0, LANES - 2 * M_HEADS)).reshape(1, LANES)
    return win, bias


def _prep_out(w_out):
    wa = w_out[M_WIDTH:].reshape(A_KV_HEADS, A_GROUP, A_HD, D_MODEL).transpose(1, 0, 2, 3)
    return jnp.concatenate([w_out[:M_WIDTH], wa.reshape(A_WIDTH, D_MODEL)], axis=0).astype(BF16)


def kernel(x_prompt, x_sample, cache_swa_k, cache_swa_v, state_mlstm_C, state_mlstm_n, state_mlstm_m,
           norm_gains, ffn_w_gate, ffn_w_up, ffn_w_down, w_in, b_gate, mlstm_norm_gain, attn_sinks, w_out):
    assert norm_gains.shape[0] == 1, "single layer"
    batch, seq, _ = x_prompt.shape
    dec_batch, steps, _ = x_sample.shape
    gains = norm_gains[0]
    ffn = tuple(w.astype(BF16) for w in (ffn_w_gate, ffn_w_up, ffn_w_down))
    win, bias = _prep_proj(w_in[0], b_gate[0])
    wo = _prep_out(w_out[0])
    gain = mlstm_norm_gain[0].reshape(1, M_WIDTH)
    sinks = attn_sinks[0]
    kv_lanes = A_KV_HEADS * A_HD

    xp = _ffn(x_prompt.reshape(batch * seq, D_MODEL), gains, *ffn, 0, 0, 1)
    zm, qa, kva, gates = _proj(xp, gains, win, bias, 2)
    um, p_c, p_n, p_m = _mlstm_prompt(zm, gates, gain, batch)
    ua = _swa_prompt(sinks, qa, kva, batch)
    yp = _mix_ffn(xp, um, ua, wo, gains, *ffn, 1, 3, 4, 5).reshape(batch, seq, D_MODEL)
    keep = min(WINDOW, seq)
    kv_tail = kva.reshape(batch, seq, KV_WIDTH)[:, seq - keep:]
    p_k = kv_tail[..., :kv_lanes].reshape(1, batch, keep, A_KV_HEADS, A_HD)
    p_v = kv_tail[..., kv_lanes:].reshape(1, batch, keep, A_KV_HEADS, A_HD)
    p_c = p_c[None]
    p_n = p_n[None, :, :, 0, :]
    p_m = p_m[None, :, :, 0, 0]

    xs = _ffn(x_sample.reshape(dec_batch * steps, D_MODEL), gains, *ffn, 0, 0, 1)
    zm_s, qa_s, kva_s, gates_s = _proj(xs, gains, win, bias, 2)
    m_rows = jnp.repeat(state_mlstm_m[0].T, steps, axis=1)[..., None]
    um_s, s_c, s_n2, s_m_rows = _mlstm_sample(zm_s, gates_s, gain, state_mlstm_C[0],
                                              state_mlstm_n[0].reshape(dec_batch, M_HEADS * M_DK),
                                              m_rows, steps)
    wc = cache_swa_k.shape[2]
    ua_s, s_k, s_v = _swa_sample(sinks, qa_s, kva_s, cache_swa_k[0].reshape(dec_batch, wc, kv_lanes),
                                 cache_swa_v[0].reshape(dec_batch, wc, kv_lanes), steps)
    ys = _mix_ffn(xs, um_s, ua_s, wo, gains, *ffn, 1, 3, 4, 5).reshape(dec_batch, steps, D_MODEL)
    s_n = s_n2.reshape(1, dec_batch, M_HEADS, M_DK)
    s_m = s_m_rows[:, ::steps, 0].T[None]
    s_k = s_k.reshape(1, dec_batch, wc, A_KV_HEADS, A_HD)
    s_v = s_v.reshape(1, dec_batch, wc, A_KV_HEADS, A_HD)
    return (yp, ys, p_k, p_v, p_c, p_n, p_m, s_k, s_v, s_c[None], s_n, s_m)
```

```python
import functools

import jax
import jax.numpy as jnp
from jax import lax
from jax.experimental import pallas as pl
from jax.experimental.pallas import tpu as pltpu

F32 = jnp.float32
BF16 = jnp.bfloat16

D_MODEL = 1024
D_FF = 2816
M_HEADS = 4
M_DK = 128
M_DV = 128
M_WIDTH = M_HEADS * M_DV
A_HEADS = 8
A_KV_HEADS = 2
A_GROUP = A_HEADS // A_KV_HEADS
A_HD = 64
A_WIDTH = A_HEADS * A_HD
WINDOW = 128
RMS_EPS = 1e-6

LANES = 128
FFN_CHUNK = 256
N_FFN_CHUNKS = D_FF // FFN_CHUNK
TOKEN_TILE = 512
MLSTM_CHUNK = 256
ZM_WIDTH = 4 * M_WIDTH
KV_WIDTH = 2 * A_KV_HEADS * A_HD
VMEM_LIMIT = 56 * 1024 * 1024

NT_DIMS = (((1,), (1,)), ((), ()))


def _rms(x, g):
    ms = jnp.mean(x * x, axis=-1, keepdims=True)
    return x * lax.rsqrt(ms + RMS_EPS) * g


def _log_sigmoid(x):
    return jnp.minimum(x, 0.0) - jnp.log1p(jnp.exp(-jnp.abs(x)))


def _dot(a, b):
    return jnp.dot(a, b, preferred_element_type=F32)


def _dot_nt(a, b):
    return lax.dot_general(a, b, NT_DIMS, preferred_element_type=F32)


def _resident(shape):
    nd = len(shape)
    return pl.BlockSpec(shape, lambda *_: (0,) * nd, pipeline_mode=pl.Buffered(1))


def _resident_ffn_weight(w, j):
    return pl.BlockSpec((None, None) + w.shape[2:], lambda *_: (0, j, 0, 0), pipeline_mode=pl.Buffered(1))


def _swiglu_residual(x, gpre, gpost, wg_ref, wu_ref, wd_ref, h_ref, acc_ref):
    h_ref[...] = _rms(x, gpre).astype(BF16)
    for f in range(N_FFN_CHUNKS):
        cols = slice(f * FFN_CHUNK, (f + 1) * FFN_CHUNK)
        h = h_ref[...]
        g = _dot(h, wg_ref[:, cols])
        u = _dot(h, wu_ref[:, cols])
        part = _dot((g * jax.nn.sigmoid(g) * u).astype(BF16), wd_ref[cols, :])
        if f == 0:
            acc_ref[...] = part
        else:
            acc_ref[...] += part
    return x + _rms(acc_ref[...], 0.5 * gpost)


def _project(x, g, win_ref, bias_ref, h_ref, zm_ref, qa_ref, kva_ref, gates_ref):
    h_ref[...] = _rms(x, g).astype(BF16)

    def cols(lo, width):
        return _dot_nt(h_ref[...], win_ref[lo:lo + width, :])

    zm_ref[:, 0:M_WIDTH] = cols(0, M_WIDTH).astype(BF16)
    zm_ref[:, M_WIDTH:2 * M_WIDTH] = (cols(M_WIDTH, M_WIDTH) * (M_DK ** -0.5)).astype(BF16)
    zm_ref[:, 2 * M_WIDTH:3 * M_WIDTH] = cols(2 * M_WIDTH, M_WIDTH).astype(BF16)
    zm_ref[:, 3 * M_WIDTH:] = jax.nn.sigmoid(cols(3 * M_WIDTH, M_WIDTH)).astype(BF16)
    qa_ref[...] = (cols(ZM_WIDTH, A_WIDTH) * (A_HD ** -0.5)).astype(BF16)
    kva_ref[...] = cols(ZM_WIDTH + A_WIDTH, KV_WIDTH)
    gates_ref[...] = cols(ZM_WIDTH + A_WIDTH + KV_WIDTH, LANES) + bias_ref[...]


def _ffn_proj_kernel(x_ref, gains_ref, wg_ref, wu_ref, wd_ref, win_ref, bias_ref,
                     x1_ref, zm_ref, qa_ref, kva_ref, gates_ref, h_ref, acc_ref, *, pre, post, mixer):
    x1 = _swiglu_residual(x_ref[...], gains_ref[pre:pre + 1, :], gains_ref[post:post + 1, :],
                          wg_ref, wu_ref, wd_ref, h_ref, acc_ref)
    x1_ref[...] = x1
    _project(x1, gains_ref[mixer:mixer + 1, :], win_ref, bias_ref, h_ref, zm_ref, qa_ref, kva_ref, gates_ref)


def _mix_ffn_kernel(x_ref, um_ref, ua_ref, wo_ref, gains_ref, wg_ref, wu_ref, wd_ref, o_ref, h_ref, acc_ref,
                    *, mix, pre, post):
    u = jnp.concatenate([um_ref[...], ua_ref[...]], axis=-1)
    x = x_ref[...] + _rms(_dot(u, wo_ref[...]), gains_ref[mix:mix + 1, :])
    o_ref[...] = _swiglu_residual(x, gains_ref[pre:pre + 1, :], gains_ref[post:post + 1, :],
                                  wg_ref, wu_ref, wd_ref, h_ref, acc_ref)


def _ffn_scratch(tm):
    return [pltpu.VMEM((tm, D_MODEL), BF16), pltpu.VMEM((tm, D_MODEL), F32)]


def _dense_params():
    return pltpu.CompilerParams(dimension_semantics=("parallel",), vmem_limit_bytes=VMEM_LIMIT)


def _ffn_proj(x, gains, wg, wu, wd, win, bias, j, pre, post, mixer):
    n = x.shape[0]
    tm = min(TOKEN_TILE, n)

    def rows(width):
        return pl.BlockSpec((tm, width), lambda i: (i, 0))

    return pl.pallas_call(
        functools.partial(_ffn_proj_kernel, pre=pre, post=post, mixer=mixer),
        grid=(n // tm,),
        in_specs=[rows(D_MODEL), _resident(gains.shape)] + [_resident_ffn_weight(w, j) for w in (wg, wu, wd)]
        + [_resident(win.shape), _resident(bias.shape)],
        out_specs=[rows(D_MODEL), rows(ZM_WIDTH), rows(A_WIDTH), rows(KV_WIDTH), rows(LANES)],
        out_shape=[jax.ShapeDtypeStruct((n, D_MODEL), F32),
                   jax.ShapeDtypeStruct((n, ZM_WIDTH), BF16),
                   jax.ShapeDtypeStruct((n, A_WIDTH), BF16),
                   jax.ShapeDtypeStruct((n, KV_WIDTH), F32),
                   jax.ShapeDtypeStruct((n, LANES), F32)],
        scratch_shapes=_ffn_scratch(tm),
        compiler_params=_dense_params(),
        name="ffn_half_step_input_projection",
    )(x, gains, wg, wu, wd, win, bias)


def _mix_ffn(x, um, ua, wo, gains, wg, wu, wd, j, mix, pre, post):
    n = x.shape[0]
    tm = min(TOKEN_TILE, n)
    row = pl.BlockSpec((tm, D_MODEL), lambda i: (i, 0))
    half = pl.BlockSpec((tm, M_WIDTH), lambda i: (i, 0))
    return pl.pallas_call(
        functools.partial(_mix_ffn_kernel, mix=mix, pre=pre, post=post),
        grid=(n // tm,),
        in_specs=[row, half, half, _resident(wo.shape), _resident(gains.shape)]
        + [_resident_ffn_weight(w, j) for w in (wg, wu, wd)],
        out_specs=row,
        out_shape=jax.ShapeDtypeStruct((n, D_MODEL), F32),
        scratch_shapes=_ffn_scratch(tm),
        compiler_params=_dense_params(),
        name="out_proj_ffn_half_step",
    )(x, um, ua, wo, gains, wg, wu, wd)


def _head_norm_gate(hh, o, gain):
    hn = hh * lax.rsqrt(jnp.mean(hh * hh, axis=-1, keepdims=True) + RMS_EPS) * gain
    return (o.astype(F32) * hn).astype(BF16)


def _mlstm_prompt_kernel(zm_ref, gates_ref, gain_ref, um_ref, c_ref, n_ref, m_ref, cnt_ref, ms_ref):
    chunk = zm_ref.shape[0]
    step = pl.program_id(1)

    @pl.when(step == 0)
    def _():
        cnt_ref[...] = jnp.zeros_like(cnt_ref)
        ms_ref[...] = jnp.zeros_like(ms_ref)

    si = lax.broadcasted_iota(jnp.int32, (chunk, chunk), 0)
    ti = lax.broadcasted_iota(jnp.int32, (chunk, chunk), 1)
    causal = si <= ti
    gates = gates_ref[...]
    gates_t = gates.T
    hi = lax.Precision.HIGHEST
    b_cols = jnp.dot((ti <= si).astype(F32), _log_sigmoid(gates), precision=hi, preferred_element_type=F32)
    b_rows = jnp.dot(_log_sigmoid(gates_t), causal.astype(F32), precision=hi, preferred_element_type=F32)
    ones_rows = jnp.where(lax.broadcasted_iota(jnp.int32, (LANES, chunk), 0) == 0, 1.0, 0.0).astype(BF16)

    for h in range(M_HEADS):
        lanes = slice(h * M_DK, (h + 1) * M_DK)
        q = zm_ref[:, lanes]
        k = zm_ref[:, M_WIDTH + h * M_DK:M_WIDTH + (h + 1) * M_DK]
        v = zm_ref[:, 2 * M_WIDTH + h * M_DV:2 * M_WIDTH + (h + 1) * M_DV]
        o = zm_ref[:, 3 * M_WIDTH + h * M_DV:3 * M_WIDTH + (h + 1) * M_DV]
        x_col = gates[:, h:h + 1] - b_cols[:, M_HEADS + h:M_HEADS + h + 1]
        ig_row = gates_t[h:h + 1, :]
        b_row = b_rows[M_HEADS + h:M_HEADS + h + 1, :]
        m_prev = ms_ref[h][:, 0:1]
        cnt = cnt_ref[h]

        xm = jnp.where(causal, x_col, -jnp.inf)
        mu = jnp.maximum(m_prev, jnp.max(xm, axis=0, keepdims=True))
        a = jnp.exp(m_prev - mu)
        s_t = _dot_nt(k, q) * jnp.exp(xm - mu)
        v_ext = jnp.concatenate([v.astype(F32).T.astype(BF16), ones_rows], axis=0)
        nd = a * _dot_nt(cnt.astype(BF16), q) + _dot(v_ext, s_t.astype(BF16))
        den = nd[M_DV:M_DV + 1, :]
        hh = nd[:M_DV, :] * (1.0 / jnp.maximum(jnp.abs(den), jnp.exp(-(b_row + mu))))
        hn = hh * lax.rsqrt(jnp.mean(hh * hh, axis=0, keepdims=True) + RMS_EPS)
        um_ref[:, lanes] = (hn.T * gain_ref[:, lanes] * o.astype(F32)).astype(BF16)

        b_last = b_row[:, chunk - 1:chunk]
        g = b_last - b_row + ig_row
        m_new = jnp.maximum(b_last + m_prev, jnp.max(g, axis=-1, keepdims=True))
        decay = jnp.exp(b_last + m_prev - m_new)
        wv = (jnp.exp(g - m_new) * v_ext.astype(F32)).astype(BF16)
        cnt_ref[h] = decay * cnt + _dot(wv, k)
        ms_ref[h] = jnp.broadcast_to(m_new, (1, LANES))

    @pl.when(step == pl.num_programs(1) - 1)
    def _():
        for h in range(M_HEADS):
            c_ref[0, h] = cnt_ref[h][:M_DV, :].T
            n_ref[0, h] = cnt_ref[h][M_DV:M_DV + 1, :]
            m_ref[0, h] = ms_ref[h]


def _mlstm_prompt(zm, gates, gain, batch):
    n = zm.shape[0]
    chunk = min(MLSTM_CHUNK, n // batch)
    nc = n // batch // chunk

    def rows(width):
        return pl.BlockSpec((chunk, width), lambda b, c: (b * nc + c, 0))

    def state(*dims):
        return pl.BlockSpec((1, M_HEADS) + dims, lambda b, c: (b, 0, 0, 0))

    return pl.pallas_call(
        _mlstm_prompt_kernel,
        grid=(batch, nc),
        in_specs=[rows(ZM_WIDTH), rows(LANES), pl.BlockSpec((1, M_WIDTH), lambda b, c: (0, 0))],
        out_specs=[rows(M_WIDTH), state(M_DK, M_DV), state(1, M_DK), state(1, LANES)],
        out_shape=[jax.ShapeDtypeStruct((n, M_WIDTH), BF16),
                   jax.ShapeDtypeStruct((batch, M_HEADS, M_DK, M_DV), F32),
                   jax.ShapeDtypeStruct((batch, M_HEADS, 1, M_DK), F32),
                   jax.ShapeDtypeStruct((batch, M_HEADS, 1, LANES), F32)],
        scratch_shapes=[pltpu.VMEM((M_HEADS, 2 * LANES, M_DK), F32), pltpu.VMEM((M_HEADS, 1, LANES), F32)],
        compiler_params=pltpu.CompilerParams(dimension_semantics=("parallel", "arbitrary"),
                                             vmem_limit_bytes=VMEM_LIMIT),
        name="mlstm_prompt",
    )(zm, gates, gain)


def _alibi_slope(head):
    return 2.0 ** (-8.0 * (head + 1) / A_HEADS)


def _swa_prompt_kernel(sinks_ref, qa_ref, kvc_ref, kvp_ref, ua_ref):
    blk = pl.program_id(1)
    kvp = kvp_ref[...]
    kvc = kvc_ref[...]
    hw = A_KV_HEADS * A_HD
    kk = jnp.concatenate([kvp[:, :hw], kvc[:, :hw]], axis=0).astype(BF16)
    vv = jnp.concatenate([kvp[:, hw:], kvc[:, hw:]], axis=0).astype(BF16)
    qi = lax.broadcasted_iota(jnp.int32, (WINDOW, 2 * WINDOW), 0)
    kj = lax.broadcasted_iota(jnp.int32, (WINDOW, 2 * WINDOW), 1)
    dist = WINDOW + qi - kj
    valid = (dist >= 0) & (dist < WINDOW) & ((kj >= WINDOW) | (blk > 0))
    distf = dist.astype(F32)
    low = lax.broadcasted_iota(jnp.int32, (WINDOW, LANES), 1) < A_HD

    for p in range(A_GROUP):
        qp = qa_ref[:, p * LANES:(p + 1) * LANES]
        outs = []
        for kvh in range(A_KV_HEADS):
            head = kvh * A_GROUP + p
            keep = low if kvh == 0 else jnp.logical_not(low)
            s = _dot_nt(jnp.where(keep, qp, jnp.zeros_like(qp)), kk)
            s = jnp.where(valid, s - _alibi_slope(head) * distf, -jnp.inf)
            sink = sinks_ref[head]
            mx = jnp.maximum(jnp.max(s, axis=-1, keepdims=True), sink)
            e = jnp.exp(s - mx)
            den = jnp.sum(e, axis=-1, keepdims=True) + jnp.exp(sink - mx)
            outs.append(_dot(e.astype(BF16), vv) / den)
        ua_ref[:, p * LANES:(p + 1) * LANES] = jnp.where(low, outs[0], outs[1]).astype(BF16)


def _swa_prompt(sinks, qa, kva, batch):
    n = qa.shape[0]
    nb = n // batch // WINDOW
    return pl.pallas_call(
        _swa_prompt_kernel,
        grid=(batch, nb),
        in_specs=[pl.BlockSpec(memory_space=pltpu.SMEM),
                  pl.BlockSpec((WINDOW, A_WIDTH), lambda b, i: (b * nb + i, 0)),
                  pl.BlockSpec((WINDOW, KV_WIDTH), lambda b, i: (b * nb + i, 0)),
                  pl.BlockSpec((WINDOW, KV_WIDTH), lambda b, i: (b * nb + jnp.maximum(i - 1, 0), 0))],
        out_specs=pl.BlockSpec((WINDOW, A_WIDTH), lambda b, i: (b * nb + i, 0)),
        out_shape=jax.ShapeDtypeStruct((n, A_WIDTH), BF16),
        compiler_params=pltpu.CompilerParams(dimension_semantics=("parallel", "parallel"),
                                             vmem_limit_bytes=VMEM_LIMIT),
        name="swa_prompt",
    )(sinks, qa, kva, kva)


SAMPLE_ROWS = 128


def _mlstm_sample_kernel(q_ref, k_ref, v_ref, o_ref, gates_ref, gain_ref, c_ref, n_ref, m_ref,
                         um_ref, co_ref, no_ref, mo_ref,
                         qf_ref, qc_ref, nexp_ref, wkk_ref, dec_ref, *, steps):
    rows = SAMPLE_ROWS
    nb = rows // steps
    head = pl.program_id(1)
    row = lax.broadcasted_iota(jnp.int32, (rows, rows), 0)
    col = lax.broadcasted_iota(jnp.int32, (rows, rows), 1)
    same = (row // steps) == (col // steps)
    causal = same & (col <= row)
    eye = row == col

    def to_row(x_col):
        return jnp.sum(jnp.where(eye, x_col, 0.0), axis=0, keepdims=True)

    def pick(mask, x_row, fill, reduce):
        return reduce(jnp.where(mask, x_row, fill), axis=-1, keepdims=True)

    gates = gates_ref[...]
    lane = lax.broadcasted_iota(jnp.int32, (rows, LANES), 1)
    ig_col = jnp.sum(jnp.where(lane == head, gates, 0.0), axis=-1, keepdims=True)
    fg_col = jnp.sum(jnp.where(lane == head + M_HEADS, gates, 0.0), axis=-1, keepdims=True)
    lf_row = to_row(_log_sigmoid(fg_col))
    b_col = pick(causal, lf_row, 0.0, jnp.sum)
    b_row = to_row(b_col)
    ig_row = to_row(ig_col)
    m_prev = m_ref[0]

    d = jnp.where(causal, b_col - b_row + ig_row, -jnp.inf)
    inter = b_col + m_prev
    m_t = jnp.maximum(inter, jnp.max(d, axis=-1, keepdims=True))
    w = jnp.exp(d - m_t)
    a = jnp.exp(inter - m_t)

    q = q_ref[...]
    k = k_ref[...]
    v = v_ref[...]
    qf_ref[...] = q.astype(F32)
    for t in range(steps):
        nexp_ref[pl.ds(t, nb, stride=steps), :] = n_ref[...]

    sub = lax.broadcasted_iota(jnp.int32, (2 * steps, M_DV), 0)

    def qc_pair(j, carry):
        q8 = qf_ref[pl.ds(pl.multiple_of(j * 2 * steps, 2 * steps), 2 * steps), :].astype(BF16)
        r0 = _dot(q8, c_ref[2 * j, 0].astype(BF16))
        r1 = _dot(q8, c_ref[2 * j + 1, 0].astype(BF16))
        qc_ref[pl.ds(pl.multiple_of(j * 2 * steps, 2 * steps), 2 * steps), :] = jnp.where(sub < steps, r0, r1)
        return carry

    lax.fori_loop(0, nb // 2, qc_pair, 0, unroll=4)

    s = _dot_nt(q, k) * w
    qn = jnp.sum(qf_ref[...] * nexp_ref[...], axis=-1, keepdims=True)
    num = a * qc_ref[...] + _dot(s.astype(BF16), v)
    den = a * qn + jnp.sum(s, axis=-1, keepdims=True)
    hh = num / jnp.maximum(jnp.abs(den), jnp.exp(-m_t))
    um_ref[...] = _head_norm_gate(hh, o_ref[...], gain_ref[...])

    last = same & ((col % steps) == steps - 1)
    b_last = pick(last, b_row, 0.0, jnp.sum)
    g_col = b_last - b_col + ig_col
    g_max = pick(same, to_row(g_col), -jnp.inf, jnp.max)
    m_new = jnp.maximum(b_last + m_prev, g_max)
    decay = jnp.exp(b_last + m_prev - m_new)
    wk = jnp.exp(g_col - m_new)
    mo_ref[0] = m_new

    kf = k.astype(F32)
    wkk_ref[...] = wk * kf
    dec_ref[...] = jnp.broadcast_to(decay, (rows, M_DV))
    n_sum = wkk_ref[pl.ds(0, nb, stride=steps), :]
    for t in range(1, steps):
        n_sum = n_sum + wkk_ref[pl.ds(t, nb, stride=steps), :]
    no_ref[...] = dec_ref[pl.ds(0, nb, stride=steps), :] * n_ref[...] + n_sum

    k_t = kf.T
    wv = (wk * v.astype(F32)).astype(BF16)
    tok_batch = lax.broadcasted_iota(jnp.int32, (M_DK, rows), 1) // steps

    def update(b, carry):
        upd = _dot(jnp.where(tok_batch == b, k_t, 0.0).astype(BF16), wv)
        co_ref[b, 0] = dec_ref[pl.ds(b * steps, 1), :] * c_ref[b, 0] + upd
        return carry

    lax.fori_loop(0, nb, update, 0, unroll=8)


def _mlstm_sample(zm, gates, gain, c, n2, m_rows, steps):
    rows = zm.shape[0]
    batch = c.shape[0]
    nb = SAMPLE_ROWS // steps
    groups = rows // SAMPLE_ROWS

    def zcol(section):
        return pl.BlockSpec((SAMPLE_ROWS, M_DK), lambda g, h: (g, section * M_HEADS + h))

    c_spec = pl.BlockSpec((nb, 1, M_DK, M_DV), lambda g, h: (g, h, 0, 0))
    n_spec = pl.BlockSpec((nb, M_DK), lambda g, h: (g, h))
    m_spec = pl.BlockSpec((1, SAMPLE_ROWS, 1), lambda g, h: (h, g, 0))
    return pl.pallas_call(
        functools.partial(_mlstm_sample_kernel, steps=steps),
        grid=(groups, M_HEADS),
        in_specs=[zcol(0), zcol(1), zcol(2), zcol(3),
                  pl.BlockSpec((SAMPLE_ROWS, LANES), lambda g, h: (g, 0)),
                  pl.BlockSpec((1, M_DV), lambda g, h: (0, h)),
                  c_spec, n_spec, m_spec],
        out_specs=[pl.BlockSpec((SAMPLE_ROWS, M_DV), lambda g, h: (g, h)), c_spec, n_spec, m_spec],
        out_shape=[jax.ShapeDtypeStruct((rows, M_WIDTH), BF16),
                   jax.ShapeDtypeStruct(c.shape, F32),
                   jax.ShapeDtypeStruct(n2.shape, F32),
                   jax.ShapeDtypeStruct(m_rows.shape, F32)],
        scratch_shapes=[pltpu.VMEM((SAMPLE_ROWS, M_DK), F32),
                        pltpu.VMEM((SAMPLE_ROWS, M_DV), F32),
                        pltpu.VMEM((SAMPLE_ROWS, M_DK), F32),
                        pltpu.VMEM((SAMPLE_ROWS, M_DK), F32),
                        pltpu.VMEM((SAMPLE_ROWS, M_DV), F32)],
        compiler_params=pltpu.CompilerParams(dimension_semantics=("parallel", "parallel"),
                                             vmem_limit_bytes=VMEM_LIMIT),
        name="mlstm_sample",
    )(zm, zm, zm, zm, gates, gain, c, n2, m_rows)


SWA_SAMPLE_BATCHES = 16


def _swa_sample_kernel(sinks_ref, qa_ref, kva_ref, ck_ref, cv_ref, ua_ref, nk_ref, nv_ref, qf_ref, uo_ref,
                       *, steps):
    pair_rows = 2 * steps
    n_pairs = SWA_SAMPLE_BATCHES // 2
    hw = A_KV_HEADS * A_HD
    srows = A_HEADS * pair_rows
    qf_ref[...] = qa_ref[...].astype(F32)

    r = lax.broadcasted_iota(jnp.int32, (srows, 1), 0)
    tok = r % pair_rows
    t_col = tok % steps
    row_batch = tok // steps
    blk = r // pair_rows
    head = (blk % A_KV_HEADS) * A_GROUP + blk // A_KV_HEADS
    slope = jnp.zeros((srows, 1), F32)
    sink = jnp.zeros((srows, 1), F32)
    for hd in range(A_HEADS):
        slope = jnp.where(head == hd, _alibi_slope(hd), slope)
        sink = jnp.where(head == hd, sinks_ref[hd], sink)
    row_low = (blk % A_KV_HEADS) == 0

    jj = lax.broadcasted_iota(jnp.int32, (srows, WINDOW), 1)
    dist_c = WINDOW + t_col - jj
    valid_c = dist_c < WINDOW
    cc = lax.broadcasted_iota(jnp.int32, (srows, pair_rows), 1)
    dist_n = t_col - cc % steps
    low = lax.broadcasted_iota(jnp.int32, (pair_rows, LANES), 1) < A_HD
    low_s = lax.broadcasted_iota(jnp.int32, (srows, LANES), 1) < A_HD
    sub = lax.broadcasted_iota(jnp.int32, (pair_rows, LANES), 0)

    def pair(j, carry):
        r0 = pl.multiple_of(j * pair_rows, pair_rows)
        q8 = qf_ref[pl.ds(r0, pair_rows), :]
        parts = []
        for p in range(A_GROUP):
            qp = q8[:, p * LANES:(p + 1) * LANES]
            parts.append(jnp.where(low, qp, 0.0))
            parts.append(jnp.where(low, 0.0, qp))
        qs = jnp.concatenate(parts, axis=0).astype(BF16)
        kv_new = kva_ref[pl.ds(r0, pair_rows), :]
        k_new = kv_new[:, :hw]
        v_new = kv_new[:, hw:]
        s_new = _dot_nt(qs, k_new.astype(BF16))
        out8 = [jnp.zeros((pair_rows, LANES), F32) for _ in range(A_GROUP)]
        for bb in range(2):
            b = 2 * j + bb
            kc = ck_ref[b]
            vc = cv_ref[b]
            s_c = jnp.where(valid_c, _dot_nt(qs, kc.astype(BF16)) - slope * dist_c.astype(F32), -jnp.inf)
            valid_n = (cc // steps == bb) & (dist_n >= 0)
            s_n = jnp.where(valid_n, s_new - slope * dist_n.astype(F32), -jnp.inf)
            mx = jnp.maximum(jnp.maximum(jnp.max(s_c, axis=-1, keepdims=True),
                                         jnp.max(s_n, axis=-1, keepdims=True)), sink)
            e_c = jnp.exp(s_c - mx)
            e_n = jnp.exp(s_n - mx)
            den = (jnp.sum(e_c, axis=-1, keepdims=True) + jnp.sum(e_n, axis=-1, keepdims=True)
                   + jnp.exp(sink - mx))
            pv = (_dot(e_c.astype(BF16), vc.astype(BF16)) + _dot(e_n.astype(BF16), v_new.astype(BF16))) / den
            pv = jnp.where(low_s == row_low, pv, 0.0)
            for p in range(A_GROUP):
                both = (pv[(2 * p) * pair_rows:(2 * p + 1) * pair_rows]
                        + pv[(2 * p + 1) * pair_rows:(2 * p + 2) * pair_rows])
                out8[p] = jnp.where(sub // steps == bb, both, out8[p])
            tail_k = k_new if bb == 1 else pltpu.roll(k_new, steps, 0)
            tail_v = v_new if bb == 1 else pltpu.roll(v_new, steps, 0)
            rk = pltpu.roll(kc, WINDOW - steps, 0)
            rv = pltpu.roll(vc, WINDOW - steps, 0)
            nk_ref[b] = rk
            nv_ref[b] = rv
            nk_ref[b, WINDOW - pair_rows:, :] = jnp.where(sub >= steps, tail_k, rk[WINDOW - pair_rows:])
            nv_ref[b, WINDOW - pair_rows:, :] = jnp.where(sub >= steps, tail_v, rv[WINDOW - pair_rows:])
        uo_ref[pl.ds(r0, pair_rows), :] = jnp.concatenate(out8, axis=-1)
        return carry

    lax.fori_loop(0, n_pairs, pair, 0, unroll=2)
    ua_ref[...] = uo_ref[...].astype(BF16)


def _swa_sample(sinks, qa, kva, ck, cv, steps):
    batch = ck.shape[0]
    nbt = SWA_SAMPLE_BATCHES
    rows = nbt * steps
    cache = pl.BlockSpec((nbt, WINDOW, LANES), lambda g: (g, 0, 0))
    return pl.pallas_call(
        functools.partial(_swa_sample_kernel, steps=steps),
        grid=(batch // nbt,),
        in_specs=[pl.BlockSpec(memory_space=pltpu.SMEM),
                  pl.BlockSpec((rows, A_WIDTH), lambda g: (g, 0)),
                  pl.BlockSpec((rows, KV_WIDTH), lambda g: (g, 0)),
                  cache, cache],
        out_specs=[pl.BlockSpec((rows, A_WIDTH), lambda g: (g, 0)), cache, cache],
        out_shape=[jax.ShapeDtypeStruct((batch * steps, A_WIDTH), BF16),
                   jax.ShapeDtypeStruct(ck.shape, F32),
                   jax.ShapeDtypeStruct(cv.shape, F32)],
        scratch_shapes=[pltpu.VMEM((rows, A_WIDTH), F32), pltpu.VMEM((rows, A_WIDTH), F32)],
        compiler_params=pltpu.CompilerParams(dimension_semantics=("parallel",),
                                             vmem_limit_bytes=VMEM_LIMIT),
        name="swa_sample",
    )(sinks, qa, kva, ck, cv)


def _prep_proj(w_in, b_gate):
    m_end = 4 * M_WIDTH
    g_end = m_end + 2 * M_HEADS
    wt = w_in.T
    qa = wt[g_end:g_end + A_WIDTH].reshape(A_KV_HEADS, A_GROUP, A_HD, D_MODEL).transpose(1, 0, 2, 3)
    gates = jnp.pad(wt[m_end:g_end], ((0, LANES - 2 * M_HEADS), (0, 0)))
    win = jnp.concatenate([wt[:m_end], qa.reshape(A_WIDTH, D_MODEL), wt[g_end + A_WIDTH:], gates],
                          axis=0).astype(BF16)
    bias = jnp.pad(b_gate, (0, LANES - 2 * M_HEADS)).reshape(1, LANES)
    return win, bias


def _prep_out(w_out):
    wa = w_out[M_WIDTH:].reshape(A_KV_HEADS, A_GROUP, A_HD, D_MODEL).transpose(1, 0, 2, 3)
    return jnp.concatenate([w_out[:M_WIDTH], wa.reshape(A_WIDTH, D_MODEL)], axis=0).astype(BF16)


def kernel(x_prompt, x_sample, cache_swa_k, cache_swa_v, state_mlstm_C, state_mlstm_n, state_mlstm_m,
           norm_gains, ffn_w_gate, ffn_w_up, ffn_w_down, w_in, b_gate, mlstm_norm_gain, attn_sinks, w_out):
    assert norm_gains.shape[0] == 1, "single layer"
    batch, seq, _ = x_prompt.shape
    dec_batch, steps, _ = x_sample.shape
    gains = norm_gains[0]
    ffn = tuple(w.astype(BF16) for w in (ffn_w_gate, ffn_w_up, ffn_w_down))
    win, bias = _prep_proj(w_in[0], b_gate[0])
    wo = _prep_out(w_out[0])
    gain = mlstm_norm_gain[0].reshape(1, M_WIDTH)
    sinks = attn_sinks[0]
    kv_lanes = A_KV_HEADS * A_HD

    xp, zm, qa, kva, gates = _ffn_proj(x_prompt.reshape(batch * seq, D_MODEL), gains, *ffn, win, bias, 0, 0, 1, 2)
    um, p_c, p_n, p_m = _mlstm_prompt(zm, gates, gain, batch)
    ua = _swa_prompt(sinks, qa, kva, batch)
    yp = _mix_ffn(xp, um, ua, wo, gains, *ffn, 1, 3, 4, 5).reshape(batch, seq, D_MODEL)
    keep = min(WINDOW, seq)
    kv_tail = kva.reshape(batch, seq, KV_WIDTH)[:, seq - keep:]
    p_k = kv_tail[..., :kv_lanes].reshape(1, batch, keep, A_KV_HEADS, A_HD)
    p_v = kv_tail[..., kv_lanes:].reshape(1, batch, keep, A_KV_HEADS, A_HD)
    p_c = p_c[None]
    p_n = p_n[None, :, :, 0, :]
    p_m = p_m[None, :, :, 0, 0]

    xs, zm_s, qa_s, kva_s, gates_s = _ffn_proj(x_sample.reshape(dec_batch * steps, D_MODEL), gains, *ffn,
                                               win, bias, 0, 0, 1, 2)
    m_rows = jnp.repeat(state_mlstm_m[0].T, steps, axis=1)[..., None]
    um_s, s_c, s_n2, s_m_rows = _mlstm_sample(zm_s, gates_s, gain, state_mlstm_C[0],
                                              state_mlstm_n[0].reshape(dec_batch, M_HEADS * M_DK),
                                              m_rows, steps)
    wc = cache_swa_k.shape[2]
    ua_s, s_k, s_v = _swa_sample(sinks, qa_s, kva_s, cache_swa_k[0].reshape(dec_batch, wc, kv_lanes),
                                 cache_swa_v[0].reshape(dec_batch, wc, kv_lanes), steps)
    ys = _mix_ffn(xs, um_s, ua_s, wo, gains, *ffn, 1, 3, 4, 5).reshape(dec_batch, steps, D_MODEL)
    s_n = s_n2.reshape(1, dec_batch, M_HEADS, M_DK)
    s_m = s_m_rows[:, ::steps, 0].T[None]
    s_k = s_k.reshape(1, dec_batch, wc, A_KV_HEADS, A_HD)
    s_v = s_v.reshape(1, dec_batch, wc, A_KV_HEADS, A_HD)
    return (yp, ys, p_k, p_v, p_c, p_n, p_m, s_k, s_v, s_c[None], s_n, s_m)
```

```python
import functools

import jax
import jax.numpy as jnp
from jax import lax
from jax.experimental import pallas as pl
from jax.experimental.pallas import tpu as pltpu

F32 = jnp.float32
BF16 = jnp.bfloat16

D_MODEL = 1024
D_FF = 2816
M_HEADS = 4
M_DK = 128
M_DV = 128
M_WIDTH = M_HEADS * M_DV
A_HEADS = 8
A_KV_HEADS = 2
A_GROUP = A_HEADS // A_KV_HEADS
A_HD = 64
A_WIDTH = A_HEADS * A_HD
WINDOW = 128
RMS_EPS = 1e-6

LANES = 128
FFN_CHUNK = 256
N_FFN_CHUNKS = D_FF // FFN_CHUNK
TOKEN_TILE = 512
MLSTM_CHUNK = 256
ZM_WIDTH = 4 * M_WIDTH
KV_WIDTH = 2 * A_KV_HEADS * A_HD
VMEM_LIMIT = 56 * 1024 * 1024
MLSTM_CHUNKS_PER_STEP = 4
GATE_I, GATE_F, GATE_B, GATE_X = 0, M_HEADS, 2 * M_HEADS, 3 * M_HEADS

NT_DIMS = (((1,), (1,)), ((), ()))


def _rms(x, g):
    ms = jnp.mean(x * x, axis=-1, keepdims=True)
    return x * lax.rsqrt(ms + RMS_EPS) * g


def _log_sigmoid(x):
    return jnp.minimum(x, 0.0) - jnp.log1p(jnp.exp(-jnp.abs(x)))


def _dot(a, b):
    return jnp.dot(a, b, preferred_element_type=F32)


def _dot_nt(a, b):
    return lax.dot_general(a, b, NT_DIMS, preferred_element_type=F32)


def _resident(shape):
    nd = len(shape)
    return pl.BlockSpec(shape, lambda *_: (0,) * nd, pipeline_mode=pl.Buffered(1))


def _resident_ffn_weight(w, j):
    return pl.BlockSpec((None, None) + w.shape[2:], lambda *_: (0, j, 0, 0), pipeline_mode=pl.Buffered(1))


def _swiglu_residual(x, gpre, gpost, wg_ref, wu_ref, wd_ref, h_ref, acc_ref):
    h_ref[...] = _rms(x, gpre).astype(BF16)
    for f in range(N_FFN_CHUNKS):
        cols = slice(f * FFN_CHUNK, (f + 1) * FFN_CHUNK)
        h = h_ref[...]
        g = _dot(h, wg_ref[:, cols])
        u = _dot(h, wu_ref[:, cols])
        part = _dot((g * jax.nn.sigmoid(g) * u).astype(BF16), wd_ref[cols, :])
        if f == 0:
            acc_ref[...] = part
        else:
            acc_ref[...] += part
    return x + _rms(acc_ref[...], 0.5 * gpost)


def _project(x, gain, win_ref, bias_ref, h_ref, zm_ref, qa_ref, kva_ref, gates_ref, chunk):
    h_ref[...] = _rms(x, gain).astype(BF16)

    def cols(lo, width):
        return _dot_nt(h_ref[...], win_ref[lo:lo + width, :])

    gates = cols(ZM_WIDTH + A_WIDTH + KV_WIDTH, LANES) + bias_ref[...]
    lane = lax.broadcasted_iota(jnp.int32, (chunk, LANES), 1)
    pos = lax.broadcasted_iota(jnp.int32, (chunk, LANES), 0)
    for c0 in range(0, gates.shape[0], chunk):
        g = gates[c0:c0 + chunk]
        b = jnp.where((lane >= GATE_F) & (lane < GATE_B), _log_sigmoid(g), 0.0)
        shift = 1
        while shift < chunk:
            b = b + jnp.where(pos >= shift, pltpu.roll(b, shift, 0), 0.0)
            shift *= 2
        x_gate = pltpu.roll(g, GATE_X - GATE_I, 1) - pltpu.roll(b, GATE_X - GATE_F, 1)
        out = jnp.where(lane < GATE_B, g, jnp.where(lane < GATE_X, pltpu.roll(b, GATE_B - GATE_F, 1), x_gate))
        gates_ref[c0:c0 + chunk, :] = jnp.where(lane < GATE_X + M_HEADS, out, 0.0)

    zm_ref[:, 0:M_WIDTH] = cols(0, M_WIDTH).astype(BF16)
    zm_ref[:, M_WIDTH:2 * M_WIDTH] = (cols(M_WIDTH, M_WIDTH) * (M_DK ** -0.5)).astype(BF16)
    zm_ref[:, 2 * M_WIDTH:3 * M_WIDTH] = cols(2 * M_WIDTH, M_WIDTH).astype(BF16)
    zm_ref[:, 3 * M_WIDTH:] = jax.nn.sigmoid(cols(3 * M_WIDTH, M_WIDTH)).astype(BF16)
    qa_ref[...] = (cols(ZM_WIDTH, A_WIDTH) * (A_HD ** -0.5)).astype(BF16)
    kva_ref[...] = cols(ZM_WIDTH + A_WIDTH, KV_WIDTH)


def _ffn_proj_kernel(x_ref, gains_ref, wg_ref, wu_ref, wd_ref, win_ref, bias_ref,
                     x1_ref, zm_ref, qa_ref, kva_ref, gates_ref, h_ref, acc_ref, *, pre, post, mixer, chunk):
    x1 = _swiglu_residual(x_ref[...], gains_ref[pre:pre + 1, :], gains_ref[post:post + 1, :],
                          wg_ref, wu_ref, wd_ref, h_ref, acc_ref)
    x1_ref[...] = x1
    _project(x1, gains_ref[mixer:mixer + 1, :], win_ref, bias_ref, h_ref, zm_ref, qa_ref, kva_ref, gates_ref,
             chunk)


def _mix_ffn_kernel(x_ref, um_ref, ua_ref, wo_ref, gains_ref, wg_ref, wu_ref, wd_ref, o_ref, h_ref, acc_ref,
                    *, mix, pre, post):
    u = jnp.concatenate([um_ref[...], ua_ref[...]], axis=-1)
    x = x_ref[...] + _rms(_dot(u, wo_ref[...]), gains_ref[mix:mix + 1, :])
    o_ref[...] = _swiglu_residual(x, gains_ref[pre:pre + 1, :], gains_ref[post:post + 1, :],
                                  wg_ref, wu_ref, wd_ref, h_ref, acc_ref)


def _ffn_scratch(tm):
    return [pltpu.VMEM((tm, D_MODEL), BF16), pltpu.VMEM((tm, D_MODEL), F32)]


def _dense_params():
    return pltpu.CompilerParams(dimension_semantics=("parallel",), vmem_limit_bytes=VMEM_LIMIT)


def _ffn_proj(x, gains, wg, wu, wd, win, bias, j, pre, post, mixer, chunk):
    n = x.shape[0]
    tm = min(TOKEN_TILE, n)
    assert tm % chunk == 0

    def rows(width):
        return pl.BlockSpec((tm, width), lambda i: (i, 0))

    return pl.pallas_call(
        functools.partial(_ffn_proj_kernel, pre=pre, post=post, mixer=mixer, chunk=chunk),
        grid=(n // tm,),
        in_specs=[rows(D_MODEL), _resident(gains.shape)] + [_resident_ffn_weight(w, j) for w in (wg, wu, wd)]
        + [_resident(win.shape), _resident(bias.shape)],
        out_specs=[rows(D_MODEL), rows(ZM_WIDTH), rows(A_WIDTH), rows(KV_WIDTH), rows(LANES)],
        out_shape=[jax.ShapeDtypeStruct((n, D_MODEL), F32),
                   jax.ShapeDtypeStruct((n, ZM_WIDTH), BF16),
                   jax.ShapeDtypeStruct((n, A_WIDTH), BF16),
                   jax.ShapeDtypeStruct((n, KV_WIDTH), F32),
                   jax.ShapeDtypeStruct((n, LANES), F32)],
        scratch_shapes=_ffn_scratch(tm),
        compiler_params=_dense_params(),
        name="ffn_half_step_input_projection",
    )(x, gains, wg, wu, wd, win, bias)


def _mix_ffn(x, um, ua, wo, gains, wg, wu, wd, j, mix, pre, post):
    n = x.shape[0]
    tm = min(TOKEN_TILE, n)
    row = pl.BlockSpec((tm, D_MODEL), lambda i: (i, 0))
    half = pl.BlockSpec((tm, M_WIDTH), lambda i: (i, 0))
    return pl.pallas_call(
        functools.partial(_mix_ffn_kernel, mix=mix, pre=pre, post=post),
        grid=(n // tm,),
        in_specs=[row, half, half, _resident(wo.shape), _resident(gains.shape)]
        + [_resident_ffn_weight(w, j) for w in (wg, wu, wd)],
        out_specs=row,
        out_shape=jax.ShapeDtypeStruct((n, D_MODEL), F32),
        scratch_shapes=_ffn_scratch(tm),
        compiler_params=_dense_params(),
        name="out_proj_ffn_half_step",
    )(x, um, ua, wo, gains, wg, wu, wd)


def _head_norm_gate(hh, o, gain):
    hn = hh * lax.rsqrt(jnp.mean(hh * hh, axis=-1, keepdims=True) + RMS_EPS) * gain
    return (o.astype(F32) * hn).astype(BF16)


def _mlstm_prompt_kernel(zm_ref, gates_ref, gain_ref, um_ref, c_ref, n_ref, m_ref, cnt_ref, ms_ref, *, chunk):
    step = pl.program_id(1)

    @pl.when(step == 0)
    def _():
        cnt_ref[...] = jnp.zeros_like(cnt_ref)
        ms_ref[...] = jnp.zeros_like(ms_ref)

    si = lax.broadcasted_iota(jnp.int32, (chunk, chunk), 0)
    ti = lax.broadcasted_iota(jnp.int32, (chunk, chunk), 1)
    causal = si <= ti
    ones_rows = jnp.where(lax.broadcasted_iota(jnp.int32, (LANES, chunk), 0) == 0, 1.0, 0.0).astype(BF16)

    for c in range(zm_ref.shape[0] // chunk):
        rows = slice(c * chunk, (c + 1) * chunk)
        gates = gates_ref[rows, :]
        gates_t = gates.T
        for h in range(M_HEADS):
            lanes = slice(h * M_DK, (h + 1) * M_DK)
            q = zm_ref[rows, lanes]
            k = zm_ref[rows, M_WIDTH + h * M_DK:M_WIDTH + (h + 1) * M_DK]
            v = zm_ref[rows, 2 * M_WIDTH + h * M_DV:2 * M_WIDTH + (h + 1) * M_DV]
            o = zm_ref[rows, 3 * M_WIDTH + h * M_DV:3 * M_WIDTH + (h + 1) * M_DV]
            x_col = gates[:, GATE_X + h:GATE_X + h + 1]
            ig_row = gates_t[GATE_I + h:GATE_I + h + 1, :]
            b_row = gates_t[GATE_B + h:GATE_B + h + 1, :]
            m_prev = ms_ref[h][:, 0:1]
            cnt = cnt_ref[h]

            xm = jnp.where(causal, x_col, -jnp.inf)
            mu = jnp.maximum(m_prev, jnp.max(xm, axis=0, keepdims=True))
            a = jnp.exp(m_prev - mu)
            s_t = _dot_nt(k, q) * jnp.exp(xm - mu)
            v_ext = jnp.concatenate([v.astype(F32).T.astype(BF16), ones_rows], axis=0)
            nd = a * _dot_nt(cnt.astype(BF16), q) + _dot(v_ext, s_t.astype(BF16))
            den = nd[M_DV:M_DV + 1, :]
            hh = nd[:M_DV, :] * (1.0 / jnp.maximum(jnp.abs(den), jnp.exp(-(b_row + mu))))
            hn = hh * lax.rsqrt(jnp.mean(hh * hh, axis=0, keepdims=True) + RMS_EPS)
            um_ref[rows, lanes] = (hn.T * gain_ref[:, lanes] * o.astype(F32)).astype(BF16)

            b_last = b_row[:, chunk - 1:chunk]
            g = b_last - b_row + ig_row
            m_new = jnp.maximum(b_last + m_prev, jnp.max(g, axis=-1, keepdims=True))
            decay = jnp.exp(b_last + m_prev - m_new)
            wv = (jnp.exp(g - m_new) * v_ext.astype(F32)).astype(BF16)
            cnt_ref[h] = decay * cnt + _dot(wv, k)
            ms_ref[h] = jnp.broadcast_to(m_new, (1, LANES))

    @pl.when(step == pl.num_programs(1) - 1)
    def _():
        for h in range(M_HEADS):
            c_ref[0, h] = cnt_ref[h][:M_DV, :].T
            n_ref[0, h] = cnt_ref[h][M_DV:M_DV + 1, :]
            m_ref[0, h] = ms_ref[h]


def _mlstm_prompt(zm, gates, gain, batch, chunk):
    n = zm.shape[0]
    per_step = min(MLSTM_CHUNKS_PER_STEP, n // batch // chunk)
    step_rows = per_step * chunk
    ns = n // batch // step_rows

    def rows(width):
        return pl.BlockSpec((step_rows, width), lambda b, c: (b * ns + c, 0))

    def state(*dims):
        return pl.BlockSpec((1, M_HEADS) + dims, lambda b, c: (b, 0, 0, 0))

    return pl.pallas_call(
        functools.partial(_mlstm_prompt_kernel, chunk=chunk),
        grid=(batch, ns),
        in_specs=[rows(ZM_WIDTH), rows(LANES), pl.BlockSpec((1, M_WIDTH), lambda b, c: (0, 0))],
        out_specs=[rows(M_WIDTH), state(M_DK, M_DV), state(1, M_DK), state(1, LANES)],
        out_shape=[jax.ShapeDtypeStruct((n, M_WIDTH), BF16),
                   jax.ShapeDtypeStruct((batch, M_HEADS, M_DK, M_DV), F32),
                   jax.ShapeDtypeStruct((batch, M_HEADS, 1, M_DK), F32),
                   jax.ShapeDtypeStruct((batch, M_HEADS, 1, LANES), F32)],
        scratch_shapes=[pltpu.VMEM((M_HEADS, 2 * LANES, M_DK), F32), pltpu.VMEM((M_HEADS, 1, LANES), F32)],
        compiler_params=pltpu.CompilerParams(dimension_semantics=("parallel", "arbitrary"),
                                             vmem_limit_bytes=VMEM_LIMIT),
        name="mlstm_prompt",
    )(zm, gates, gain)


SWA_BLOCKS_PER_STEP = 4


def _alibi_slope(head):
    return 2.0 ** (-8.0 * (head + 1) / A_HEADS)


def _swa_prompt_kernel(sinks_ref, qa_ref, kvc_ref, kvp_ref, ua_ref):
    hw = A_KV_HEADS * A_HD
    n_blocks = qa_ref.shape[0] // WINDOW
    qi = lax.broadcasted_iota(jnp.int32, (WINDOW, 2 * WINDOW), 0)
    kj = lax.broadcasted_iota(jnp.int32, (WINDOW, 2 * WINDOW), 1)
    dist = WINDOW + qi - kj
    band = (dist >= 0) & (dist < WINDOW)
    distf = dist.astype(F32)
    low = lax.broadcasted_iota(jnp.int32, (WINDOW, LANES), 1) < A_HD
    first = pl.program_id(1) == 0

    for j in range(n_blocks):
        rows = slice(j * WINDOW, (j + 1) * WINDOW)
        kvc = kvc_ref[rows, :]
        kvp = kvp_ref[...] if j == 0 else kvc_ref[(j - 1) * WINDOW:j * WINDOW, :]
        valid = band & ((kj >= WINDOW) | jnp.logical_not(first)) if j == 0 else band
        kk = jnp.concatenate([kvp[:, :hw], kvc[:, :hw]], axis=0).astype(BF16)
        vv = jnp.concatenate([kvp[:, hw:], kvc[:, hw:]], axis=0).astype(BF16)
        for p in range(A_GROUP):
            qp = qa_ref[rows, p * LANES:(p + 1) * LANES]
            outs = []
            for kvh in range(A_KV_HEADS):
                head = kvh * A_GROUP + p
                keep = low if kvh == 0 else jnp.logical_not(low)
                s = _dot_nt(jnp.where(keep, qp, jnp.zeros_like(qp)), kk)
                s = jnp.where(valid, s - _alibi_slope(head) * distf, -jnp.inf)
                sink = sinks_ref[head]
                mx = jnp.maximum(jnp.max(s, axis=-1, keepdims=True), sink)
                e = jnp.exp(s - mx)
                den = jnp.sum(e, axis=-1, keepdims=True) + jnp.exp(sink - mx)
                outs.append(_dot(e.astype(BF16), vv) / den)
            ua_ref[rows, p * LANES:(p + 1) * LANES] = jnp.where(low, outs[0], outs[1]).astype(BF16)


def _swa_prompt(sinks, qa, kva, batch):
    n = qa.shape[0]
    per_step = min(SWA_BLOCKS_PER_STEP, n // batch // WINDOW)
    rows = per_step * WINDOW
    ns = n // batch // rows

    def step_rows(width):
        return pl.BlockSpec((rows, width), lambda b, i: (b * ns + i, 0))

    def prev_block(b, i):
        return (jnp.maximum((b * ns + i) * per_step - 1, 0), 0)

    return pl.pallas_call(
        _swa_prompt_kernel,
        grid=(batch, ns),
        in_specs=[pl.BlockSpec(memory_space=pltpu.SMEM), step_rows(A_WIDTH), step_rows(KV_WIDTH),
                  pl.BlockSpec((WINDOW, KV_WIDTH), prev_block)],
        out_specs=step_rows(A_WIDTH),
        out_shape=jax.ShapeDtypeStruct((n, A_WIDTH), BF16),
        compiler_params=pltpu.CompilerParams(dimension_semantics=("parallel", "parallel"),
                                             vmem_limit_bytes=VMEM_LIMIT),
        name="swa_prompt",
    )(sinks, qa, kva, kva)


SAMPLE_ROWS = 128


def _mlstm_sample_kernel(q_ref, k_ref, v_ref, o_ref, gates_ref, gain_ref, c_ref, n_ref, m_ref,
                         um_ref, co_ref, no_ref, mo_ref,
                         qf_ref, qc_ref, nexp_ref, wkk_ref, dec_ref, *, steps):
    rows = SAMPLE_ROWS
    nb = rows // steps
    head = pl.program_id(1)
    row = lax.broadcasted_iota(jnp.int32, (rows, rows), 0)
    col = lax.broadcasted_iota(jnp.int32, (rows, rows), 1)
    same = (row // steps) == (col // steps)
    causal = same & (col <= row)
    eye = row == col

    def to_row(x_col):
        return jnp.sum(jnp.where(eye, x_col, 0.0), axis=0, keepdims=True)

    def pick(mask, x_row, fill, reduce):
        return reduce(jnp.where(mask, x_row, fill), axis=-1, keepdims=True)

    gates = gates_ref[...]
    lane = lax.broadcasted_iota(jnp.int32, (rows, LANES), 1)
    ig_col = jnp.sum(jnp.where(lane == head, gates, 0.0), axis=-1, keepdims=True)
    fg_col = jnp.sum(jnp.where(lane == head + M_HEADS, gates, 0.0), axis=-1, keepdims=True)
    lf_row = to_row(_log_sigmoid(fg_col))
    b_col = pick(causal, lf_row, 0.0, jnp.sum)
    b_row = to_row(b_col)
    ig_row = to_row(ig_col)
    m_prev = m_ref[0]

    d = jnp.where(causal, b_col - b_row + ig_row, -jnp.inf)
    inter = b_col + m_prev
    m_t = jnp.maximum(inter, jnp.max(d, axis=-1, keepdims=True))
    w = jnp.exp(d - m_t)
    a = jnp.exp(inter - m_t)

    q = q_ref[...]
    k = k_ref[...]
    v = v_ref[...]
    qf_ref[...] = q.astype(F32)
    for t in range(steps):
        nexp_ref[pl.ds(t, nb, stride=steps), :] = n_ref[...]

    sub = lax.broadcasted_iota(jnp.int32, (2 * steps, M_DV), 0)

    def qc_pair(j, carry):
        q8 = qf_ref[pl.ds(pl.multiple_of(j * 2 * steps, 2 * steps), 2 * steps), :].astype(BF16)
        r0 = _dot(q8, c_ref[2 * j, 0].astype(BF16))
        r1 = _dot(q8, c_ref[2 * j + 1, 0].astype(BF16))
        qc_ref[pl.ds(pl.multiple_of(j * 2 * steps, 2 * steps), 2 * steps), :] = jnp.where(sub < steps, r0, r1)
        return carry

    lax.fori_loop(0, nb // 2, qc_pair, 0, unroll=4)

    s = _dot_nt(q, k) * w
    qn = jnp.sum(qf_ref[...] * nexp_ref[...], axis=-1, keepdims=True)
    num = a * qc_ref[...] + _dot(s.astype(BF16), v)
    den = a * qn + jnp.sum(s, axis=-1, keepdims=True)
    hh = num / jnp.maximum(jnp.abs(den), jnp.exp(-m_t))
    um_ref[...] = _head_norm_gate(hh, o_ref[...], gain_ref[...])

    last = same & ((col % steps) == steps - 1)
    b_last = pick(last, b_row, 0.0, jnp.sum)
    g_col = b_last - b_col + ig_col
    g_max = pick(same, to_row(g_col), -jnp.inf, jnp.max)
    m_new = jnp.maximum(b_last + m_prev, g_max)
    decay = jnp.exp(b_last + m_prev - m_new)
    wk = jnp.exp(g_col - m_new)
    mo_ref[0] = m_new

    kf = k.astype(F32)
    wkk_ref[...] = wk * kf
    dec_ref[...] = jnp.broadcast_to(decay, (rows, M_DV))
    n_sum = wkk_ref[pl.ds(0, nb, stride=steps), :]
    for t in range(1, steps):
        n_sum = n_sum + wkk_ref[pl.ds(t, nb, stride=steps), :]
    no_ref[...] = dec_ref[pl.ds(0, nb, stride=steps), :] * n_ref[...] + n_sum

    k_t = kf.T
    wv = (wk * v.astype(F32)).astype(BF16)
    tok_batch = lax.broadcasted_iota(jnp.int32, (M_DK, rows), 1) // steps

    def update(b, carry):
        upd = _dot(jnp.where(tok_batch == b, k_t, 0.0).astype(BF16), wv)
        co_ref[b, 0] = dec_ref[pl.ds(b * steps, 1), :] * c_ref[b, 0] + upd
        return carry

    lax.fori_loop(0, nb, update, 0, unroll=8)


def _mlstm_sample(zm, gates, gain, c, n2, m_rows, steps):
    rows = zm.shape[0]
    batch = c.shape[0]
    nb = SAMPLE_ROWS // steps
    groups = rows // SAMPLE_ROWS

    def zcol(section):
        return pl.BlockSpec((SAMPLE_ROWS, M_DK), lambda g, h: (g, section * M_HEADS + h))

    c_spec = pl.BlockSpec((nb, 1, M_DK, M_DV), lambda g, h: (g, h, 0, 0))
    n_spec = pl.BlockSpec((nb, M_DK), lambda g, h: (g, h))
    m_spec = pl.BlockSpec((1, SAMPLE_ROWS, 1), lambda g, h: (h, g, 0))
    return pl.pallas_call(
        functools.partial(_mlstm_sample_kernel, steps=steps),
        grid=(groups, M_HEADS),
        in_specs=[zcol(0), zcol(1), zcol(2), zcol(3),
                  pl.BlockSpec((SAMPLE_ROWS, LANES), lambda g, h: (g, 0)),
                  pl.BlockSpec((1, M_DV), lambda g, h: (0, h)),
                  c_spec, n_spec, m_spec],
        out_specs=[pl.BlockSpec((SAMPLE_ROWS, M_DV), lambda g, h: (g, h)), c_spec, n_spec, m_spec],
        out_shape=[jax.ShapeDtypeStruct((rows, M_WIDTH), BF16),
                   jax.ShapeDtypeStruct(c.shape, F32),
                   jax.ShapeDtypeStruct(n2.shape, F32),
                   jax.ShapeDtypeStruct(m_rows.shape, F32)],
        scratch_shapes=[pltpu.VMEM((SAMPLE_ROWS, M_DK), F32),
                        pltpu.VMEM((SAMPLE_ROWS, M_DV), F32),
                        pltpu.VMEM((SAMPLE_ROWS, M_DK), F32),
                        pltpu.VMEM((SAMPLE_ROWS, M_DK), F32),
                        pltpu.VMEM((SAMPLE_ROWS, M_DV), F32)],
        compiler_params=pltpu.CompilerParams(dimension_semantics=("parallel", "parallel"),
                                             vmem_limit_bytes=VMEM_LIMIT),
        name="mlstm_sample",
    )(zm, zm, zm, zm, gates, gain, c, n2, m_rows)


SWA_SAMPLE_BATCHES = 16


def _swa_sample_kernel(sinks_ref, qa_ref, kva_ref, ck_ref, cv_ref, ua_ref, nk_ref, nv_ref, qf_ref, uo_ref,
                       *, steps):
    pair_rows = 2 * steps
    n_pairs = SWA_SAMPLE_BATCHES // 2
    hw = A_KV_HEADS * A_HD
    srows = A_HEADS * pair_rows
    qf_ref[...] = qa_ref[...].astype(F32)

    r = lax.broadcasted_iota(jnp.int32, (srows, 1), 0)
    tok = r % pair_rows
    t_col = tok % steps
    row_batch = tok // steps
    blk = r // pair_rows
    head = (blk % A_KV_HEADS) * A_GROUP + blk // A_KV_HEADS
    slope = jnp.zeros((srows, 1), F32)
    sink = jnp.zeros((srows, 1), F32)
    for hd in range(A_HEADS):
        slope = jnp.where(head == hd, _alibi_slope(hd), slope)
        sink = jnp.where(head == hd, sinks_ref[hd], sink)
    row_low = (blk % A_KV_HEADS) == 0

    jj = lax.broadcasted_iota(jnp.int32, (srows, WINDOW), 1)
    dist_c = WINDOW + t_col - jj
    valid_c = dist_c < WINDOW
    cc = lax.broadcasted_iota(jnp.int32, (srows, pair_rows), 1)
    dist_n = t_col - cc % steps
    low = lax.broadcasted_iota(jnp.int32, (pair_rows, LANES), 1) < A_HD
    low_s = lax.broadcasted_iota(jnp.int32, (srows, LANES), 1) < A_HD
    sub = lax.broadcasted_iota(jnp.int32, (pair_rows, LANES), 0)

    def pair(j, carry):
        r0 = pl.multiple_of(j * pair_rows, pair_rows)
        q8 = qf_ref[pl.ds(r0, pair_rows), :]
        parts = []
        for p in range(A_GROUP):
            qp = q8[:, p * LANES:(p + 1) * LANES]
            parts.append(jnp.where(low, qp, 0.0))
            parts.append(jnp.where(low, 0.0, qp))
        qs = jnp.concatenate(parts, axis=0).astype(BF16)
        kv_new = kva_ref[pl.ds(r0, pair_rows), :]
        k_new = kv_new[:, :hw]
        v_new = kv_new[:, hw:]
        s_new = _dot_nt(qs, k_new.astype(BF16))
        out8 = [jnp.zeros((pair_rows, LANES), F32) for _ in range(A_GROUP)]
        for bb in range(2):
            b = 2 * j + bb
            kc = ck_ref[b]
            vc = cv_ref[b]
            s_c = jnp.where(valid_c, _dot_nt(qs, kc.astype(BF16)) - slope * dist_c.astype(F32), -jnp.inf)
            valid_n = (cc // steps == bb) & (dist_n >= 0)
            s_n = jnp.where(valid_n, s_new - slope * dist_n.astype(F32), -jnp.inf)
            mx = jnp.maximum(jnp.maximum(jnp.max(s_c, axis=-1, keepdims=True),
                                         jnp.max(s_n, axis=-1, keepdims=True)), sink)
            e_c = jnp.exp(s_c - mx)
            e_n = jnp.exp(s_n - mx)
            den = (jnp.sum(e_c, axis=-1, keepdims=True) + jnp.sum(e_n, axis=-1, keepdims=True)
                   + jnp.exp(sink - mx))
            pv = (_dot(e_c.astype(BF16), vc.astype(BF16)) + _dot(e_n.astype(BF16), v_new.astype(BF16))) / den
            pv = jnp.where(low_s == row_low, pv, 0.0)
            for p in range(A_GROUP):
                both = (pv[(2 * p) * pair_rows:(2 * p + 1) * pair_rows]
                        + pv[(2 * p + 1) * pair_rows:(2 * p + 2) * pair_rows])
                out8[p] = jnp.where(sub // steps == bb, both, out8[p])
            tail_k = k_new if bb == 1 else pltpu.roll(k_new, steps, 0)
            tail_v = v_new if bb == 1 else pltpu.roll(v_new, steps, 0)
            rk = pltpu.roll(kc, WINDOW - steps, 0)
            rv = pltpu.roll(vc, WINDOW - steps, 0)
            nk_ref[b] = rk
            nv_ref[b] = rv
            nk_ref[b, WINDOW - pair_rows:, :] = jnp.where(sub >= steps, tail_k, rk[WINDOW - pair_rows:])
            nv_ref[b, WINDOW - pair_rows:, :] = jnp.where(sub >= steps, tail_v, rv[WINDOW - pair_rows:])
        uo_ref[pl.ds(r0, pair_rows), :] = jnp.concatenate(out8, axis=-1)
        return carry

    lax.fori_loop(0, n_pairs, pair, 0, unroll=2)
    ua_ref[...] = uo_ref[...].astype(BF16)


def _swa_sample(sinks, qa, kva, ck, cv, steps):
    batch = ck.shape[0]
    nbt = SWA_SAMPLE_BATCHES
    rows = nbt * steps
    cache = pl.BlockSpec((nbt, WINDOW, LANES), lambda g: (g, 0, 0))
    return pl.pallas_call(
        functools.partial(_swa_sample_kernel, steps=steps),
        grid=(batch // nbt,),
        in_specs=[pl.BlockSpec(memory_space=pltpu.SMEM),
                  pl.BlockSpec((rows, A_WIDTH), lambda g: (g, 0)),
                  pl.BlockSpec((rows, KV_WIDTH), lambda g: (g, 0)),
                  cache, cache],
        out_specs=[pl.BlockSpec((rows, A_WIDTH), lambda g: (g, 0)), cache, cache],
        out_shape=[jax.ShapeDtypeStruct((batch * steps, A_WIDTH), BF16),
                   jax.ShapeDtypeStruct(ck.shape, F32),
                   jax.ShapeDtypeStruct(cv.shape, F32)],
        scratch_shapes=[pltpu.VMEM((rows, A_WIDTH), F32), pltpu.VMEM((rows, A_WIDTH), F32)],
        compiler_params=pltpu.CompilerParams(dimension_semantics=("parallel",),
                                             vmem_limit_bytes=VMEM_LIMIT),
        name="swa_sample",
    )(sinks, qa, kva, ck, cv)


def _prep_proj(w_in, b_gate):
    m_end = 4 * M_WIDTH
    g_end = m_end + 2 * M_HEADS
    wt = w_in.T
    qa = wt[g_end:g_end + A_WIDTH].reshape(A_KV_HEADS, A_GROUP, A_HD, D_MODEL).transpose(1, 0, 2, 3)
    gates = jnp.pad(wt[m_end:g_end], ((0, LANES - 2 * M_HEADS), (0, 0)))
    win = jnp.concatenate([wt[:m_end], qa.reshape(A_WIDTH, D_MODEL), wt[g_end + A_WIDTH:], gates],
                          axis=0).astype(BF16)
    bias = jnp.pad(b_gate, (0, LANES - 2 * M_HEADS)).reshape(1, LANES)
    return win, bias


def _prep_out(w_out):
    wa = w_out[M_WIDTH:].reshape(A_KV_HEADS, A_GROUP, A_HD, D_MODEL).transpose(1, 0, 2, 3)
    return jnp.concatenate([w_out[:M_WIDTH], wa.reshape(A_WIDTH, D_MODEL)], axis=0).astype(BF16)


def kernel(x_prompt, x_sample, cache_swa_k, cache_swa_v, state_mlstm_C, state_mlstm_n, state_mlstm_m,
           norm_gains, ffn_w_gate, ffn_w_up, ffn_w_down, w_in, b_gate, mlstm_norm_gain, attn_sinks, w_out):
    assert norm_gains.shape[0] == 1, "single layer"
    batch, seq, _ = x_prompt.shape
    dec_batch, steps, _ = x_sample.shape
    gains = norm_gains[0]
    ffn = tuple(w.astype(BF16) for w in (ffn_w_gate, ffn_w_up, ffn_w_down))
    win, bias = _prep_proj(w_in[0], b_gate[0])
    wo = _prep_out(w_out[0])
    gain = mlstm_norm_gain[0].reshape(1, M_WIDTH)
    sinks = attn_sinks[0]
    kv_lanes = A_KV_HEADS * A_HD

    chunk = min(MLSTM_CHUNK, seq)
    assert seq % chunk == 0
    xp, zm, qa, kva, gates = _ffn_proj(x_prompt.reshape(batch * seq, D_MODEL), gains, *ffn, win, bias,
                                       0, 0, 1, 2, chunk)
    um, p_c, p_n, p_m = _mlstm_prompt(zm, gates, gain, batch, chunk)
    ua = _swa_prompt(sinks, qa, kva, batch)
    yp = _mix_ffn(xp, um, ua, wo, gains, *ffn, 1, 3, 4, 5).reshape(batch, seq, D_MODEL)
    keep = min(WINDOW, seq)
    kv_tail = kva.reshape(batch, seq, KV_WIDTH)[:, seq - keep:]
    p_k = kv_tail[..., :kv_lanes].reshape(1, batch, keep, A_KV_HEADS, A_HD)
    p_v = kv_tail[..., kv_lanes:].reshape(1, batch, keep, A_KV_HEADS, A_HD)
    p_c = p_c[None]
    p_n = p_n[None, :, :, 0, :]
    p_m = p_m[None, :, :, 0, 0]

    xs, zm_s, qa_s, kva_s, gates_s = _ffn_proj(x_sample.reshape(dec_batch * steps, D_MODEL), gains, *ffn,
                                               win, bias, 0, 0, 1, 2, min(chunk, dec_batch * steps))
    m_rows = jnp.repeat(state_mlstm_m[0].T, steps, axis=1)[..., None]
    um_s, s_c, s_n2, s_m_rows = _mlstm_sample(zm_s, gates_s, gain, state_mlstm_C[0],
                                              state_mlstm_n[0].reshape(dec_batch, M_HEADS * M_DK),
                                              m_rows, steps)
    wc = cache_swa_k.shape[2]
    ua_s, s_k, s_v = _swa_sample(sinks, qa_s, kva_s, cache_swa_k[0].reshape(dec_batch, wc, kv_lanes),
                                 cache_swa_v[0].reshape(dec_batch, wc, kv_lanes), steps)
    ys = _mix_ffn(xs, um_s, ua_s, wo, gains, *ffn, 1, 3, 4, 5).reshape(dec_batch, steps, D_MODEL)
    s_n = s_n2.reshape(1, dec_batch, M_HEADS, M_DK)
    s_m = s_m_rows[:, ::steps, 0].T[None]
    s_k = s_k.reshape(1, dec_batch, wc, A_KV_HEADS, A_HD)
    s_v = s_v.reshape(1, dec_batch, wc, A_KV_HEADS, A_HD)
    return (yp, ys, p_k, p_v, p_c, p_n, p_m, s_k, s_v, s_c[None], s_n, s_m)
```

```python
import functools

import jax
import jax.numpy as jnp
from jax import lax
from jax.experimental import pallas as pl
from jax.experimental.pallas import tpu as pltpu

F32 = jnp.float32
BF16 = jnp.bfloat16

D_MODEL = 1024
D_FF = 2816
M_HEADS = 4
M_DK = 128
M_DV = 128
M_WIDTH = M_HEADS * M_DV
A_HEADS = 8
A_KV_HEADS = 2
A_GROUP = A_HEADS // A_KV_HEADS
A_HD = 64
A_WIDTH = A_HEADS * A_HD
WINDOW = 128
RMS_EPS = 1e-6

LANES = 128
FFN_CHUNK = 256
N_FFN_CHUNKS = D_FF // FFN_CHUNK
TOKEN_TILE = 512
MLSTM_CHUNK = 256
ZM_WIDTH = 4 * M_WIDTH
KV_WIDTH = 2 * A_KV_HEADS * A_HD
VMEM_LIMIT = 56 * 1024 * 1024
MLSTM_CHUNKS_PER_STEP = 8
GATE_I, GATE_F, GATE_B, GATE_X = 0, M_HEADS, 2 * M_HEADS, 3 * M_HEADS

NT_DIMS = (((1,), (1,)), ((), ()))


def _rms(x, g):
    ms = jnp.mean(x * x, axis=-1, keepdims=True)
    return x * lax.rsqrt(ms + RMS_EPS) * g


def _log_sigmoid(x):
    return jnp.minimum(x, 0.0) - jnp.log1p(jnp.exp(-jnp.abs(x)))


def _dot(a, b):
    return jnp.dot(a, b, preferred_element_type=F32)


def _dot_nt(a, b):
    return lax.dot_general(a, b, NT_DIMS, preferred_element_type=F32)


def _resident(shape):
    nd = len(shape)
    return pl.BlockSpec(shape, lambda *_: (0,) * nd, pipeline_mode=pl.Buffered(1))


def _resident_ffn_weight(w, j):
    return pl.BlockSpec((None, None) + w.shape[2:], lambda *_: (0, j, 0, 0), pipeline_mode=pl.Buffered(1))


def _swiglu_residual(x, gpre, gpost, wg_ref, wu_ref, wd_ref, h_ref, acc_ref):
    h_ref[...] = _rms(x, gpre).astype(BF16)
    for f in range(N_FFN_CHUNKS):
        cols = slice(f * FFN_CHUNK, (f + 1) * FFN_CHUNK)
        h = h_ref[...]
        g = _dot(h, wg_ref[:, cols])
        u = _dot(h, wu_ref[:, cols])
        part = _dot((g * jax.nn.sigmoid(g) * u).astype(BF16), wd_ref[cols, :])
        if f == 0:
            acc_ref[...] = part
        else:
            acc_ref[...] += part
    return x + _rms(acc_ref[...], 0.5 * gpost)


def _project(x, gain, win_ref, bias_ref, h_ref, zm_ref, qa_ref, kva_ref, gates_ref, chunk):
    h_ref[...] = _rms(x, gain).astype(BF16)

    def cols(lo, width):
        return _dot_nt(h_ref[...], win_ref[lo:lo + width, :])

    gates = cols(ZM_WIDTH + A_WIDTH + KV_WIDTH, LANES) + bias_ref[...]
    lane = lax.broadcasted_iota(jnp.int32, (chunk, LANES), 1)
    pos = lax.broadcasted_iota(jnp.int32, (chunk, LANES), 0)
    for c0 in range(0, gates.shape[0], chunk):
        g = gates[c0:c0 + chunk]
        b = jnp.where((lane >= GATE_F) & (lane < GATE_B), _log_sigmoid(g), 0.0)
        shift = 1
        while shift < chunk:
            b = b + jnp.where(pos >= shift, pltpu.roll(b, shift, 0), 0.0)
            shift *= 2
        x_gate = pltpu.roll(g, GATE_X - GATE_I, 1) - pltpu.roll(b, GATE_X - GATE_F, 1)
        out = jnp.where(lane < GATE_B, g, jnp.where(lane < GATE_X, pltpu.roll(b, GATE_B - GATE_F, 1), x_gate))
        gates_ref[c0:c0 + chunk, :] = jnp.where(lane < GATE_X + M_HEADS, out, 0.0)

    zm_ref[:, 0:M_WIDTH] = cols(0, M_WIDTH).astype(BF16)
    zm_ref[:, M_WIDTH:2 * M_WIDTH] = (cols(M_WIDTH, M_WIDTH) * (M_DK ** -0.5)).astype(BF16)
    zm_ref[:, 2 * M_WIDTH:3 * M_WIDTH] = cols(2 * M_WIDTH, M_WIDTH).astype(BF16)
    zm_ref[:, 3 * M_WIDTH:] = jax.nn.sigmoid(cols(3 * M_WIDTH, M_WIDTH)).astype(BF16)
    qa_ref[...] = (cols(ZM_WIDTH, A_WIDTH) * (A_HD ** -0.5)).astype(BF16)
    kva_ref[...] = cols(ZM_WIDTH + A_WIDTH, KV_WIDTH)


def _ffn_proj_kernel(x_ref, gains_ref, wg_ref, wu_ref, wd_ref, win_ref, bias_ref,
                     x1_ref, zm_ref, qa_ref, kva_ref, gates_ref, h_ref, acc_ref, *, pre, post, mixer, chunk):
    x1 = _swiglu_residual(x_ref[...], gains_ref[pre:pre + 1, :], gains_ref[post:post + 1, :],
                          wg_ref, wu_ref, wd_ref, h_ref, acc_ref)
    x1_ref[...] = x1
    _project(x1, gains_ref[mixer:mixer + 1, :], win_ref, bias_ref, h_ref, zm_ref, qa_ref, kva_ref, gates_ref,
             chunk)


def _mix_ffn_kernel(x_ref, um_ref, ua_ref, wo_ref, gains_ref, wg_ref, wu_ref, wd_ref, o_ref, h_ref, acc_ref,
                    *, mix, pre, post):
    u = jnp.concatenate([um_ref[...], ua_ref[...]], axis=-1)
    x = x_ref[...] + _rms(_dot(u, wo_ref[...]), gains_ref[mix:mix + 1, :])
    o_ref[...] = _swiglu_residual(x, gains_ref[pre:pre + 1, :], gains_ref[post:post + 1, :],
                                  wg_ref, wu_ref, wd_ref, h_ref, acc_ref)


def _ffn_scratch(tm):
    return [pltpu.VMEM((tm, D_MODEL), BF16), pltpu.VMEM((tm, D_MODEL), F32)]


def _dense_params():
    return pltpu.CompilerParams(dimension_semantics=("parallel",), vmem_limit_bytes=VMEM_LIMIT)


def _ffn_proj(x, gains, wg, wu, wd, win, bias, j, pre, post, mixer, chunk):
    n = x.shape[0]
    tm = min(TOKEN_TILE, n)
    assert tm % chunk == 0

    def rows(width):
        return pl.BlockSpec((tm, width), lambda i: (i, 0))

    return pl.pallas_call(
        functools.partial(_ffn_proj_kernel, pre=pre, post=post, mixer=mixer, chunk=chunk),
        grid=(n // tm,),
        in_specs=[rows(D_MODEL), _resident(gains.shape)] + [_resident_ffn_weight(w, j) for w in (wg, wu, wd)]
        + [_resident(win.shape), _resident(bias.shape)],
        out_specs=[rows(D_MODEL), rows(ZM_WIDTH), rows(A_WIDTH), rows(KV_WIDTH), rows(LANES)],
        out_shape=[jax.ShapeDtypeStruct((n, D_MODEL), F32),
                   jax.ShapeDtypeStruct((n, ZM_WIDTH), BF16),
                   jax.ShapeDtypeStruct((n, A_WIDTH), BF16),
                   jax.ShapeDtypeStruct((n, KV_WIDTH), F32),
                   jax.ShapeDtypeStruct((n, LANES), F32)],
        scratch_shapes=_ffn_scratch(tm),
        compiler_params=_dense_params(),
        name="ffn_half_step_input_projection",
    )(x, gains, wg, wu, wd, win, bias)


def _mix_ffn(x, um, ua, wo, gains, wg, wu, wd, j, mix, pre, post):
    n = x.shape[0]
    tm = min(TOKEN_TILE, n)
    row = pl.BlockSpec((tm, D_MODEL), lambda i: (i, 0))
    half = pl.BlockSpec((tm, M_WIDTH), lambda i: (i, 0))
    return pl.pallas_call(
        functools.partial(_mix_ffn_kernel, mix=mix, pre=pre, post=post),
        grid=(n // tm,),
        in_specs=[row, half, half, _resident(wo.shape), _resident(gains.shape)]
        + [_resident_ffn_weight(w, j) for w in (wg, wu, wd)],
        out_specs=row,
        out_shape=jax.ShapeDtypeStruct((n, D_MODEL), F32),
        scratch_shapes=_ffn_scratch(tm),
        compiler_params=_dense_params(),
        name="out_proj_ffn_half_step",
    )(x, um, ua, wo, gains, wg, wu, wd)


def _head_norm_gate(hh, o, gain):
    hn = hh * lax.rsqrt(jnp.mean(hh * hh, axis=-1, keepdims=True) + RMS_EPS) * gain
    return (o.astype(F32) * hn).astype(BF16)


def _mlstm_prompt_kernel(zm_ref, gates_ref, gain_ref, um_ref, c_ref, n_ref, m_ref, cnt_ref, ms_ref, *, chunk):
    step = pl.program_id(1)

    @pl.when(step == 0)
    def _():
        cnt_ref[...] = jnp.zeros_like(cnt_ref)
        ms_ref[...] = jnp.zeros_like(ms_ref)

    si = lax.broadcasted_iota(jnp.int32, (chunk, chunk), 0)
    ti = lax.broadcasted_iota(jnp.int32, (chunk, chunk), 1)
    causal = si <= ti
    ones_rows = jnp.where(lax.broadcasted_iota(jnp.int32, (LANES, chunk), 0) == 0, 1.0, 0.0).astype(BF16)

    for c in range(zm_ref.shape[0] // chunk):
        rows = slice(c * chunk, (c + 1) * chunk)
        gates = gates_ref[rows, :]
        gates_t = gates.T
        for h in range(M_HEADS):
            lanes = slice(h * M_DK, (h + 1) * M_DK)
            q = zm_ref[rows, lanes]
            k = zm_ref[rows, M_WIDTH + h * M_DK:M_WIDTH + (h + 1) * M_DK]
            v = zm_ref[rows, 2 * M_WIDTH + h * M_DV:2 * M_WIDTH + (h + 1) * M_DV]
            o = zm_ref[rows, 3 * M_WIDTH + h * M_DV:3 * M_WIDTH + (h + 1) * M_DV]
            x_col = gates[:, GATE_X + h:GATE_X + h + 1]
            ig_row = gates_t[GATE_I + h:GATE_I + h + 1, :]
            b_row = gates_t[GATE_B + h:GATE_B + h + 1, :]
            m_prev = ms_ref[h][:, 0:1]
            cnt = cnt_ref[h]

            xm = jnp.where(causal, x_col, -jnp.inf)
            mu = jnp.maximum(m_prev, jnp.max(xm, axis=0, keepdims=True))
            a = jnp.exp(m_prev - mu)
            s_t = _dot_nt(k, q) * jnp.exp(xm - mu)
            v_ext = jnp.concatenate([v.astype(F32).T.astype(BF16), ones_rows], axis=0)
            nd = a * _dot_nt(cnt.astype(BF16), q) + _dot(v_ext, s_t.astype(BF16))
            den = nd[M_DV:M_DV + 1, :]
            hh = nd[:M_DV, :] * (1.0 / jnp.maximum(jnp.abs(den), jnp.exp(-(b_row + mu))))
            hn = hh * lax.rsqrt(jnp.mean(hh * hh, axis=0, keepdims=True) + RMS_EPS)
            um_ref[rows, lanes] = (hn.T * gain_ref[:, lanes] * o.astype(F32)).astype(BF16)

            b_last = b_row[:, chunk - 1:chunk]
            g = b_last - b_row + ig_row
            m_new = jnp.maximum(b_last + m_prev, jnp.max(g, axis=-1, keepdims=True))
            decay = jnp.exp(b_last + m_prev - m_new)
            wv = (jnp.exp(g - m_new) * v_ext.astype(F32)).astype(BF16)
            cnt_ref[h] = decay * cnt + _dot(wv, k)
            ms_ref[h] = jnp.broadcast_to(m_new, (1, LANES))

    @pl.when(step == pl.num_programs(1) - 1)
    def _():
        for h in range(M_HEADS):
            c_ref[0, h] = cnt_ref[h][:M_DV, :].T
            n_ref[0, h] = cnt_ref[h][M_DV:M_DV + 1, :]
            m_ref[0, h] = ms_ref[h]


def _mlstm_prompt(zm, gates, gain, batch, chunk):
    n = zm.shape[0]
    per_step = min(MLSTM_CHUNKS_PER_STEP, n // batch // chunk)
    step_rows = per_step * chunk
    ns = n // batch // step_rows

    def rows(width):
        return pl.BlockSpec((step_rows, width), lambda b, c: (b * ns + c, 0))

    def state(*dims):
        return pl.BlockSpec((1, M_HEADS) + dims, lambda b, c: (b, 0, 0, 0))

    return pl.pallas_call(
        functools.partial(_mlstm_prompt_kernel, chunk=chunk),
        grid=(batch, ns),
        in_specs=[rows(ZM_WIDTH), rows(LANES), pl.BlockSpec((1, M_WIDTH), lambda b, c: (0, 0))],
        out_specs=[rows(M_WIDTH), state(M_DK, M_DV), state(1, M_DK), state(1, LANES)],
        out_shape=[jax.ShapeDtypeStruct((n, M_WIDTH), BF16),
                   jax.ShapeDtypeStruct((batch, M_HEADS, M_DK, M_DV), F32),
                   jax.ShapeDtypeStruct((batch, M_HEADS, 1, M_DK), F32),
                   jax.ShapeDtypeStruct((batch, M_HEADS, 1, LANES), F32)],
        scratch_shapes=[pltpu.VMEM((M_HEADS, 2 * LANES, M_DK), F32), pltpu.VMEM((M_HEADS, 1, LANES), F32)],
        compiler_params=pltpu.CompilerParams(dimension_semantics=("parallel", "arbitrary"),
                                             vmem_limit_bytes=VMEM_LIMIT),
        name="mlstm_prompt",
    )(zm, gates, gain)


SWA_BLOCKS_PER_STEP = 8


def _alibi_slope(head):
    return 2.0 ** (-8.0 * (head + 1) / A_HEADS)


def _swa_prompt_kernel(sinks_ref, qa_ref, kvc_ref, kvp_ref, ua_ref):
    hw = A_KV_HEADS * A_HD
    n_blocks = qa_ref.shape[0] // WINDOW
    qi = lax.broadcasted_iota(jnp.int32, (WINDOW, 2 * WINDOW), 0)
    kj = lax.broadcasted_iota(jnp.int32, (WINDOW, 2 * WINDOW), 1)
    dist = WINDOW + qi - kj
    band = (dist >= 0) & (dist < WINDOW)
    distf = dist.astype(F32)
    low = lax.broadcasted_iota(jnp.int32, (WINDOW, LANES), 1) < A_HD
    first = pl.program_id(1) == 0

    for j in range(n_blocks):
        rows = slice(j * WINDOW, (j + 1) * WINDOW)
        kvc = kvc_ref[rows, :]
        kvp = kvp_ref[...] if j == 0 else kvc_ref[(j - 1) * WINDOW:j * WINDOW, :]
        valid = band & ((kj >= WINDOW) | jnp.logical_not(first)) if j == 0 else band
        kk = jnp.concatenate([kvp[:, :hw], kvc[:, :hw]], axis=0).astype(BF16)
        vv = jnp.concatenate([kvp[:, hw:], kvc[:, hw:]], axis=0).astype(BF16)
        for p in range(A_GROUP):
            qp = qa_ref[rows, p * LANES:(p + 1) * LANES]
            outs = []
            for kvh in range(A_KV_HEADS):
                head = kvh * A_GROUP + p
                keep = low if kvh == 0 else jnp.logical_not(low)
                s = _dot_nt(jnp.where(keep, qp, jnp.zeros_like(qp)), kk)
                s = jnp.where(valid, s - _alibi_slope(head) * distf, -jnp.inf)
                sink = sinks_ref[head]
                mx = jnp.maximum(jnp.max(s, axis=-1, keepdims=True), sink)
                e = jnp.exp(s - mx)
                den = jnp.sum(e, axis=-1, keepdims=True) + jnp.exp(sink - mx)
                outs.append(_dot(e.astype(BF16), vv) / den)
            ua_ref[rows, p * LANES:(p + 1) * LANES] = jnp.where(low, outs[0], outs[1]).astype(BF16)


def _swa_prompt(sinks, qa, kva, batch):
    n = qa.shape[0]
    per_step = min(SWA_BLOCKS_PER_STEP, n // batch // WINDOW)
    rows = per_step * WINDOW
    ns = n // batch // rows

    def step_rows(width):
        return pl.BlockSpec((rows, width), lambda b, i: (b * ns + i, 0))

    def prev_block(b, i):
        return (jnp.maximum((b * ns + i) * per_step - 1, 0), 0)

    return pl.pallas_call(
        _swa_prompt_kernel,
        grid=(batch, ns),
        in_specs=[pl.BlockSpec(memory_space=pltpu.SMEM), step_rows(A_WIDTH), step_rows(KV_WIDTH),
                  pl.BlockSpec((WINDOW, KV_WIDTH), prev_block)],
        out_specs=step_rows(A_WIDTH),
        out_shape=jax.ShapeDtypeStruct((n, A_WIDTH), BF16),
        compiler_params=pltpu.CompilerParams(dimension_semantics=("parallel", "parallel"),
                                             vmem_limit_bytes=VMEM_LIMIT),
        name="swa_prompt",
    )(sinks, qa, kva, kva)


SAMPLE_ROWS = 128


def _mlstm_sample_kernel(q_ref, k_ref, v_ref, o_ref, gates_ref, gain_ref, c_ref, n_ref, m_ref,
                         um_ref, co_ref, no_ref, mo_ref,
                         qf_ref, qc_ref, nexp_ref, wkk_ref, dec_ref, *, steps):
    rows = SAMPLE_ROWS
    nb = rows // steps
    head = pl.program_id(1)
    row = lax.broadcasted_iota(jnp.int32, (rows, rows), 0)
    col = lax.broadcasted_iota(jnp.int32, (rows, rows), 1)
    same = (row // steps) == (col // steps)
    causal = same & (col <= row)
    eye = row == col

    def to_row(x_col):
        return jnp.sum(jnp.where(eye, x_col, 0.0), axis=0, keepdims=True)

    def pick(mask, x_row, fill, reduce):
        return reduce(jnp.where(mask, x_row, fill), axis=-1, keepdims=True)

    gates = gates_ref[...]
    lane = lax.broadcasted_iota(jnp.int32, (rows, LANES), 1)
    ig_col = jnp.sum(jnp.where(lane == head, gates, 0.0), axis=-1, keepdims=True)
    fg_col = jnp.sum(jnp.where(lane == head + M_HEADS, gates, 0.0), axis=-1, keepdims=True)
    lf_row = to_row(_log_sigmoid(fg_col))
    b_col = pick(causal, lf_row, 0.0, jnp.sum)
    b_row = to_row(b_col)
    ig_row = to_row(ig_col)
    m_prev = m_ref[0]

    d = jnp.where(causal, b_col - b_row + ig_row, -jnp.inf)
    inter = b_col + m_prev
    m_t = jnp.maximum(inter, jnp.max(d, axis=-1, keepdims=True))
    w = jnp.exp(d - m_t)
    a = jnp.exp(inter - m_t)

    q = q_ref[...]
    k = k_ref[...]
    v = v_ref[...]
    qf_ref[...] = q.astype(F32)
    for t in range(steps):
        nexp_ref[pl.ds(t, nb, stride=steps), :] = n_ref[...]

    sub = lax.broadcasted_iota(jnp.int32, (2 * steps, M_DV), 0)

    def qc_pair(j, carry):
        q8 = qf_ref[pl.ds(pl.multiple_of(j * 2 * steps, 2 * steps), 2 * steps), :].astype(BF16)
        r0 = _dot(q8, c_ref[2 * j, 0].astype(BF16))
        r1 = _dot(q8, c_ref[2 * j + 1, 0].astype(BF16))
        qc_ref[pl.ds(pl.multiple_of(j * 2 * steps, 2 * steps), 2 * steps), :] = jnp.where(sub < steps, r0, r1)
        return carry

    lax.fori_loop(0, nb // 2, qc_pair, 0, unroll=True)

    s = _dot_nt(q, k) * w
    qn = jnp.sum(qf_ref[...] * nexp_ref[...], axis=-1, keepdims=True)
    num = a * qc_ref[...] + _dot(s.astype(BF16), v)
    den = a * qn + jnp.sum(s, axis=-1, keepdims=True)
    hh = num / jnp.maximum(jnp.abs(den), jnp.exp(-m_t))
    um_ref[...] = _head_norm_gate(hh, o_ref[...], gain_ref[...])

    last = same & ((col % steps) == steps - 1)
    b_last = pick(last, b_row, 0.0, jnp.sum)
    g_col = b_last - b_col + ig_col
    g_max = pick(same, to_row(g_col), -jnp.inf, jnp.max)
    m_new = jnp.maximum(b_last + m_prev, g_max)
    decay = jnp.exp(b_last + m_prev - m_new)
    wk = jnp.exp(g_col - m_new)
    mo_ref[0] = m_new

    kf = k.astype(F32)
    wkk_ref[...] = wk * kf
    dec_ref[...] = jnp.broadcast_to(decay, (rows, M_DV))
    n_sum = wkk_ref[pl.ds(0, nb, stride=steps), :]
    for t in range(1, steps):
        n_sum = n_sum + wkk_ref[pl.ds(t, nb, stride=steps), :]
    no_ref[...] = dec_ref[pl.ds(0, nb, stride=steps), :] * n_ref[...] + n_sum

    k_t = kf.T
    wv = (wk * v.astype(F32)).astype(BF16)
    tok_batch = lax.broadcasted_iota(jnp.int32, (M_DK, rows), 1) // steps

    def update(b, carry):
        upd = _dot(jnp.where(tok_batch == b, k_t, 0.0).astype(BF16), wv)
        co_ref[b, 0] = dec_ref[pl.ds(b * steps, 1), :] * c_ref[b, 0] + upd
        return carry

    lax.fori_loop(0, nb, update, 0, unroll=True)


def _mlstm_sample(zm, gates, gain, c, n2, m_rows, steps):
    rows = zm.shape[0]
    batch = c.shape[0]
    nb = SAMPLE_ROWS // steps
    groups = rows // SAMPLE_ROWS

    def zcol(section):
        return pl.BlockSpec((SAMPLE_ROWS, M_DK), lambda g, h: (g, section * M_HEADS + h))

    c_spec = pl.BlockSpec((nb, 1, M_DK, M_DV), lambda g, h: (g, h, 0, 0))
    n_spec = pl.BlockSpec((nb, M_DK), lambda g, h: (g, h))
    m_spec = pl.BlockSpec((1, SAMPLE_ROWS, 1), lambda g, h: (h, g, 0))
    return pl.pallas_call(
        functools.partial(_mlstm_sample_kernel, steps=steps),
        grid=(groups, M_HEADS),
        in_specs=[zcol(0), zcol(1), zcol(2), zcol(3),
                  pl.BlockSpec((SAMPLE_ROWS, LANES), lambda g, h: (g, 0)),
                  pl.BlockSpec((1, M_DV), lambda g, h: (0, h)),
                  c_spec, n_spec, m_spec],
        out_specs=[pl.BlockSpec((SAMPLE_ROWS, M_DV), lambda g, h: (g, h)), c_spec, n_spec, m_spec],
        out_shape=[jax.ShapeDtypeStruct((rows, M_WIDTH), BF16),
                   jax.ShapeDtypeStruct(c.shape, F32),
                   jax.ShapeDtypeStruct(n2.shape, F32),
                   jax.ShapeDtypeStruct(m_rows.shape, F32)],
        scratch_shapes=[pltpu.VMEM((SAMPLE_ROWS, M_DK), F32),
                        pltpu.VMEM((SAMPLE_ROWS, M_DV), F32),
                        pltpu.VMEM((SAMPLE_ROWS, M_DK), F32),
                        pltpu.VMEM((SAMPLE_ROWS, M_DK), F32),
                        pltpu.VMEM((SAMPLE_ROWS, M_DV), F32)],
        compiler_params=pltpu.CompilerParams(dimension_semantics=("parallel", "parallel"),
                                             vmem_limit_bytes=VMEM_LIMIT),
        name="mlstm_sample",
    )(zm, zm, zm, zm, gates, gain, c, n2, m_rows)


SWA_SAMPLE_BATCHES = 16


def _swa_sample_kernel(sinks_ref, qa_ref, kva_ref, ck_ref, cv_ref, ua_ref, nk_ref, nv_ref, qf_ref, uo_ref,
                       *, steps):
    pair_rows = 2 * steps
    n_pairs = SWA_SAMPLE_BATCHES // 2
    hw = A_KV_HEADS * A_HD
    srows = A_HEADS * pair_rows
    qf_ref[...] = qa_ref[...].astype(F32)

    r = lax.broadcasted_iota(jnp.int32, (srows, 1), 0)
    tok = r % pair_rows
    t_col = tok % steps
    row_batch = tok // steps
    blk = r // pair_rows
    head = (blk % A_KV_HEADS) * A_GROUP + blk // A_KV_HEADS
    slope = jnp.zeros((srows, 1), F32)
    sink = jnp.zeros((srows, 1), F32)
    for hd in range(A_HEADS):
        slope = jnp.where(head == hd, _alibi_slope(hd), slope)
        sink = jnp.where(head == hd, sinks_ref[hd], sink)
    row_low = (blk % A_KV_HEADS) == 0

    jj = lax.broadcasted_iota(jnp.int32, (srows, WINDOW), 1)
    dist_c = WINDOW + t_col - jj
    valid_c = dist_c < WINDOW
    cc = lax.broadcasted_iota(jnp.int32, (srows, pair_rows), 1)
    dist_n = t_col - cc % steps
    low = lax.broadcasted_iota(jnp.int32, (pair_rows, LANES), 1) < A_HD
    low_s = lax.broadcasted_iota(jnp.int32, (srows, LANES), 1) < A_HD
    sub = lax.broadcasted_iota(jnp.int32, (pair_rows, LANES), 0)

    def pair(j, carry):
        r0 = pl.multiple_of(j * pair_rows, pair_rows)
        q8 = qf_ref[pl.ds(r0, pair_rows), :]
        parts = []
        for p in range(A_GROUP):
            qp = q8[:, p * LANES:(p + 1) * LANES]
            parts.append(jnp.where(low, qp, 0.0))
            parts.append(jnp.where(low, 0.0, qp))
        qs = jnp.concatenate(parts, axis=0).astype(BF16)
        kv_new = kva_ref[pl.ds(r0, pair_rows), :]
        k_new = kv_new[:, :hw]
        v_new = kv_new[:, hw:]
        s_new = _dot_nt(qs, k_new.astype(BF16))
        out8 = [jnp.zeros((pair_rows, LANES), F32) for _ in range(A_GROUP)]
        for bb in range(2):
            b = 2 * j + bb
            kc = ck_ref[b]
            vc = cv_ref[b]
            s_c = jnp.where(valid_c, _dot_nt(qs, kc.astype(BF16)) - slope * dist_c.astype(F32), -jnp.inf)
            valid_n = (cc // steps == bb) & (dist_n >= 0)
            s_n = jnp.where(valid_n, s_new - slope * dist_n.astype(F32), -jnp.inf)
            mx = jnp.maximum(jnp.maximum(jnp.max(s_c, axis=-1, keepdims=True),
                                         jnp.max(s_n, axis=-1, keepdims=True)), sink)
            e_c = jnp.exp(s_c - mx)
            e_n = jnp.exp(s_n - mx)
            den = (jnp.sum(e_c, axis=-1, keepdims=True) + jnp.sum(e_n, axis=-1, keepdims=True)
                   + jnp.exp(sink - mx))
            pv = (_dot(e_c.astype(BF16), vc.astype(BF16)) + _dot(e_n.astype(BF16), v_new.astype(BF16))) / den
            pv = jnp.where(low_s == row_low, pv, 0.0)
            for p in range(A_GROUP):
                both = (pv[(2 * p) * pair_rows:(2 * p + 1) * pair_rows]
                        + pv[(2 * p + 1) * pair_rows:(2 * p + 2) * pair_rows])
                out8[p] = jnp.where(sub // steps == bb, both, out8[p])
            tail_k = k_new if bb == 1 else pltpu.roll(k_new, steps, 0)
            tail_v = v_new if bb == 1 else pltpu.roll(v_new, steps, 0)
            rk = pltpu.roll(kc, WINDOW - steps, 0)
            rv = pltpu.roll(vc, WINDOW - steps, 0)
            nk_ref[b] = rk
            nv_ref[b] = rv
            nk_ref[b, WINDOW - pair_rows:, :] = jnp.where(sub >= steps, tail_k, rk[WINDOW - pair_rows:])
            nv_ref[b, WINDOW - pair_rows:, :] = jnp.where(sub >= steps, tail_v, rv[WINDOW - pair_rows:])
        uo_ref[pl.ds(r0, pair_rows), :] = jnp.concatenate(out8, axis=-1)
        return carry

    lax.fori_loop(0, n_pairs, pair, 0, unroll=True)
    ua_ref[...] = uo_ref[...].astype(BF16)


def _swa_sample(sinks, qa, kva, ck, cv, steps):
    batch = ck.shape[0]
    nbt = SWA_SAMPLE_BATCHES
    rows = nbt * steps
    cache = pl.BlockSpec((nbt, WINDOW, LANES), lambda g: (g, 0, 0))
    return pl.pallas_call(
        functools.partial(_swa_sample_kernel, steps=steps),
        grid=(batch // nbt,),
        in_specs=[pl.BlockSpec(memory_space=pltpu.SMEM),
                  pl.BlockSpec((rows, A_WIDTH), lambda g: (g, 0)),
                  pl.BlockSpec((rows, KV_WIDTH), lambda g: (g, 0)),
                  cache, cache],
        out_specs=[pl.BlockSpec((rows, A_WIDTH), lambda g: (g, 0)), cache, cache],
        out_shape=[jax.ShapeDtypeStruct((batch * steps, A_WIDTH), BF16),
                   jax.ShapeDtypeStruct(ck.shape, F32),
                   jax.ShapeDtypeStruct(cv.shape, F32)],
        scratch_shapes=[pltpu.VMEM((rows, A_WIDTH), F32), pltpu.VMEM((rows, A_WIDTH), F32)],
        compiler_params=pltpu.CompilerParams(dimension_semantics=("parallel",),
                                             vmem_limit_bytes=VMEM_LIMIT),
        name="swa_sample",
    )(sinks, qa, kva, ck, cv)


def _prep_proj(w_in, b_gate):
    m_end = 4 * M_WIDTH
    g_end = m_end + 2 * M_HEADS
    wt = w_in.T
    qa = wt[g_end:g_end + A_WIDTH].reshape(A_KV_HEADS, A_GROUP, A_HD, D_MODEL).transpose(1, 0, 2, 3)
    gates = jnp.pad(wt[m_end:g_end], ((0, LANES - 2 * M_HEADS), (0, 0)))
    win = jnp.concatenate([wt[:m_end], qa.reshape(A_WIDTH, D_MODEL), wt[g_end + A_WIDTH:], gates],
                          axis=0).astype(BF16)
    bias = jnp.pad(b_gate, (0, LANES - 2 * M_HEADS)).reshape(1, LANES)
    return win, bias


def _prep_out(w_out):
    wa = w_out[M_WIDTH:].reshape(A_KV_HEADS, A_GROUP, A_HD, D_MODEL).transpose(1, 0, 2, 3)
    return jnp.concatenate([w_out[:M_WIDTH], wa.reshape(A_WIDTH, D_MODEL)], axis=0).astype(BF16)


def kernel(x_prompt, x_sample, cache_swa_k, cache_swa_v, state_mlstm_C, state_mlstm_n, state_mlstm_m,
           norm_gains, ffn_w_gate, ffn_w_up, ffn_w_down, w_in, b_gate, mlstm_norm_gain, attn_sinks, w_out):
    assert norm_gains.shape[0] == 1, "single layer"
    batch, seq, _ = x_prompt.shape
    dec_batch, steps, _ = x_sample.shape
    gains = norm_gains[0]
    ffn = tuple(w.astype(BF16) for w in (ffn_w_gate, ffn_w_up, ffn_w_down))
    win, bias = _prep_proj(w_in[0], b_gate[0])
    wo = _prep_out(w_out[0])
    gain = mlstm_norm_gain[0].reshape(1, M_WIDTH)
    sinks = attn_sinks[0]
    kv_lanes = A_KV_HEADS * A_HD

    chunk = min(MLSTM_CHUNK, seq)
    assert seq % chunk == 0
    xp, zm, qa, kva, gates = _ffn_proj(x_prompt.reshape(batch * seq, D_MODEL), gains, *ffn, win, bias,
                                       0, 0, 1, 2, chunk)
    um, p_c, p_n, p_m = _mlstm_prompt(zm, gates, gain, batch, chunk)
    ua = _swa_prompt(sinks, qa, kva, batch)
    yp = _mix_ffn(xp, um, ua, wo, gains, *ffn, 1, 3, 4, 5).reshape(batch, seq, D_MODEL)
    keep = min(WINDOW, seq)
    kv_tail = kva.reshape(batch, seq, KV_WIDTH)[:, seq - keep:]
    p_k = kv_tail[..., :kv_lanes].reshape(1, batch, keep, A_KV_HEADS, A_HD)
    p_v = kv_tail[..., kv_lanes:].reshape(1, batch, keep, A_KV_HEADS, A_HD)
    p_c = p_c[None]
    p_n = p_n[None, :, :, 0, :]
    p_m = p_m[None, :, :, 0, 0]

    xs, zm_s, qa_s, kva_s, gates_s = _ffn_proj(x_sample.reshape(dec_batch * steps, D_MODEL), gains, *ffn,
                                               win, bias, 0, 0, 1, 2, min(chunk, dec_batch * steps))
    m_rows = jnp.repeat(state_mlstm_m[0].T, steps, axis=1)[..., None]
    um_s, s_c, s_n2, s_m_rows = _mlstm_sample(zm_s, gates_s, gain, state_mlstm_C[0],
                                              state_mlstm_n[0].reshape(dec_batch, M_HEADS * M_DK),
                                              m_rows, steps)
    wc = cache_swa_k.shape[2]
    ua_s, s_k, s_v = _swa_sample(sinks, qa_s, kva_s, cache_swa_k[0].reshape(dec_batch, wc, kv_lanes),
                                 cache_swa_v[0].reshape(dec_batch, wc, kv_lanes), steps)
    ys = _mix_ffn(xs, um_s, ua_s, wo, gains, *ffn, 1, 3, 4, 5).reshape(dec_batch, steps, D_MODEL)
    s_n = s_n2.reshape(1, dec_batch, M_HEADS, M_DK)
    s_m = s_m_rows[:, ::steps, 0].T[None]
    s_k = s_k.reshape(1, dec_batch, wc, A_KV_HEADS, A_HD)
    s_v = s_v.reshape(1, dec_batch, wc, A_KV_HEADS, A_HD)
    return (yp, ys, p_k, p_v, p_c, p_n, p_m, s_k, s_v, s_c[None], s_n, s_m)
```

```python
import functools

import jax
import jax.numpy as jnp
from jax import lax
from jax.experimental import pallas as pl
from jax.experimental.pallas import tpu as pltpu

F32 = jnp.float32
BF16 = jnp.bfloat16

D_MODEL = 1024
D_FF = 2816
M_HEADS = 4
M_DK = 128
M_DV = 128
M_WIDTH = M_HEADS * M_DV
A_HEADS = 8
A_KV_HEADS = 2
A_GROUP = A_HEADS // A_KV_HEADS
A_HD = 64
A_WIDTH = A_HEADS * A_HD
WINDOW = 128
RMS_EPS = 1e-6

LANES = 128
FFN_CHUNK = 256
N_FFN_CHUNKS = D_FF // FFN_CHUNK
TOKEN_TILE = 512
MLSTM_CHUNK = 256
ZM_WIDTH = 4 * M_WIDTH
KV_WIDTH = 2 * A_KV_HEADS * A_HD
VMEM_LIMIT = 56 * 1024 * 1024
MLSTM_CHUNKS_PER_STEP = 8
GATE_I, GATE_F, GATE_B, GATE_X = 0, M_HEADS, 2 * M_HEADS, 3 * M_HEADS

NT_DIMS = (((1,), (1,)), ((), ()))


def _rms(x, g):
    ms = jnp.mean(x * x, axis=-1, keepdims=True)
    return x * lax.rsqrt(ms + RMS_EPS) * g


def _log_sigmoid(x):
    return jnp.minimum(x, 0.0) - jnp.log1p(jnp.exp(-jnp.abs(x)))


def _dot(a, b):
    return jnp.dot(a, b, preferred_element_type=F32)


def _dot_nt(a, b):
    return lax.dot_general(a, b, NT_DIMS, preferred_element_type=F32)


def _resident(shape):
    nd = len(shape)
    return pl.BlockSpec(shape, lambda *_: (0,) * nd, pipeline_mode=pl.Buffered(1))


def _resident_ffn_weight(w, j):
    return pl.BlockSpec((None, None) + w.shape[2:], lambda *_: (0, j, 0, 0), pipeline_mode=pl.Buffered(1))


def _swiglu_residual(x, gpre, gpost, wg_ref, wu_ref, wd_ref, h_ref, acc_ref):
    h_ref[...] = _rms(x, gpre).astype(BF16)
    for f in range(N_FFN_CHUNKS):
        cols = slice(f * FFN_CHUNK, (f + 1) * FFN_CHUNK)
        h = h_ref[...]
        g = _dot(h, wg_ref[:, cols])
        u = _dot(h, wu_ref[:, cols])
        part = _dot((g * jax.nn.sigmoid(g) * u).astype(BF16), wd_ref[cols, :])
        if f == 0:
            acc_ref[...] = part
        else:
            acc_ref[...] += part
    return x + _rms(acc_ref[...], 0.5 * gpost)


def _project(x, gain, win_ref, bias_ref, h_ref, zm_ref, qa_ref, kva_ref, gates_ref, chunk):
    h_ref[...] = _rms(x, gain).astype(BF16)

    def cols(lo, width):
        return _dot_nt(h_ref[...], win_ref[lo:lo + width, :])

    gates = cols(ZM_WIDTH + A_WIDTH + KV_WIDTH, LANES) + bias_ref[...]
    lane = lax.broadcasted_iota(jnp.int32, (chunk, LANES), 1)
    pos = lax.broadcasted_iota(jnp.int32, (chunk, LANES), 0)
    for c0 in range(0, gates.shape[0], chunk):
        g = gates[c0:c0 + chunk]
        b = jnp.where((lane >= GATE_F) & (lane < GATE_B), _log_sigmoid(g), 0.0)
        shift = 1
        while shift < chunk:
            b = b + jnp.where(pos >= shift, pltpu.roll(b, shift, 0), 0.0)
            shift *= 2
        x_gate = pltpu.roll(g, GATE_X - GATE_I, 1) - pltpu.roll(b, GATE_X - GATE_F, 1)
        out = jnp.where(lane < GATE_B, g, jnp.where(lane < GATE_X, pltpu.roll(b, GATE_B - GATE_F, 1), x_gate))
        gates_ref[c0:c0 + chunk, :] = jnp.where(lane < GATE_X + M_HEADS, out, 0.0)

    zm_ref[:, 0:M_WIDTH] = cols(0, M_WIDTH).astype(BF16)
    zm_ref[:, M_WIDTH:2 * M_WIDTH] = (cols(M_WIDTH, M_WIDTH) * (M_DK ** -0.5)).astype(BF16)
    zm_ref[:, 2 * M_WIDTH:3 * M_WIDTH] = cols(2 * M_WIDTH, M_WIDTH).astype(BF16)
    zm_ref[:, 3 * M_WIDTH:] = jax.nn.sigmoid(cols(3 * M_WIDTH, M_WIDTH)).astype(BF16)
    qa_ref[...] = (cols(ZM_WIDTH, A_WIDTH) * (A_HD ** -0.5)).astype(BF16)
    kva_ref[...] = cols(ZM_WIDTH + A_WIDTH, KV_WIDTH)


def _ffn_proj_kernel(x_ref, gains_ref, wg_ref, wu_ref, wd_ref, win_ref, bias_ref,
                     x1_ref, zm_ref, qa_ref, kva_ref, gates_ref, h_ref, acc_ref, *, pre, post, mixer, chunk):
    x1 = _swiglu_residual(x_ref[...], gains_ref[pre:pre + 1, :], gains_ref[post:post + 1, :],
                          wg_ref, wu_ref, wd_ref, h_ref, acc_ref)
    x1_ref[...] = x1
    _project(x1, gains_ref[mixer:mixer + 1, :], win_ref, bias_ref, h_ref, zm_ref, qa_ref, kva_ref, gates_ref,
             chunk)


def _mix_ffn_kernel(x_ref, um_ref, ua_ref, wo_ref, gains_ref, wg_ref, wu_ref, wd_ref, o_ref, h_ref, acc_ref,
                    *, mix, pre, post):
    u = jnp.concatenate([um_ref[...], ua_ref[...]], axis=-1)
    x = x_ref[...] + _rms(_dot(u, wo_ref[...]), gains_ref[mix:mix + 1, :])
    o_ref[...] = _swiglu_residual(x, gains_ref[pre:pre + 1, :], gains_ref[post:post + 1, :],
                                  wg_ref, wu_ref, wd_ref, h_ref, acc_ref)


def _ffn_scratch(tm):
    return [pltpu.VMEM((tm, D_MODEL), BF16), pltpu.VMEM((tm, D_MODEL), F32)]


def _dense_params():
    return pltpu.CompilerParams(dimension_semantics=("parallel",), vmem_limit_bytes=VMEM_LIMIT)


def _ffn_proj(x, gains, wg, wu, wd, win, bias, j, pre, post, mixer, chunk):
    n = x.shape[0]
    tm = min(TOKEN_TILE, n)
    assert tm % chunk == 0

    def rows(width):
        return pl.BlockSpec((tm, width), lambda i: (i, 0))

    return pl.pallas_call(
        functools.partial(_ffn_proj_kernel, pre=pre, post=post, mixer=mixer, chunk=chunk),
        grid=(n // tm,),
        in_specs=[rows(D_MODEL), _resident(gains.shape)] + [_resident_ffn_weight(w, j) for w in (wg, wu, wd)]
        + [_resident(win.shape), _resident(bias.shape)],
        out_specs=[rows(D_MODEL), rows(ZM_WIDTH), rows(A_WIDTH), rows(KV_WIDTH), rows(LANES)],
        out_shape=[jax.ShapeDtypeStruct((n, D_MODEL), F32),
                   jax.ShapeDtypeStruct((n, ZM_WIDTH), BF16),
                   jax.ShapeDtypeStruct((n, A_WIDTH), BF16),
                   jax.ShapeDtypeStruct((n, KV_WIDTH), F32),
                   jax.ShapeDtypeStruct((n, LANES), F32)],
        scratch_shapes=_ffn_scratch(tm),
        compiler_params=_dense_params(),
        name="ffn_half_step_input_projection",
    )(x, gains, wg, wu, wd, win, bias)


def _mix_ffn(x, um, ua, wo, gains, wg, wu, wd, j, mix, pre, post):
    n = x.shape[0]
    tm = min(TOKEN_TILE, n)
    row = pl.BlockSpec((tm, D_MODEL), lambda i: (i, 0))
    half = pl.BlockSpec((tm, M_WIDTH), lambda i: (i, 0))
    return pl.pallas_call(
        functools.partial(_mix_ffn_kernel, mix=mix, pre=pre, post=post),
        grid=(n // tm,),
        in_specs=[row, half, half, _resident(wo.shape), _resident(gains.shape)]
        + [_resident_ffn_weight(w, j) for w in (wg, wu, wd)],
        out_specs=row,
        out_shape=jax.ShapeDtypeStruct((n, D_MODEL), F32),
        scratch_shapes=_ffn_scratch(tm),
        compiler_params=_dense_params(),
        name="out_proj_ffn_half_step",
    )(x, um, ua, wo, gains, wg, wu, wd)


def _head_norm_gate(hh, o, gain):
    hn = hh * lax.rsqrt(jnp.mean(hh * hh, axis=-1, keepdims=True) + RMS_EPS) * gain
    return (o.astype(F32) * hn).astype(BF16)


def _mlstm_prompt_kernel(zm_ref, gates_ref, gain_ref, um_ref, c_ref, n_ref, m_ref, cnt_ref, ms_ref, *, chunk):
    step = pl.program_id(1)

    @pl.when(step == 0)
    def _():
        cnt_ref[...] = jnp.zeros_like(cnt_ref)
        ms_ref[...] = jnp.zeros_like(ms_ref)

    si = lax.broadcasted_iota(jnp.int32, (chunk, chunk), 0)
    ti = lax.broadcasted_iota(jnp.int32, (chunk, chunk), 1)
    causal = si <= ti
    ones_rows = jnp.where(lax.broadcasted_iota(jnp.int32, (LANES, chunk), 0) == 0, 1.0, 0.0).astype(BF16)

    for c in range(zm_ref.shape[0] // chunk):
        rows = slice(c * chunk, (c + 1) * chunk)
        gates = gates_ref[rows, :]
        gates_t = gates.T
        for h in range(M_HEADS):
            lanes = slice(h * M_DK, (h + 1) * M_DK)
            q = zm_ref[rows, lanes]
            k = zm_ref[rows, M_WIDTH + h * M_DK:M_WIDTH + (h + 1) * M_DK]
            v = zm_ref[rows, 2 * M_WIDTH + h * M_DV:2 * M_WIDTH + (h + 1) * M_DV]
            o = zm_ref[rows, 3 * M_WIDTH + h * M_DV:3 * M_WIDTH + (h + 1) * M_DV]
            x_col = gates[:, GATE_X + h:GATE_X + h + 1]
            ig_row = gates_t[GATE_I + h:GATE_I + h + 1, :]
            b_row = gates_t[GATE_B + h:GATE_B + h + 1, :]
            m_prev = ms_ref[h][:, 0:1]
            cnt = cnt_ref[h]

            xm = jnp.where(causal, x_col, -jnp.inf)
            mu = jnp.maximum(m_prev, jnp.max(xm, axis=0, keepdims=True))
            a = jnp.exp(m_prev - mu)
            s_t = _dot_nt(k, q) * jnp.exp(xm - mu)
            v_ext = jnp.concatenate([v.astype(F32).T.astype(BF16), ones_rows], axis=0)
            nd = a * _dot_nt(cnt.astype(BF16), q) + _dot(v_ext, s_t.astype(BF16))
            den = nd[M_DV:M_DV + 1, :]
            hh = nd[:M_DV, :] * (1.0 / jnp.maximum(jnp.abs(den), jnp.exp(-(b_row + mu))))
            hn = hh * lax.rsqrt(jnp.mean(hh * hh, axis=0, keepdims=True) + RMS_EPS)
            um_ref[rows, lanes] = (hn.T * gain_ref[:, lanes] * o.astype(F32)).astype(BF16)

            b_last = b_row[:, chunk - 1:chunk]
            g = b_last - b_row + ig_row
            m_new = jnp.maximum(b_last + m_prev, jnp.max(g, axis=-1, keepdims=True))
            decay = jnp.exp(b_last + m_prev - m_new)
            wv = (jnp.exp(g - m_new) * v_ext.astype(F32)).astype(BF16)
            cnt_ref[h] = decay * cnt + _dot(wv, k)
            ms_ref[h] = jnp.broadcast_to(m_new, (1, LANES))

    @pl.when(step == pl.num_programs(1) - 1)
    def _():
        for h in range(M_HEADS):
            c_ref[0, h] = cnt_ref[h][:M_DV, :].T
            n_ref[0, h] = cnt_ref[h][M_DV:M_DV + 1, :]
            m_ref[0, h] = ms_ref[h]


def _mlstm_prompt(zm, gates, gain, batch, chunk):
    n = zm.shape[0]
    per_step = min(MLSTM_CHUNKS_PER_STEP, n // batch // chunk)
    step_rows = per_step * chunk
    ns = n // batch // step_rows

    def rows(width):
        return pl.BlockSpec((step_rows, width), lambda b, c: (b * ns + c, 0))

    def state(*dims):
        return pl.BlockSpec((1, M_HEADS) + dims, lambda b, c: (b, 0, 0, 0))

    return pl.pallas_call(
        functools.partial(_mlstm_prompt_kernel, chunk=chunk),
        grid=(batch, ns),
        in_specs=[rows(ZM_WIDTH), rows(LANES), pl.BlockSpec((1, M_WIDTH), lambda b, c: (0, 0))],
        out_specs=[rows(M_WIDTH), state(M_DK, M_DV), state(1, M_DK), state(1, LANES)],
        out_shape=[jax.ShapeDtypeStruct((n, M_WIDTH), BF16),
                   jax.ShapeDtypeStruct((batch, M_HEADS, M_DK, M_DV), F32),
                   jax.ShapeDtypeStruct((batch, M_HEADS, 1, M_DK), F32),
                   jax.ShapeDtypeStruct((batch, M_HEADS, 1, LANES), F32)],
        scratch_shapes=[pltpu.VMEM((M_HEADS, 2 * LANES, M_DK), F32), pltpu.VMEM((M_HEADS, 1, LANES), F32)],
        compiler_params=pltpu.CompilerParams(dimension_semantics=("parallel", "arbitrary"),
                                             vmem_limit_bytes=VMEM_LIMIT),
        name="mlstm_prompt",
    )(zm, gates, gain)


SWA_BLOCKS_PER_STEP = 8


def _alibi_slope(head):
    return 2.0 ** (-8.0 * (head + 1) / A_HEADS)


def _swa_prompt_kernel(sinks_ref, qa_ref, kvc_ref, kvp_ref, ua_ref):
    hw = A_KV_HEADS * A_HD
    n_blocks = qa_ref.shape[0] // WINDOW
    qi = lax.broadcasted_iota(jnp.int32, (WINDOW, 2 * WINDOW), 0)
    kj = lax.broadcasted_iota(jnp.int32, (WINDOW, 2 * WINDOW), 1)
    dist = WINDOW + qi - kj
    band = (dist >= 0) & (dist < WINDOW)
    distf = dist.astype(F32)
    low = lax.broadcasted_iota(jnp.int32, (WINDOW, LANES), 1) < A_HD
    first = pl.program_id(1) == 0

    for j in range(n_blocks):
        rows = slice(j * WINDOW, (j + 1) * WINDOW)
        kvc = kvc_ref[rows, :]
        kvp = kvp_ref[...] if j == 0 else kvc_ref[(j - 1) * WINDOW:j * WINDOW, :]
        valid = band & ((kj >= WINDOW) | jnp.logical_not(first)) if j == 0 else band
        kk = jnp.concatenate([kvp[:, :hw], kvc[:, :hw]], axis=0).astype(BF16)
        vv = jnp.concatenate([kvp[:, hw:], kvc[:, hw:]], axis=0).astype(BF16)
        for p in range(A_GROUP):
            qp = qa_ref[rows, p * LANES:(p + 1) * LANES]
            outs = []
            for kvh in range(A_KV_HEADS):
                head = kvh * A_GROUP + p
                keep = low if kvh == 0 else jnp.logical_not(low)
                s = _dot_nt(jnp.where(keep, qp, jnp.zeros_like(qp)), kk)
                s = jnp.where(valid, s - _alibi_slope(head) * distf, -jnp.inf)
                sink = sinks_ref[head]
                mx = jnp.maximum(jnp.max(s, axis=-1, keepdims=True), sink)
                e = jnp.exp(s - mx)
                den = jnp.sum(e, axis=-1, keepdims=True) + jnp.exp(sink - mx)
                outs.append(_dot(e.astype(BF16), vv) / den)
            ua_ref[rows, p * LANES:(p + 1) * LANES] = jnp.where(low, outs[0], outs[1]).astype(BF16)


def _swa_prompt(sinks, qa, kva, batch):
    n = qa.shape[0]
    per_step = min(SWA_BLOCKS_PER_STEP, n // batch // WINDOW)
    rows = per_step * WINDOW
    ns = n // batch // rows

    def step_rows(width):
        return pl.BlockSpec((rows, width), lambda b, i: (b * ns + i, 0))

    def prev_block(b, i):
        return (jnp.maximum((b * ns + i) * per_step - 1, 0), 0)

    return pl.pallas_call(
        _swa_prompt_kernel,
        grid=(batch, ns),
        in_specs=[pl.BlockSpec(memory_space=pltpu.SMEM), step_rows(A_WIDTH), step_rows(KV_WIDTH),
                  pl.BlockSpec((WINDOW, KV_WIDTH), prev_block)],
        out_specs=step_rows(A_WIDTH),
        out_shape=jax.ShapeDtypeStruct((n, A_WIDTH), BF16),
        compiler_params=pltpu.CompilerParams(dimension_semantics=("parallel", "parallel"),
                                             vmem_limit_bytes=VMEM_LIMIT),
        name="swa_prompt",
    )(sinks, qa, kva, kva)


SAMPLE_ROWS = 128


def _mlstm_sample_kernel(q_ref, k_ref, v_ref, o_ref, gates_ref, gain_ref, c_ref, n_ref, m_ref,
                         um_ref, co_ref, no_ref, mo_ref,
                         qf_ref, qc_ref, nexp_ref, wkk_ref, dec_ref, *, steps):
    rows = SAMPLE_ROWS
    nb = rows // steps
    head = pl.program_id(1)
    row = lax.broadcasted_iota(jnp.int32, (rows, rows), 0)
    col = lax.broadcasted_iota(jnp.int32, (rows, rows), 1)
    same = (row // steps) == (col // steps)
    causal = same & (col <= row)
    eye = row == col

    def to_row(x_col):
        return jnp.sum(jnp.where(eye, x_col, 0.0), axis=0, keepdims=True)

    def pick(mask, x_row, fill, reduce):
        return reduce(jnp.where(mask, x_row, fill), axis=-1, keepdims=True)

    gates = gates_ref[...]
    lane = lax.broadcasted_iota(jnp.int32, (rows, LANES), 1)
    ig_col = jnp.sum(jnp.where(lane == head, gates, 0.0), axis=-1, keepdims=True)
    fg_col = jnp.sum(jnp.where(lane == head + M_HEADS, gates, 0.0), axis=-1, keepdims=True)
    lf_row = to_row(_log_sigmoid(fg_col))
    b_col = pick(causal, lf_row, 0.0, jnp.sum)
    b_row = to_row(b_col)
    ig_row = to_row(ig_col)
    m_prev = m_ref[0]

    d = jnp.where(causal, b_col - b_row + ig_row, -jnp.inf)
    inter = b_col + m_prev
    m_t = jnp.maximum(inter, jnp.max(d, axis=-1, keepdims=True))
    w = jnp.exp(d - m_t)
    a = jnp.exp(inter - m_t)

    q = q_ref[...]
    k = k_ref[...]
    v = v_ref[...]
    qf_ref[...] = q.astype(F32)
    for t in range(steps):
        nexp_ref[pl.ds(t, nb, stride=steps), :] = n_ref[...]

    sub = lax.broadcasted_iota(jnp.int32, (2 * steps, M_DV), 0)

    def qc_pair(j, carry):
        q8 = qf_ref[pl.ds(pl.multiple_of(j * 2 * steps, 2 * steps), 2 * steps), :].astype(BF16)
        r0 = _dot(q8, c_ref[2 * j, 0].astype(BF16))
        r1 = _dot(q8, c_ref[2 * j + 1, 0].astype(BF16))
        qc_ref[pl.ds(pl.multiple_of(j * 2 * steps, 2 * steps), 2 * steps), :] = jnp.where(sub < steps, r0, r1)
        return carry

    lax.fori_loop(0, nb // 2, qc_pair, 0, unroll=True)

    s = _dot_nt(q, k) * w
    qn = jnp.sum(qf_ref[...] * nexp_ref[...], axis=-1, keepdims=True)
    num = a * qc_ref[...] + _dot(s.astype(BF16), v)
    den = a * qn + jnp.sum(s, axis=-1, keepdims=True)
    hh = num / jnp.maximum(jnp.abs(den), jnp.exp(-m_t))
    um_ref[...] = _head_norm_gate(hh, o_ref[...], gain_ref[...])

    last = same & ((col % steps) == steps - 1)
    b_last = pick(last, b_row, 0.0, jnp.sum)
    g_col = b_last - b_col + ig_col
    g_max = pick(same, to_row(g_col), -jnp.inf, jnp.max)
    m_new = jnp.maximum(b_last + m_prev, g_max)
    decay = jnp.exp(b_last + m_prev - m_new)
    wk = jnp.exp(g_col - m_new)
    mo_ref[0] = m_new

    kf = k.astype(F32)
    wkk_ref[...] = wk * kf
    dec_ref[...] = jnp.broadcast_to(decay, (rows, M_DV))
    n_sum = wkk_ref[pl.ds(0, nb, stride=steps), :]
    for t in range(1, steps):
        n_sum = n_sum + wkk_ref[pl.ds(t, nb, stride=steps), :]
    no_ref[...] = dec_ref[pl.ds(0, nb, stride=steps), :] * n_ref[...] + n_sum

    k_t = kf.T
    wv = (wk * v.astype(F32)).astype(BF16)
    tok_batch = lax.broadcasted_iota(jnp.int32, (M_DK, rows), 1) // steps

    def update(b, carry):
        upd = _dot(jnp.where(tok_batch == b, k_t, 0.0).astype(BF16), wv)
        co_ref[b, 0] = dec_ref[pl.ds(b * steps, 1), :] * c_ref[b, 0] + upd
        return carry

    lax.fori_loop(0, nb, update, 0, unroll=True)


def _mlstm_sample(zm, gates, gain, c, n2, m_rows, steps):
    rows = zm.shape[0]
    batch = c.shape[0]
    nb = SAMPLE_ROWS // steps
    groups = rows // SAMPLE_ROWS

    def zcol(section):
        return pl.BlockSpec((SAMPLE_ROWS, M_DK), lambda g, h: (g, section * M_HEADS + h))

    c_spec = pl.BlockSpec((nb, 1, M_DK, M_DV), lambda g, h: (g, h, 0, 0))
    n_spec = pl.BlockSpec((nb, M_DK), lambda g, h: (g, h))
    m_spec = pl.BlockSpec((1, SAMPLE_ROWS, 1), lambda g, h: (h, g, 0))
    return pl.pallas_call(
        functools.partial(_mlstm_sample_kernel, steps=steps),
        grid=(groups, M_HEADS),
        in_specs=[zcol(0), zcol(1), zcol(2), zcol(3),
                  pl.BlockSpec((SAMPLE_ROWS, LANES), lambda g, h: (g, 0)),
                  pl.BlockSpec((1, M_DV), lambda g, h: (0, h)),
                  c_spec, n_spec, m_spec],
        out_specs=[pl.BlockSpec((SAMPLE_ROWS, M_DV), lambda g, h: (g, h)), c_spec, n_spec, m_spec],
        out_shape=[jax.ShapeDtypeStruct((rows, M_WIDTH), BF16),
                   jax.ShapeDtypeStruct(c.shape, F32),
                   jax.ShapeDtypeStruct(n2.shape, F32),
                   jax.ShapeDtypeStruct(m_rows.shape, F32)],
        scratch_shapes=[pltpu.VMEM((SAMPLE_ROWS, M_DK), F32),
                        pltpu.VMEM((SAMPLE_ROWS, M_DV), F32),
                        pltpu.VMEM((SAMPLE_ROWS, M_DK), F32),
                        pltpu.VMEM((SAMPLE_ROWS, M_DK), F32),
                        pltpu.VMEM((SAMPLE_ROWS, M_DV), F32)],
        compiler_params=pltpu.CompilerParams(dimension_semantics=("parallel", "parallel"),
                                             vmem_limit_bytes=VMEM_LIMIT),
        name="mlstm_sample",
    )(zm, zm, zm, zm, gates, gain, c, n2, m_rows)


def _swa_sample_kernel(sinks_ref, qa_ref, kva_ref, ckt_ref, cvt_ref, ua_ref, nkt_ref, nvt_ref,
                       s_ref, e_ref, pv_ref, *, steps):
    rows = SAMPLE_ROWS
    nb = rows // steps
    pair_rows = 2 * steps
    hw = A_KV_HEADS * A_HD
    tok = lax.broadcasted_iota(jnp.int32, (rows, LANES), 0)
    lane = lax.broadcasted_iota(jnp.int32, (rows, LANES), 1)
    low = lane < A_HD

    qf = qa_ref[...].astype(F32)
    q_blocks = []
    for p in range(A_GROUP):
        qp = qf[:, p * LANES:(p + 1) * LANES]
        q_blocks.append(jnp.where(low, qp, 0.0))
        q_blocks.append(jnp.where(low, 0.0, qp))
    n_blk = len(q_blocks)

    kv_new = kva_ref[...]
    kv_new_t = kv_new.T
    k_new_t = kv_new_t[:hw]
    v_new_t = kv_new_t[hw:]
    v_new = kv_new[:, hw:].astype(BF16)
    s_new = _dot(jnp.concatenate(q_blocks, axis=0).astype(BF16), k_new_t.astype(BF16))

    def pair_rows_of(blocks, j):
        return jnp.concatenate([blk[j * pair_rows:(j + 1) * pair_rows] for blk in blocks], axis=0)

    first = (lax.broadcasted_iota(jnp.int32, (n_blk * pair_rows, LANES), 0) % pair_rows) < steps

    for j in range(nb // 2):
        q_pair = pair_rows_of(q_blocks, j).astype(BF16)
        s_pair = jnp.where(first, _dot(q_pair, ckt_ref[2 * j].astype(BF16)),
                           _dot(q_pair, ckt_ref[2 * j + 1].astype(BF16)))
        for blk in range(n_blk):
            s_ref[pl.ds(blk * rows + j * pair_rows, pair_rows), :] = s_pair[blk * pair_rows:(blk + 1) * pair_rows]

    t_q = tok % steps
    dist_c = WINDOW + t_q - lane
    valid_c = dist_c < WINDOW
    dist_n = t_q - lane % steps
    valid_n = ((tok // steps) == (lane // steps)) & (dist_n >= 0)
    dist_cf = dist_c.astype(F32)
    dist_nf = dist_n.astype(F32)
    pv_new = []
    inv_den = []
    for blk in range(n_blk):
        head = (blk % A_KV_HEADS) * A_GROUP + blk // A_KV_HEADS
        slope = _alibi_slope(head)
        sink = sinks_ref[head]
        r = pl.ds(blk * rows, rows)
        s_c = jnp.where(valid_c, s_ref[r, :] - slope * dist_cf, -jnp.inf)
        s_n = jnp.where(valid_n, s_new[blk * rows:(blk + 1) * rows] - slope * dist_nf, -jnp.inf)
        mx = jnp.maximum(jnp.maximum(jnp.max(s_c, axis=-1, keepdims=True),
                                     jnp.max(s_n, axis=-1, keepdims=True)), sink)
        e_c = jnp.exp(s_c - mx)
        e_n = jnp.exp(s_n - mx)
        den = jnp.sum(e_c, axis=-1, keepdims=True) + jnp.sum(e_n, axis=-1, keepdims=True) + jnp.exp(sink - mx)
        e_ref[r, :] = e_c
        pv_new.append(_dot(e_n.astype(BF16), v_new))
        inv_den.append(1.0 / den)

    for j in range(nb // 2):
        e_pair = jnp.concatenate([e_ref[pl.ds(blk * rows + j * pair_rows, pair_rows), :] for blk in range(n_blk)],
                                 axis=0).astype(BF16)
        pv_pair = jnp.where(first, _dot_nt(e_pair, cvt_ref[2 * j].astype(BF16)),
                            _dot_nt(e_pair, cvt_ref[2 * j + 1].astype(BF16)))
        for blk in range(n_blk):
            pv_ref[pl.ds(blk * rows + j * pair_rows, pair_rows), :] = pv_pair[blk * pair_rows:(blk + 1) * pair_rows]

    for p in range(A_GROUP):
        halves = []
        for kvh in range(A_KV_HEADS):
            blk = p * A_KV_HEADS + kvh
            halves.append((pv_ref[pl.ds(blk * rows, rows), :] + pv_new[blk]) * inv_den[blk])
        ua_ref[:, p * LANES:(p + 1) * LANES] = jnp.where(low, halves[0], halves[1]).astype(BF16)

    tail = lane >= WINDOW - steps
    for b in range(nb):
        shift = (WINDOW - steps - b * steps) % rows
        nkt_ref[b] = jnp.where(tail, pltpu.roll(k_new_t, shift, 1), pltpu.roll(ckt_ref[b], WINDOW - steps, 1))
        nvt_ref[b] = jnp.where(tail, pltpu.roll(v_new_t, shift, 1), pltpu.roll(cvt_ref[b], WINDOW - steps, 1))


def _swa_sample(sinks, qa, kva, ckt, cvt, steps):
    batch = ckt.shape[0]
    assert ckt.shape[1:] == (A_KV_HEADS * A_HD, WINDOW) and SAMPLE_ROWS == WINDOW == LANES
    nb = SAMPLE_ROWS // steps
    cache = pl.BlockSpec((nb, LANES, WINDOW), lambda g: (g, 0, 0))

    def rows(width):
        return pl.BlockSpec((SAMPLE_ROWS, width), lambda g: (g, 0))

    stacked = (A_HEADS * SAMPLE_ROWS, LANES)
    return pl.pallas_call(
        functools.partial(_swa_sample_kernel, steps=steps),
        grid=(batch // nb,),
        in_specs=[pl.BlockSpec(memory_space=pltpu.SMEM), rows(A_WIDTH), rows(KV_WIDTH), cache, cache],
        out_specs=[rows(A_WIDTH), cache, cache],
        out_shape=[jax.ShapeDtypeStruct((batch * steps, A_WIDTH), BF16),
                   jax.ShapeDtypeStruct(ckt.shape, F32),
                   jax.ShapeDtypeStruct(cvt.shape, F32)],
        scratch_shapes=[pltpu.VMEM(stacked, F32), pltpu.VMEM(stacked, F32), pltpu.VMEM(stacked, F32)],
        compiler_params=pltpu.CompilerParams(dimension_semantics=("parallel",),
                                             vmem_limit_bytes=VMEM_LIMIT),
        name="swa_sample",
    )(sinks, qa, kva, ckt, cvt)


def _prep_proj(w_in, b_gate):
    m_end = 4 * M_WIDTH
    g_end = m_end + 2 * M_HEADS
    wt = w_in.T
    qa = wt[g_end:g_end + A_WIDTH].reshape(A_KV_HEADS, A_GROUP, A_HD, D_MODEL).transpose(1, 0, 2, 3)
    gates = jnp.pad(wt[m_end:g_end], ((0, LANES - 2 * M_HEADS), (0, 0)))
    win = jnp.concatenate([wt[:m_end], qa.reshape(A_WIDTH, D_MODEL), wt[g_end + A_WIDTH:], gates],
                          axis=0).astype(BF16)
    bias = jnp.pad(b_gate, (0, LANES - 2 * M_HEADS)).reshape(1, LANES)
    return win, bias


def _prep_out(w_out):
    wa = w_out[M_WIDTH:].reshape(A_KV_HEADS, A_GROUP, A_HD, D_MODEL).transpose(1, 0, 2, 3)
    return jnp.concatenate([w_out[:M_WIDTH], wa.reshape(A_WIDTH, D_MODEL)], axis=0).astype(BF16)


def kernel(x_prompt, x_sample, cache_swa_k, cache_swa_v, state_mlstm_C, state_mlstm_n, state_mlstm_m,
           norm_gains, ffn_w_gate, ffn_w_up, ffn_w_down, w_in, b_gate, mlstm_norm_gain, attn_sinks, w_out):
    assert norm_gains.shape[0] == 1, "single layer"
    batch, seq, _ = x_prompt.shape
    dec_batch, steps, _ = x_sample.shape
    gains = norm_gains[0]
    ffn = tuple(w.astype(BF16) for w in (ffn_w_gate, ffn_w_up, ffn_w_down))
    win, bias = _prep_proj(w_in[0], b_gate[0])
    wo = _prep_out(w_out[0])
    gain = mlstm_norm_gain[0].reshape(1, M_WIDTH)
    sinks = attn_sinks[0]
    kv_lanes = A_KV_HEADS * A_HD

    chunk = min(MLSTM_CHUNK, seq)
    assert seq % chunk == 0
    xp, zm, qa, kva, gates = _ffn_proj(x_prompt.reshape(batch * seq, D_MODEL), gains, *ffn, win, bias,
                                       0, 0, 1, 2, chunk)
    um, p_c, p_n, p_m = _mlstm_prompt(zm, gates, gain, batch, chunk)
    ua = _swa_prompt(sinks, qa, kva, batch)
    yp = _mix_ffn(xp, um, ua, wo, gains, *ffn, 1, 3, 4, 5).reshape(batch, seq, D_MODEL)
    keep = min(WINDOW, seq)
    kv_tail = kva.reshape(batch, seq, KV_WIDTH)[:, seq - keep:]
    p_k = kv_tail[..., :kv_lanes].reshape(1, batch, keep, A_KV_HEADS, A_HD)
    p_v = kv_tail[..., kv_lanes:].reshape(1, batch, keep, A_KV_HEADS, A_HD)
    p_c = p_c[None]
    p_n = p_n[None, :, :, 0, :]
    p_m = p_m[None, :, :, 0, 0]

    xs, zm_s, qa_s, kva_s, gates_s = _ffn_proj(x_sample.reshape(dec_batch * steps, D_MODEL), gains, *ffn,
                                               win, bias, 0, 0, 1, 2, min(chunk, dec_batch * steps))
    m_rows = jnp.repeat(state_mlstm_m[0].T, steps, axis=1)[..., None]
    um_s, s_c, s_n2, s_m_rows = _mlstm_sample(zm_s, gates_s, gain, state_mlstm_C[0],
                                              state_mlstm_n[0].reshape(dec_batch, M_HEADS * M_DK),
                                              m_rows, steps)
    wc = cache_swa_k.shape[2]
    ua_s, s_kt, s_vt = _swa_sample(sinks, qa_s, kva_s,
                                   cache_swa_k[0].reshape(dec_batch, wc, kv_lanes).transpose(0, 2, 1),
                                   cache_swa_v[0].reshape(dec_batch, wc, kv_lanes).transpose(0, 2, 1), steps)
    ys = _mix_ffn(xs, um_s, ua_s, wo, gains, *ffn, 1, 3, 4, 5).reshape(dec_batch, steps, D_MODEL)
    s_n = s_n2.reshape(1, dec_batch, M_HEADS, M_DK)
    s_m = s_m_rows[:, ::steps, 0].T[None]
    s_k = s_kt.transpose(0, 2, 1).reshape(1, dec_batch, wc, A_KV_HEADS, A_HD)
    s_v = s_vt.transpose(0, 2, 1).reshape(1, dec_batch, wc, A_KV_HEADS, A_HD)
    return (yp, ys, p_k, p_v, p_c, p_n, p_m, s_k, s_v, s_c[None], s_n, s_m)
```

```python
import functools

import jax
import jax.numpy as jnp
from jax import lax
from jax.experimental import pallas as pl
from jax.experimental.pallas import tpu as pltpu

F32 = jnp.float32
BF16 = jnp.bfloat16

D_MODEL = 1024
D_FF = 2816
M_HEADS = 4
M_DK = 128
M_DV = 128
M_WIDTH = M_HEADS * M_DV
A_HEADS = 8
A_KV_HEADS = 2
A_GROUP = A_HEADS // A_KV_HEADS
A_HD = 64
A_WIDTH = A_HEADS * A_HD
WINDOW = 128
RMS_EPS = 1e-6

LANES = 128
FFN_CHUNK = 256
N_FFN_CHUNKS = D_FF // FFN_CHUNK
FFN_STAGE_SLOTS = 2
TOKEN_TILE = 512
MLSTM_CHUNK = 256
ZM_WIDTH = 4 * M_WIDTH
KV_WIDTH = 2 * A_KV_HEADS * A_HD
VMEM_LIMIT = 56 * 1024 * 1024
MLSTM_CHUNKS_PER_STEP = 8
GATE_I, GATE_F, GATE_B, GATE_X = 0, M_HEADS, 2 * M_HEADS, 3 * M_HEADS

NT_DIMS = (((1,), (1,)), ((), ()))


def _rms(x, g):
    ms = jnp.mean(x * x, axis=-1, keepdims=True)
    return x * lax.rsqrt(ms + RMS_EPS) * g


def _log_sigmoid(x):
    return jnp.minimum(x, 0.0) - jnp.log1p(jnp.exp(-jnp.abs(x)))


def _dot(a, b):
    return jnp.dot(a, b, preferred_element_type=F32)


def _dot_nt(a, b):
    return lax.dot_general(a, b, NT_DIMS, preferred_element_type=F32)


def _resident(shape):
    nd = len(shape)
    return pl.BlockSpec(shape, lambda *_: (0,) * nd, pipeline_mode=pl.Buffered(1))


def _ffn_weight_copies(w_hbm, j, f, slot, stage_refs, sems):
    wg_hbm, wu_hbm, wd_hbm = w_hbm
    cols = pl.ds(f * FFN_CHUNK, FFN_CHUNK)
    sources = (wg_hbm.at[0, j, :, cols], wu_hbm.at[0, j, :, cols], wd_hbm.at[0, j, cols, :])
    return [pltpu.make_async_copy(src, stage.at[slot], sems.at[slot, k])
            for k, (src, stage) in enumerate(zip(sources, stage_refs))]


def _swiglu(x, gpre, w_refs, h_ref, acc_ref, fetch=None):
    wg_ref, wu_ref, wd_ref = w_refs
    if fetch is not None:
        for f in range(min(FFN_STAGE_SLOTS, N_FFN_CHUNKS)):
            for copy in _ffn_weight_copies(*fetch[:2], f, f % FFN_STAGE_SLOTS, *fetch[2:]):
                copy.start()
    h_ref[...] = _rms(x, gpre).astype(BF16)
    for f in range(N_FFN_CHUNKS):
        cols = slice(f * FFN_CHUNK, (f + 1) * FFN_CHUNK)
        if fetch is not None:
            slot = f % FFN_STAGE_SLOTS
            stage_g, stage_u, stage_d = fetch[2]
            for copy in _ffn_weight_copies(*fetch[:2], f, slot, *fetch[2:]):
                copy.wait()
            wg_ref[:, cols] = stage_g[slot].astype(BF16)
            wu_ref[:, cols] = stage_u[slot].astype(BF16)
            wd_ref[cols, :] = stage_d[slot].astype(BF16)
            if f + FFN_STAGE_SLOTS < N_FFN_CHUNKS:
                for copy in _ffn_weight_copies(*fetch[:2], f + FFN_STAGE_SLOTS, slot, *fetch[2:]):
                    copy.start()
        h = h_ref[...]
        g = _dot(h, wg_ref[:, cols])
        u = _dot(h, wu_ref[:, cols])
        part = _dot((g * jax.nn.sigmoid(g) * u).astype(BF16), wd_ref[cols, :])
        if f == 0:
            acc_ref[...] = part
        else:
            acc_ref[...] += part


def _swiglu_first_step_fetches(x, gpre, w_hbm, j, w_refs, stage_refs, sems, h_ref, acc_ref):
    first = pl.program_id(0) == 0

    @pl.when(first)
    def _():
        _swiglu(x, gpre, w_refs, h_ref, acc_ref, fetch=(w_hbm, j, stage_refs, sems))

    @pl.when(jnp.logical_not(first))
    def _():
        _swiglu(x, gpre, w_refs, h_ref, acc_ref)


def _half_step(x, acc_ref, gpost):
    return x + _rms(acc_ref[...], 0.5 * gpost)


def _project(x, gain, win_ref, bias_ref, h_ref, zm_ref, qa_ref, kva_ref, gates_ref, chunk):
    h_ref[...] = _rms(x, gain).astype(BF16)

    def cols(lo, width):
        return _dot_nt(h_ref[...], win_ref[lo:lo + width, :])

    gates = cols(ZM_WIDTH + A_WIDTH + KV_WIDTH, LANES) + bias_ref[...]
    lane = lax.broadcasted_iota(jnp.int32, (chunk, LANES), 1)
    pos = lax.broadcasted_iota(jnp.int32, (chunk, LANES), 0)
    for c0 in range(0, gates.shape[0], chunk):
        g = gates[c0:c0 + chunk]
        b = jnp.where((lane >= GATE_F) & (lane < GATE_B), _log_sigmoid(g), 0.0)
        shift = 1
        while shift < chunk:
            b = b + jnp.where(pos >= shift, pltpu.roll(b, shift, 0), 0.0)
            shift *= 2
        x_gate = pltpu.roll(g, GATE_X - GATE_I, 1) - pltpu.roll(b, GATE_X - GATE_F, 1)
        out = jnp.where(lane < GATE_B, g, jnp.where(lane < GATE_X, pltpu.roll(b, GATE_B - GATE_F, 1), x_gate))
        gates_ref[c0:c0 + chunk, :] = jnp.where(lane < GATE_X + M_HEADS, out, 0.0)

    zm_ref[:, 0:M_WIDTH] = cols(0, M_WIDTH).astype(BF16)
    zm_ref[:, M_WIDTH:2 * M_WIDTH] = (cols(M_WIDTH, M_WIDTH) * (M_DK ** -0.5)).astype(BF16)
    zm_ref[:, 2 * M_WIDTH:3 * M_WIDTH] = cols(2 * M_WIDTH, M_WIDTH).astype(BF16)
    zm_ref[:, 3 * M_WIDTH:] = jax.nn.sigmoid(cols(3 * M_WIDTH, M_WIDTH)).astype(BF16)
    qa_ref[...] = (cols(ZM_WIDTH, A_WIDTH) * (A_HD ** -0.5)).astype(BF16)
    kva_ref[...] = cols(ZM_WIDTH + A_WIDTH, KV_WIDTH)


def _ffn_proj_kernel(x_ref, gains_ref, wg_hbm, wu_hbm, wd_hbm, win_ref, bias_ref,
                     x1_ref, zm_ref, qa_ref, kva_ref, gates_ref,
                     h_ref, acc_ref, wg_ref, wu_ref, wd_ref, stage_g, stage_u, stage_d, sems,
                     *, j, pre, post, mixer, chunk):
    x = x_ref[...]
    _swiglu_first_step_fetches(x, gains_ref[pre:pre + 1, :], (wg_hbm, wu_hbm, wd_hbm), j,
                               (wg_ref, wu_ref, wd_ref), (stage_g, stage_u, stage_d), sems, h_ref, acc_ref)
    x1 = _half_step(x, acc_ref, gains_ref[post:post + 1, :])
    x1_ref[...] = x1
    _project(x1, gains_ref[mixer:mixer + 1, :], win_ref, bias_ref, h_ref, zm_ref, qa_ref, kva_ref, gates_ref,
             chunk)


def _mix_ffn_kernel(x_ref, um_ref, ua_ref, wo_ref, gains_ref, wg_hbm, wu_hbm, wd_hbm, o_ref,
                    h_ref, acc_ref, wg_ref, wu_ref, wd_ref, stage_g, stage_u, stage_d, sems,
                    *, j, mix, pre, post):
    u = jnp.concatenate([um_ref[...], ua_ref[...]], axis=-1)
    x = x_ref[...] + _rms(_dot(u, wo_ref[...]), gains_ref[mix:mix + 1, :])
    _swiglu_first_step_fetches(x, gains_ref[pre:pre + 1, :], (wg_hbm, wu_hbm, wd_hbm), j,
                               (wg_ref, wu_ref, wd_ref), (stage_g, stage_u, stage_d), sems, h_ref, acc_ref)
    o_ref[...] = _half_step(x, acc_ref, gains_ref[post:post + 1, :])


def _ffn_scratch(tm):
    return [pltpu.VMEM((tm, D_MODEL), BF16),
            pltpu.VMEM((tm, D_MODEL), F32),
            pltpu.VMEM((D_MODEL, D_FF), BF16),
            pltpu.VMEM((D_MODEL, D_FF), BF16),
            pltpu.VMEM((D_FF, D_MODEL), BF16),
            pltpu.VMEM((FFN_STAGE_SLOTS, D_MODEL, FFN_CHUNK), F32),
            pltpu.VMEM((FFN_STAGE_SLOTS, D_MODEL, FFN_CHUNK), F32),
            pltpu.VMEM((FFN_STAGE_SLOTS, FFN_CHUNK, D_MODEL), F32),
            pltpu.SemaphoreType.DMA((FFN_STAGE_SLOTS, 3))]


def _dense_params():
    return pltpu.CompilerParams(dimension_semantics=("arbitrary",), vmem_limit_bytes=VMEM_LIMIT)


_HBM = pl.BlockSpec(memory_space=pl.ANY)


def _ffn_proj(x, gains, wg, wu, wd, win, bias, j, pre, post, mixer, chunk):
    n = x.shape[0]
    tm = min(TOKEN_TILE, n)
    assert tm % chunk == 0

    def rows(width):
        return pl.BlockSpec((tm, width), lambda i: (i, 0))

    return pl.pallas_call(
        functools.partial(_ffn_proj_kernel, j=j, pre=pre, post=post, mixer=mixer, chunk=chunk),
        grid=(n // tm,),
        in_specs=[rows(D_MODEL), _resident(gains.shape), _HBM, _HBM, _HBM,
                  _resident(win.shape), _resident(bias.shape)],
        out_specs=[rows(D_MODEL), rows(ZM_WIDTH), rows(A_WIDTH), rows(KV_WIDTH), rows(LANES)],
        out_shape=[jax.ShapeDtypeStruct((n, D_MODEL), F32),
                   jax.ShapeDtypeStruct((n, ZM_WIDTH), BF16),
                   jax.ShapeDtypeStruct((n, A_WIDTH), BF16),
                   jax.ShapeDtypeStruct((n, KV_WIDTH), F32),
                   jax.ShapeDtypeStruct((n, LANES), F32)],
        scratch_shapes=_ffn_scratch(tm),
        compiler_params=_dense_params(),
        name="ffn_half_step_input_projection",
    )(x, gains, wg, wu, wd, win, bias)


def _mix_ffn(x, um, ua, wo, gains, wg, wu, wd, j, mix, pre, post):
    n = x.shape[0]
    tm = min(TOKEN_TILE, n)
    row = pl.BlockSpec((tm, D_MODEL), lambda i: (i, 0))
    half = pl.BlockSpec((tm, M_WIDTH), lambda i: (i, 0))
    return pl.pallas_call(
        functools.partial(_mix_ffn_kernel, j=j, mix=mix, pre=pre, post=post),
        grid=(n // tm,),
        in_specs=[row, half, half, _resident(wo.shape), _resident(gains.shape), _HBM, _HBM, _HBM],
        out_specs=row,
        out_shape=jax.ShapeDtypeStruct((n, D_MODEL), F32),
        scratch_shapes=_ffn_scratch(tm),
        compiler_params=_dense_params(),
        name="out_proj_ffn_half_step",
    )(x, um, ua, wo, gains, wg, wu, wd)


def _head_norm_gate(hh, o, gain):
    hn = hh * lax.rsqrt(jnp.mean(hh * hh, axis=-1, keepdims=True) + RMS_EPS) * gain
    return (o.astype(F32) * hn).astype(BF16)


def _mlstm_prompt_kernel(zm_ref, gates_ref, gain_ref, um_ref, c_ref, n_ref, m_ref, cnt_ref, ms_ref, *, chunk):
    step = pl.program_id(1)

    @pl.when(step == 0)
    def _():
        cnt_ref[...] = jnp.zeros_like(cnt_ref)
        ms_ref[...] = jnp.zeros_like(ms_ref)

    si = lax.broadcasted_iota(jnp.int32, (chunk, chunk), 0)
    ti = lax.broadcasted_iota(jnp.int32, (chunk, chunk), 1)
    causal = si <= ti
    ones_rows = jnp.where(lax.broadcasted_iota(jnp.int32, (LANES, chunk), 0) == 0, 1.0, 0.0).astype(BF16)

    for c in range(zm_ref.shape[0] // chunk):
        rows = slice(c * chunk, (c + 1) * chunk)
        gates = gates_ref[rows, :]
        gates_t = gates.T
        for h in range(M_HEADS):
            lanes = slice(h * M_DK, (h + 1) * M_DK)
            q = zm_ref[rows, lanes]
            k = zm_ref[rows, M_WIDTH + h * M_DK:M_WIDTH + (h + 1) * M_DK]
            v = zm_ref[rows, 2 * M_WIDTH + h * M_DV:2 * M_WIDTH + (h + 1) * M_DV]
            o = zm_ref[rows, 3 * M_WIDTH + h * M_DV:3 * M_WIDTH + (h + 1) * M_DV]
            x_col = gates[:, GATE_X + h:GATE_X + h + 1]
            ig_row = gates_t[GATE_I + h:GATE_I + h + 1, :]
            b_row = gates_t[GATE_B + h:GATE_B + h + 1, :]
            m_prev = ms_ref[h][:, 0:1]
            cnt = cnt_ref[h]

            xm = jnp.where(causal, x_col, -jnp.inf)
            mu = jnp.maximum(m_prev, jnp.max(xm, axis=0, keepdims=True))
            a = jnp.exp(m_prev - mu)
            s_t = _dot_nt(k, q) * jnp.exp(xm - mu)
            v_ext = jnp.concatenate([v.astype(F32).T.astype(BF16), ones_rows], axis=0)
            nd = a * _dot_nt(cnt.astype(BF16), q) + _dot(v_ext, s_t.astype(BF16))
            den = nd[M_DV:M_DV + 1, :]
            hh = nd[:M_DV, :] * (1.0 / jnp.maximum(jnp.abs(den), jnp.exp(-(b_row + mu))))
            hn = hh * lax.rsqrt(jnp.mean(hh * hh, axis=0, keepdims=True) + RMS_EPS)
            um_ref[rows, lanes] = (hn.T * gain_ref[:, lanes] * o.astype(F32)).astype(BF16)

            b_last = b_row[:, chunk - 1:chunk]
            g = b_last - b_row + ig_row
            m_new = jnp.maximum(b_last + m_prev, jnp.max(g, axis=-1, keepdims=True))
            decay = jnp.exp(b_last + m_prev - m_new)
            wv = (jnp.exp(g - m_new) * v_ext.astype(F32)).astype(BF16)
            cnt_ref[h] = decay * cnt + _dot(wv, k)
            ms_ref[h] = jnp.broadcast_to(m_new, (1, LANES))

    @pl.when(step == pl.num_programs(1) - 1)
    def _():
        for h in range(M_HEADS):
            c_ref[0, h] = cnt_ref[h][:M_DV, :].T
            n_ref[0, h] = cnt_ref[h][M_DV:M_DV + 1, :]
            m_ref[0, h] = ms_ref[h]


def _mlstm_prompt(zm, gates, gain, batch, seq, chunk):
    n = batch * seq
    per_step = min(MLSTM_CHUNKS_PER_STEP, n // batch // chunk)
    step_rows = per_step * chunk
    ns = n // batch // step_rows

    def rows(width):
        return pl.BlockSpec((step_rows, width), lambda b, c: (b * ns + c, 0))

    def state(*dims):
        return pl.BlockSpec((1, M_HEADS) + dims, lambda b, c: (b, 0, 0, 0))

    return pl.pallas_call(
        functools.partial(_mlstm_prompt_kernel, chunk=chunk),
        grid=(batch, ns),
        in_specs=[rows(ZM_WIDTH), rows(LANES), pl.BlockSpec((1, M_WIDTH), lambda b, c: (0, 0))],
        out_specs=[rows(M_WIDTH), state(M_DK, M_DV), state(1, M_DK), state(1, LANES)],
        out_shape=[jax.ShapeDtypeStruct((n, M_WIDTH), BF16),
                   jax.ShapeDtypeStruct((batch, M_HEADS, M_DK, M_DV), F32),
                   jax.ShapeDtypeStruct((batch, M_HEADS, 1, M_DK), F32),
                   jax.ShapeDtypeStruct((batch, M_HEADS, 1, LANES), F32)],
        scratch_shapes=[pltpu.VMEM((M_HEADS, 2 * LANES, M_DK), F32), pltpu.VMEM((M_HEADS, 1, LANES), F32)],
        compiler_params=pltpu.CompilerParams(dimension_semantics=("parallel", "arbitrary"),
                                             vmem_limit_bytes=VMEM_LIMIT),
        name="mlstm_prompt",
    )(zm, gates, gain)


SWA_BLOCKS_PER_STEP = 8


def _alibi_slope(head):
    return 2.0 ** (-8.0 * (head + 1) / A_HEADS)


def _swa_prompt_kernel(sinks_ref, qa_ref, kvc_ref, kvp_ref, ua_ref):
    hw = A_KV_HEADS * A_HD
    n_blocks = qa_ref.shape[0] // WINDOW
    qi = lax.broadcasted_iota(jnp.int32, (WINDOW, 2 * WINDOW), 0)
    kj = lax.broadcasted_iota(jnp.int32, (WINDOW, 2 * WINDOW), 1)
    dist = WINDOW + qi - kj
    band = (dist >= 0) & (dist < WINDOW)
    distf = dist.astype(F32)
    low = lax.broadcasted_iota(jnp.int32, (WINDOW, LANES), 1) < A_HD
    first = pl.program_id(1) == 0

    for j in range(n_blocks):
        rows = slice(j * WINDOW, (j + 1) * WINDOW)
        kvc = kvc_ref[rows, :]
        kvp = kvp_ref[...] if j == 0 else kvc_ref[(j - 1) * WINDOW:j * WINDOW, :]
        valid = band & ((kj >= WINDOW) | jnp.logical_not(first)) if j == 0 else band
        kk = jnp.concatenate([kvp[:, :hw], kvc[:, :hw]], axis=0).astype(BF16)
        vv = jnp.concatenate([kvp[:, hw:], kvc[:, hw:]], axis=0).astype(BF16)
        for p in range(A_GROUP):
            qp = qa_ref[rows, p * LANES:(p + 1) * LANES]
            outs = []
            for kvh in range(A_KV_HEADS):
                head = kvh * A_GROUP + p
                keep = low if kvh == 0 else jnp.logical_not(low)
                s = _dot_nt(jnp.where(keep, qp, jnp.zeros_like(qp)), kk)
                s = jnp.where(valid, s - _alibi_slope(head) * distf, -jnp.inf)
                sink = sinks_ref[head]
                mx = jnp.maximum(jnp.max(s, axis=-1, keepdims=True), sink)
                e = jnp.exp(s - mx)
                den = jnp.sum(e, axis=-1, keepdims=True) + jnp.exp(sink - mx)
                outs.append(_dot(e.astype(BF16), vv) / den)
            ua_ref[rows, p * LANES:(p + 1) * LANES] = jnp.where(low, outs[0], outs[1]).astype(BF16)


def _swa_prompt(sinks, qa, kva, batch, seq):
    n = batch * seq
    per_step = min(SWA_BLOCKS_PER_STEP, n // batch // WINDOW)
    rows = per_step * WINDOW
    ns = n // batch // rows

    def step_rows(width):
        return pl.BlockSpec((rows, width), lambda b, i: (b * ns + i, 0))

    def prev_block(b, i):
        return (jnp.maximum((b * ns + i) * per_step - 1, 0), 0)

    return pl.pallas_call(
        _swa_prompt_kernel,
        grid=(batch, ns),
        in_specs=[pl.BlockSpec(memory_space=pltpu.SMEM), step_rows(A_WIDTH), step_rows(KV_WIDTH),
                  pl.BlockSpec((WINDOW, KV_WIDTH), prev_block)],
        out_specs=step_rows(A_WIDTH),
        out_shape=jax.ShapeDtypeStruct((n, A_WIDTH), BF16),
        compiler_params=pltpu.CompilerParams(dimension_semantics=("parallel", "parallel"),
                                             vmem_limit_bytes=VMEM_LIMIT),
        name="swa_prompt",
    )(sinks, qa, kva, kva)


SAMPLE_ROWS = 128


def _mlstm_sample_kernel(q_ref, k_ref, v_ref, o_ref, gates_ref, gain_ref, c_ref, n_ref, m_ref,
                         um_ref, co_ref, no_ref, mo_ref,
                         qf_ref, qc_ref, nexp_ref, wkk_ref, dec_ref, *, steps):
    rows = SAMPLE_ROWS
    nb = rows // steps
    head = pl.program_id(1)
    row = lax.broadcasted_iota(jnp.int32, (rows, rows), 0)
    col = lax.broadcasted_iota(jnp.int32, (rows, rows), 1)
    same = (row // steps) == (col // steps)
    causal = same & (col <= row)
    eye = row == col

    def to_row(x_col):
        return jnp.sum(jnp.where(eye, x_col, 0.0), axis=0, keepdims=True)

    def pick(mask, x_row, fill, reduce):
        return reduce(jnp.where(mask, x_row, fill), axis=-1, keepdims=True)

    gates = gates_ref[...]
    lane = lax.broadcasted_iota(jnp.int32, (rows, LANES), 1)
    ig_col = jnp.sum(jnp.where(lane == head, gates, 0.0), axis=-1, keepdims=True)
    fg_col = jnp.sum(jnp.where(lane == head + M_HEADS, gates, 0.0), axis=-1, keepdims=True)
    lf_row = to_row(_log_sigmoid(fg_col))
    b_col = pick(causal, lf_row, 0.0, jnp.sum)
    b_row = to_row(b_col)
    ig_row = to_row(ig_col)
    m_prev = m_ref[0]

    d = jnp.where(causal, b_col - b_row + ig_row, -jnp.inf)
    inter = b_col + m_prev
    m_t = jnp.maximum(inter, jnp.max(d, axis=-1, keepdims=True))
    w = jnp.exp(d - m_t)
    a = jnp.exp(inter - m_t)

    q = q_ref[...]
    k = k_ref[...]
    v = v_ref[...]
    qf_ref[...] = q.astype(F32)
    for t in range(steps):
        nexp_ref[pl.ds(t, nb, stride=steps), :] = n_ref[...]

    sub = lax.broadcasted_iota(jnp.int32, (2 * steps, M_DV), 0)

    def qc_pair(j, carry):
        q8 = qf_ref[pl.ds(pl.multiple_of(j * 2 * steps, 2 * steps), 2 * steps), :].astype(BF16)
        r0 = _dot(q8, c_ref[2 * j, 0].astype(BF16))
        r1 = _dot(q8, c_ref[2 * j + 1, 0].astype(BF16))
        qc_ref[pl.ds(pl.multiple_of(j * 2 * steps, 2 * steps), 2 * steps), :] = jnp.where(sub < steps, r0, r1)
        return carry

    lax.fori_loop(0, nb // 2, qc_pair, 0, unroll=True)

    s = _dot_nt(q, k) * w
    qn = jnp.sum(qf_ref[...] * nexp_ref[...], axis=-1, keepdims=True)
    num = a * qc_ref[...] + _dot(s.astype(BF16), v)
    den = a * qn + jnp.sum(s, axis=-1, keepdims=True)
    hh = num / jnp.maximum(jnp.abs(den), jnp.exp(-m_t))
    um_ref[...] = _head_norm_gate(hh, o_ref[...], gain_ref[...])

    last = same & ((col % steps) == steps - 1)
    b_last = pick(last, b_row, 0.0, jnp.sum)
    g_col = b_last - b_col + ig_col
    g_max = pick(same, to_row(g_col), -jnp.inf, jnp.max)
    m_new = jnp.maximum(b_last + m_prev, g_max)
    decay = jnp.exp(b_last + m_prev - m_new)
    wk = jnp.exp(g_col - m_new)
    mo_ref[0] = m_new

    kf = k.astype(F32)
    wkk_ref[...] = wk * kf
    dec_ref[...] = jnp.broadcast_to(decay, (rows, M_DV))
    n_sum = wkk_ref[pl.ds(0, nb, stride=steps), :]
    for t in range(1, steps):
        n_sum = n_sum + wkk_ref[pl.ds(t, nb, stride=steps), :]
    no_ref[...] = dec_ref[pl.ds(0, nb, stride=steps), :] * n_ref[...] + n_sum

    k_t = kf.T
    wv = (wk * v.astype(F32)).astype(BF16)
    tok_batch = lax.broadcasted_iota(jnp.int32, (M_DK, rows), 1) // steps

    def update(b, carry):
        upd = _dot(jnp.where(tok_batch == b, k_t, 0.0).astype(BF16), wv)
        co_ref[b, 0] = dec_ref[pl.ds(b * steps, 1), :] * c_ref[b, 0] + upd
        return carry

    lax.fori_loop(0, nb, update, 0, unroll=True)


def _mlstm_sample(zm, gates, gain, c, n2, m_rows, steps):
    rows = zm.shape[0]
    batch = c.shape[0]
    nb = SAMPLE_ROWS // steps
    groups = rows // SAMPLE_ROWS

    def zcol(section):
        return pl.BlockSpec((SAMPLE_ROWS, M_DK), lambda g, h: (g, section * M_HEADS + h))

    c_spec = pl.BlockSpec((nb, 1, M_DK, M_DV), lambda g, h: (g, h, 0, 0))
    n_spec = pl.BlockSpec((nb, M_DK), lambda g, h: (g, h))
    m_spec = pl.BlockSpec((1, SAMPLE_ROWS, 1), lambda g, h: (h, g, 0))
    return pl.pallas_call(
        functools.partial(_mlstm_sample_kernel, steps=steps),
        grid=(groups, M_HEADS),
        in_specs=[zcol(0), zcol(1), zcol(2), zcol(3),
                  pl.BlockSpec((SAMPLE_ROWS, LANES), lambda g, h: (g, 0)),
                  pl.BlockSpec((1, M_DV), lambda g, h: (0, h)),
                  c_spec, n_spec, m_spec],
        out_specs=[pl.BlockSpec((SAMPLE_ROWS, M_DV), lambda g, h: (g, h)), c_spec, n_spec, m_spec],
        out_shape=[jax.ShapeDtypeStruct((rows, M_WIDTH), BF16),
                   jax.ShapeDtypeStruct(c.shape, F32),
                   jax.ShapeDtypeStruct(n2.shape, F32),
                   jax.ShapeDtypeStruct(m_rows.shape, F32)],
        scratch_shapes=[pltpu.VMEM((SAMPLE_ROWS, M_DK), F32),
                        pltpu.VMEM((SAMPLE_ROWS, M_DV), F32),
                        pltpu.VMEM((SAMPLE_ROWS, M_DK), F32),
                        pltpu.VMEM((SAMPLE_ROWS, M_DK), F32),
                        pltpu.VMEM((SAMPLE_ROWS, M_DV), F32)],
        compiler_params=pltpu.CompilerParams(dimension_semantics=("parallel", "parallel"),
                                             vmem_limit_bytes=VMEM_LIMIT),
        name="mlstm_sample",
    )(zm, zm, zm, zm, gates, gain, c, n2, m_rows)


def _swa_sample_kernel(sinks_ref, qa_ref, kva_ref, ckt_ref, cvt_ref, ua_ref, nkt_ref, nvt_ref,
                       s_ref, e_ref, pv_ref, *, steps):
    rows = SAMPLE_ROWS
    nb = rows // steps
    pair_rows = 2 * steps
    hw = A_KV_HEADS * A_HD
    tok = lax.broadcasted_iota(jnp.int32, (rows, LANES), 0)
    lane = lax.broadcasted_iota(jnp.int32, (rows, LANES), 1)
    low = lane < A_HD

    qf = qa_ref[...].astype(F32)
    q_blocks = []
    for p in range(A_GROUP):
        qp = qf[:, p * LANES:(p + 1) * LANES]
        q_blocks.append(jnp.where(low, qp, 0.0))
        q_blocks.append(jnp.where(low, 0.0, qp))
    n_blk = len(q_blocks)

    kv_new = kva_ref[...]
    kv_new_t = kv_new.T
    k_new_t = kv_new_t[:hw]
    v_new_t = kv_new_t[hw:]
    v_new = kv_new[:, hw:].astype(BF16)
    s_new = _dot(jnp.concatenate(q_blocks, axis=0).astype(BF16), k_new_t.astype(BF16))

    def pair_rows_of(blocks, j):
        return jnp.concatenate([blk[j * pair_rows:(j + 1) * pair_rows] for blk in blocks], axis=0)

    first = (lax.broadcasted_iota(jnp.int32, (n_blk * pair_rows, LANES), 0) % pair_rows) < steps

    for j in range(nb // 2):
        q_pair = pair_rows_of(q_blocks, j).astype(BF16)
        s_pair = jnp.where(first, _dot(q_pair, ckt_ref[2 * j].astype(BF16)),
                           _dot(q_pair, ckt_ref[2 * j + 1].astype(BF16)))
        for blk in range(n_blk):
            s_ref[pl.ds(blk * rows + j * pair_rows, pair_rows), :] = s_pair[blk * pair_rows:(blk + 1) * pair_rows]

    t_q = tok % steps
    dist_c = WINDOW + t_q - lane
    valid_c = dist_c < WINDOW
    dist_n = t_q - lane % steps
    valid_n = ((tok // steps) == (lane // steps)) & (dist_n >= 0)
    dist_cf = dist_c.astype(F32)
    dist_nf = dist_n.astype(F32)
    pv_new = []
    inv_den = []
    for blk in range(n_blk):
        head = (blk % A_KV_HEADS) * A_GROUP + blk // A_KV_HEADS
        slope = _alibi_slope(head)
        sink = sinks_ref[head]
        r = pl.ds(blk * rows, rows)
        s_c = jnp.where(valid_c, s_ref[r, :] - slope * dist_cf, -jnp.inf)
        s_n = jnp.where(valid_n, s_new[blk * rows:(blk + 1) * rows] - slope * dist_nf, -jnp.inf)
        mx = jnp.maximum(jnp.maximum(jnp.max(s_c, axis=-1, keepdims=True),
                                     jnp.max(s_n, axis=-1, keepdims=True)), sink)
        e_c = jnp.exp(s_c - mx)
        e_n = jnp.exp(s_n - mx)
        den = jnp.sum(e_c, axis=-1, keepdims=True) + jnp.sum(e_n, axis=-1, keepdims=True) + jnp.exp(sink - mx)
        e_ref[r, :] = e_c
        pv_new.append(_dot(e_n.astype(BF16), v_new))
        inv_den.append(1.0 / den)

    for j in range(nb // 2):
        e_pair = jnp.concatenate([e_ref[pl.ds(blk * rows + j * pair_rows, pair_rows), :] for blk in range(n_blk)],
                                 axis=0).astype(BF16)
        pv_pair = jnp.where(first, _dot_nt(e_pair, cvt_ref[2 * j].astype(BF16)),
                            _dot_nt(e_pair, cvt_ref[2 * j + 1].astype(BF16)))
        for blk in range(n_blk):
            pv_ref[pl.ds(blk * rows + j * pair_rows, pair_rows), :] = pv_pair[blk * pair_rows:(blk + 1) * pair_rows]

    for p in range(A_GROUP):
        halves = []
        for kvh in range(A_KV_HEADS):
            blk = p * A_KV_HEADS + kvh
            halves.append((pv_ref[pl.ds(blk * rows, rows), :] + pv_new[blk]) * inv_den[blk])
        ua_ref[:, p * LANES:(p + 1) * LANES] = jnp.where(low, halves[0], halves[1]).astype(BF16)

    tail = lane >= WINDOW - steps
    for b in range(nb):
        shift = (WINDOW - steps - b * steps) % rows
        nkt_ref[b] = jnp.where(tail, pltpu.roll(k_new_t, shift, 1), pltpu.roll(ckt_ref[b], WINDOW - steps, 1))
        nvt_ref[b] = jnp.where(tail, pltpu.roll(v_new_t, shift, 1), pltpu.roll(cvt_ref[b], WINDOW - steps, 1))


def _swa_sample(sinks, qa, kva, ckt, cvt, steps):
    batch = ckt.shape[0]
    assert ckt.shape[1:] == (A_KV_HEADS * A_HD, WINDOW) and SAMPLE_ROWS == WINDOW == LANES
    nb = SAMPLE_ROWS // steps
    cache = pl.BlockSpec((nb, LANES, WINDOW), lambda g: (g, 0, 0))

    def rows(width):
        return pl.BlockSpec((SAMPLE_ROWS, width), lambda g: (g, 0))

    stacked = (A_HEADS * SAMPLE_ROWS, LANES)
    return pl.pallas_call(
        functools.partial(_swa_sample_kernel, steps=steps),
        grid=(batch // nb,),
        in_specs=[pl.BlockSpec(memory_space=pltpu.SMEM), rows(A_WIDTH), rows(KV_WIDTH), cache, cache],
        out_specs=[rows(A_WIDTH), cache, cache],
        out_shape=[jax.ShapeDtypeStruct((batch * steps, A_WIDTH), BF16),
                   jax.ShapeDtypeStruct(ckt.shape, F32),
                   jax.ShapeDtypeStruct(cvt.shape, F32)],
        scratch_shapes=[pltpu.VMEM(stacked, F32), pltpu.VMEM(stacked, F32), pltpu.VMEM(stacked, F32)],
        compiler_params=pltpu.CompilerParams(dimension_semantics=("parallel",),
                                             vmem_limit_bytes=VMEM_LIMIT),
        name="swa_sample",
    )(sinks, qa, kva, ckt, cvt)


def _prep_proj(w_in, b_gate):
    m_end = 4 * M_WIDTH
    g_end = m_end + 2 * M_HEADS
    wt = w_in.T
    qa = wt[g_end:g_end + A_WIDTH].reshape(A_KV_HEADS, A_GROUP, A_HD, D_MODEL).transpose(1, 0, 2, 3)
    gates = jnp.pad(wt[m_end:g_end], ((0, LANES - 2 * M_HEADS), (0, 0)))
    win = jnp.concatenate([wt[:m_end], qa.reshape(A_WIDTH, D_MODEL), wt[g_end + A_WIDTH:], gates],
                          axis=0).astype(BF16)
    bias = jnp.pad(b_gate, (0, LANES - 2 * M_HEADS)).reshape(1, LANES)
    return win, bias


def _prep_out(w_out):
    wa = w_out[M_WIDTH:].reshape(A_KV_HEADS, A_GROUP, A_HD, D_MODEL).transpose(1, 0, 2, 3)
    return jnp.concatenate([w_out[:M_WIDTH], wa.reshape(A_WIDTH, D_MODEL)], axis=0).astype(BF16)


def kernel(x_prompt, x_sample, cache_swa_k, cache_swa_v, state_mlstm_C, state_mlstm_n, state_mlstm_m,
           norm_gains, ffn_w_gate, ffn_w_up, ffn_w_down, w_in, b_gate, mlstm_norm_gain, attn_sinks, w_out):
    assert norm_gains.shape[0] == 1, "single layer"
    batch, seq, _ = x_prompt.shape
    dec_batch, steps, _ = x_sample.shape
    gains = norm_gains[0]
    ffn = (ffn_w_gate, ffn_w_up, ffn_w_down)
    win, bias = _prep_proj(w_in[0], b_gate[0])
    wo = _prep_out(w_out[0])
    gain = mlstm_norm_gain[0].reshape(1, M_WIDTH)
    sinks = attn_sinks[0]
    kv_lanes = A_KV_HEADS * A_HD

    chunk = min(MLSTM_CHUNK, seq)
    assert seq % chunk == 0
    xp, zm, qa, kva, gates = _ffn_proj(x_prompt.reshape(batch * seq, D_MODEL), gains, *ffn, win, bias,
                                       0, 0, 1, 2, chunk)
    um, p_c, p_n, p_m = _mlstm_prompt(zm, gates, gain, batch, seq, chunk)
    ua = _swa_prompt(sinks, qa, kva, batch, seq)
    yp = _mix_ffn(xp, um, ua, wo, gains, *ffn, 1, 3, 4, 5).reshape(batch, seq, D_MODEL)
    keep = min(WINDOW, seq)
    kv_tail = kva.reshape(batch, seq, KV_WIDTH)[:, seq - keep:]
    p_k = kv_tail[..., :kv_lanes].reshape(1, batch, keep, A_KV_HEADS, A_HD)
    p_v = kv_tail[..., kv_lanes:].reshape(1, batch, keep, A_KV_HEADS, A_HD)
    p_c = p_c[None]
    p_n = p_n[None, :, :, 0, :]
    p_m = p_m[None, :, :, 0, 0]

    xs, zm_s, qa_s, kva_s, gates_s = _ffn_proj(x_sample.reshape(dec_batch * steps, D_MODEL), gains, *ffn,
                                               win, bias, 0, 0, 1, 2, min(chunk, dec_batch * steps))
    m_rows = jnp.repeat(state_mlstm_m[0].T, steps, axis=1)[..., None]
    um_s, s_c, s_n2, s_m_rows = _mlstm_sample(zm_s, gates_s, gain, state_mlstm_C[0],
                                              state_mlstm_n[0].reshape(dec_batch, M_HEADS * M_DK),
                                              m_rows, steps)
    wc = cache_swa_k.shape[2]
    ua_s, s_kt, s_vt = _swa_sample(sinks, qa_s, kva_s,
                                   cache_swa_k[0].reshape(dec_batch, wc, kv_lanes).transpose(0, 2, 1),
                                   cache_swa_v[0].reshape(dec_batch, wc, kv_lanes).transpose(0, 2, 1), steps)
    ys = _mix_ffn(xs, um_s, ua_s, wo, gains, *ffn, 1, 3, 4, 5).reshape(dec_batch, steps, D_MODEL)
    s_n = s_n2.reshape(1, dec_batch, M_HEADS, M_DK)
    s_m = s_m_rows[:, ::steps, 0].T[None]
    s_k = s_kt.transpose(0, 2, 1).reshape(1, dec_batch, wc, A_KV_HEADS, A_HD)
    s_v = s_vt.transpose(0, 2, 1).reshape(1, dec_batch, wc, A_KV_HEADS, A_HD)
    return (yp, ys, p_k, p_v, p_c, p_n, p_m, s_k, s_v, s_c[None], s_n, s_m)
```

```python
import functools

import jax
import jax.numpy as jnp
from jax import lax
from jax.experimental import pallas as pl
from jax.experimental.pallas import tpu as pltpu

F32 = jnp.float32
BF16 = jnp.bfloat16

D_MODEL = 1024
D_FF = 2816
M_HEADS = 4
M_DK = 128
M_DV = 128
M_WIDTH = M_HEADS * M_DV
A_HEADS = 8
A_KV_HEADS = 2
A_GROUP = A_HEADS // A_KV_HEADS
A_HD = 64
A_WIDTH = A_HEADS * A_HD
WINDOW = 128
RMS_EPS = 1e-6

LANES = 128
FFN_CHUNK = 256
N_FFN_CHUNKS = D_FF // FFN_CHUNK
FFN_STAGE_SLOTS = 4
TOKEN_TILE = 512
MLSTM_CHUNK = 256
ZM_WIDTH = 4 * M_WIDTH
KV_WIDTH = 2 * A_KV_HEADS * A_HD
VMEM_LIMIT = 56 * 1024 * 1024
MLSTM_CHUNKS_PER_STEP = 8
GATE_I, GATE_F, GATE_B, GATE_X = 0, M_HEADS, 2 * M_HEADS, 3 * M_HEADS

NT_DIMS = (((1,), (1,)), ((), ()))


def _rms(x, g):
    ms = jnp.mean(x * x, axis=-1, keepdims=True)
    return x * lax.rsqrt(ms + RMS_EPS) * g


def _log_sigmoid(x):
    return jnp.minimum(x, 0.0) - jnp.log1p(jnp.exp(-jnp.abs(x)))


def _dot(a, b):
    return jnp.dot(a, b, preferred_element_type=F32)


def _dot_nt(a, b):
    return lax.dot_general(a, b, NT_DIMS, preferred_element_type=F32)


def _resident(shape):
    nd = len(shape)
    return pl.BlockSpec(shape, lambda *_: (0,) * nd, pipeline_mode=pl.Buffered(1))


def _ffn_weight_copies(w_hbm, j, f, slot, stage_refs, sems):
    wg_hbm, wu_hbm, wd_hbm = w_hbm
    cols = pl.ds(f * FFN_CHUNK, FFN_CHUNK)
    sources = (wg_hbm.at[0, j, :, cols], wu_hbm.at[0, j, :, cols], wd_hbm.at[0, j, cols, :])
    return [pltpu.make_async_copy(src, stage.at[slot], sems.at[slot, k])
            for k, (src, stage) in enumerate(zip(sources, stage_refs))]


def _swiglu(x, gpre, w_refs, h_ref, acc_ref, fetch=None):
    wg_ref, wu_ref, wd_ref = w_refs
    if fetch is not None:
        for f in range(min(FFN_STAGE_SLOTS, N_FFN_CHUNKS)):
            for copy in _ffn_weight_copies(*fetch[:2], f, f % FFN_STAGE_SLOTS, *fetch[2:]):
                copy.start()
    h_ref[...] = _rms(x, gpre).astype(BF16)
    for f in range(N_FFN_CHUNKS):
        cols = slice(f * FFN_CHUNK, (f + 1) * FFN_CHUNK)
        if fetch is not None:
            slot = f % FFN_STAGE_SLOTS
            stage_g, stage_u, stage_d = fetch[2]
            for copy in _ffn_weight_copies(*fetch[:2], f, slot, *fetch[2:]):
                copy.wait()
            wg_ref[:, cols] = stage_g[slot].astype(BF16)
            wu_ref[:, cols] = stage_u[slot].astype(BF16)
            wd_ref[cols, :] = stage_d[slot].astype(BF16)
            if f + FFN_STAGE_SLOTS < N_FFN_CHUNKS:
                for copy in _ffn_weight_copies(*fetch[:2], f + FFN_STAGE_SLOTS, slot, *fetch[2:]):
                    copy.start()
        h = h_ref[...]
        g = _dot(h, wg_ref[:, cols])
        u = _dot(h, wu_ref[:, cols])
        part = _dot((g * jax.nn.sigmoid(g) * u).astype(BF16), wd_ref[cols, :])
        if f == 0:
            acc_ref[...] = part
        else:
            acc_ref[...] += part


def _first_step_fetches(body, fetch):
    first = pl.program_id(0) == 0

    @pl.when(first)
    def _():
        body(fetch)

    @pl.when(jnp.logical_not(first))
    def _():
        body(None)


def _half_step(x, acc_ref, gpost):
    return x + _rms(acc_ref[...], 0.5 * gpost)


def _project(x, gain, win_ref, bias_ref, h_ref, zm_ref, qa_ref, kva_ref, gates_ref, chunk):
    h_ref[...] = _rms(x, gain).astype(BF16)

    def cols(lo, width):
        return _dot_nt(h_ref[...], win_ref[lo:lo + width, :])

    gates = cols(ZM_WIDTH + A_WIDTH + KV_WIDTH, LANES) + bias_ref[...]
    lane = lax.broadcasted_iota(jnp.int32, (chunk, LANES), 1)
    pos = lax.broadcasted_iota(jnp.int32, (chunk, LANES), 0)
    for c0 in range(0, gates.shape[0], chunk):
        g = gates[c0:c0 + chunk]
        b = jnp.where((lane >= GATE_F) & (lane < GATE_B), _log_sigmoid(g), 0.0)
        shift = 1
        while shift < chunk:
            b = b + jnp.where(pos >= shift, pltpu.roll(b, shift, 0), 0.0)
            shift *= 2
        x_gate = pltpu.roll(g, GATE_X - GATE_I, 1) - pltpu.roll(b, GATE_X - GATE_F, 1)
        out = jnp.where(lane < GATE_B, g, jnp.where(lane < GATE_X, pltpu.roll(b, GATE_B - GATE_F, 1), x_gate))
        gates_ref[c0:c0 + chunk, :] = jnp.where(lane < GATE_X + M_HEADS, out, 0.0)

    zm_ref[:, 0:M_WIDTH] = cols(0, M_WIDTH).astype(BF16)
    zm_ref[:, M_WIDTH:2 * M_WIDTH] = (cols(M_WIDTH, M_WIDTH) * (M_DK ** -0.5)).astype(BF16)
    zm_ref[:, 2 * M_WIDTH:3 * M_WIDTH] = cols(2 * M_WIDTH, M_WIDTH).astype(BF16)
    zm_ref[:, 3 * M_WIDTH:] = jax.nn.sigmoid(cols(3 * M_WIDTH, M_WIDTH)).astype(BF16)
    qa_ref[...] = (cols(ZM_WIDTH, A_WIDTH) * (A_HD ** -0.5)).astype(BF16)
    kva_ref[...] = cols(ZM_WIDTH + A_WIDTH, KV_WIDTH)


def _ffn_proj_kernel(x_ref, gains_ref, wg_hbm, wu_hbm, wd_hbm, win_ref, bias_ref,
                     x1_ref, zm_ref, qa_ref, kva_ref, gates_ref,
                     h_ref, acc_ref, wg_ref, wu_ref, wd_ref, stage_g, stage_u, stage_d, sems,
                     *, j, pre, post, mixer, chunk):
    def body(fetch):
        x = x_ref[...]
        _swiglu(x, gains_ref[pre:pre + 1, :], (wg_ref, wu_ref, wd_ref), h_ref, acc_ref, fetch)
        x1 = _half_step(x, acc_ref, gains_ref[post:post + 1, :])
        x1_ref[...] = x1
        _project(x1, gains_ref[mixer:mixer + 1, :], win_ref, bias_ref, h_ref, zm_ref, qa_ref, kva_ref,
                 gates_ref, chunk)

    _first_step_fetches(body, ((wg_hbm, wu_hbm, wd_hbm), j, (stage_g, stage_u, stage_d), sems))


def _mix_ffn_kernel(x_ref, um_ref, ua_ref, wo_ref, gains_ref, wg_hbm, wu_hbm, wd_hbm, o_ref,
                    h_ref, acc_ref, wg_ref, wu_ref, wd_ref, stage_g, stage_u, stage_d, sems,
                    *, j, mix, pre, post):
    def body(fetch):
        u = jnp.concatenate([um_ref[...], ua_ref[...]], axis=-1)
        x = x_ref[...] + _rms(_dot(u, wo_ref[...]), gains_ref[mix:mix + 1, :])
        _swiglu(x, gains_ref[pre:pre + 1, :], (wg_ref, wu_ref, wd_ref), h_ref, acc_ref, fetch)
        o_ref[...] = _half_step(x, acc_ref, gains_ref[post:post + 1, :])

    _first_step_fetches(body, ((wg_hbm, wu_hbm, wd_hbm), j, (stage_g, stage_u, stage_d), sems))


def _ffn_scratch(tm):
    return [pltpu.VMEM((tm, D_MODEL), BF16),
            pltpu.VMEM((tm, D_MODEL), F32),
            pltpu.VMEM((D_MODEL, D_FF), BF16),
            pltpu.VMEM((D_MODEL, D_FF), BF16),
            pltpu.VMEM((D_FF, D_MODEL), BF16),
            pltpu.VMEM((FFN_STAGE_SLOTS, D_MODEL, FFN_CHUNK), F32),
            pltpu.VMEM((FFN_STAGE_SLOTS, D_MODEL, FFN_CHUNK), F32),
            pltpu.VMEM((FFN_STAGE_SLOTS, FFN_CHUNK, D_MODEL), F32),
            pltpu.SemaphoreType.DMA((FFN_STAGE_SLOTS, 3))]


def _dense_params():
    return pltpu.CompilerParams(dimension_semantics=("arbitrary",), vmem_limit_bytes=VMEM_LIMIT)


_HBM = pl.BlockSpec(memory_space=pl.ANY)


def _ffn_proj(x, gains, wg, wu, wd, win, bias, j, pre, post, mixer, chunk):
    n = x.shape[0]
    tm = min(TOKEN_TILE, n)
    assert tm % chunk == 0

    def rows(width):
        return pl.BlockSpec((tm, width), lambda i: (i, 0))

    return pl.pallas_call(
        functools.partial(_ffn_proj_kernel, j=j, pre=pre, post=post, mixer=mixer, chunk=chunk),
        grid=(n // tm,),
        in_specs=[rows(D_MODEL), _resident(gains.shape), _HBM, _HBM, _HBM,
                  _resident(win.shape), _resident(bias.shape)],
        out_specs=[rows(D_MODEL), rows(ZM_WIDTH), rows(A_WIDTH), rows(KV_WIDTH), rows(LANES)],
        out_shape=[jax.ShapeDtypeStruct((n, D_MODEL), F32),
                   jax.ShapeDtypeStruct((n, ZM_WIDTH), BF16),
                   jax.ShapeDtypeStruct((n, A_WIDTH), BF16),
                   jax.ShapeDtypeStruct((n, KV_WIDTH), F32),
                   jax.ShapeDtypeStruct((n, LANES), F32)],
        scratch_shapes=_ffn_scratch(tm),
        compiler_params=_dense_params(),
        name="ffn_half_step_input_projection",
    )(x, gains, wg, wu, wd, win, bias)


def _mix_ffn(x, um, ua, wo, gains, wg, wu, wd, j, mix, pre, post):
    n = x.shape[0]
    tm = min(TOKEN_TILE, n)
    row = pl.BlockSpec((tm, D_MODEL), lambda i: (i, 0))
    half = pl.BlockSpec((tm, M_WIDTH), lambda i: (i, 0))
    return pl.pallas_call(
        functools.partial(_mix_ffn_kernel, j=j, mix=mix, pre=pre, post=post),
        grid=(n // tm,),
        in_specs=[row, half, half, _resident(wo.shape), _resident(gains.shape), _HBM, _HBM, _HBM],
        out_specs=row,
        out_shape=jax.ShapeDtypeStruct((n, D_MODEL), F32),
        scratch_shapes=_ffn_scratch(tm),
        compiler_params=_dense_params(),
        name="out_proj_ffn_half_step",
    )(x, um, ua, wo, gains, wg, wu, wd)


def _head_norm_gate(hh, o, gain):
    hn = hh * lax.rsqrt(jnp.mean(hh * hh, axis=-1, keepdims=True) + RMS_EPS) * gain
    return (o.astype(F32) * hn).astype(BF16)


def _mlstm_prompt_kernel(zm_ref, gates_ref, gain_ref, um_ref, c_ref, n_ref, m_ref, cnt_ref, ms_ref, *, chunk):
    step = pl.program_id(1)

    @pl.when(step == 0)
    def _():
        cnt_ref[...] = jnp.zeros_like(cnt_ref)
        ms_ref[...] = jnp.zeros_like(ms_ref)

    si = lax.broadcasted_iota(jnp.int32, (chunk, chunk), 0)
    ti = lax.broadcasted_iota(jnp.int32, (chunk, chunk), 1)
    causal = si <= ti
    ones_rows = jnp.where(lax.broadcasted_iota(jnp.int32, (LANES, chunk), 0) == 0, 1.0, 0.0).astype(BF16)

    for c in range(zm_ref.shape[0] // chunk):
        rows = slice(c * chunk, (c + 1) * chunk)
        gates = gates_ref[rows, :]
        gates_t = gates.T
        for h in range(M_HEADS):
            lanes = slice(h * M_DK, (h + 1) * M_DK)
            q = zm_ref[rows, lanes]
            k = zm_ref[rows, M_WIDTH + h * M_DK:M_WIDTH + (h + 1) * M_DK]
            v = zm_ref[rows, 2 * M_WIDTH + h * M_DV:2 * M_WIDTH + (h + 1) * M_DV]
            o = zm_ref[rows, 3 * M_WIDTH + h * M_DV:3 * M_WIDTH + (h + 1) * M_DV]
            x_col = gates[:, GATE_X + h:GATE_X + h + 1]
            ig_row = gates_t[GATE_I + h:GATE_I + h + 1, :]
            b_row = gates_t[GATE_B + h:GATE_B + h + 1, :]
            m_prev = ms_ref[h][:, 0:1]
            cnt = cnt_ref[h]

            xm = jnp.where(causal, x_col, -jnp.inf)
            mu = jnp.maximum(m_prev, jnp.max(xm, axis=0, keepdims=True))
            a = jnp.exp(m_prev - mu)
            s_t = _dot_nt(k, q) * jnp.exp(xm - mu)
            v_ext = jnp.concatenate([v.astype(F32).T.astype(BF16), ones_rows], axis=0)
            nd = a * _dot_nt(cnt.astype(BF16), q) + _dot(v_ext, s_t.astype(BF16))
            den = nd[M_DV:M_DV + 1, :]
            hh = nd[:M_DV, :] * (1.0 / jnp.maximum(jnp.abs(den), jnp.exp(-(b_row + mu))))
            hn = hh * lax.rsqrt(jnp.mean(hh * hh, axis=0, keepdims=True) + RMS_EPS)
            um_ref[rows, lanes] = (hn.T * gain_ref[:, lanes] * o.astype(F32)).astype(BF16)

            b_last = b_row[:, chunk - 1:chunk]
            g = b_last - b_row + ig_row
            m_new = jnp.maximum(b_last + m_prev, jnp.max(g, axis=-1, keepdims=True))
            decay = jnp.exp(b_last + m_prev - m_new)
            wv = (jnp.exp(g - m_new) * v_ext.astype(F32)).astype(BF16)
            cnt_ref[h] = decay * cnt + _dot(wv, k)
            ms_ref[h] = jnp.broadcast_to(m_new, (1, LANES))

    @pl.when(step == pl.num_programs(1) - 1)
    def _():
        for h in range(M_HEADS):
            c_ref[0, h] = cnt_ref[h][:M_DV, :].T
            n_ref[0, h] = cnt_ref[h][M_DV:M_DV + 1, :]
            m_ref[0, h] = ms_ref[h]


def _mlstm_prompt(zm, gates, gain, batch, seq, chunk):
    n = batch * seq
    per_step = min(MLSTM_CHUNKS_PER_STEP, n // batch // chunk)
    step_rows = per_step * chunk
    ns = n // batch // step_rows

    def rows(width):
        return pl.BlockSpec((step_rows, width), lambda b, c: (b * ns + c, 0))

    def state(*dims):
        return pl.BlockSpec((1, M_HEADS) + dims, lambda b, c: (b, 0, 0, 0))

    return pl.pallas_call(
        functools.partial(_mlstm_prompt_kernel, chunk=chunk),
        grid=(batch, ns),
        in_specs=[rows(ZM_WIDTH), rows(LANES), pl.BlockSpec((1, M_WIDTH), lambda b, c: (0, 0))],
        out_specs=[rows(M_WIDTH), state(M_DK, M_DV), state(1, M_DK), state(1, LANES)],
        out_shape=[jax.ShapeDtypeStruct((n, M_WIDTH), BF16),
                   jax.ShapeDtypeStruct((batch, M_HEADS, M_DK, M_DV), F32),
                   jax.ShapeDtypeStruct((batch, M_HEADS, 1, M_DK), F32),
                   jax.ShapeDtypeStruct((batch, M_HEADS, 1, LANES), F32)],
        scratch_shapes=[pltpu.VMEM((M_HEADS, 2 * LANES, M_DK), F32), pltpu.VMEM((M_HEADS, 1, LANES), F32)],
        compiler_params=pltpu.CompilerParams(dimension_semantics=("parallel", "arbitrary"),
                                             vmem_limit_bytes=VMEM_LIMIT),
        name="mlstm_prompt",
    )(zm, gates, gain)


SWA_BLOCKS_PER_STEP = 8


def _alibi_slope(head):
    return 2.0 ** (-8.0 * (head + 1) / A_HEADS)


def _swa_prompt_kernel(sinks_ref, qa_ref, kvc_ref, kvp_ref, ua_ref):
    hw = A_KV_HEADS * A_HD
    n_blocks = qa_ref.shape[0] // WINDOW
    qi = lax.broadcasted_iota(jnp.int32, (WINDOW, 2 * WINDOW), 0)
    kj = lax.broadcasted_iota(jnp.int32, (WINDOW, 2 * WINDOW), 1)
    dist = WINDOW + qi - kj
    band = (dist >= 0) & (dist < WINDOW)
    distf = dist.astype(F32)
    low = lax.broadcasted_iota(jnp.int32, (WINDOW, LANES), 1) < A_HD
    first = pl.program_id(1) == 0

    for j in range(n_blocks):
        rows = slice(j * WINDOW, (j + 1) * WINDOW)
        kvc = kvc_ref[rows, :]
        kvp = kvp_ref[...] if j == 0 else kvc_ref[(j - 1) * WINDOW:j * WINDOW, :]
        valid = band & ((kj >= WINDOW) | jnp.logical_not(first)) if j == 0 else band
        kk = jnp.concatenate([kvp[:, :hw], kvc[:, :hw]], axis=0).astype(BF16)
        vv = jnp.concatenate([kvp[:, hw:], kvc[:, hw:]], axis=0).astype(BF16)
        for p in range(A_GROUP):
            qp = qa_ref[rows, p * LANES:(p + 1) * LANES]
            outs = []
            for kvh in range(A_KV_HEADS):
                head = kvh * A_GROUP + p
                keep = low if kvh == 0 else jnp.logical_not(low)
                s = _dot_nt(jnp.where(keep, qp, jnp.zeros_like(qp)), kk)
                s = jnp.where(valid, s - _alibi_slope(head) * distf, -jnp.inf)
                sink = sinks_ref[head]
                mx = jnp.maximum(jnp.max(s, axis=-1, keepdims=True), sink)
                e = jnp.exp(s - mx)
                den = jnp.sum(e, axis=-1, keepdims=True) + jnp.exp(sink - mx)
                outs.append(_dot(e.astype(BF16), vv) / den)
            ua_ref[rows, p * LANES:(p + 1) * LANES] = jnp.where(low, outs[0], outs[1]).astype(BF16)


def _swa_prompt(sinks, qa, kva, batch, seq):
    n = batch * seq
    per_step = min(SWA_BLOCKS_PER_STEP, n // batch // WINDOW)
    rows = per_step * WINDOW
    ns = n // batch // rows

    def step_rows(width):
        return pl.BlockSpec((rows, width), lambda b, i: (b * ns + i, 0))

    def prev_block(b, i):
        return (jnp.maximum((b * ns + i) * per_step - 1, 0), 0)

    return pl.pallas_call(
        _swa_prompt_kernel,
        grid=(batch, ns),
        in_specs=[pl.BlockSpec(memory_space=pltpu.SMEM), step_rows(A_WIDTH), step_rows(KV_WIDTH),
                  pl.BlockSpec((WINDOW, KV_WIDTH), prev_block)],
        out_specs=step_rows(A_WIDTH),
        out_shape=jax.ShapeDtypeStruct((n, A_WIDTH), BF16),
        compiler_params=pltpu.CompilerParams(dimension_semantics=("parallel", "parallel"),
                                             vmem_limit_bytes=VMEM_LIMIT),
        name="swa_prompt",
    )(sinks, qa, kva, kva)


SAMPLE_ROWS = 128


def _mlstm_sample_kernel(q_ref, k_ref, v_ref, o_ref, gates_ref, gain_ref, c_ref, n_ref, m_ref,
                         um_ref, co_ref, no_ref, mo_ref,
                         qf_ref, qc_ref, nexp_ref, wkk_ref, dec_ref, *, steps):
    rows = SAMPLE_ROWS
    nb = rows // steps
    head = pl.program_id(1)
    row = lax.broadcasted_iota(jnp.int32, (rows, rows), 0)
    col = lax.broadcasted_iota(jnp.int32, (rows, rows), 1)
    same = (row // steps) == (col // steps)
    causal = same & (col <= row)
    eye = row == col

    def to_row(x_col):
        return jnp.sum(jnp.where(eye, x_col, 0.0), axis=0, keepdims=True)

    def pick(mask, x_row, fill, reduce):
        return reduce(jnp.where(mask, x_row, fill), axis=-1, keepdims=True)

    gates = gates_ref[...]
    lane = lax.broadcasted_iota(jnp.int32, (rows, LANES), 1)
    ig_col = jnp.sum(jnp.where(lane == head, gates, 0.0), axis=-1, keepdims=True)
    fg_col = jnp.sum(jnp.where(lane == head + M_HEADS, gates, 0.0), axis=-1, keepdims=True)
    lf_row = to_row(_log_sigmoid(fg_col))
    b_col = pick(causal, lf_row, 0.0, jnp.sum)
    b_row = to_row(b_col)
    ig_row = to_row(ig_col)
    m_prev = m_ref[0]

    d = jnp.where(causal, b_col - b_row + ig_row, -jnp.inf)
    inter = b_col + m_prev
    m_t = jnp.maximum(inter, jnp.max(d, axis=-1, keepdims=True))
    w = jnp.exp(d - m_t)
    a = jnp.exp(inter - m_t)

    q = q_ref[...]
    k = k_ref[...]
    v = v_ref[...]
    qf_ref[...] = q.astype(F32)
    for t in range(steps):
        nexp_ref[pl.ds(t, nb, stride=steps), :] = n_ref[...]

    sub = lax.broadcasted_iota(jnp.int32, (2 * steps, M_DV), 0)

    def qc_pair(j, carry):
        q8 = qf_ref[pl.ds(pl.multiple_of(j * 2 * steps, 2 * steps), 2 * steps), :].astype(BF16)
        r0 = _dot(q8, c_ref[2 * j, 0].astype(BF16))
        r1 = _dot(q8, c_ref[2 * j + 1, 0].astype(BF16))
        qc_ref[pl.ds(pl.multiple_of(j * 2 * steps, 2 * steps), 2 * steps), :] = jnp.where(sub < steps, r0, r1)
        return carry

    lax.fori_loop(0, nb // 2, qc_pair, 0, unroll=True)

    s = _dot_nt(q, k) * w
    qn = jnp.sum(qf_ref[...] * nexp_ref[...], axis=-1, keepdims=True)
    num = a * qc_ref[...] + _dot(s.astype(BF16), v)
    den = a * qn + jnp.sum(s, axis=-1, keepdims=True)
    hh = num / jnp.maximum(jnp.abs(den), jnp.exp(-m_t))
    um_ref[...] = _head_norm_gate(hh, o_ref[...], gain_ref[...])

    last = same & ((col % steps) == steps - 1)
    b_last = pick(last, b_row, 0.0, jnp.sum)
    g_col = b_last - b_col + ig_col
    g_max = pick(same, to_row(g_col), -jnp.inf, jnp.max)
    m_new = jnp.maximum(b_last + m_prev, g_max)
    decay = jnp.exp(b_last + m_prev - m_new)
    wk = jnp.exp(g_col - m_new)
    mo_ref[0] = m_new

    kf = k.astype(F32)
    wkk_ref[...] = wk * kf
    dec_ref[...] = jnp.broadcast_to(decay, (rows, M_DV))
    n_sum = wkk_ref[pl.ds(0, nb, stride=steps), :]
    for t in range(1, steps):
        n_sum = n_sum + wkk_ref[pl.ds(t, nb, stride=steps), :]
    no_ref[...] = dec_ref[pl.ds(0, nb, stride=steps), :] * n_ref[...] + n_sum

    k_t = kf.T
    wv = (wk * v.astype(F32)).astype(BF16)
    tok_batch = lax.broadcasted_iota(jnp.int32, (M_DK, rows), 1) // steps

    def update(b, carry):
        upd = _dot(jnp.where(tok_batch == b, k_t, 0.0).astype(BF16), wv)
        co_ref[b, 0] = dec_ref[pl.ds(b * steps, 1), :] * c_ref[b, 0] + upd
        return carry

    lax.fori_loop(0, nb, update, 0, unroll=True)


def _mlstm_sample(zm, gates, gain, c, n2, m_rows, steps):
    rows = zm.shape[0]
    batch = c.shape[0]
    nb = SAMPLE_ROWS // steps
    groups = rows // SAMPLE_ROWS

    def zcol(section):
        return pl.BlockSpec((SAMPLE_ROWS, M_DK), lambda g, h: (g, section * M_HEADS + h))

    c_spec = pl.BlockSpec((nb, 1, M_DK, M_DV), lambda g, h: (g, h, 0, 0))
    n_spec = pl.BlockSpec((nb, M_DK), lambda g, h: (g, h))
    m_spec = pl.BlockSpec((1, SAMPLE_ROWS, 1), lambda g, h: (h, g, 0))
    return pl.pallas_call(
        functools.partial(_mlstm_sample_kernel, steps=steps),
        grid=(groups, M_HEADS),
        in_specs=[zcol(0), zcol(1), zcol(2), zcol(3),
                  pl.BlockSpec((SAMPLE_ROWS, LANES), lambda g, h: (g, 0)),
                  pl.BlockSpec((1, M_DV), lambda g, h: (0, h)),
                  c_spec, n_spec, m_spec],
        out_specs=[pl.BlockSpec((SAMPLE_ROWS, M_DV), lambda g, h: (g, h)), c_spec, n_spec, m_spec],
        out_shape=[jax.ShapeDtypeStruct((rows, M_WIDTH), BF16),
                   jax.ShapeDtypeStruct(c.shape, F32),
                   jax.ShapeDtypeStruct(n2.shape, F32),
                   jax.ShapeDtypeStruct(m_rows.shape, F32)],
        scratch_shapes=[pltpu.VMEM((SAMPLE_ROWS, M_DK), F32),
                        pltpu.VMEM((SAMPLE_ROWS, M_DV), F32),
                        pltpu.VMEM((SAMPLE_ROWS, M_DK), F32),
                        pltpu.VMEM((SAMPLE_ROWS, M_DK), F32),
                        pltpu.VMEM((SAMPLE_ROWS, M_DV), F32)],
        compiler_params=pltpu.CompilerParams(dimension_semantics=("parallel", "parallel"),
                                             vmem_limit_bytes=VMEM_LIMIT),
        name="mlstm_sample",
    )(zm, zm, zm, zm, gates, gain, c, n2, m_rows)


def _swa_sample_kernel(sinks_ref, qa_ref, kva_ref, ckt_ref, cvt_ref, ua_ref, nkt_ref, nvt_ref,
                       s_ref, e_ref, pv_ref, *, steps):
    rows = SAMPLE_ROWS
    nb = rows // steps
    pair_rows = 2 * steps
    hw = A_KV_HEADS * A_HD
    tok = lax.broadcasted_iota(jnp.int32, (rows, LANES), 0)
    lane = lax.broadcasted_iota(jnp.int32, (rows, LANES), 1)
    low = lane < A_HD

    qf = qa_ref[...].astype(F32)
    q_blocks = []
    for p in range(A_GROUP):
        qp = qf[:, p * LANES:(p + 1) * LANES]
        q_blocks.append(jnp.where(low, qp, 0.0))
        q_blocks.append(jnp.where(low, 0.0, qp))
    n_blk = len(q_blocks)

    kv_new = kva_ref[...]
    kv_new_t = kv_new.T
    k_new_t = kv_new_t[:hw]
    v_new_t = kv_new_t[hw:]
    v_new = kv_new[:, hw:].astype(BF16)
    s_new = _dot(jnp.concatenate(q_blocks, axis=0).astype(BF16), k_new_t.astype(BF16))

    def pair_rows_of(blocks, j):
        return jnp.concatenate([blk[j * pair_rows:(j + 1) * pair_rows] for blk in blocks], axis=0)

    first = (lax.broadcasted_iota(jnp.int32, (n_blk * pair_rows, LANES), 0) % pair_rows) < steps

    for j in range(nb // 2):
        q_pair = pair_rows_of(q_blocks, j).astype(BF16)
        s_pair = jnp.where(first, _dot(q_pair, ckt_ref[2 * j].astype(BF16)),
                           _dot(q_pair, ckt_ref[2 * j + 1].astype(BF16)))
        for blk in range(n_blk):
            s_ref[pl.ds(blk * rows + j * pair_rows, pair_rows), :] = s_pair[blk * pair_rows:(blk + 1) * pair_rows]

    t_q = tok % steps
    dist_c = WINDOW + t_q - lane
    valid_c = dist_c < WINDOW
    dist_n = t_q - lane % steps
    valid_n = ((tok // steps) == (lane // steps)) & (dist_n >= 0)
    dist_cf = dist_c.astype(F32)
    dist_nf = dist_n.astype(F32)
    pv_new = []
    inv_den = []
    for blk in range(n_blk):
        head = (blk % A_KV_HEADS) * A_GROUP + blk // A_KV_HEADS
        slope = _alibi_slope(head)
        sink = sinks_ref[head]
        r = pl.ds(blk * rows, rows)
        s_c = jnp.where(valid_c, s_ref[r, :] - slope * dist_cf, -jnp.inf)
        s_n = jnp.where(valid_n, s_new[blk * rows:(blk + 1) * rows] - slope * dist_nf, -jnp.inf)
        mx = jnp.maximum(jnp.maximum(jnp.max(s_c, axis=-1, keepdims=True),
                                     jnp.max(s_n, axis=-1, keepdims=True)), sink)
        e_c = jnp.exp(s_c - mx)
        e_n = jnp.exp(s_n - mx)
        den = jnp.sum(e_c, axis=-1, keepdims=True) + jnp.sum(e_n, axis=-1, keepdims=True) + jnp.exp(sink - mx)
        e_ref[r, :] = e_c
        pv_new.append(_dot(e_n.astype(BF16), v_new))
        inv_den.append(1.0 / den)

    for j in range(nb // 2):
        e_pair = jnp.concatenate([e_ref[pl.ds(blk * rows + j * pair_rows, pair_rows), :] for blk in range(n_blk)],
                                 axis=0).astype(BF16)
        pv_pair = jnp.where(first, _dot_nt(e_pair, cvt_ref[2 * j].astype(BF16)),
                            _dot_nt(e_pair, cvt_ref[2 * j + 1].astype(BF16)))
        for blk in range(n_blk):
            pv_ref[pl.ds(blk * rows + j * pair_rows, pair_rows), :] = pv_pair[blk * pair_rows:(blk + 1) * pair_rows]

    for p in range(A_GROUP):
        halves = []
        for kvh in range(A_KV_HEADS):
            blk = p * A_KV_HEADS + kvh
            halves.append((pv_ref[pl.ds(blk * rows, rows), :] + pv_new[blk]) * inv_den[blk])
        ua_ref[:, p * LANES:(p + 1) * LANES] = jnp.where(low, halves[0], halves[1]).astype(BF16)

    tail = lane >= WINDOW - steps
    for b in range(nb):
        shift = (WINDOW - steps - b * steps) % rows
        nkt_ref[b] = jnp.where(tail, pltpu.roll(k_new_t, shift, 1), pltpu.roll(ckt_ref[b], WINDOW - steps, 1))
        nvt_ref[b] = jnp.where(tail, pltpu.roll(v_new_t, shift, 1), pltpu.roll(cvt_ref[b], WINDOW - steps, 1))


def _swa_sample(sinks, qa, kva, ckt, cvt, steps):
    batch = ckt.shape[0]
    assert ckt.shape[1:] == (A_KV_HEADS * A_HD, WINDOW) and SAMPLE_ROWS == WINDOW == LANES
    nb = SAMPLE_ROWS // steps
    cache = pl.BlockSpec((nb, LANES, WINDOW), lambda g: (g, 0, 0))

    def rows(width):
        return pl.BlockSpec((SAMPLE_ROWS, width), lambda g: (g, 0))

    stacked = (A_HEADS * SAMPLE_ROWS, LANES)
    return pl.pallas_call(
        functools.partial(_swa_sample_kernel, steps=steps),
        grid=(batch // nb,),
        in_specs=[pl.BlockSpec(memory_space=pltpu.SMEM), rows(A_WIDTH), rows(KV_WIDTH), cache, cache],
        out_specs=[rows(A_WIDTH), cache, cache],
        out_shape=[jax.ShapeDtypeStruct((batch * steps, A_WIDTH), BF16),
                   jax.ShapeDtypeStruct(ckt.shape, F32),
                   jax.ShapeDtypeStruct(cvt.shape, F32)],
        scratch_shapes=[pltpu.VMEM(stacked, F32), pltpu.VMEM(stacked, F32), pltpu.VMEM(stacked, F32)],
        compiler_params=pltpu.CompilerParams(dimension_semantics=("parallel",),
                                             vmem_limit_bytes=VMEM_LIMIT),
        name="swa_sample",
    )(sinks, qa, kva, ckt, cvt)


def _prep_proj(w_in, b_gate):
    m_end = 4 * M_WIDTH
    g_end = m_end + 2 * M_HEADS
    wt = w_in.T
    qa = wt[g_end:g_end + A_WIDTH].reshape(A_KV_HEADS, A_GROUP, A_HD, D_MODEL).transpose(1, 0, 2, 3)
    gates = jnp.pad(wt[m_end:g_end], ((0, LANES - 2 * M_HEADS), (0, 0)))
    win = jnp.concatenate([wt[:m_end], qa.reshape(A_WIDTH, D_MODEL), wt[g_end + A_WIDTH:], gates],
                          axis=0).astype(BF16)
    bias = jnp.pad(b_gate, (0, LANES - 2 * M_HEADS)).reshape(1, LANES)
    return win, bias


def _prep_out(w_out):
    wa = w_out[M_WIDTH:].reshape(A_KV_HEADS, A_GROUP, A_HD, D_MODEL).transpose(1, 0, 2, 3)
    return jnp.concatenate([w_out[:M_WIDTH], wa.reshape(A_WIDTH, D_MODEL)], axis=0).astype(BF16)


def kernel(x_prompt, x_sample, cache_swa_k, cache_swa_v, state_mlstm_C, state_mlstm_n, state_mlstm_m,
           norm_gains, ffn_w_gate, ffn_w_up, ffn_w_down, w_in, b_gate, mlstm_norm_gain, attn_sinks, w_out):
    assert norm_gains.shape[0] == 1, "single layer"
    batch, seq, _ = x_prompt.shape
    dec_batch, steps, _ = x_sample.shape
    gains = norm_gains[0]
    ffn = (ffn_w_gate, ffn_w_up, ffn_w_down)
    win, bias = _prep_proj(w_in[0], b_gate[0])
    wo = _prep_out(w_out[0])
    gain = mlstm_norm_gain[0].reshape(1, M_WIDTH)
    sinks = attn_sinks[0]
    kv_lanes = A_KV_HEADS * A_HD

    chunk = min(MLSTM_CHUNK, seq)
    assert seq % chunk == 0
    xp, zm, qa, kva, gates = _ffn_proj(x_prompt.reshape(batch * seq, D_MODEL), gains, *ffn, win, bias,
                                       0, 0, 1, 2, chunk)
    um, p_c, p_n, p_m = _mlstm_prompt(zm, gates, gain, batch, seq, chunk)
    ua = _swa_prompt(sinks, qa, kva, batch, seq)
    yp = _mix_ffn(xp, um, ua, wo, gains, *ffn, 1, 3, 4, 5).reshape(batch, seq, D_MODEL)
    keep = min(WINDOW, seq)
    kv_tail = kva.reshape(batch, seq, KV_WIDTH)[:, seq - keep:]
    p_k = kv_tail[..., :kv_lanes].reshape(1, batch, keep, A_KV_HEADS, A_HD)
    p_v = kv_tail[..., kv_lanes:].reshape(1, batch, keep, A_KV_HEADS, A_HD)
    p_c = p_c[None]
    p_n = p_n[None, :, :, 0, :]
    p_m = p_m[None, :, :, 0, 0]

    xs, zm_s, qa_s, kva_s, gates_s = _ffn_proj(x_sample.reshape(dec_batch * steps, D_MODEL), gains, *ffn,
                                               win, bias, 0, 0, 1, 2, min(chunk, dec_batch * steps))
    m_rows = jnp.repeat(state_mlstm_m[0].T, steps, axis=1)[..., None]
    um_s, s_c, s_n2, s_m_rows = _mlstm_sample(zm_s, gates_s, gain, state_mlstm_C[0],
                                              state_mlstm_n[0].reshape(dec_batch, M_HEADS * M_DK),
                                              m_rows, steps)
    wc = cache_swa_k.shape[2]
    ua_s, s_kt, s_vt = _swa_sample(sinks, qa_s, kva_s,
                                   cache_swa_k[0].reshape(dec_batch, wc, kv_lanes).transpose(0, 2, 1),
                                   cache_swa_v[0].reshape(dec_batch, wc, kv_lanes).transpose(0, 2, 1), steps)
    ys = _mix_ffn(xs, um_s, ua_s, wo, gains, *ffn, 1, 3, 4, 5).reshape(dec_batch, steps, D_MODEL)
    s_n = s_n2.reshape(1, dec_batch, M_HEADS, M_DK)
    s_m = s_m_rows[:, ::steps, 0].T[None]
    s_k = s_kt.transpose(0, 2, 1).reshape(1, dec_batch, wc, A_KV_HEADS, A_HD)
    s_v = s_vt.transpose(0, 2, 1).reshape(1, dec_batch, wc, A_KV_HEADS, A_HD)
    return (yp, ys, p_k, p_v, p_c, p_n, p_m, s_k, s_v, s_c[None], s_n, s_m)
```

```python
import functools

import jax
import jax.numpy as jnp
from jax import lax
from jax.experimental import pallas as pl
from jax.experimental.pallas import tpu as pltpu

F32 = jnp.float32
BF16 = jnp.bfloat16

D_MODEL = 1024
D_FF = 2816
M_HEADS = 4
M_DK = 128
M_DV = 128
M_WIDTH = M_HEADS * M_DV
A_HEADS = 8
A_KV_HEADS = 2
A_GROUP = A_HEADS // A_KV_HEADS
A_HD = 64
A_WIDTH = A_HEADS * A_HD
WINDOW = 128
RMS_EPS = 1e-6

LANES = 128
FFN_CHUNK = 256
N_FFN_CHUNKS = D_FF // FFN_CHUNK
FFN_STAGE_SLOTS = 4
TOKEN_TILE = 512
MLSTM_CHUNK = 256
ZM_WIDTH = 4 * M_WIDTH
KV_WIDTH = 2 * A_KV_HEADS * A_HD
VMEM_LIMIT = 56 * 1024 * 1024
MLSTM_CHUNKS_PER_STEP = 8
GATE_I, GATE_F, GATE_B, GATE_X = 0, M_HEADS, 2 * M_HEADS, 3 * M_HEADS

NT_DIMS = (((1,), (1,)), ((), ()))


def _rms(x, g):
    ms = jnp.mean(x * x, axis=-1, keepdims=True)
    return x * lax.rsqrt(ms + RMS_EPS) * g


def _log_sigmoid(x):
    return jnp.minimum(x, 0.0) - jnp.log1p(jnp.exp(-jnp.abs(x)))


def _dot(a, b):
    return jnp.dot(a, b, preferred_element_type=F32)


def _dot_nt(a, b):
    return lax.dot_general(a, b, NT_DIMS, preferred_element_type=F32)


def _resident(shape):
    nd = len(shape)
    return pl.BlockSpec(shape, lambda *_: (0,) * nd, pipeline_mode=pl.Buffered(1))


def _ffn_weight_copies(w_hbm, j, f, slot, stage_refs, sems):
    wg_hbm, wu_hbm, wd_hbm = w_hbm
    cols = pl.ds(f * FFN_CHUNK, FFN_CHUNK)
    sources = (wg_hbm.at[0, j, :, cols], wu_hbm.at[0, j, :, cols], wd_hbm.at[0, j, cols, :])
    return [pltpu.make_async_copy(src, stage.at[slot], sems.at[slot, k])
            for k, (src, stage) in enumerate(zip(sources, stage_refs))]


def _swiglu(x, gpre, w_refs, h_ref, acc_ref, fetch=None):
    wg_ref, wu_ref, wd_ref = w_refs
    if fetch is not None:
        for f in range(min(FFN_STAGE_SLOTS, N_FFN_CHUNKS)):
            for copy in _ffn_weight_copies(*fetch[:2], f, f % FFN_STAGE_SLOTS, *fetch[2:]):
                copy.start()
    h_ref[...] = _rms(x, gpre).astype(BF16)
    for f in range(N_FFN_CHUNKS):
        cols = slice(f * FFN_CHUNK, (f + 1) * FFN_CHUNK)
        if fetch is not None:
            slot = f % FFN_STAGE_SLOTS
            stage_g, stage_u, stage_d = fetch[2]
            for copy in _ffn_weight_copies(*fetch[:2], f, slot, *fetch[2:]):
                copy.wait()
            wg_ref[:, cols] = stage_g[slot].astype(BF16)
            wu_ref[:, cols] = stage_u[slot].astype(BF16)
            wd_ref[cols, :] = stage_d[slot].astype(BF16)
            if f + FFN_STAGE_SLOTS < N_FFN_CHUNKS:
                for copy in _ffn_weight_copies(*fetch[:2], f + FFN_STAGE_SLOTS, slot, *fetch[2:]):
                    copy.start()
        h = h_ref[...]
        g = _dot(h, wg_ref[:, cols])
        u = _dot(h, wu_ref[:, cols])
        part = _dot((g * jax.nn.sigmoid(g) * u).astype(BF16), wd_ref[cols, :])
        if f == 0:
            acc_ref[...] = part
        else:
            acc_ref[...] += part


def _first_step_fetches(body, fetch):
    first = pl.program_id(0) == 0

    @pl.when(first)
    def _():
        body(fetch)

    @pl.when(jnp.logical_not(first))
    def _():
        body(None)


def _half_step(x, acc_ref, gpost):
    return x + _rms(acc_ref[...], 0.5 * gpost)


def _project(x, gain, win_ref, bias_ref, h_ref, zm_ref, qa_ref, kva_ref, gates_ref, chunk):
    h_ref[...] = _rms(x, gain).astype(BF16)

    def cols(lo, width):
        return _dot_nt(h_ref[...], win_ref[lo:lo + width, :])

    gates = cols(ZM_WIDTH + A_WIDTH + KV_WIDTH, LANES) + bias_ref[...]
    lane = lax.broadcasted_iota(jnp.int32, (chunk, LANES), 1)
    pos = lax.broadcasted_iota(jnp.int32, (chunk, LANES), 0)
    for c0 in range(0, gates.shape[0], chunk):
        g = gates[c0:c0 + chunk]
        b = jnp.where((lane >= GATE_F) & (lane < GATE_B), _log_sigmoid(g), 0.0)
        shift = 1
        while shift < chunk:
            b = b + jnp.where(pos >= shift, pltpu.roll(b, shift, 0), 0.0)
            shift *= 2
        x_gate = pltpu.roll(g, GATE_X - GATE_I, 1) - pltpu.roll(b, GATE_X - GATE_F, 1)
        out = jnp.where(lane < GATE_B, g, jnp.where(lane < GATE_X, pltpu.roll(b, GATE_B - GATE_F, 1), x_gate))
        gates_ref[c0:c0 + chunk, :] = jnp.where(lane < GATE_X + M_HEADS, out, 0.0)

    zm_ref[:, 0:M_WIDTH] = cols(0, M_WIDTH).astype(BF16)
    zm_ref[:, M_WIDTH:2 * M_WIDTH] = (cols(M_WIDTH, M_WIDTH) * (M_DK ** -0.5)).astype(BF16)
    zm_ref[:, 2 * M_WIDTH:3 * M_WIDTH] = cols(2 * M_WIDTH, M_WIDTH).astype(BF16)
    zm_ref[:, 3 * M_WIDTH:] = jax.nn.sigmoid(cols(3 * M_WIDTH, M_WIDTH)).astype(BF16)
    qa_ref[...] = (cols(ZM_WIDTH, A_WIDTH) * (A_HD ** -0.5)).astype(BF16)
    kva_ref[...] = cols(ZM_WIDTH + A_WIDTH, KV_WIDTH)


def _ffn_proj_kernel(x_ref, gains_ref, wg_hbm, wu_hbm, wd_hbm, win_ref, bias_ref,
                     x1_ref, zm_ref, qa_ref, kva_ref, gates_ref,
                     h_ref, acc_ref, wg_ref, wu_ref, wd_ref, stage_g, stage_u, stage_d, sems,
                     *, j, pre, post, mixer, chunk):
    def body(fetch):
        x = x_ref[...]
        _swiglu(x, gains_ref[pre:pre + 1, :], (wg_ref, wu_ref, wd_ref), h_ref, acc_ref, fetch)
        x1 = _half_step(x, acc_ref, gains_ref[post:post + 1, :])
        x1_ref[...] = x1
        _project(x1, gains_ref[mixer:mixer + 1, :], win_ref, bias_ref, h_ref, zm_ref, qa_ref, kva_ref,
                 gates_ref, chunk)

    _first_step_fetches(body, ((wg_hbm, wu_hbm, wd_hbm), j, (stage_g, stage_u, stage_d), sems))


def _mix_ffn_kernel(x_ref, um_ref, ua_ref, wo_ref, gains_ref, wg_hbm, wu_hbm, wd_hbm, o_ref,
                    h_ref, acc_ref, wg_ref, wu_ref, wd_ref, stage_g, stage_u, stage_d, sems,
                    *, j, mix, pre, post):
    def body(fetch):
        u = jnp.concatenate([um_ref[...], ua_ref[...]], axis=-1)
        x = x_ref[...] + _rms(_dot(u, wo_ref[...]), gains_ref[mix:mix + 1, :])
        _swiglu(x, gains_ref[pre:pre + 1, :], (wg_ref, wu_ref, wd_ref), h_ref, acc_ref, fetch)
        o_ref[...] = _half_step(x, acc_ref, gains_ref[post:post + 1, :])

    _first_step_fetches(body, ((wg_hbm, wu_hbm, wd_hbm), j, (stage_g, stage_u, stage_d), sems))


def _ffn_scratch(tm):
    return [pltpu.VMEM((tm, D_MODEL), BF16),
            pltpu.VMEM((tm, D_MODEL), F32),
            pltpu.VMEM((D_MODEL, D_FF), BF16),
            pltpu.VMEM((D_MODEL, D_FF), BF16),
            pltpu.VMEM((D_FF, D_MODEL), BF16),
            pltpu.VMEM((FFN_STAGE_SLOTS, D_MODEL, FFN_CHUNK), F32),
            pltpu.VMEM((FFN_STAGE_SLOTS, D_MODEL, FFN_CHUNK), F32),
            pltpu.VMEM((FFN_STAGE_SLOTS, FFN_CHUNK, D_MODEL), F32),
            pltpu.SemaphoreType.DMA((FFN_STAGE_SLOTS, 3))]


def _dense_params():
    return pltpu.CompilerParams(dimension_semantics=("arbitrary",), vmem_limit_bytes=VMEM_LIMIT)


_HBM = pl.BlockSpec(memory_space=pl.ANY)


def _ffn_proj(x, gains, wg, wu, wd, win, bias, j, pre, post, mixer, chunk):
    n = x.shape[0]
    tm = min(TOKEN_TILE, n)
    assert tm % chunk == 0

    def rows(width):
        return pl.BlockSpec((tm, width), lambda i: (i, 0))

    return pl.pallas_call(
        functools.partial(_ffn_proj_kernel, j=j, pre=pre, post=post, mixer=mixer, chunk=chunk),
        grid=(n // tm,),
        in_specs=[rows(D_MODEL), _resident(gains.shape), _HBM, _HBM, _HBM,
                  _resident(win.shape), _resident(bias.shape)],
        out_specs=[rows(D_MODEL), rows(ZM_WIDTH), rows(A_WIDTH), rows(KV_WIDTH), rows(LANES)],
        out_shape=[jax.ShapeDtypeStruct((n, D_MODEL), F32),
                   jax.ShapeDtypeStruct((n, ZM_WIDTH), BF16),
                   jax.ShapeDtypeStruct((n, A_WIDTH), BF16),
                   jax.ShapeDtypeStruct((n, KV_WIDTH), F32),
                   jax.ShapeDtypeStruct((n, LANES), F32)],
        scratch_shapes=_ffn_scratch(tm),
        compiler_params=_dense_params(),
        name="ffn_half_step_input_projection",
    )(x, gains, wg, wu, wd, win, bias)


def _mix_ffn(x, um, ua, wo, gains, wg, wu, wd, j, mix, pre, post):
    n = x.shape[0]
    tm = min(TOKEN_TILE, n)
    row = pl.BlockSpec((tm, D_MODEL), lambda i: (i, 0))
    half = pl.BlockSpec((tm, M_WIDTH), lambda i: (i, 0))
    return pl.pallas_call(
        functools.partial(_mix_ffn_kernel, j=j, mix=mix, pre=pre, post=post),
        grid=(n // tm,),
        in_specs=[row, half, half, _resident(wo.shape), _resident(gains.shape), _HBM, _HBM, _HBM],
        out_specs=row,
        out_shape=jax.ShapeDtypeStruct((n, D_MODEL), F32),
        scratch_shapes=_ffn_scratch(tm),
        compiler_params=_dense_params(),
        name="out_proj_ffn_half_step",
    )(x, um, ua, wo, gains, wg, wu, wd)


def _head_norm_gate(hh, o, gain):
    hn = hh * lax.rsqrt(jnp.mean(hh * hh, axis=-1, keepdims=True) + RMS_EPS) * gain
    return (o.astype(F32) * hn).astype(BF16)


def _mlstm_prompt_kernel(zm_ref, gates_ref, gain_ref, um_ref, c_ref, n_ref, m_ref, cnt_ref, ms_ref, *, chunk):
    step = pl.program_id(1)

    @pl.when(step == 0)
    def _():
        cnt_ref[...] = jnp.zeros_like(cnt_ref)
        ms_ref[...] = jnp.zeros_like(ms_ref)

    si = lax.broadcasted_iota(jnp.int32, (chunk, chunk), 0)
    ti = lax.broadcasted_iota(jnp.int32, (chunk, chunk), 1)
    causal = si <= ti
    ones_rows = jnp.where(lax.broadcasted_iota(jnp.int32, (LANES, chunk), 0) == 0, 1.0, 0.0).astype(BF16)

    for c in range(zm_ref.shape[0] // chunk):
        rows = slice(c * chunk, (c + 1) * chunk)
        gates = gates_ref[rows, :]
        gates_t = gates.T
        for h in range(M_HEADS):
            lanes = slice(h * M_DK, (h + 1) * M_DK)
            q = zm_ref[rows, lanes]
            k = zm_ref[rows, M_WIDTH + h * M_DK:M_WIDTH + (h + 1) * M_DK]
            v = zm_ref[rows, 2 * M_WIDTH + h * M_DV:2 * M_WIDTH + (h + 1) * M_DV]
            o = zm_ref[rows, 3 * M_WIDTH + h * M_DV:3 * M_WIDTH + (h + 1) * M_DV]
            x_col = gates[:, GATE_X + h:GATE_X + h + 1]
            ig_row = gates_t[GATE_I + h:GATE_I + h + 1, :]
            b_row = gates_t[GATE_B + h:GATE_B + h + 1, :]
            m_prev = ms_ref[h][:, 0:1]
            cnt = cnt_ref[h]

            xm = jnp.where(causal, x_col, -jnp.inf)
            mu = jnp.maximum(m_prev, jnp.max(xm, axis=0, keepdims=True))
            a = jnp.exp(m_prev - mu)
            s_t = _dot_nt(k, q) * jnp.exp(xm - mu)
            v_ext = jnp.concatenate([v.astype(F32).T.astype(BF16), ones_rows], axis=0)
            nd = a * _dot_nt(cnt.astype(BF16), q) + _dot(v_ext, s_t.astype(BF16))
            den = nd[M_DV:M_DV + 1, :]
            hh = nd[:M_DV, :] * (1.0 / jnp.maximum(jnp.abs(den), jnp.exp(-(b_row + mu))))
            hn = hh * lax.rsqrt(jnp.mean(hh * hh, axis=0, keepdims=True) + RMS_EPS)
            um_ref[rows, lanes] = (hn.T * gain_ref[:, lanes] * o.astype(F32)).astype(BF16)

            b_last = b_row[:, chunk - 1:chunk]
            g = b_last - b_row + ig_row
            m_new = jnp.maximum(b_last + m_prev, jnp.max(g, axis=-1, keepdims=True))
            decay = jnp.exp(b_last + m_prev - m_new)
            wv = (jnp.exp(g - m_new) * v_ext.astype(F32)).astype(BF16)
            cnt_ref[h] = decay * cnt + _dot(wv, k)
            ms_ref[h] = jnp.broadcast_to(m_new, (1, LANES))

    @pl.when(step == pl.num_programs(1) - 1)
    def _():
        for h in range(M_HEADS):
            c_ref[0, h] = cnt_ref[h][:M_DV, :].T
            n_ref[0, h] = cnt_ref[h][M_DV:M_DV + 1, :]
            m_ref[0, h] = ms_ref[h]


def _mlstm_prompt(zm, gates, gain, batch, seq, chunk):
    n = batch * seq
    per_step = min(MLSTM_CHUNKS_PER_STEP, n // batch // chunk)
    step_rows = per_step * chunk
    ns = n // batch // step_rows

    def rows(width):
        return pl.BlockSpec((step_rows, width), lambda b, c: (b * ns + c, 0))

    def state(*dims):
        return pl.BlockSpec((1, M_HEADS) + dims, lambda b, c: (b, 0, 0, 0))

    return pl.pallas_call(
        functools.partial(_mlstm_prompt_kernel, chunk=chunk),
        grid=(batch, ns),
        in_specs=[rows(ZM_WIDTH), rows(LANES), pl.BlockSpec((1, M_WIDTH), lambda b, c: (0, 0))],
        out_specs=[rows(M_WIDTH), state(M_DK, M_DV), state(1, M_DK), state(1, LANES)],
        out_shape=[jax.ShapeDtypeStruct((n, M_WIDTH), BF16),
                   jax.ShapeDtypeStruct((batch, M_HEADS, M_DK, M_DV), F32),
                   jax.ShapeDtypeStruct((batch, M_HEADS, 1, M_DK), F32),
                   jax.ShapeDtypeStruct((batch, M_HEADS, 1, LANES), F32)],
        scratch_shapes=[pltpu.VMEM((M_HEADS, 2 * LANES, M_DK), F32), pltpu.VMEM((M_HEADS, 1, LANES), F32)],
        compiler_params=pltpu.CompilerParams(dimension_semantics=("parallel", "arbitrary"),
                                             vmem_limit_bytes=VMEM_LIMIT),
        name="mlstm_prompt",
    )(zm, gates, gain)


SWA_BLOCKS_PER_STEP = 8


def _alibi_slope(head):
    return 2.0 ** (-8.0 * (head + 1) / A_HEADS)


def _swa_prompt_kernel(sinks_ref, qa_ref, kvc_ref, kvp_ref, ua_ref, mask_ref):
    hw = A_KV_HEADS * A_HD
    n_blocks = qa_ref.shape[0] // WINDOW
    qi = lax.broadcasted_iota(jnp.int32, (WINDOW, 2 * WINDOW), 0)
    kj = lax.broadcasted_iota(jnp.int32, (WINDOW, 2 * WINDOW), 1)
    dist = WINDOW + qi - kj
    band = (dist >= 0) & (dist < WINDOW)
    mask_ref[0] = jnp.where(band, 0.0, -jnp.inf)
    mask_ref[1] = jnp.where(band & (kj >= WINDOW), 0.0, -jnp.inf)
    lane = lax.broadcasted_iota(jnp.int32, (WINDOW, LANES), 1)
    low = lane < A_HD
    first = pl.program_id(1) == 0

    assert 2 * WINDOW <= 256, "positions must stay exactly representable in bf16"
    q_pos = (WINDOW + lax.broadcasted_iota(jnp.int32, (WINDOW, LANES), 0)).astype(F32)
    k_lane = lax.broadcasted_iota(jnp.int32, (2 * WINDOW, LANES), 1)
    k_pos = lax.broadcasted_iota(jnp.int32, (2 * WINDOW, LANES), 0).astype(F32)
    spare = (A_HD, 0)
    k_bias = [jnp.where(k_lane == spare[kvh], 1.0, jnp.where(k_lane == spare[kvh] + 1, k_pos, 0.0)).astype(BF16)
              for kvh in range(A_KV_HEADS)]
    q_bias = {}
    for head in range(A_HEADS):
        slope = _alibi_slope(head)
        at = spare[head // A_GROUP]
        q_bias[head] = jnp.where(lane == at, -slope * q_pos, jnp.where(lane == at + 1, slope, 0.0)).astype(BF16)

    for j in range(n_blocks):
        rows = slice(j * WINDOW, (j + 1) * WINDOW)
        kvc = kvc_ref[rows, :]
        kvp = kvp_ref[...] if j == 0 else kvc_ref[(j - 1) * WINDOW:j * WINDOW, :]
        mask = mask_ref[jnp.where(first, 1, 0)] if j == 0 else mask_ref[0]
        kk = jnp.concatenate([kvp[:, :hw], kvc[:, :hw]], axis=0).astype(BF16)
        kk_biased = [jnp.where(k_lane < A_HD, kk, k_bias[0]), jnp.where(k_lane < A_HD, k_bias[1], kk)]
        vv = jnp.concatenate([kvp[:, hw:], kvc[:, hw:]], axis=0).astype(BF16)
        for p in range(A_GROUP):
            qp = qa_ref[rows, p * LANES:(p + 1) * LANES]
            outs = []
            for kvh in range(A_KV_HEADS):
                head = kvh * A_GROUP + p
                keep = low if kvh == 0 else jnp.logical_not(low)
                s = _dot_nt(jnp.where(keep, qp, q_bias[head]), kk_biased[kvh])
                s = s + mask
                sink = sinks_ref[head]
                mx = jnp.maximum(jnp.max(s, axis=-1, keepdims=True), sink)
                e = jnp.exp(s - mx)
                den = jnp.sum(e, axis=-1, keepdims=True) + jnp.exp(sink - mx)
                outs.append(_dot(e.astype(BF16), vv) / den)
            ua_ref[rows, p * LANES:(p + 1) * LANES] = jnp.where(low, outs[0], outs[1]).astype(BF16)


def _swa_prompt(sinks, qa, kva, batch, seq):
    n = batch * seq
    per_step = min(SWA_BLOCKS_PER_STEP, n // batch // WINDOW)
    rows = per_step * WINDOW
    ns = n // batch // rows

    def step_rows(width):
        return pl.BlockSpec((rows, width), lambda b, i: (b * ns + i, 0))

    def prev_block(b, i):
        return (jnp.maximum((b * ns + i) * per_step - 1, 0), 0)

    return pl.pallas_call(
        _swa_prompt_kernel,
        grid=(batch, ns),
        in_specs=[pl.BlockSpec(memory_space=pltpu.SMEM), step_rows(A_WIDTH), step_rows(KV_WIDTH),
                  pl.BlockSpec((WINDOW, KV_WIDTH), prev_block)],
        out_specs=step_rows(A_WIDTH),
        out_shape=jax.ShapeDtypeStruct((n, A_WIDTH), BF16),
        scratch_shapes=[pltpu.VMEM((2, WINDOW, 2 * WINDOW), F32)],
        compiler_params=pltpu.CompilerParams(dimension_semantics=("parallel", "parallel"),
                                             vmem_limit_bytes=VMEM_LIMIT),
        name="swa_prompt",
    )(sinks, qa, kva, kva)


SAMPLE_ROWS = 128


def _mlstm_sample_kernel(q_ref, k_ref, v_ref, o_ref, gates_ref, gain_ref, c_ref, n_ref, m_ref,
                         um_ref, co_ref, no_ref, mo_ref,
                         qf_ref, qc_ref, nexp_ref, wkk_ref, dec_ref, *, steps):
    rows = SAMPLE_ROWS
    nb = rows // steps
    head = pl.program_id(1)
    row = lax.broadcasted_iota(jnp.int32, (rows, rows), 0)
    col = lax.broadcasted_iota(jnp.int32, (rows, rows), 1)
    same = (row // steps) == (col // steps)
    causal = same & (col <= row)
    eye = row == col

    def to_row(x_col):
        return jnp.sum(jnp.where(eye, x_col, 0.0), axis=0, keepdims=True)

    def pick(mask, x_row, fill, reduce):
        return reduce(jnp.where(mask, x_row, fill), axis=-1, keepdims=True)

    gates = gates_ref[...]
    lane = lax.broadcasted_iota(jnp.int32, (rows, LANES), 1)
    ig_col = jnp.sum(jnp.where(lane == head, gates, 0.0), axis=-1, keepdims=True)
    fg_col = jnp.sum(jnp.where(lane == head + M_HEADS, gates, 0.0), axis=-1, keepdims=True)
    lf_row = to_row(_log_sigmoid(fg_col))
    b_col = pick(causal, lf_row, 0.0, jnp.sum)
    b_row = to_row(b_col)
    ig_row = to_row(ig_col)
    m_prev = m_ref[0]

    d = jnp.where(causal, b_col - b_row + ig_row, -jnp.inf)
    inter = b_col + m_prev
    m_t = jnp.maximum(inter, jnp.max(d, axis=-1, keepdims=True))
    w = jnp.exp(d - m_t)
    a = jnp.exp(inter - m_t)

    q = q_ref[...]
    k = k_ref[...]
    v = v_ref[...]
    qf_ref[...] = q.astype(F32)
    for t in range(steps):
        nexp_ref[pl.ds(t, nb, stride=steps), :] = n_ref[...]

    sub = lax.broadcasted_iota(jnp.int32, (2 * steps, M_DV), 0)

    def qc_pair(j, carry):
        q8 = qf_ref[pl.ds(pl.multiple_of(j * 2 * steps, 2 * steps), 2 * steps), :].astype(BF16)
        r0 = _dot(q8, c_ref[2 * j, 0].astype(BF16))
        r1 = _dot(q8, c_ref[2 * j + 1, 0].astype(BF16))
        qc_ref[pl.ds(pl.multiple_of(j * 2 * steps, 2 * steps), 2 * steps), :] = jnp.where(sub < steps, r0, r1)
        return carry

    lax.fori_loop(0, nb // 2, qc_pair, 0, unroll=True)

    s = _dot_nt(q, k) * w
    qn = jnp.sum(qf_ref[...] * nexp_ref[...], axis=-1, keepdims=True)
    num = a * qc_ref[...] + _dot(s.astype(BF16), v)
    den = a * qn + jnp.sum(s, axis=-1, keepdims=True)
    hh = num / jnp.maximum(jnp.abs(den), jnp.exp(-m_t))
    um_ref[...] = _head_norm_gate(hh, o_ref[...], gain_ref[...])

    last = same & ((col % steps) == steps - 1)
    b_last = pick(last, b_row, 0.0, jnp.sum)
    g_col = b_last - b_col + ig_col
    g_max = pick(same, to_row(g_col), -jnp.inf, jnp.max)
    m_new = jnp.maximum(b_last + m_prev, g_max)
    decay = jnp.exp(b_last + m_prev - m_new)
    wk = jnp.exp(g_col - m_new)
    mo_ref[0] = m_new

    kf = k.astype(F32)
    wkk_ref[...] = wk * kf
    dec_ref[...] = jnp.broadcast_to(decay, (rows, M_DV))
    n_sum = wkk_ref[pl.ds(0, nb, stride=steps), :]
    for t in range(1, steps):
        n_sum = n_sum + wkk_ref[pl.ds(t, nb, stride=steps), :]
    no_ref[...] = dec_ref[pl.ds(0, nb, stride=steps), :] * n_ref[...] + n_sum

    k_t = kf.T
    wv = (wk * v.astype(F32)).astype(BF16)
    tok_batch = lax.broadcasted_iota(jnp.int32, (M_DK, rows), 1) // steps

    def update(b, carry):
        upd = _dot(jnp.where(tok_batch == b, k_t, 0.0).astype(BF16), wv)
        co_ref[b, 0] = dec_ref[pl.ds(b * steps, 1), :] * c_ref[b, 0] + upd
        return carry

    lax.fori_loop(0, nb, update, 0, unroll=True)


def _mlstm_sample(zm, gates, gain, c, n2, m_rows, steps):
    rows = zm.shape[0]
    batch = c.shape[0]
    nb = SAMPLE_ROWS // steps
    groups = rows // SAMPLE_ROWS

    def zcol(section):
        return pl.BlockSpec((SAMPLE_ROWS, M_DK), lambda g, h: (g, section * M_HEADS + h))

    c_spec = pl.BlockSpec((nb, 1, M_DK, M_DV), lambda g, h: (g, h, 0, 0))
    n_spec = pl.BlockSpec((nb, M_DK), lambda g, h: (g, h))
    m_spec = pl.BlockSpec((1, SAMPLE_ROWS, 1), lambda g, h: (h, g, 0))
    return pl.pallas_call(
        functools.partial(_mlstm_sample_kernel, steps=steps),
        grid=(groups, M_HEADS),
        in_specs=[zcol(0), zcol(1), zcol(2), zcol(3),
                  pl.BlockSpec((SAMPLE_ROWS, LANES), lambda g, h: (g, 0)),
                  pl.BlockSpec((1, M_DV), lambda g, h: (0, h)),
                  c_spec, n_spec, m_spec],
        out_specs=[pl.BlockSpec((SAMPLE_ROWS, M_DV), lambda g, h: (g, h)), c_spec, n_spec, m_spec],
        out_shape=[jax.ShapeDtypeStruct((rows, M_WIDTH), BF16),
                   jax.ShapeDtypeStruct(c.shape, F32),
                   jax.ShapeDtypeStruct(n2.shape, F32),
                   jax.ShapeDtypeStruct(m_rows.shape, F32)],
        scratch_shapes=[pltpu.VMEM((SAMPLE_ROWS, M_DK), F32),
                        pltpu.VMEM((SAMPLE_ROWS, M_DV), F32),
                        pltpu.VMEM((SAMPLE_ROWS, M_DK), F32),
                        pltpu.VMEM((SAMPLE_ROWS, M_DK), F32),
                        pltpu.VMEM((SAMPLE_ROWS, M_DV), F32)],
        compiler_params=pltpu.CompilerParams(dimension_semantics=("parallel", "parallel"),
                                             vmem_limit_bytes=VMEM_LIMIT),
        name="mlstm_sample",
    )(zm, zm, zm, zm, gates, gain, c, n2, m_rows)


def _swa_sample_kernel(sinks_ref, qa_ref, kva_ref, ckt_ref, cvt_ref, ua_ref, nkt_ref, nvt_ref,
                       s_ref, e_ref, pv_ref, *, steps):
    rows = SAMPLE_ROWS
    nb = rows // steps
    pair_rows = 2 * steps
    hw = A_KV_HEADS * A_HD
    tok = lax.broadcasted_iota(jnp.int32, (rows, LANES), 0)
    lane = lax.broadcasted_iota(jnp.int32, (rows, LANES), 1)
    low = lane < A_HD

    qf = qa_ref[...].astype(F32)
    q_blocks = []
    for p in range(A_GROUP):
        qp = qf[:, p * LANES:(p + 1) * LANES]
        q_blocks.append(jnp.where(low, qp, 0.0))
        q_blocks.append(jnp.where(low, 0.0, qp))
    n_blk = len(q_blocks)

    kv_new = kva_ref[...]
    kv_new_t = kv_new.T
    k_new_t = kv_new_t[:hw]
    v_new_t = kv_new_t[hw:]
    v_new = kv_new[:, hw:].astype(BF16)
    s_new = _dot(jnp.concatenate(q_blocks, axis=0).astype(BF16), k_new_t.astype(BF16))

    def pair_rows_of(blocks, j):
        return jnp.concatenate([blk[j * pair_rows:(j + 1) * pair_rows] for blk in blocks], axis=0)

    first = (lax.broadcasted_iota(jnp.int32, (n_blk * pair_rows, LANES), 0) % pair_rows) < steps

    for j in range(nb // 2):
        q_pair = pair_rows_of(q_blocks, j).astype(BF16)
        s_pair = jnp.where(first, _dot(q_pair, ckt_ref[2 * j].astype(BF16)),
                           _dot(q_pair, ckt_ref[2 * j + 1].astype(BF16)))
        for blk in range(n_blk):
            s_ref[pl.ds(blk * rows + j * pair_rows, pair_rows), :] = s_pair[blk * pair_rows:(blk + 1) * pair_rows]

    t_q = tok % steps
    dist_c = WINDOW + t_q - lane
    valid_c = dist_c < WINDOW
    dist_n = t_q - lane % steps
    valid_n = ((tok // steps) == (lane // steps)) & (dist_n >= 0)
    dist_cf = dist_c.astype(F32)
    dist_nf = dist_n.astype(F32)
    pv_new = []
    inv_den = []
    for blk in range(n_blk):
        head = (blk % A_KV_HEADS) * A_GROUP + blk // A_KV_HEADS
        slope = _alibi_slope(head)
        sink = sinks_ref[head]
        r = pl.ds(blk * rows, rows)
        s_c = jnp.where(valid_c, s_ref[r, :] - slope * dist_cf, -jnp.inf)
        s_n = jnp.where(valid_n, s_new[blk * rows:(blk + 1) * rows] - slope * dist_nf, -jnp.inf)
        mx = jnp.maximum(jnp.maximum(jnp.max(s_c, axis=-1, keepdims=True),
                                     jnp.max(s_n, axis=-1, keepdims=True)), sink)
        e_c = jnp.exp(s_c - mx)
        e_n = jnp.exp(s_n - mx)
        den = jnp.sum(e_c, axis=-1, keepdims=True) + jnp.sum(e_n, axis=-1, keepdims=True) + jnp.exp(sink - mx)
        e_ref[r, :] = e_c
        pv_new.append(_dot(e_n.astype(BF16), v_new))
        inv_den.append(1.0 / den)

    for j in range(nb // 2):
        e_pair = jnp.concatenate([e_ref[pl.ds(blk * rows + j * pair_rows, pair_rows), :] for blk in range(n_blk)],
                                 axis=0).astype(BF16)
        pv_pair = jnp.where(first, _dot_nt(e_pair, cvt_ref[2 * j].astype(BF16)),
                            _dot_nt(e_pair, cvt_ref[2 * j + 1].astype(BF16)))
        for blk in range(n_blk):
            pv_ref[pl.ds(blk * rows + j * pair_rows, pair_rows), :] = pv_pair[blk * pair_rows:(blk + 1) * pair_rows]

    for p in range(A_GROUP):
        halves = []
        for kvh in range(A_KV_HEADS):
            blk = p * A_KV_HEADS + kvh
            halves.append((pv_ref[pl.ds(blk * rows, rows), :] + pv_new[blk]) * inv_den[blk])
        ua_ref[:, p * LANES:(p + 1) * LANES] = jnp.where(low, halves[0], halves[1]).astype(BF16)

    tail = lane >= WINDOW - steps
    for b in range(nb):
        shift = (WINDOW - steps - b * steps) % rows
        nkt_ref[b] = jnp.where(tail, pltpu.roll(k_new_t, shift, 1), pltpu.roll(ckt_ref[b], WINDOW - steps, 1))
        nvt_ref[b] = jnp.where(tail, pltpu.roll(v_new_t, shift, 1), pltpu.roll(cvt_ref[b], WINDOW - steps, 1))


def _swa_sample(sinks, qa, kva, ckt, cvt, steps):
    batch = ckt.shape[0]
    assert ckt.shape[1:] == (A_KV_HEADS * A_HD, WINDOW) and SAMPLE_ROWS == WINDOW == LANES
    nb = SAMPLE_ROWS // steps
    cache = pl.BlockSpec((nb, LANES, WINDOW), lambda g: (g, 0, 0))

    def rows(width):
        return pl.BlockSpec((SAMPLE_ROWS, width), lambda g: (g, 0))

    stacked = (A_HEADS * SAMPLE_ROWS, LANES)
    return pl.pallas_call(
        functools.partial(_swa_sample_kernel, steps=steps),
        grid=(batch // nb,),
        in_specs=[pl.BlockSpec(memory_space=pltpu.SMEM), rows(A_WIDTH), rows(KV_WIDTH), cache, cache],
        out_specs=[rows(A_WIDTH), cache, cache],
        out_shape=[jax.ShapeDtypeStruct((batch * steps, A_WIDTH), BF16),
                   jax.ShapeDtypeStruct(ckt.shape, F32),
                   jax.ShapeDtypeStruct(cvt.shape, F32)],
        scratch_shapes=[pltpu.VMEM(stacked, F32), pltpu.VMEM(stacked, F32), pltpu.VMEM(stacked, F32)],
        compiler_params=pltpu.CompilerParams(dimension_semantics=("parallel",),
                                             vmem_limit_bytes=VMEM_LIMIT),
        name="swa_sample",
    )(sinks, qa, kva, ckt, cvt)


def _prep_proj(w_in, b_gate):
    m_end = 4 * M_WIDTH
    g_end = m_end + 2 * M_HEADS
    wt = w_in.T
    qa = wt[g_end:g_end + A_WIDTH].reshape(A_KV_HEADS, A_GROUP, A_HD, D_MODEL).transpose(1, 0, 2, 3)
    gates = jnp.pad(wt[m_end:g_end], ((0, LANES - 2 * M_HEADS), (0, 0)))
    win = jnp.concatenate([wt[:m_end], qa.reshape(A_WIDTH, D_MODEL), wt[g_end + A_WIDTH:], gates],
                          axis=0).astype(BF16)
    bias = jnp.pad(b_gate, (0, LANES - 2 * M_HEADS)).reshape(1, LANES)
    return win, bias


def _prep_out(w_out):
    wa = w_out[M_WIDTH:].reshape(A_KV_HEADS, A_GROUP, A_HD, D_MODEL).transpose(1, 0, 2, 3)
    return jnp.concatenate([w_out[:M_WIDTH], wa.reshape(A_WIDTH, D_MODEL)], axis=0).astype(BF16)


def kernel(x_prompt, x_sample, cache_swa_k, cache_swa_v, state_mlstm_C, state_mlstm_n, state_mlstm_m,
           norm_gains, ffn_w_gate, ffn_w_up, ffn_w_down, w_in, b_gate, mlstm_norm_gain, attn_sinks, w_out):
    assert norm_gains.shape[0] == 1, "single layer"
    batch, seq, _ = x_prompt.shape
    dec_batch, steps, _ = x_sample.shape
    gains = norm_gains[0]
    ffn = (ffn_w_gate, ffn_w_up, ffn_w_down)
    win, bias = _prep_proj(w_in[0], b_gate[0])
    wo = _prep_out(w_out[0])
    gain = mlstm_norm_gain[0].reshape(1, M_WIDTH)
    sinks = attn_sinks[0]
    kv_lanes = A_KV_HEADS * A_HD

    chunk = min(MLSTM_CHUNK, seq)
    assert seq % chunk == 0
    xp, zm, qa, kva, gates = _ffn_proj(x_prompt.reshape(batch * seq, D_MODEL), gains, *ffn, win, bias,
                                       0, 0, 1, 2, chunk)
    um, p_c, p_n, p_m = _mlstm_prompt(zm, gates, gain, batch, seq, chunk)
    ua = _swa_prompt(sinks, qa, kva, batch, seq)
    yp = _mix_ffn(xp, um, ua, wo, gains, *ffn, 1, 3, 4, 5).reshape(batch, seq, D_MODEL)
    keep = min(WINDOW, seq)
    kv_tail = kva.reshape(batch, seq, KV_WIDTH)[:, seq - keep:]
    p_k = kv_tail[..., :kv_lanes].reshape(1, batch, keep, A_KV_HEADS, A_HD)
    p_v = kv_tail[..., kv_lanes:].reshape(1, batch, keep, A_KV_HEADS, A_HD)
    p_c = p_c[None]
    p_n = p_n[None, :, :, 0, :]
    p_m = p_m[None, :, :, 0, 0]

    xs, zm_s, qa_s, kva_s, gates_s = _ffn_proj(x_sample.reshape(dec_batch * steps, D_MODEL), gains, *ffn,
                                               win, bias, 0, 0, 1, 2, min(chunk, dec_batch * steps))
    m_rows = jnp.repeat(state_mlstm_m[0].T, steps, axis=1)[..., None]
    um_s, s_c, s_n2, s_m_rows = _mlstm_sample(zm_s, gates_s, gain, state_mlstm_C[0],
                                              state_mlstm_n[0].reshape(dec_batch, M_HEADS * M_DK),
                                              m_rows, steps)
    wc = cache_swa_k.shape[2]
    ua_s, s_kt, s_vt = _swa_sample(sinks, qa_s, kva_s,
                                   cache_swa_k[0].reshape(dec_batch, wc, kv_lanes).transpose(0, 2, 1),
                                   cache_swa_v[0].reshape(dec_batch, wc, kv_lanes).transpose(0, 2, 1), steps)
    ys = _mix_ffn(xs, um_s, ua_s, wo, gains, *ffn, 1, 3, 4, 5).reshape(dec_batch, steps, D_MODEL)
    s_n = s_n2.reshape(1, dec_batch, M_HEADS, M_DK)
    s_m = s_m_rows[:, ::steps, 0].T[None]
    s_k = s_kt.transpose(0, 2, 1).reshape(1, dec_batch, wc, A_KV_HEADS, A_HD)
    s_v = s_vt.transpose(0, 2, 1).reshape(1, dec_batch, wc, A_KV_HEADS, A_HD)
    return (yp, ys, p_k, p_v, p_c, p_n, p_m, s_k, s_v, s_c[None], s_n, s_m)
```

```python
import functools

import jax
import jax.numpy as jnp
from jax import lax
from jax.experimental import pallas as pl
from jax.experimental.pallas import tpu as pltpu

F32 = jnp.float32
BF16 = jnp.bfloat16

D_MODEL = 1024
D_FF = 2816
M_HEADS = 4
M_DK = 128
M_DV = 128
M_WIDTH = M_HEADS * M_DV
A_HEADS = 8
A_KV_HEADS = 2
A_GROUP = A_HEADS // A_KV_HEADS
A_HD = 64
A_WIDTH = A_HEADS * A_HD
WINDOW = 128
RMS_EPS = 1e-6

LANES = 128
FFN_CHUNK = 256
N_FFN_CHUNKS = D_FF // FFN_CHUNK
FFN_STAGE_SLOTS = 2
TOKEN_TILE = 512
MLSTM_CHUNK = 256
ZM_WIDTH = 4 * M_WIDTH
KV_WIDTH = 2 * A_KV_HEADS * A_HD
VMEM_LIMIT = 56 * 1024 * 1024
MLSTM_CHUNKS_PER_STEP = 8
GATE_I, GATE_F, GATE_B, GATE_X = 0, M_HEADS, 2 * M_HEADS, 3 * M_HEADS

NT_DIMS = (((1,), (1,)), ((), ()))


def _rms(x, g):
    ms = jnp.mean(x * x, axis=-1, keepdims=True)
    return x * lax.rsqrt(ms + RMS_EPS) * g


def _log_sigmoid(x):
    return jnp.minimum(x, 0.0) - jnp.log1p(jnp.exp(-jnp.abs(x)))


def _dot(a, b):
    return jnp.dot(a, b, preferred_element_type=F32)


def _dot_nt(a, b):
    return lax.dot_general(a, b, NT_DIMS, preferred_element_type=F32)


def _resident(shape):
    nd = len(shape)
    return pl.BlockSpec(shape, lambda *_: (0,) * nd, pipeline_mode=pl.Buffered(1))


def _ffn_weight_copies(w_hbm, j, f, slot, stage_refs, sems):
    wg_hbm, wu_hbm, wd_hbm = w_hbm
    cols = pl.ds(f * FFN_CHUNK, FFN_CHUNK)
    sources = (wg_hbm.at[0, j, :, cols], wu_hbm.at[0, j, :, cols], wd_hbm.at[0, j, cols, :])
    return [pltpu.make_async_copy(src, stage.at[slot], sems.at[slot, k])
            for k, (src, stage) in enumerate(zip(sources, stage_refs))]


def _swiglu(x, gpre, w_refs, h_ref, acc_ref, fetch=None):
    wg_ref, wu_ref, wd_ref = w_refs
    if fetch is not None:
        for f in range(min(FFN_STAGE_SLOTS, N_FFN_CHUNKS)):
            for copy in _ffn_weight_copies(*fetch[:2], f, f % FFN_STAGE_SLOTS, *fetch[2:]):
                copy.start()
    h_ref[...] = _rms(x, gpre).astype(BF16)
    for f in range(N_FFN_CHUNKS):
        cols = slice(f * FFN_CHUNK, (f + 1) * FFN_CHUNK)
        if fetch is not None:
            slot = f % FFN_STAGE_SLOTS
            stage_g, stage_u, stage_d = fetch[2]
            for copy in _ffn_weight_copies(*fetch[:2], f, slot, *fetch[2:]):
                copy.wait()
            wg_ref[:, cols] = stage_g[slot].astype(BF16)
            wu_ref[:, cols] = stage_u[slot].astype(BF16)
            wd_ref[cols, :] = stage_d[slot].astype(BF16)
            if f + FFN_STAGE_SLOTS < N_FFN_CHUNKS:
                for copy in _ffn_weight_copies(*fetch[:2], f + FFN_STAGE_SLOTS, slot, *fetch[2:]):
                    copy.start()
        h = h_ref[...]
        g = _dot(h, wg_ref[:, cols])
        u = _dot(h, wu_ref[:, cols])
        part = _dot((g * jax.nn.sigmoid(g) * u).astype(BF16), wd_ref[cols, :])
        if f == 0:
            acc_ref[...] = part
        else:
            acc_ref[...] += part


def _first_step_fetches(body, fetch):
    first = pl.program_id(0) == 0

    @pl.when(first)
    def _():
        body(fetch)

    @pl.when(jnp.logical_not(first))
    def _():
        body(None)


def _half_step(x, acc_ref, gpost):
    return x + _rms(acc_ref[...], 0.5 * gpost)


def _project(x, gain, win_ref, bias_ref, h_ref, zm_ref, qa_ref, kva_ref, gates_ref, chunk):
    h_ref[...] = _rms(x, gain).astype(BF16)

    def cols(lo, width):
        return _dot_nt(h_ref[...], win_ref[lo:lo + width, :])

    gates = cols(ZM_WIDTH + A_WIDTH + KV_WIDTH, LANES) + bias_ref[...]
    lane = lax.broadcasted_iota(jnp.int32, (chunk, LANES), 1)
    pos = lax.broadcasted_iota(jnp.int32, (chunk, LANES), 0)
    for c0 in range(0, gates.shape[0], chunk):
        g = gates[c0:c0 + chunk]
        b = jnp.where((lane >= GATE_F) & (lane < GATE_B), _log_sigmoid(g), 0.0)
        shift = 1
        while shift < chunk:
            b = b + jnp.where(pos >= shift, pltpu.roll(b, shift, 0), 0.0)
            shift *= 2
        x_gate = pltpu.roll(g, GATE_X - GATE_I, 1) - pltpu.roll(b, GATE_X - GATE_F, 1)
        out = jnp.where(lane < GATE_B, g, jnp.where(lane < GATE_X, pltpu.roll(b, GATE_B - GATE_F, 1), x_gate))
        gates_ref[c0:c0 + chunk, :] = jnp.where(lane < GATE_X + M_HEADS, out, 0.0)

    zm_ref[:, 0:M_WIDTH] = cols(0, M_WIDTH).astype(BF16)
    zm_ref[:, M_WIDTH:2 * M_WIDTH] = (cols(M_WIDTH, M_WIDTH) * (M_DK ** -0.5)).astype(BF16)
    zm_ref[:, 2 * M_WIDTH:3 * M_WIDTH] = cols(2 * M_WIDTH, M_WIDTH).astype(BF16)
    zm_ref[:, 3 * M_WIDTH:] = jax.nn.sigmoid(cols(3 * M_WIDTH, M_WIDTH)).astype(BF16)
    qa_ref[...] = (cols(ZM_WIDTH, A_WIDTH) * (A_HD ** -0.5)).astype(BF16)
    kva_ref[...] = cols(ZM_WIDTH + A_WIDTH, KV_WIDTH)


def _ffn_proj_kernel(x_ref, gains_ref, wg_hbm, wu_hbm, wd_hbm, win_ref, bias_ref,
                     x1_ref, zm_ref, qa_ref, kva_ref, gates_ref,
                     h_ref, acc_ref, wg_ref, wu_ref, wd_ref, stage_g, stage_u, stage_d, sems,
                     *, j, pre, post, mixer, chunk):
    def body(fetch):
        x = x_ref[...]
        _swiglu(x, gains_ref[pre:pre + 1, :], (wg_ref, wu_ref, wd_ref), h_ref, acc_ref, fetch)
        x1 = _half_step(x, acc_ref, gains_ref[post:post + 1, :])
        x1_ref[...] = x1
        _project(x1, gains_ref[mixer:mixer + 1, :], win_ref, bias_ref, h_ref, zm_ref, qa_ref, kva_ref,
                 gates_ref, chunk)

    _first_step_fetches(body, ((wg_hbm, wu_hbm, wd_hbm), j, (stage_g, stage_u, stage_d), sems))


def _mix_ffn_kernel(x_ref, um_ref, ua_ref, wo_ref, gains_ref, wg_hbm, wu_hbm, wd_hbm, o_ref,
                    h_ref, acc_ref, wg_ref, wu_ref, wd_ref, stage_g, stage_u, stage_d, sems,
                    *, j, mix, pre, post):
    def body(fetch):
        u = jnp.concatenate([um_ref[...], ua_ref[...]], axis=-1)
        x = x_ref[...] + _rms(_dot(u, wo_ref[...]), gains_ref[mix:mix + 1, :])
        _swiglu(x, gains_ref[pre:pre + 1, :], (wg_ref, wu_ref, wd_ref), h_ref, acc_ref, fetch)
        o_ref[...] = _half_step(x, acc_ref, gains_ref[post:post + 1, :])

    _first_step_fetches(body, ((wg_hbm, wu_hbm, wd_hbm), j, (stage_g, stage_u, stage_d), sems))


def _ffn_scratch(tm):
    return [pltpu.VMEM((tm, D_MODEL), BF16),
            pltpu.VMEM((tm, D_MODEL), F32),
            pltpu.VMEM((D_MODEL, D_FF), BF16),
            pltpu.VMEM((D_MODEL, D_FF), BF16),
            pltpu.VMEM((D_FF, D_MODEL), BF16),
            pltpu.VMEM((FFN_STAGE_SLOTS, D_MODEL, FFN_CHUNK), F32),
            pltpu.VMEM((FFN_STAGE_SLOTS, D_MODEL, FFN_CHUNK), F32),
            pltpu.VMEM((FFN_STAGE_SLOTS, FFN_CHUNK, D_MODEL), F32),
            pltpu.SemaphoreType.DMA((FFN_STAGE_SLOTS, 3))]


def _dense_params():
    return pltpu.CompilerParams(dimension_semantics=("arbitrary",), vmem_limit_bytes=VMEM_LIMIT)


_HBM = pl.BlockSpec(memory_space=pl.ANY)


def _ffn_proj(x, gains, wg, wu, wd, win, bias, j, pre, post, mixer, chunk):
    n = x.shape[0]
    tm = min(TOKEN_TILE, n)
    assert tm % chunk == 0

    def rows(width):
        return pl.BlockSpec((tm, width), lambda i: (i, 0))

    return pl.pallas_call(
        functools.partial(_ffn_proj_kernel, j=j, pre=pre, post=post, mixer=mixer, chunk=chunk),
        grid=(n // tm,),
        in_specs=[rows(D_MODEL), _resident(gains.shape), _HBM, _HBM, _HBM,
                  _resident(win.shape), _resident(bias.shape)],
        out_specs=[rows(D_MODEL), rows(ZM_WIDTH), rows(A_WIDTH), rows(KV_WIDTH), rows(LANES)],
        out_shape=[jax.ShapeDtypeStruct((n, D_MODEL), F32),
                   jax.ShapeDtypeStruct((n, ZM_WIDTH), BF16),
                   jax.ShapeDtypeStruct((n, A_WIDTH), BF16),
                   jax.ShapeDtypeStruct((n, KV_WIDTH), F32),
                   jax.ShapeDtypeStruct((n, LANES), F32)],
        scratch_shapes=_ffn_scratch(tm),
        compiler_params=_dense_params(),
        name="ffn_half_step_input_projection",
    )(x, gains, wg, wu, wd, win, bias)


def _mix_ffn(x, um, ua, wo, gains, wg, wu, wd, j, mix, pre, post):
    n = x.shape[0]
    tm = min(TOKEN_TILE, n)
    row = pl.BlockSpec((tm, D_MODEL), lambda i: (i, 0))
    half = pl.BlockSpec((tm, M_WIDTH), lambda i: (i, 0))
    return pl.pallas_call(
        functools.partial(_mix_ffn_kernel, j=j, mix=mix, pre=pre, post=post),
        grid=(n // tm,),
        in_specs=[row, half, half, _resident(wo.shape), _resident(gains.shape), _HBM, _HBM, _HBM],
        out_specs=row,
        out_shape=jax.ShapeDtypeStruct((n, D_MODEL), F32),
        scratch_shapes=_ffn_scratch(tm),
        compiler_params=_dense_params(),
        name="out_proj_ffn_half_step",
    )(x, um, ua, wo, gains, wg, wu, wd)


def _head_norm_gate(hh, o, gain):
    hn = hh * lax.rsqrt(jnp.mean(hh * hh, axis=-1, keepdims=True) + RMS_EPS) * gain
    return (o.astype(F32) * hn).astype(BF16)


def _mlstm_prompt_kernel(zm_ref, gates_ref, gain_ref, um_ref, c_ref, n_ref, m_ref, cnt_ref, ms_ref, *, chunk):
    step = pl.program_id(1)

    @pl.when(step == 0)
    def _():
        cnt_ref[...] = jnp.zeros_like(cnt_ref)
        ms_ref[...] = jnp.zeros_like(ms_ref)

    si = lax.broadcasted_iota(jnp.int32, (chunk, chunk), 0)
    ti = lax.broadcasted_iota(jnp.int32, (chunk, chunk), 1)
    causal = si <= ti
    ones_rows = jnp.where(lax.broadcasted_iota(jnp.int32, (LANES, chunk), 0) == 0, 1.0, 0.0).astype(BF16)

    for c in range(zm_ref.shape[0] // chunk):
        rows = slice(c * chunk, (c + 1) * chunk)
        gates = gates_ref[rows, :]
        gates_t = gates.T
        for h in range(M_HEADS):
            lanes = slice(h * M_DK, (h + 1) * M_DK)
            q = zm_ref[rows, lanes]
            k = zm_ref[rows, M_WIDTH + h * M_DK:M_WIDTH + (h + 1) * M_DK]
            v = zm_ref[rows, 2 * M_WIDTH + h * M_DV:2 * M_WIDTH + (h + 1) * M_DV]
            o = zm_ref[rows, 3 * M_WIDTH + h * M_DV:3 * M_WIDTH + (h + 1) * M_DV]
            x_col = gates[:, GATE_X + h:GATE_X + h + 1]
            ig_row = gates_t[GATE_I + h:GATE_I + h + 1, :]
            b_row = gates_t[GATE_B + h:GATE_B + h + 1, :]
            m_prev = ms_ref[h][:, 0:1]
            cnt = cnt_ref[h]

            xm = jnp.where(causal, x_col, -jnp.inf)
            mu = jnp.maximum(m_prev, jnp.max(xm, axis=0, keepdims=True))
            a = jnp.exp(m_prev - mu)
            s_t = _dot_nt(k, q) * jnp.exp(xm - mu)
            v_ext = jnp.concatenate([v.astype(F32).T.astype(BF16), ones_rows], axis=0)
            nd = a * _dot_nt(cnt.astype(BF16), q) + _dot(v_ext, s_t.astype(BF16))
            den = nd[M_DV:M_DV + 1, :]
            hh = nd[:M_DV, :] * (1.0 / jnp.maximum(jnp.abs(den), jnp.exp(-(b_row + mu))))
            hn = hh * lax.rsqrt(jnp.mean(hh * hh, axis=0, keepdims=True) + RMS_EPS)
            um_ref[rows, lanes] = (hn.T * gain_ref[:, lanes] * o.astype(F32)).astype(BF16)

            b_last = b_row[:, chunk - 1:chunk]
            g = b_last - b_row + ig_row
            m_new = jnp.maximum(b_last + m_prev, jnp.max(g, axis=-1, keepdims=True))
            decay = jnp.exp(b_last + m_prev - m_new)
            wv = (jnp.exp(g - m_new) * v_ext.astype(F32)).astype(BF16)
            cnt_ref[h] = decay * cnt + _dot(wv, k)
            ms_ref[h] = jnp.broadcast_to(m_new, (1, LANES))

    @pl.when(step == pl.num_programs(1) - 1)
    def _():
        for h in range(M_HEADS):
            c_ref[0, h] = cnt_ref[h][:M_DV, :].T
            n_ref[0, h] = cnt_ref[h][M_DV:M_DV + 1, :]
            m_ref[0, h] = ms_ref[h]


def _mlstm_prompt(zm, gates, gain, batch, seq, chunk):
    n = batch * seq
    per_step = min(MLSTM_CHUNKS_PER_STEP, n // batch // chunk)
    step_rows = per_step * chunk
    ns = n // batch // step_rows

    def rows(width):
        return pl.BlockSpec((step_rows, width), lambda b, c: (b * ns + c, 0))

    def state(*dims):
        return pl.BlockSpec((1, M_HEADS) + dims, lambda b, c: (b, 0, 0, 0))

    return pl.pallas_call(
        functools.partial(_mlstm_prompt_kernel, chunk=chunk),
        grid=(batch, ns),
        in_specs=[rows(ZM_WIDTH), rows(LANES), pl.BlockSpec((1, M_WIDTH), lambda b, c: (0, 0))],
        out_specs=[rows(M_WIDTH), state(M_DK, M_DV), state(1, M_DK), state(1, LANES)],
        out_shape=[jax.ShapeDtypeStruct((n, M_WIDTH), BF16),
                   jax.ShapeDtypeStruct((batch, M_HEADS, M_DK, M_DV), F32),
                   jax.ShapeDtypeStruct((batch, M_HEADS, 1, M_DK), F32),
                   jax.ShapeDtypeStruct((batch, M_HEADS, 1, LANES), F32)],
        scratch_shapes=[pltpu.VMEM((M_HEADS, 2 * LANES, M_DK), F32), pltpu.VMEM((M_HEADS, 1, LANES), F32)],
        compiler_params=pltpu.CompilerParams(dimension_semantics=("parallel", "arbitrary"),
                                             vmem_limit_bytes=VMEM_LIMIT),
        name="mlstm_prompt",
    )(zm, gates, gain)


SWA_BLOCKS_PER_STEP = 8


def _alibi_slope(head):
    return 2.0 ** (-8.0 * (head + 1) / A_HEADS)


def _swa_prompt_kernel(sinks_ref, qa_ref, kvc_ref, kvp_ref, ua_ref, mask_ref):
    hw = A_KV_HEADS * A_HD
    n_blocks = qa_ref.shape[0] // WINDOW
    qi = lax.broadcasted_iota(jnp.int32, (WINDOW, 2 * WINDOW), 0)
    kj = lax.broadcasted_iota(jnp.int32, (WINDOW, 2 * WINDOW), 1)
    dist = WINDOW + qi - kj
    band = (dist >= 0) & (dist < WINDOW)
    mask_ref[0] = jnp.where(band, 0.0, -jnp.inf)
    mask_ref[1] = jnp.where(band & (kj >= WINDOW), 0.0, -jnp.inf)
    lane = lax.broadcasted_iota(jnp.int32, (WINDOW, LANES), 1)
    low = lane < A_HD
    first = pl.program_id(1) == 0

    assert 2 * WINDOW <= 256, "positions must stay exactly representable in bf16"
    q_pos = (WINDOW + lax.broadcasted_iota(jnp.int32, (WINDOW, LANES), 0)).astype(F32)
    k_lane = lax.broadcasted_iota(jnp.int32, (2 * WINDOW, LANES), 1)
    k_pos = lax.broadcasted_iota(jnp.int32, (2 * WINDOW, LANES), 0).astype(F32)
    spare = (A_HD, 0)
    k_bias = [jnp.where(k_lane == spare[kvh], 1.0, jnp.where(k_lane == spare[kvh] + 1, k_pos, 0.0)).astype(BF16)
              for kvh in range(A_KV_HEADS)]
    q_bias = {}
    for head in range(A_HEADS):
        slope = _alibi_slope(head)
        at = spare[head // A_GROUP]
        q_bias[head] = jnp.where(lane == at, -slope * q_pos, jnp.where(lane == at + 1, slope, 0.0)).astype(BF16)

    for j in range(n_blocks):
        rows = slice(j * WINDOW, (j + 1) * WINDOW)
        kvc = kvc_ref[rows, :]
        kvp = kvp_ref[...] if j == 0 else kvc_ref[(j - 1) * WINDOW:j * WINDOW, :]
        mask = mask_ref[jnp.where(first, 1, 0)] if j == 0 else mask_ref[0]
        kk = jnp.concatenate([kvp[:, :hw], kvc[:, :hw]], axis=0).astype(BF16)
        kk_biased = [jnp.where(k_lane < A_HD, kk, k_bias[0]), jnp.where(k_lane < A_HD, k_bias[1], kk)]
        vv = jnp.concatenate([kvp[:, hw:], kvc[:, hw:]], axis=0).astype(BF16)
        for p in range(A_GROUP):
            qp = qa_ref[rows, p * LANES:(p + 1) * LANES]
            outs = []
            for kvh in range(A_KV_HEADS):
                head = kvh * A_GROUP + p
                keep = low if kvh == 0 else jnp.logical_not(low)
                s = _dot_nt(jnp.where(keep, qp, q_bias[head]), kk_biased[kvh])
                s = s + mask
                sink = sinks_ref[head]
                mx = jnp.maximum(jnp.max(s, axis=-1, keepdims=True), sink)
                e = jnp.exp(s - mx)
                den = jnp.sum(e, axis=-1, keepdims=True) + jnp.exp(sink - mx)
                outs.append(_dot(e.astype(BF16), vv) / den)
            ua_ref[rows, p * LANES:(p + 1) * LANES] = jnp.where(low, outs[0], outs[1]).astype(BF16)


def _swa_prompt(sinks, qa, kva, batch, seq):
    n = batch * seq
    per_step = min(SWA_BLOCKS_PER_STEP, n // batch // WINDOW)
    rows = per_step * WINDOW
    ns = n // batch // rows

    def step_rows(width):
        return pl.BlockSpec((rows, width), lambda b, i: (b * ns + i, 0))

    def prev_block(b, i):
        return (jnp.maximum((b * ns + i) * per_step - 1, 0), 0)

    return pl.pallas_call(
        _swa_prompt_kernel,
        grid=(batch, ns),
        in_specs=[pl.BlockSpec(memory_space=pltpu.SMEM), step_rows(A_WIDTH), step_rows(KV_WIDTH),
                  pl.BlockSpec((WINDOW, KV_WIDTH), prev_block)],
        out_specs=step_rows(A_WIDTH),
        out_shape=jax.ShapeDtypeStruct((n, A_WIDTH), BF16),
        scratch_shapes=[pltpu.VMEM((2, WINDOW, 2 * WINDOW), F32)],
        compiler_params=pltpu.CompilerParams(dimension_semantics=("parallel", "parallel"),
                                             vmem_limit_bytes=VMEM_LIMIT),
        name="swa_prompt",
    )(sinks, qa, kva, kva)


SAMPLE_ROWS = 128


def _mlstm_sample_kernel(q_ref, k_ref, v_ref, o_ref, gates_ref, gain_ref, c_ref, n_ref, m_ref,
                         um_ref, co_ref, no_ref, mo_ref,
                         qf_ref, qc_ref, nexp_ref, wkk_ref, dec_ref, *, steps):
    rows = SAMPLE_ROWS
    nb = rows // steps
    head = pl.program_id(1)
    row = lax.broadcasted_iota(jnp.int32, (rows, rows), 0)
    col = lax.broadcasted_iota(jnp.int32, (rows, rows), 1)
    same = (row // steps) == (col // steps)
    causal = same & (col <= row)
    eye = row == col

    def to_row(x_col):
        return jnp.sum(jnp.where(eye, x_col, 0.0), axis=0, keepdims=True)

    def pick(mask, x_row, fill, reduce):
        return reduce(jnp.where(mask, x_row, fill), axis=-1, keepdims=True)

    gates = gates_ref[...]
    lane = lax.broadcasted_iota(jnp.int32, (rows, LANES), 1)
    ig_col = jnp.sum(jnp.where(lane == head, gates, 0.0), axis=-1, keepdims=True)
    fg_col = jnp.sum(jnp.where(lane == head + M_HEADS, gates, 0.0), axis=-1, keepdims=True)
    lf_row = to_row(_log_sigmoid(fg_col))
    b_col = pick(causal, lf_row, 0.0, jnp.sum)
    b_row = to_row(b_col)
    ig_row = to_row(ig_col)
    m_prev = m_ref[0]

    d = jnp.where(causal, b_col - b_row + ig_row, -jnp.inf)
    inter = b_col + m_prev
    m_t = jnp.maximum(inter, jnp.max(d, axis=-1, keepdims=True))
    w = jnp.exp(d - m_t)
    a = jnp.exp(inter - m_t)

    q = q_ref[...]
    k = k_ref[...]
    v = v_ref[...]
    qf_ref[...] = q.astype(F32)
    for t in range(steps):
        nexp_ref[pl.ds(t, nb, stride=steps), :] = n_ref[...]

    sub = lax.broadcasted_iota(jnp.int32, (2 * steps, M_DV), 0)

    def qc_pair(j, carry):
        q8 = qf_ref[pl.ds(pl.multiple_of(j * 2 * steps, 2 * steps), 2 * steps), :].astype(BF16)
        r0 = _dot(q8, c_ref[2 * j, 0].astype(BF16))
        r1 = _dot(q8, c_ref[2 * j + 1, 0].astype(BF16))
        qc_ref[pl.ds(pl.multiple_of(j * 2 * steps, 2 * steps), 2 * steps), :] = jnp.where(sub < steps, r0, r1)
        return carry

    lax.fori_loop(0, nb // 2, qc_pair, 0, unroll=True)

    s = _dot_nt(q, k) * w
    qn = jnp.sum(qf_ref[...] * nexp_ref[...], axis=-1, keepdims=True)
    num = a * qc_ref[...] + _dot(s.astype(BF16), v)
    den = a * qn + jnp.sum(s, axis=-1, keepdims=True)
    hh = num / jnp.maximum(jnp.abs(den), jnp.exp(-m_t))
    um_ref[...] = _head_norm_gate(hh, o_ref[...], gain_ref[...])

    last = same & ((col % steps) == steps - 1)
    b_last = pick(last, b_row, 0.0, jnp.sum)
    g_col = b_last - b_col + ig_col
    g_max = pick(same, to_row(g_col), -jnp.inf, jnp.max)
    m_new = jnp.maximum(b_last + m_prev, g_max)
    decay = jnp.exp(b_last + m_prev - m_new)
    wk = jnp.exp(g_col - m_new)
    mo_ref[0] = m_new

    kf = k.astype(F32)
    wkk_ref[...] = wk * kf
    dec_ref[...] = jnp.broadcast_to(decay, (rows, M_DV))
    n_sum = wkk_ref[pl.ds(0, nb, stride=steps), :]
    for t in range(1, steps):
        n_sum = n_sum + wkk_ref[pl.ds(t, nb, stride=steps), :]
    no_ref[...] = dec_ref[pl.ds(0, nb, stride=steps), :] * n_ref[...] + n_sum

    k_t = kf.T
    wv = (wk * v.astype(F32)).astype(BF16)
    tok_batch = lax.broadcasted_iota(jnp.int32, (M_DK, rows), 1) // steps

    def update(b, carry):
        upd = _dot(jnp.where(tok_batch == b, k_t, 0.0).astype(BF16), wv)
        co_ref[b, 0] = dec_ref[pl.ds(b * steps, 1), :] * c_ref[b, 0] + upd
        return carry

    lax.fori_loop(0, nb, update, 0, unroll=True)


def _mlstm_sample(zm, gates, gain, c, n2, m_rows, steps):
    rows = zm.shape[0]
    batch = c.shape[0]
    nb = SAMPLE_ROWS // steps
    groups = rows // SAMPLE_ROWS

    def zcol(section):
        return pl.BlockSpec((SAMPLE_ROWS, M_DK), lambda g, h: (g, section * M_HEADS + h))

    c_spec = pl.BlockSpec((nb, 1, M_DK, M_DV), lambda g, h: (g, h, 0, 0))
    n_spec = pl.BlockSpec((nb, M_DK), lambda g, h: (g, h))
    m_spec = pl.BlockSpec((1, SAMPLE_ROWS, 1), lambda g, h: (h, g, 0))
    return pl.pallas_call(
        functools.partial(_mlstm_sample_kernel, steps=steps),
        grid=(groups, M_HEADS),
        in_specs=[zcol(0), zcol(1), zcol(2), zcol(3),
                  pl.BlockSpec((SAMPLE_ROWS, LANES), lambda g, h: (g, 0)),
                  pl.BlockSpec((1, M_DV), lambda g, h: (0, h)),
                  c_spec, n_spec, m_spec],
        out_specs=[pl.BlockSpec((SAMPLE_ROWS, M_DV), lambda g, h: (g, h)), c_spec, n_spec, m_spec],
        out_shape=[jax.ShapeDtypeStruct((rows, M_WIDTH), BF16),
                   jax.ShapeDtypeStruct(c.shape, F32),
                   jax.ShapeDtypeStruct(n2.shape, F32),
                   jax.ShapeDtypeStruct(m_rows.shape, F32)],
        scratch_shapes=[pltpu.VMEM((SAMPLE_ROWS, M_DK), F32),
                        pltpu.VMEM((SAMPLE_ROWS, M_DV), F32),
                        pltpu.VMEM((SAMPLE_ROWS, M_DK), F32),
                        pltpu.VMEM((SAMPLE_ROWS, M_DK), F32),
                        pltpu.VMEM((SAMPLE_ROWS, M_DV), F32)],
        compiler_params=pltpu.CompilerParams(dimension_semantics=("parallel", "parallel"),
                                             vmem_limit_bytes=VMEM_LIMIT),
        name="mlstm_sample",
    )(zm, zm, zm, zm, gates, gain, c, n2, m_rows)


def _swa_sample_kernel(sinks_ref, qa_ref, kva_ref, ckt_ref, cvt_ref, ua_ref, nkt_ref, nvt_ref,
                       s_ref, e_ref, pv_ref, *, steps):
    rows = SAMPLE_ROWS
    nb = rows // steps
    pair_rows = 2 * steps
    hw = A_KV_HEADS * A_HD
    tok = lax.broadcasted_iota(jnp.int32, (rows, LANES), 0)
    lane = lax.broadcasted_iota(jnp.int32, (rows, LANES), 1)
    low = lane < A_HD

    qf = qa_ref[...].astype(F32)
    q_blocks = []
    for p in range(A_GROUP):
        qp = qf[:, p * LANES:(p + 1) * LANES]
        q_blocks.append(jnp.where(low, qp, 0.0))
        q_blocks.append(jnp.where(low, 0.0, qp))
    n_blk = len(q_blocks)

    kv_new = kva_ref[...]
    kv_new_t = kv_new.T
    k_new_t = kv_new_t[:hw]
    v_new_t = kv_new_t[hw:]
    v_new = kv_new[:, hw:].astype(BF16)
    s_new = _dot(jnp.concatenate(q_blocks, axis=0).astype(BF16), k_new_t.astype(BF16))

    def pair_rows_of(blocks, j):
        return jnp.concatenate([blk[j * pair_rows:(j + 1) * pair_rows] for blk in blocks], axis=0)

    first = (lax.broadcasted_iota(jnp.int32, (n_blk * pair_rows, LANES), 0) % pair_rows) < steps

    for j in range(nb // 2):
        q_pair = pair_rows_of(q_blocks, j).astype(BF16)
        s_pair = jnp.where(first, _dot(q_pair, ckt_ref[2 * j].astype(BF16)),
                           _dot(q_pair, ckt_ref[2 * j + 1].astype(BF16)))
        for blk in range(n_blk):
            s_ref[pl.ds(blk * rows + j * pair_rows, pair_rows), :] = s_pair[blk * pair_rows:(blk + 1) * pair_rows]

    t_q = tok % steps
    dist_c = WINDOW + t_q - lane
    valid_c = dist_c < WINDOW
    dist_n = t_q - lane % steps
    valid_n = ((tok // steps) == (lane // steps)) & (dist_n >= 0)
    dist_cf = dist_c.astype(F32)
    dist_nf = dist_n.astype(F32)
    pv_new = []
    inv_den = []
    for blk in range(n_blk):
        head = (blk % A_KV_HEADS) * A_GROUP + blk // A_KV_HEADS
        slope = _alibi_slope(head)
        sink = sinks_ref[head]
        r = pl.ds(blk * rows, rows)
        s_c = jnp.where(valid_c, s_ref[r, :] - slope * dist_cf, -jnp.inf)
        s_n = jnp.where(valid_n, s_new[blk * rows:(blk + 1) * rows] - slope * dist_nf, -jnp.inf)
        mx = jnp.maximum(jnp.maximum(jnp.max(s_c, axis=-1, keepdims=True),
                                     jnp.max(s_n, axis=-1, keepdims=True)), sink)
        e_c = jnp.exp(s_c - mx)
        e_n = jnp.exp(s_n - mx)
        den = jnp.sum(e_c, axis=-1, keepdims=True) + jnp.sum(e_n, axis=-1, keepdims=True) + jnp.exp(sink - mx)
        e_ref[r, :] = e_c
        pv_new.append(_dot(e_n.astype(BF16), v_new))
        inv_den.append(1.0 / den)

    for j in range(nb // 2):
        e_pair = jnp.concatenate([e_ref[pl.ds(blk * rows + j * pair_rows, pair_rows), :] for blk in range(n_blk)],
                                 axis=0).astype(BF16)
        pv_pair = jnp.where(first, _dot_nt(e_pair, cvt_ref[2 * j].astype(BF16)),
                            _dot_nt(e_pair, cvt_ref[2 * j + 1].astype(BF16)))
        for blk in range(n_blk):
            pv_ref[pl.ds(blk * rows + j * pair_rows, pair_rows), :] = pv_pair[blk * pair_rows:(blk + 1) * pair_rows]

    for p in range(A_GROUP):
        halves = []
        for kvh in range(A_KV_HEADS):
            blk = p * A_KV_HEADS + kvh
            halves.append((pv_ref[pl.ds(blk * rows, rows), :] + pv_new[blk]) * inv_den[blk])
        ua_ref[:, p * LANES:(p + 1) * LANES] = jnp.where(low, halves[0], halves[1]).astype(BF16)

    tail = lane >= WINDOW - steps
    for b in range(nb):
        shift = (WINDOW - steps - b * steps) % rows
        nkt_ref[b] = jnp.where(tail, pltpu.roll(k_new_t, shift, 1), pltpu.roll(ckt_ref[b], WINDOW - steps, 1))
        nvt_ref[b] = jnp.where(tail, pltpu.roll(v_new_t, shift, 1), pltpu.roll(cvt_ref[b], WINDOW - steps, 1))


def _swa_sample(sinks, qa, kva, ckt, cvt, steps):
    batch = ckt.shape[0]
    assert ckt.shape[1:] == (A_KV_HEADS * A_HD, WINDOW) and SAMPLE_ROWS == WINDOW == LANES
    nb = SAMPLE_ROWS // steps
    cache = pl.BlockSpec((nb, LANES, WINDOW), lambda g: (g, 0, 0))

    def rows(width):
        return pl.BlockSpec((SAMPLE_ROWS, width), lambda g: (g, 0))

    stacked = (A_HEADS * SAMPLE_ROWS, LANES)
    return pl.pallas_call(
        functools.partial(_swa_sample_kernel, steps=steps),
        grid=(batch // nb,),
        in_specs=[pl.BlockSpec(memory_space=pltpu.SMEM), rows(A_WIDTH), rows(KV_WIDTH), cache, cache],
        out_specs=[rows(A_WIDTH), cache, cache],
        out_shape=[jax.ShapeDtypeStruct((batch * steps, A_WIDTH), BF16),
                   jax.ShapeDtypeStruct(ckt.shape, F32),
                   jax.ShapeDtypeStruct(cvt.shape, F32)],
        scratch_shapes=[pltpu.VMEM(stacked, F32), pltpu.VMEM(stacked, F32), pltpu.VMEM(stacked, F32)],
        compiler_params=pltpu.CompilerParams(dimension_semantics=("parallel",),
                                             vmem_limit_bytes=VMEM_LIMIT),
        name="swa_sample",
    )(sinks, qa, kva, ckt, cvt)


def _prep_proj(w_in, b_gate):
    m_end = 4 * M_WIDTH
    g_end = m_end + 2 * M_HEADS
    wt = w_in.T
    qa = wt[g_end:g_end + A_WIDTH].reshape(A_KV_HEADS, A_GROUP, A_HD, D_MODEL).transpose(1, 0, 2, 3)
    gates = jnp.pad(wt[m_end:g_end], ((0, LANES - 2 * M_HEADS), (0, 0)))
    win = jnp.concatenate([wt[:m_end], qa.reshape(A_WIDTH, D_MODEL), wt[g_end + A_WIDTH:], gates],
                          axis=0).astype(BF16)
    bias = jnp.pad(b_gate, (0, LANES - 2 * M_HEADS)).reshape(1, LANES)
    return win, bias


def _prep_out(w_out):
    wa = w_out[M_WIDTH:].reshape(A_KV_HEADS, A_GROUP, A_HD, D_MODEL).transpose(1, 0, 2, 3)
    return jnp.concatenate([w_out[:M_WIDTH], wa.reshape(A_WIDTH, D_MODEL)], axis=0).astype(BF16)


def kernel(x_prompt, x_sample, cache_swa_k, cache_swa_v, state_mlstm_C, state_mlstm_n, state_mlstm_m,
           norm_gains, ffn_w_gate, ffn_w_up, ffn_w_down, w_in, b_gate, mlstm_norm_gain, attn_sinks, w_out):
    assert norm_gains.shape[0] == 1, "single layer"
    batch, seq, _ = x_prompt.shape
    dec_batch, steps, _ = x_sample.shape
    gains = norm_gains[0]
    ffn = (ffn_w_gate, ffn_w_up, ffn_w_down)
    win, bias = _prep_proj(w_in[0], b_gate[0])
    wo = _prep_out(w_out[0])
    gain = mlstm_norm_gain[0].reshape(1, M_WIDTH)
    sinks = attn_sinks[0]
    kv_lanes = A_KV_HEADS * A_HD

    chunk = min(MLSTM_CHUNK, seq)
    assert seq % chunk == 0
    xp, zm, qa, kva, gates = _ffn_proj(x_prompt.reshape(batch * seq, D_MODEL), gains, *ffn, win, bias,
                                       0, 0, 1, 2, chunk)
    um, p_c, p_n, p_m = _mlstm_prompt(zm, gates, gain, batch, seq, chunk)
    ua = _swa_prompt(sinks, qa, kva, batch, seq)
    yp = _mix_ffn(xp, um, ua, wo, gains, *ffn, 1, 3, 4, 5).reshape(batch, seq, D_MODEL)
    keep = min(WINDOW, seq)
    kv_tail = kva.reshape(batch, seq, KV_WIDTH)[:, seq - keep:]
    p_k = kv_tail[..., :kv_lanes].reshape(1, batch, keep, A_KV_HEADS, A_HD)
    p_v = kv_tail[..., kv_lanes:].reshape(1, batch, keep, A_KV_HEADS, A_HD)
    p_c = p_c[None]
    p_n = p_n[None, :, :, 0, :]
    p_m = p_m[None, :, :, 0, 0]

    xs, zm_s, qa_s, kva_s, gates_s = _ffn_proj(x_sample.reshape(dec_batch * steps, D_MODEL), gains, *ffn,
                                               win, bias, 0, 0, 1, 2, min(chunk, dec_batch * steps))
    m_rows = jnp.repeat(state_mlstm_m[0].T, steps, axis=1)[..., None]
    um_s, s_c, s_n2, s_m_rows = _mlstm_sample(zm_s, gates_s, gain, state_mlstm_C[0],
                                              state_mlstm_n[0].reshape(dec_batch, M_HEADS * M_DK),
                                              m_rows, steps)
    wc = cache_swa_k.shape[2]
    ua_s, s_kt, s_vt = _swa_sample(sinks, qa_s, kva_s,
                                   cache_swa_k[0].reshape(dec_batch, wc, kv_lanes).transpose(0, 2, 1),
                                   cache_swa_v[0].reshape(dec_batch, wc, kv_lanes).transpose(0, 2, 1), steps)
    ys = _mix_ffn(xs, um_s, ua_s, wo, gains, *ffn, 1, 3, 4, 5).reshape(dec_batch, steps, D_MODEL)
    s_n = s_n2.reshape(1, dec_batch, M_HEADS, M_DK)
    s_m = s_m_rows[:, ::steps, 0].T[None]
    s_k = s_kt.transpose(0, 2, 1).reshape(1, dec_batch, wc, A_KV_HEADS, A_HD)
    s_v = s_vt.transpose(0, 2, 1).reshape(1, dec_batch, wc, A_KV_HEADS, A_HD)
    return (yp, ys, p_k, p_v, p_c, p_n, p_m, s_k, s_v, s_c[None], s_n, s_m)
```

```python
import functools

import jax
import jax.numpy as jnp
from jax import lax
from jax.experimental import pallas as pl
from jax.experimental.pallas import tpu as pltpu

F32 = jnp.float32
BF16 = jnp.bfloat16

D_MODEL = 1024
D_FF = 2816
M_HEADS = 4
M_DK = 128
M_DV = 128
M_WIDTH = M_HEADS * M_DV
A_HEADS = 8
A_KV_HEADS = 2
A_GROUP = A_HEADS // A_KV_HEADS
A_HD = 64
A_WIDTH = A_HEADS * A_HD
WINDOW = 128
RMS_EPS = 1e-6

LANES = 128
FFN_CHUNK = 256
N_FFN_CHUNKS = D_FF // FFN_CHUNK
FFN_STAGE_SLOTS = 4
TOKEN_TILE = 512
MLSTM_CHUNK = 256
ZM_WIDTH = 4 * M_WIDTH
KV_WIDTH = 2 * A_KV_HEADS * A_HD
VMEM_LIMIT = 56 * 1024 * 1024
MLSTM_CHUNKS_PER_STEP = 8
GATE_I, GATE_F, GATE_B, GATE_X = 0, M_HEADS, 2 * M_HEADS, 3 * M_HEADS

NT_DIMS = (((1,), (1,)), ((), ()))


def _rms(x, g):
    ms = jnp.mean(x * x, axis=-1, keepdims=True)
    return x * lax.rsqrt(ms + RMS_EPS) * g


def _log_sigmoid(x):
    return jnp.minimum(x, 0.0) - jnp.log1p(jnp.exp(-jnp.abs(x)))


def _dot(a, b):
    return jnp.dot(a, b, preferred_element_type=F32)


def _dot_nt(a, b):
    return lax.dot_general(a, b, NT_DIMS, preferred_element_type=F32)


def _resident(shape):
    nd = len(shape)
    return pl.BlockSpec(shape, lambda *_: (0,) * nd, pipeline_mode=pl.Buffered(1))


def _ffn_weight_copies(w_hbm, j, f, slot, stage_refs, sems):
    wg_hbm, wu_hbm, wd_hbm = w_hbm
    cols = pl.ds(f * FFN_CHUNK, FFN_CHUNK)
    sources = (wg_hbm.at[0, j, :, cols], wu_hbm.at[0, j, :, cols], wd_hbm.at[0, j, cols, :])
    return [pltpu.make_async_copy(src, stage.at[slot], sems.at[slot, k])
            for k, (src, stage) in enumerate(zip(sources, stage_refs))]


def _swiglu(x, gpre, w_refs, h_ref, acc_ref, fetch=None):
    wg_ref, wu_ref, wd_ref = w_refs
    if fetch is not None:
        for f in range(min(FFN_STAGE_SLOTS, N_FFN_CHUNKS)):
            for copy in _ffn_weight_copies(*fetch[:2], f, f % FFN_STAGE_SLOTS, *fetch[2:]):
                copy.start()
    h_ref[...] = _rms(x, gpre).astype(BF16)
    for f in range(N_FFN_CHUNKS):
        cols = slice(f * FFN_CHUNK, (f + 1) * FFN_CHUNK)
        if fetch is not None:
            slot = f % FFN_STAGE_SLOTS
            stage_g, stage_u, stage_d = fetch[2]
            for copy in _ffn_weight_copies(*fetch[:2], f, slot, *fetch[2:]):
                copy.wait()
            wg_ref[:, cols] = stage_g[slot].astype(BF16)
            wu_ref[:, cols] = stage_u[slot].astype(BF16)
            wd_ref[cols, :] = stage_d[slot].astype(BF16)
            if f + FFN_STAGE_SLOTS < N_FFN_CHUNKS:
                for copy in _ffn_weight_copies(*fetch[:2], f + FFN_STAGE_SLOTS, slot, *fetch[2:]):
                    copy.start()
        h = h_ref[...]
        g = _dot(h, wg_ref[:, cols])
        u = _dot(h, wu_ref[:, cols])
        part = _dot((g * jax.nn.sigmoid(g) * u).astype(BF16), wd_ref[cols, :])
        if f == 0:
            acc_ref[...] = part
        else:
            acc_ref[...] += part


def _first_step_fetches(body, fetch):
    first = pl.program_id(0) == 0

    @pl.when(first)
    def _():
        body(fetch)

    @pl.when(jnp.logical_not(first))
    def _():
        body(None)


def _half_step(x, acc_ref, gpost):
    return x + _rms(acc_ref[...], 0.5 * gpost)


def _project(x, gain, win_ref, bias_ref, h_ref, zm_ref, qa_ref, kva_ref, gates_ref, chunk):
    h_ref[...] = _rms(x, gain).astype(BF16)

    def cols(lo, width):
        return _dot_nt(h_ref[...], win_ref[lo:lo + width, :])

    gates = cols(ZM_WIDTH + A_WIDTH + KV_WIDTH, LANES) + bias_ref[...]
    lane = lax.broadcasted_iota(jnp.int32, (chunk, LANES), 1)
    pos = lax.broadcasted_iota(jnp.int32, (chunk, LANES), 0)
    for c0 in range(0, gates.shape[0], chunk):
        g = gates[c0:c0 + chunk]
        b = jnp.where((lane >= GATE_F) & (lane < GATE_B), _log_sigmoid(g), 0.0)
        shift = 1
        while shift < chunk:
            b = b + jnp.where(pos >= shift, pltpu.roll(b, shift, 0), 0.0)
            shift *= 2
        x_gate = pltpu.roll(g, GATE_X - GATE_I, 1) - pltpu.roll(b, GATE_X - GATE_F, 1)
        out = jnp.where(lane < GATE_B, g, jnp.where(lane < GATE_X, pltpu.roll(b, GATE_B - GATE_F, 1), x_gate))
        gates_ref[c0:c0 + chunk, :] = jnp.where(lane < GATE_X + M_HEADS, out, 0.0)

    zm_ref[:, 0:M_WIDTH] = cols(0, M_WIDTH).astype(BF16)
    zm_ref[:, M_WIDTH:2 * M_WIDTH] = (cols(M_WIDTH, M_WIDTH) * (M_DK ** -0.5)).astype(BF16)
    zm_ref[:, 2 * M_WIDTH:3 * M_WIDTH] = cols(2 * M_WIDTH, M_WIDTH).astype(BF16)
    zm_ref[:, 3 * M_WIDTH:] = jax.nn.sigmoid(cols(3 * M_WIDTH, M_WIDTH)).astype(BF16)
    qa_ref[...] = (cols(ZM_WIDTH, A_WIDTH) * (A_HD ** -0.5)).astype(BF16)
    kva_ref[...] = cols(ZM_WIDTH + A_WIDTH, KV_WIDTH)


def _ffn_proj_kernel(x_ref, gains_ref, wg_hbm, wu_hbm, wd_hbm, win_ref, bias_ref,
                     x1_ref, zm_ref, qa_ref, kva_ref, gates_ref,
                     h_ref, acc_ref, wg_ref, wu_ref, wd_ref, stage_g, stage_u, stage_d, sems,
                     *, j, pre, post, mixer, chunk):
    def body(fetch):
        x = x_ref[...]
        _swiglu(x, gains_ref[pre:pre + 1, :], (wg_ref, wu_ref, wd_ref), h_ref, acc_ref, fetch)
        x1 = _half_step(x, acc_ref, gains_ref[post:post + 1, :])
        x1_ref[...] = x1
        _project(x1, gains_ref[mixer:mixer + 1, :], win_ref, bias_ref, h_ref, zm_ref, qa_ref, kva_ref,
                 gates_ref, chunk)

    _first_step_fetches(body, ((wg_hbm, wu_hbm, wd_hbm), j, (stage_g, stage_u, stage_d), sems))


def _mix_ffn_kernel(x_ref, um_ref, ua_ref, wo_ref, gains_ref, wg_hbm, wu_hbm, wd_hbm, o_ref,
                    h_ref, acc_ref, wg_ref, wu_ref, wd_ref, stage_g, stage_u, stage_d, sems,
                    *, j, mix, pre, post):
    def body(fetch):
        u = jnp.concatenate([um_ref[...], ua_ref[...]], axis=-1)
        x = x_ref[...] + _rms(_dot(u, wo_ref[...]), gains_ref[mix:mix + 1, :])
        _swiglu(x, gains_ref[pre:pre + 1, :], (wg_ref, wu_ref, wd_ref), h_ref, acc_ref, fetch)
        o_ref[...] = _half_step(x, acc_ref, gains_ref[post:post + 1, :])

    _first_step_fetches(body, ((wg_hbm, wu_hbm, wd_hbm), j, (stage_g, stage_u, stage_d), sems))


def _ffn_scratch(tm):
    return [pltpu.VMEM((tm, D_MODEL), BF16),
            pltpu.VMEM((tm, D_MODEL), F32),
            pltpu.VMEM((D_MODEL, D_FF), BF16),
            pltpu.VMEM((D_MODEL, D_FF), BF16),
            pltpu.VMEM((D_FF, D_MODEL), BF16),
            pltpu.VMEM((FFN_STAGE_SLOTS, D_MODEL, FFN_CHUNK), F32),
            pltpu.VMEM((FFN_STAGE_SLOTS, D_MODEL, FFN_CHUNK), F32),
            pltpu.VMEM((FFN_STAGE_SLOTS, FFN_CHUNK, D_MODEL), F32),
            pltpu.SemaphoreType.DMA((FFN_STAGE_SLOTS, 3))]


def _dense_params():
    return pltpu.CompilerParams(dimension_semantics=("arbitrary",), vmem_limit_bytes=VMEM_LIMIT)


_HBM = pl.BlockSpec(memory_space=pl.ANY)


def _ffn_proj(x, gains, wg, wu, wd, win, bias, j, pre, post, mixer, chunk):
    n = x.shape[0]
    tm = min(TOKEN_TILE, n)
    assert tm % chunk == 0

    def rows(width):
        return pl.BlockSpec((tm, width), lambda i: (i, 0))

    return pl.pallas_call(
        functools.partial(_ffn_proj_kernel, j=j, pre=pre, post=post, mixer=mixer, chunk=chunk),
        grid=(n // tm,),
        in_specs=[rows(D_MODEL), _resident(gains.shape), _HBM, _HBM, _HBM,
                  _resident(win.shape), _resident(bias.shape)],
        out_specs=[rows(D_MODEL), rows(ZM_WIDTH), rows(A_WIDTH), rows(KV_WIDTH), rows(LANES)],
        out_shape=[jax.ShapeDtypeStruct((n, D_MODEL), F32),
                   jax.ShapeDtypeStruct((n, ZM_WIDTH), BF16),
                   jax.ShapeDtypeStruct((n, A_WIDTH), BF16),
                   jax.ShapeDtypeStruct((n, KV_WIDTH), F32),
                   jax.ShapeDtypeStruct((n, LANES), F32)],
        scratch_shapes=_ffn_scratch(tm),
        compiler_params=_dense_params(),
        name="ffn_half_step_input_projection",
    )(x, gains, wg, wu, wd, win, bias)


def _mix_ffn(x, um, ua, wo, gains, wg, wu, wd, j, mix, pre, post):
    n = x.shape[0]
    tm = min(TOKEN_TILE, n)
    row = pl.BlockSpec((tm, D_MODEL), lambda i: (i, 0))
    half = pl.BlockSpec((tm, M_WIDTH), lambda i: (i, 0))
    return pl.pallas_call(
        functools.partial(_mix_ffn_kernel, j=j, mix=mix, pre=pre, post=post),
        grid=(n // tm,),
        in_specs=[row, half, half, _resident(wo.shape), _resident(gains.shape), _HBM, _HBM, _HBM],
        out_specs=row,
        out_shape=jax.ShapeDtypeStruct((n, D_MODEL), F32),
        scratch_shapes=_ffn_scratch(tm),
        compiler_params=_dense_params(),
        name="out_proj_ffn_half_step",
    )(x, um, ua, wo, gains, wg, wu, wd)


def _head_norm_gate(hh, o, gain):
    hn = hh * lax.rsqrt(jnp.mean(hh * hh, axis=-1, keepdims=True) + RMS_EPS) * gain
    return (o.astype(F32) * hn).astype(BF16)


def _mlstm_prompt_kernel(zm_ref, gates_ref, gain_ref, um_ref, c_ref, n_ref, m_ref, cnt_ref, ms_ref, *, chunk):
    step = pl.program_id(1)

    @pl.when(step == 0)
    def _():
        cnt_ref[...] = jnp.zeros_like(cnt_ref)
        ms_ref[...] = jnp.zeros_like(ms_ref)

    si = lax.broadcasted_iota(jnp.int32, (chunk, chunk), 0)
    ti = lax.broadcasted_iota(jnp.int32, (chunk, chunk), 1)
    causal = si <= ti
    ones_rows = jnp.where(lax.broadcasted_iota(jnp.int32, (LANES, chunk), 0) == 0, 1.0, 0.0).astype(BF16)

    for c in range(zm_ref.shape[0] // chunk):
        rows = slice(c * chunk, (c + 1) * chunk)
        gates = gates_ref[rows, :]
        gates_t = gates.T
        for h in range(M_HEADS):
            lanes = slice(h * M_DK, (h + 1) * M_DK)
            q = zm_ref[rows, lanes]
            k = zm_ref[rows, M_WIDTH + h * M_DK:M_WIDTH + (h + 1) * M_DK]
            v = zm_ref[rows, 2 * M_WIDTH + h * M_DV:2 * M_WIDTH + (h + 1) * M_DV]
            o = zm_ref[rows, 3 * M_WIDTH + h * M_DV:3 * M_WIDTH + (h + 1) * M_DV]
            x_col = gates[:, GATE_X + h:GATE_X + h + 1]
            ig_row = gates_t[GATE_I + h:GATE_I + h + 1, :]
            b_row = gates_t[GATE_B + h:GATE_B + h + 1, :]
            m_prev = ms_ref[h][:, 0:1]
            cnt = cnt_ref[h]

            xm = jnp.where(causal, x_col, -jnp.inf)
            mu = jnp.maximum(m_prev, jnp.max(xm, axis=0, keepdims=True))
            a = jnp.exp(m_prev - mu)
            s_t = _dot_nt(k, q) * jnp.exp(xm - mu)
            v_ext = jnp.concatenate([v.astype(F32).T.astype(BF16), ones_rows], axis=0)
            nd = a * _dot_nt(cnt.astype(BF16), q) + _dot(v_ext, s_t.astype(BF16))
            den = nd[M_DV:M_DV + 1, :]
            hh = nd[:M_DV, :] * (1.0 / jnp.maximum(jnp.abs(den), jnp.exp(-(b_row + mu))))
            hn = hh * lax.rsqrt(jnp.mean(hh * hh, axis=0, keepdims=True) + RMS_EPS)
            um_ref[rows, lanes] = (hn.T * gain_ref[:, lanes] * o.astype(F32)).astype(BF16)

            b_last = b_row[:, chunk - 1:chunk]
            g = b_last - b_row + ig_row
            m_new = jnp.maximum(b_last + m_prev, jnp.max(g, axis=-1, keepdims=True))
            decay = jnp.exp(b_last + m_prev - m_new)
            wv = (jnp.exp(g - m_new) * v_ext.astype(F32)).astype(BF16)
            cnt_ref[h] = decay * cnt + _dot(wv, k)
            ms_ref[h] = jnp.broadcast_to(m_new, (1, LANES))

    @pl.when(step == pl.num_programs(1) - 1)
    def _():
        for h in range(M_HEADS):
            c_ref[0, h] = cnt_ref[h][:M_DV, :].T
            n_ref[0, h] = cnt_ref[h][M_DV:M_DV + 1, :]
            m_ref[0, h] = ms_ref[h]


def _mlstm_prompt(zm, gates, gain, batch, seq, chunk):
    n = batch * seq
    per_step = min(MLSTM_CHUNKS_PER_STEP, n // batch // chunk)
    step_rows = per_step * chunk
    ns = n // batch // step_rows

    def rows(width):
        return pl.BlockSpec((step_rows, width), lambda b, c: (b * ns + c, 0))

    def state(*dims):
        return pl.BlockSpec((1, M_HEADS) + dims, lambda b, c: (b, 0, 0, 0))

    return pl.pallas_call(
        functools.partial(_mlstm_prompt_kernel, chunk=chunk),
        grid=(batch, ns),
        in_specs=[rows(ZM_WIDTH), rows(LANES), pl.BlockSpec((1, M_WIDTH), lambda b, c: (0, 0))],
        out_specs=[rows(M_WIDTH), state(M_DK, M_DV), state(1, M_DK), state(1, LANES)],
        out_shape=[jax.ShapeDtypeStruct((n, M_WIDTH), BF16),
                   jax.ShapeDtypeStruct((batch, M_HEADS, M_DK, M_DV), F32),
                   jax.ShapeDtypeStruct((batch, M_HEADS, 1, M_DK), F32),
                   jax.ShapeDtypeStruct((batch, M_HEADS, 1, LANES), F32)],
        scratch_shapes=[pltpu.VMEM((M_HEADS, 2 * LANES, M_DK), F32), pltpu.VMEM((M_HEADS, 1, LANES), F32)],
        compiler_params=pltpu.CompilerParams(dimension_semantics=("parallel", "arbitrary"),
                                             vmem_limit_bytes=VMEM_LIMIT),
        name="mlstm_prompt",
    )(zm, gates, gain)


SWA_BLOCKS_PER_STEP = 8


def _alibi_slope(head):
    return 2.0 ** (-8.0 * (head + 1) / A_HEADS)


def _swa_prompt_kernel(sinks_ref, qa_ref, kvc_ref, kvp_ref, ua_ref, mask_ref):
    hw = A_KV_HEADS * A_HD
    n_blocks = qa_ref.shape[0] // WINDOW
    qi = lax.broadcasted_iota(jnp.int32, (WINDOW, 2 * WINDOW), 0)
    kj = lax.broadcasted_iota(jnp.int32, (WINDOW, 2 * WINDOW), 1)
    dist = WINDOW + qi - kj
    band = (dist >= 0) & (dist < WINDOW)
    mask_ref[0] = jnp.where(band, 0.0, -jnp.inf)
    mask_ref[1] = jnp.where(band & (kj >= WINDOW), 0.0, -jnp.inf)
    lane = lax.broadcasted_iota(jnp.int32, (WINDOW, LANES), 1)
    low = lane < A_HD
    first = pl.program_id(1) == 0

    assert 2 * WINDOW <= 256, "positions must stay exactly representable in bf16"
    q_pos = (WINDOW + lax.broadcasted_iota(jnp.int32, (WINDOW, LANES), 0)).astype(F32)
    k_lane = lax.broadcasted_iota(jnp.int32, (2 * WINDOW, LANES), 1)
    k_pos = lax.broadcasted_iota(jnp.int32, (2 * WINDOW, LANES), 0).astype(F32)
    spare = (A_HD, 0)
    k_bias = [jnp.where(k_lane == spare[kvh], 1.0, jnp.where(k_lane == spare[kvh] + 1, k_pos, 0.0)).astype(BF16)
              for kvh in range(A_KV_HEADS)]
    q_bias = {}
    for head in range(A_HEADS):
        slope = _alibi_slope(head)
        at = spare[head // A_GROUP]
        q_bias[head] = jnp.where(lane == at, -slope * q_pos, jnp.where(lane == at + 1, slope, 0.0)).astype(BF16)

    for j in range(n_blocks):
        rows = slice(j * WINDOW, (j + 1) * WINDOW)
        kvc = kvc_ref[rows, :]
        kvp = kvp_ref[...] if j == 0 else kvc_ref[(j - 1) * WINDOW:j * WINDOW, :]
        mask = mask_ref[jnp.where(first, 1, 0)] if j == 0 else mask_ref[0]
        kk = jnp.concatenate([kvp[:, :hw], kvc[:, :hw]], axis=0).astype(BF16)
        kk_biased = [jnp.where(k_lane < A_HD, kk, k_bias[0]), jnp.where(k_lane < A_HD, k_bias[1], kk)]
        vv = jnp.concatenate([kvp[:, hw:], kvc[:, hw:]], axis=0).astype(BF16)
        for p in range(A_GROUP):
            qp = qa_ref[rows, p * LANES:(p + 1) * LANES]
            outs = []
            for kvh in range(A_KV_HEADS):
                head = kvh * A_GROUP + p
                keep = low if kvh == 0 else jnp.logical_not(low)
                s = _dot_nt(jnp.where(keep, qp, q_bias[head]), kk_biased[kvh])
                s = s + mask
                sink = sinks_ref[head]
                mx = jnp.maximum(jnp.max(s, axis=-1, keepdims=True), sink)
                e = jnp.exp(s - mx)
                den = jnp.sum(e, axis=-1, keepdims=True) + jnp.exp(sink - mx)
                outs.append(_dot(e.astype(BF16), vv) / den)
            ua_ref[rows, p * LANES:(p + 1) * LANES] = jnp.where(low, outs[0], outs[1]).astype(BF16)


def _swa_prompt(sinks, qa, kva, batch, seq):
    n = batch * seq
    per_step = min(SWA_BLOCKS_PER_STEP, n // batch // WINDOW)
    rows = per_step * WINDOW
    ns = n // batch // rows

    def step_rows(width):
        return pl.BlockSpec((rows, width), lambda b, i: (b * ns + i, 0))

    def prev_block(b, i):
        return (jnp.maximum((b * ns + i) * per_step - 1, 0), 0)

    return pl.pallas_call(
        _swa_prompt_kernel,
        grid=(batch, ns),
        in_specs=[pl.BlockSpec(memory_space=pltpu.SMEM), step_rows(A_WIDTH), step_rows(KV_WIDTH),
                  pl.BlockSpec((WINDOW, KV_WIDTH), prev_block)],
        out_specs=step_rows(A_WIDTH),
        out_shape=jax.ShapeDtypeStruct((n, A_WIDTH), BF16),
        scratch_shapes=[pltpu.VMEM((2, WINDOW, 2 * WINDOW), F32)],
        compiler_params=pltpu.CompilerParams(dimension_semantics=("parallel", "parallel"),
                                             vmem_limit_bytes=VMEM_LIMIT),
        name="swa_prompt",
    )(sinks, qa, kva, kva)


SAMPLE_ROWS = 128


def _mlstm_sample_kernel(zm_ref, gates_ref, gain_ref, c_ref, n_ref, m_ref,
                         um_ref, co_ref, no_ref, mo_ref,
                         qf_ref, qc_ref, nexp_ref, wkk_ref, dec_ref, *, steps):
    rows = SAMPLE_ROWS
    nb = rows // steps
    row = lax.broadcasted_iota(jnp.int32, (rows, rows), 0)
    col = lax.broadcasted_iota(jnp.int32, (rows, rows), 1)
    same = (row // steps) == (col // steps)
    causal = same & (col <= row)
    eye = row == col
    last = same & ((col % steps) == steps - 1)
    sub = lax.broadcasted_iota(jnp.int32, (2 * steps, M_DV), 0)
    tok_batch = lax.broadcasted_iota(jnp.int32, (M_DK, rows), 1) // steps

    def to_row(x_col):
        return jnp.sum(jnp.where(eye, x_col, 0.0), axis=0, keepdims=True)

    def pick(mask, x_row, fill, reduce):
        return reduce(jnp.where(mask, x_row, fill), axis=-1, keepdims=True)

    gates = gates_ref[...]
    for head in range(M_HEADS):
        lanes = slice(head * M_DK, (head + 1) * M_DK)
        ig_col = gates[:, GATE_I + head:GATE_I + head + 1]
        fg_col = gates[:, GATE_F + head:GATE_F + head + 1]
        lf_row = to_row(_log_sigmoid(fg_col))
        b_col = pick(causal, lf_row, 0.0, jnp.sum)
        b_row = to_row(b_col)
        ig_row = to_row(ig_col)
        m_prev = m_ref[head]

        d = jnp.where(causal, b_col - b_row + ig_row, -jnp.inf)
        inter = b_col + m_prev
        m_t = jnp.maximum(inter, jnp.max(d, axis=-1, keepdims=True))
        w = jnp.exp(d - m_t)
        a = jnp.exp(inter - m_t)

        q = zm_ref[:, lanes]
        k = zm_ref[:, M_WIDTH + head * M_DK:M_WIDTH + (head + 1) * M_DK]
        v = zm_ref[:, 2 * M_WIDTH + head * M_DV:2 * M_WIDTH + (head + 1) * M_DV]
        o = zm_ref[:, 3 * M_WIDTH + head * M_DV:3 * M_WIDTH + (head + 1) * M_DV]
        qf_ref[head] = q.astype(F32)
        for t in range(steps):
            nexp_ref[head, pl.ds(t, nb, stride=steps), :] = n_ref[:, lanes]

        for j in range(nb // 2):
            pair = pl.ds(j * 2 * steps, 2 * steps)
            q8 = qf_ref[head, pair, :].astype(BF16)
            r0 = _dot(q8, c_ref[2 * j, head].astype(BF16))
            r1 = _dot(q8, c_ref[2 * j + 1, head].astype(BF16))
            qc_ref[head, pair, :] = jnp.where(sub < steps, r0, r1)

        s = _dot_nt(q, k) * w
        qn = jnp.sum(qf_ref[head] * nexp_ref[head], axis=-1, keepdims=True)
        num = a * qc_ref[head] + _dot(s.astype(BF16), v)
        den = a * qn + jnp.sum(s, axis=-1, keepdims=True)
        hh = num / jnp.maximum(jnp.abs(den), jnp.exp(-m_t))
        um_ref[:, lanes] = _head_norm_gate(hh, o, gain_ref[:, lanes])

        b_last = pick(last, b_row, 0.0, jnp.sum)
        g_col = b_last - b_col + ig_col
        g_max = pick(same, to_row(g_col), -jnp.inf, jnp.max)
        m_new = jnp.maximum(b_last + m_prev, g_max)
        decay = jnp.exp(b_last + m_prev - m_new)
        wk = jnp.exp(g_col - m_new)
        mo_ref[head] = m_new

        kf = k.astype(F32)
        wkk_ref[head] = wk * kf
        dec_ref[head] = jnp.broadcast_to(decay, (rows, M_DV))
        n_sum = wkk_ref[head, pl.ds(0, nb, stride=steps), :]
        for t in range(1, steps):
            n_sum = n_sum + wkk_ref[head, pl.ds(t, nb, stride=steps), :]
        no_ref[:, lanes] = dec_ref[head, pl.ds(0, nb, stride=steps), :] * n_ref[:, lanes] + n_sum

        k_t = kf.T
        wv = (wk * v.astype(F32)).astype(BF16)
        for b in range(nb):
            upd = _dot(jnp.where(tok_batch == b, k_t, 0.0).astype(BF16), wv)
            co_ref[b, head] = dec_ref[head, pl.ds(b * steps, 1), :] * c_ref[b, head] + upd


def _mlstm_sample(zm, gates, gain, c, n2, m_rows, steps):
    rows = zm.shape[0]
    nb = SAMPLE_ROWS // steps

    def tokens(width):
        return pl.BlockSpec((SAMPLE_ROWS, width), lambda g: (g, 0))

    c_spec = pl.BlockSpec((nb, M_HEADS, M_DK, M_DV), lambda g: (g, 0, 0, 0))
    n_spec = pl.BlockSpec((nb, M_HEADS * M_DK), lambda g: (g, 0))
    m_spec = pl.BlockSpec((M_HEADS, SAMPLE_ROWS, 1), lambda g: (0, g, 0))
    per_head = pltpu.VMEM((M_HEADS, SAMPLE_ROWS, M_DK), F32)
    return pl.pallas_call(
        functools.partial(_mlstm_sample_kernel, steps=steps),
        grid=(rows // SAMPLE_ROWS,),
        in_specs=[tokens(ZM_WIDTH), tokens(LANES), pl.BlockSpec((1, M_WIDTH), lambda g: (0, 0)),
                  c_spec, n_spec, m_spec],
        out_specs=[tokens(M_WIDTH), c_spec, n_spec, m_spec],
        out_shape=[jax.ShapeDtypeStruct((rows, M_WIDTH), BF16),
                   jax.ShapeDtypeStruct(c.shape, F32),
                   jax.ShapeDtypeStruct(n2.shape, F32),
                   jax.ShapeDtypeStruct(m_rows.shape, F32)],
        scratch_shapes=[per_head,
                        per_head,
                        per_head,
                        per_head,
                        per_head],
        compiler_params=pltpu.CompilerParams(dimension_semantics=("parallel",),
                                             vmem_limit_bytes=VMEM_LIMIT),
        name="mlstm_sample",
    )(zm, gates, gain, c, n2, m_rows)


def _swa_sample_kernel(sinks_ref, qa_ref, kva_ref, ckt_ref, cvt_ref, ua_ref, nkt_ref, nvt_ref,
                       s_ref, e_ref, pv_ref, *, steps):
    rows = SAMPLE_ROWS
    nb = rows // steps
    pair_rows = 2 * steps
    hw = A_KV_HEADS * A_HD
    tok = lax.broadcasted_iota(jnp.int32, (rows, LANES), 0)
    lane = lax.broadcasted_iota(jnp.int32, (rows, LANES), 1)
    low = lane < A_HD

    qf = qa_ref[...].astype(F32)
    q_blocks = []
    for p in range(A_GROUP):
        qp = qf[:, p * LANES:(p + 1) * LANES]
        q_blocks.append(jnp.where(low, qp, 0.0))
        q_blocks.append(jnp.where(low, 0.0, qp))
    n_blk = len(q_blocks)

    kv_new = kva_ref[...]
    kv_new_t = kv_new.T
    k_new_t = kv_new_t[:hw]
    v_new_t = kv_new_t[hw:]
    v_new = kv_new[:, hw:].astype(BF16)
    s_new = _dot(jnp.concatenate(q_blocks, axis=0).astype(BF16), k_new_t.astype(BF16))

    def pair_rows_of(blocks, j):
        return jnp.concatenate([blk[j * pair_rows:(j + 1) * pair_rows] for blk in blocks], axis=0)

    first = (lax.broadcasted_iota(jnp.int32, (n_blk * pair_rows, LANES), 0) % pair_rows) < steps

    for j in range(nb // 2):
        q_pair = pair_rows_of(q_blocks, j).astype(BF16)
        s_pair = jnp.where(first, _dot(q_pair, ckt_ref[2 * j].astype(BF16)),
                           _dot(q_pair, ckt_ref[2 * j + 1].astype(BF16)))
        for blk in range(n_blk):
            s_ref[pl.ds(blk * rows + j * pair_rows, pair_rows), :] = s_pair[blk * pair_rows:(blk + 1) * pair_rows]

    t_q = tok % steps
    dist_c = WINDOW + t_q - lane
    valid_c = dist_c < WINDOW
    dist_n = t_q - lane % steps
    valid_n = ((tok // steps) == (lane // steps)) & (dist_n >= 0)
    dist_cf = dist_c.astype(F32)
    dist_nf = dist_n.astype(F32)
    pv_new = []
    inv_den = []
    for blk in range(n_blk):
        head = (blk % A_KV_HEADS) * A_GROUP + blk // A_KV_HEADS
        slope = _alibi_slope(head)
        sink = sinks_ref[head]
        r = pl.ds(blk * rows, rows)
        s_c = jnp.where(valid_c, s_ref[r, :] - slope * dist_cf, -jnp.inf)
        s_n = jnp.where(valid_n, s_new[blk * rows:(blk + 1) * rows] - slope * dist_nf, -jnp.inf)
        mx = jnp.maximum(jnp.maximum(jnp.max(s_c, axis=-1, keepdims=True),
                                     jnp.max(s_n, axis=-1, keepdims=True)), sink)
        e_c = jnp.exp(s_c - mx)
        e_n = jnp.exp(s_n - mx)
        den = jnp.sum(e_c, axis=-1, keepdims=True) + jnp.sum(e_n, axis=-1, keepdims=True) + jnp.exp(sink - mx)
        e_ref[r, :] = e_c
        pv_new.append(_dot(e_n.astype(BF16), v_new))
        inv_den.append(1.0 / den)

    for j in range(nb // 2):
        e_pair = jnp.concatenate([e_ref[pl.ds(blk * rows + j * pair_rows, pair_rows), :] for blk in range(n_blk)],
                                 axis=0).astype(BF16)
        pv_pair = jnp.where(first, _dot_nt(e_pair, cvt_ref[2 * j].astype(BF16)),
                            _dot_nt(e_pair, cvt_ref[2 * j + 1].astype(BF16)))
        for blk in range(n_blk):
            pv_ref[pl.ds(blk * rows + j * pair_rows, pair_rows), :] = pv_pair[blk * pair_rows:(blk + 1) * pair_rows]

    for p in range(A_GROUP):
        halves = []
        for kvh in range(A_KV_HEADS):
            blk = p * A_KV_HEADS + kvh
            halves.append((pv_ref[pl.ds(blk * rows, rows), :] + pv_new[blk]) * inv_den[blk])
        ua_ref[:, p * LANES:(p + 1) * LANES] = jnp.where(low, halves[0], halves[1]).astype(BF16)

    tail = lane >= WINDOW - steps
    for b in range(nb):
        shift = (WINDOW - steps - b * steps) % rows
        nkt_ref[b] = jnp.where(tail, pltpu.roll(k_new_t, shift, 1), pltpu.roll(ckt_ref[b], WINDOW - steps, 1))
        nvt_ref[b] = jnp.where(tail, pltpu.roll(v_new_t, shift, 1), pltpu.roll(cvt_ref[b], WINDOW - steps, 1))


def _swa_sample(sinks, qa, kva, ckt, cvt, steps):
    batch = ckt.shape[0]
    assert ckt.shape[1:] == (A_KV_HEADS * A_HD, WINDOW) and SAMPLE_ROWS == WINDOW == LANES
    nb = SAMPLE_ROWS // steps
    cache = pl.BlockSpec((nb, LANES, WINDOW), lambda g: (g, 0, 0))

    def rows(width):
        return pl.BlockSpec((SAMPLE_ROWS, width), lambda g: (g, 0))

    stacked = (A_HEADS * SAMPLE_ROWS, LANES)
    return pl.pallas_call(
        functools.partial(_swa_sample_kernel, steps=steps),
        grid=(batch // nb,),
        in_specs=[pl.BlockSpec(memory_space=pltpu.SMEM), rows(A_WIDTH), rows(KV_WIDTH), cache, cache],
        out_specs=[rows(A_WIDTH), cache, cache],
        out_shape=[jax.ShapeDtypeStruct((batch * steps, A_WIDTH), BF16),
                   jax.ShapeDtypeStruct(ckt.shape, F32),
                   jax.ShapeDtypeStruct(cvt.shape, F32)],
        scratch_shapes=[pltpu.VMEM(stacked, F32), pltpu.VMEM(stacked, F32), pltpu.VMEM(stacked, F32)],
        compiler_params=pltpu.CompilerParams(dimension_semantics=("parallel",),
                                             vmem_limit_bytes=VMEM_LIMIT),
        name="swa_sample",
    )(sinks, qa, kva, ckt, cvt)


def _prep_proj(w_in, b_gate):
    m_end = 4 * M_WIDTH
    g_end = m_end + 2 * M_HEADS
    wt = w_in.T
    qa = wt[g_end:g_end + A_WIDTH].reshape(A_KV_HEADS, A_GROUP, A_HD, D_MODEL).transpose(1, 0, 2, 3)
    gates = jnp.pad(wt[m_end:g_end], ((0, LANES - 2 * M_HEADS), (0, 0)))
    win = jnp.concatenate([wt[:m_end], qa.reshape(A_WIDTH, D_MODEL), wt[g_end + A_WIDTH:], gates],
                          axis=0).astype(BF16)
    bias = jnp.pad(b_gate, (0, LANES - 2 * M_HEADS)).reshape(1, LANES)
    return win, bias


def _prep_out(w_out):
    wa = w_out[M_WIDTH:].reshape(A_KV_HEADS, A_GROUP, A_HD, D_MODEL).transpose(1, 0, 2, 3)
    return jnp.concatenate([w_out[:M_WIDTH], wa.reshape(A_WIDTH, D_MODEL)], axis=0).astype(BF16)


def kernel(x_prompt, x_sample, cache_swa_k, cache_swa_v, state_mlstm_C, state_mlstm_n, state_mlstm_m,
           norm_gains, ffn_w_gate, ffn_w_up, ffn_w_down, w_in, b_gate, mlstm_norm_gain, attn_sinks, w_out):
    assert norm_gains.shape[0] == 1, "single layer"
    batch, seq, _ = x_prompt.shape
    dec_batch, steps, _ = x_sample.shape
    gains = norm_gains[0]
    ffn = (ffn_w_gate, ffn_w_up, ffn_w_down)
    win, bias = _prep_proj(w_in[0], b_gate[0])
    wo = _prep_out(w_out[0])
    gain = mlstm_norm_gain[0].reshape(1, M_WIDTH)
    sinks = attn_sinks[0]
    kv_lanes = A_KV_HEADS * A_HD

    chunk = min(MLSTM_CHUNK, seq)
    assert seq % chunk == 0
    xp, zm, qa, kva, gates = _ffn_proj(x_prompt.reshape(batch * seq, D_MODEL), gains, *ffn, win, bias,
                                       0, 0, 1, 2, chunk)
    um, p_c, p_n, p_m = _mlstm_prompt(zm, gates, gain, batch, seq, chunk)
    ua = _swa_prompt(sinks, qa, kva, batch, seq)
    yp = _mix_ffn(xp, um, ua, wo, gains, *ffn, 1, 3, 4, 5).reshape(batch, seq, D_MODEL)
    keep = min(WINDOW, seq)
    kv_tail = kva.reshape(batch, seq, KV_WIDTH)[:, seq - keep:]
    p_k = kv_tail[..., :kv_lanes].reshape(1, batch, keep, A_KV_HEADS, A_HD)
    p_v = kv_tail[..., kv_lanes:].reshape(1, batch, keep, A_KV_HEADS, A_HD)
    p_c = p_c[None]
    p_n = p_n[None, :, :, 0, :]
    p_m = p_m[None, :, :, 0, 0]

    xs, zm_s, qa_s, kva_s, gates_s = _ffn_proj(x_sample.reshape(dec_batch * steps, D_MODEL), gains, *ffn,
                                               win, bias, 0, 0, 1, 2, min(chunk, dec_batch * steps))
    m_rows = jnp.repeat(state_mlstm_m[0].T, steps, axis=1)[..., None]
    um_s, s_c, s_n2, s_m_rows = _mlstm_sample(zm_s, gates_s, gain, state_mlstm_C[0],
                                              state_mlstm_n[0].reshape(dec_batch, M_HEADS * M_DK),
                                              m_rows, steps)
    wc = cache_swa_k.shape[2]
    ua_s, s_kt, s_vt = _swa_sample(sinks, qa_s, kva_s,
                                   cache_swa_k[0].reshape(dec_batch, wc, kv_lanes).transpose(0, 2, 1),
                                   cache_swa_v[0].reshape(dec_batch, wc, kv_lanes).transpose(0, 2, 1), steps)
    ys = _mix_ffn(xs, um_s, ua_s, wo, gains, *ffn, 1, 3, 4, 5).reshape(dec_batch, steps, D_MODEL)
    s_n = s_n2.reshape(1, dec_batch, M_HEADS, M_DK)
    s_m = s_m_rows[:, ::steps, 0].T[None]
    s_k = s_kt.transpose(0, 2, 1).reshape(1, dec_batch, wc, A_KV_HEADS, A_HD)
    s_v = s_vt.transpose(0, 2, 1).reshape(1, dec_batch, wc, A_KV_HEADS, A_HD)
    return (yp, ys, p_k, p_v, p_c, p_n, p_m, s_k, s_v, s_c[None], s_n, s_m)
```

```python
import functools

import jax
import jax.numpy as jnp
from jax import lax
from jax.experimental import pallas as pl
from jax.experimental.pallas import tpu as pltpu

F32 = jnp.float32
BF16 = jnp.bfloat16

D_MODEL = 1024
D_FF = 2816
M_HEADS = 4
M_DK = 128
M_DV = 128
M_WIDTH = M_HEADS * M_DV
A_HEADS = 8
A_KV_HEADS = 2
A_GROUP = A_HEADS // A_KV_HEADS
A_HD = 64
A_WIDTH = A_HEADS * A_HD
WINDOW = 128
RMS_EPS = 1e-6

LANES = 128
FFN_CHUNK = 256
N_FFN_CHUNKS = D_FF // FFN_CHUNK
FFN_STAGE_SLOTS = 4
TOKEN_TILE = 512
MLSTM_CHUNK = 256
ZM_WIDTH = 4 * M_WIDTH
KV_WIDTH = 2 * A_KV_HEADS * A_HD
VMEM_LIMIT = 56 * 1024 * 1024
MLSTM_CHUNKS_PER_STEP = 8
GATE_I, GATE_F, GATE_B, GATE_X = 0, M_HEADS, 2 * M_HEADS, 3 * M_HEADS

NT_DIMS = (((1,), (1,)), ((), ()))


def _rms(x, g):
    ms = jnp.mean(x * x, axis=-1, keepdims=True)
    return x * lax.rsqrt(ms + RMS_EPS) * g


def _log_sigmoid(x):
    return jnp.minimum(x, 0.0) - jnp.log1p(jnp.exp(-jnp.abs(x)))


def _dot(a, b):
    return jnp.dot(a, b, preferred_element_type=F32)


def _dot_nt(a, b):
    return lax.dot_general(a, b, NT_DIMS, preferred_element_type=F32)


def _resident(shape):
    nd = len(shape)
    return pl.BlockSpec(shape, lambda *_: (0,) * nd, pipeline_mode=pl.Buffered(1))


def _ffn_weight_copies(w_hbm, j, f, slot, stage_refs, sems):
    wg_hbm, wu_hbm, wd_hbm = w_hbm
    cols = pl.ds(f * FFN_CHUNK, FFN_CHUNK)
    sources = (wg_hbm.at[0, j, :, cols], wu_hbm.at[0, j, :, cols], wd_hbm.at[0, j, cols, :])
    return [pltpu.make_async_copy(src, stage.at[slot], sems.at[slot, k])
            for k, (src, stage) in enumerate(zip(sources, stage_refs))]


def _swiglu(x, gpre, w_refs, h_ref, acc_ref, fetch=None):
    wg_ref, wu_ref, wd_ref = w_refs
    if fetch is not None:
        for f in range(min(FFN_STAGE_SLOTS, N_FFN_CHUNKS)):
            for copy in _ffn_weight_copies(*fetch[:2], f, f % FFN_STAGE_SLOTS, *fetch[2:]):
                copy.start()
    h_ref[...] = _rms(x, gpre).astype(BF16)
    for f in range(N_FFN_CHUNKS):
        cols = slice(f * FFN_CHUNK, (f + 1) * FFN_CHUNK)
        if fetch is not None:
            slot = f % FFN_STAGE_SLOTS
            stage_g, stage_u, stage_d = fetch[2]
            for copy in _ffn_weight_copies(*fetch[:2], f, slot, *fetch[2:]):
                copy.wait()
            wg_ref[:, cols] = stage_g[slot].astype(BF16)
            wu_ref[:, cols] = stage_u[slot].astype(BF16)
            wd_ref[cols, :] = stage_d[slot].astype(BF16)
            if f + FFN_STAGE_SLOTS < N_FFN_CHUNKS:
                for copy in _ffn_weight_copies(*fetch[:2], f + FFN_STAGE_SLOTS, slot, *fetch[2:]):
                    copy.start()
        h = h_ref[...]
        g = _dot(h, wg_ref[:, cols])
        u = _dot(h, wu_ref[:, cols])
        part = _dot((g * jax.nn.sigmoid(g) * u).astype(BF16), wd_ref[cols, :])
        if f == 0:
            acc_ref[...] = part
        else:
            acc_ref[...] += part


def _first_step_fetches(body, fetch):
    first = pl.program_id(0) == 0

    @pl.when(first)
    def _():
        body(fetch)

    @pl.when(jnp.logical_not(first))
    def _():
        body(None)


def _half_step(x, acc_ref, gpost):
    return x + _rms(acc_ref[...], 0.5 * gpost)


def _project(x, gain, win_ref, bias_ref, h_ref, zm_ref, qa_ref, kva_ref, gates_ref, chunk):
    h_ref[...] = _rms(x, gain).astype(BF16)

    def cols(lo, width):
        return _dot_nt(h_ref[...], win_ref[lo:lo + width, :])

    gates = cols(ZM_WIDTH + A_WIDTH + KV_WIDTH, LANES) + bias_ref[...]
    lane = lax.broadcasted_iota(jnp.int32, (chunk, LANES), 1)
    pos = lax.broadcasted_iota(jnp.int32, (chunk, LANES), 0)
    for c0 in range(0, gates.shape[0], chunk):
        g = gates[c0:c0 + chunk]
        b = jnp.where((lane >= GATE_F) & (lane < GATE_B), _log_sigmoid(g), 0.0)
        shift = 1
        while shift < chunk:
            b = b + jnp.where(pos >= shift, pltpu.roll(b, shift, 0), 0.0)
            shift *= 2
        x_gate = pltpu.roll(g, GATE_X - GATE_I, 1) - pltpu.roll(b, GATE_X - GATE_F, 1)
        out = jnp.where(lane < GATE_B, g, jnp.where(lane < GATE_X, pltpu.roll(b, GATE_B - GATE_F, 1), x_gate))
        gates_ref[c0:c0 + chunk, :] = jnp.where(lane < GATE_X + M_HEADS, out, 0.0)

    zm_ref[:, 0:M_WIDTH] = cols(0, M_WIDTH).astype(BF16)
    zm_ref[:, M_WIDTH:2 * M_WIDTH] = (cols(M_WIDTH, M_WIDTH) * (M_DK ** -0.5)).astype(BF16)
    zm_ref[:, 2 * M_WIDTH:3 * M_WIDTH] = cols(2 * M_WIDTH, M_WIDTH).astype(BF16)
    zm_ref[:, 3 * M_WIDTH:] = jax.nn.sigmoid(cols(3 * M_WIDTH, M_WIDTH)).astype(BF16)
    qa_ref[...] = (cols(ZM_WIDTH, A_WIDTH) * (A_HD ** -0.5)).astype(BF16)
    kva_ref[...] = cols(ZM_WIDTH + A_WIDTH, KV_WIDTH)


def _ffn_proj_kernel(x_ref, gains_ref, wg_hbm, wu_hbm, wd_hbm, win_ref, bias_ref,
                     x1_ref, zm_ref, qa_ref, kva_ref, gates_ref,
                     h_ref, acc_ref, wg_ref, wu_ref, wd_ref, stage_g, stage_u, stage_d, sems,
                     *, j, pre, post, mixer, chunk):
    def body(fetch):
        x = x_ref[...]
        _swiglu(x, gains_ref[pre:pre + 1, :], (wg_ref, wu_ref, wd_ref), h_ref, acc_ref, fetch)
        x1 = _half_step(x, acc_ref, gains_ref[post:post + 1, :])
        x1_ref[...] = x1
        _project(x1, gains_ref[mixer:mixer + 1, :], win_ref, bias_ref, h_ref, zm_ref, qa_ref, kva_ref,
                 gates_ref, chunk)

    _first_step_fetches(body, ((wg_hbm, wu_hbm, wd_hbm), j, (stage_g, stage_u, stage_d), sems))


def _mix_ffn_kernel(x_ref, um_ref, ua_ref, wo_ref, gains_ref, wg_hbm, wu_hbm, wd_hbm, o_ref,
                    h_ref, acc_ref, wg_ref, wu_ref, wd_ref, stage_g, stage_u, stage_d, sems,
                    *, j, mix, pre, post):
    def body(fetch):
        u = jnp.concatenate([um_ref[...], ua_ref[...]], axis=-1)
        x = x_ref[...] + _rms(_dot(u, wo_ref[...]), gains_ref[mix:mix + 1, :])
        _swiglu(x, gains_ref[pre:pre + 1, :], (wg_ref, wu_ref, wd_ref), h_ref, acc_ref, fetch)
        o_ref[...] = _half_step(x, acc_ref, gains_ref[post:post + 1, :])

    _first_step_fetches(body, ((wg_hbm, wu_hbm, wd_hbm), j, (stage_g, stage_u, stage_d), sems))


def _ffn_scratch(tm):
    return [pltpu.VMEM((tm, D_MODEL), BF16),
            pltpu.VMEM((tm, D_MODEL), F32),
            pltpu.VMEM((D_MODEL, D_FF), BF16),
            pltpu.VMEM((D_MODEL, D_FF), BF16),
            pltpu.VMEM((D_FF, D_MODEL), BF16),
            pltpu.VMEM((FFN_STAGE_SLOTS, D_MODEL, FFN_CHUNK), F32),
            pltpu.VMEM((FFN_STAGE_SLOTS, D_MODEL, FFN_CHUNK), F32),
            pltpu.VMEM((FFN_STAGE_SLOTS, FFN_CHUNK, D_MODEL), F32),
            pltpu.SemaphoreType.DMA((FFN_STAGE_SLOTS, 3))]


def _dense_params():
    return pltpu.CompilerParams(dimension_semantics=("arbitrary",), vmem_limit_bytes=VMEM_LIMIT)


_HBM = pl.BlockSpec(memory_space=pl.ANY)


def _ffn_proj(x, gains, wg, wu, wd, win, bias, j, pre, post, mixer, chunk):
    n = x.shape[0]
    tm = min(TOKEN_TILE, n)
    assert tm % chunk == 0

    def rows(width):
        return pl.BlockSpec((tm, width), lambda i: (i, 0))

    return pl.pallas_call(
        functools.partial(_ffn_proj_kernel, j=j, pre=pre, post=post, mixer=mixer, chunk=chunk),
        grid=(n // tm,),
        in_specs=[rows(D_MODEL), _resident(gains.shape), _HBM, _HBM, _HBM,
                  _resident(win.shape), _resident(bias.shape)],
        out_specs=[rows(D_MODEL), rows(ZM_WIDTH), rows(A_WIDTH), rows(KV_WIDTH), rows(LANES)],
        out_shape=[jax.ShapeDtypeStruct((n, D_MODEL), F32),
                   jax.ShapeDtypeStruct((n, ZM_WIDTH), BF16),
                   jax.ShapeDtypeStruct((n, A_WIDTH), BF16),
                   jax.ShapeDtypeStruct((n, KV_WIDTH), F32),
                   jax.ShapeDtypeStruct((n, LANES), F32)],
        scratch_shapes=_ffn_scratch(tm),
        compiler_params=_dense_params(),
        name="ffn_half_step_input_projection",
    )(x, gains, wg, wu, wd, win, bias)


def _mix_ffn(x, um, ua, wo, gains, wg, wu, wd, j, mix, pre, post):
    n = x.shape[0]
    tm = min(TOKEN_TILE, n)
    row = pl.BlockSpec((tm, D_MODEL), lambda i: (i, 0))
    half = pl.BlockSpec((tm, M_WIDTH), lambda i: (i, 0))
    return pl.pallas_call(
        functools.partial(_mix_ffn_kernel, j=j, mix=mix, pre=pre, post=post),
        grid=(n // tm,),
        in_specs=[row, half, half, _resident(wo.shape), _resident(gains.shape), _HBM, _HBM, _HBM],
        out_specs=row,
        out_shape=jax.ShapeDtypeStruct((n, D_MODEL), F32),
        scratch_shapes=_ffn_scratch(tm),
        compiler_params=_dense_params(),
        name="out_proj_ffn_half_step",
    )(x, um, ua, wo, gains, wg, wu, wd)


def _head_norm_gate(hh, o, gain):
    hn = hh * lax.rsqrt(jnp.mean(hh * hh, axis=-1, keepdims=True) + RMS_EPS) * gain
    return (o.astype(F32) * hn).astype(BF16)


def _mlstm_chunks(zm_ref, gates_ref, gain_ref, um_ref, cnt_ref, ms_ref, chunk):
    si = lax.broadcasted_iota(jnp.int32, (chunk, chunk), 0)
    ti = lax.broadcasted_iota(jnp.int32, (chunk, chunk), 1)
    causal = si <= ti
    ones_rows = jnp.where(lax.broadcasted_iota(jnp.int32, (LANES, chunk), 0) == 0, 1.0, 0.0).astype(BF16)

    def run_chunk(c):
        rows = slice(c * chunk, (c + 1) * chunk)
        gates = gates_ref[rows, :]
        gates_t = gates.T
        for h in range(M_HEADS):
            lanes = slice(h * M_DK, (h + 1) * M_DK)
            q = zm_ref[rows, lanes]
            k = zm_ref[rows, M_WIDTH + h * M_DK:M_WIDTH + (h + 1) * M_DK]
            v = zm_ref[rows, 2 * M_WIDTH + h * M_DV:2 * M_WIDTH + (h + 1) * M_DV]
            o = zm_ref[rows, 3 * M_WIDTH + h * M_DV:3 * M_WIDTH + (h + 1) * M_DV]
            x_col = gates[:, GATE_X + h:GATE_X + h + 1]
            ig_row = gates_t[GATE_I + h:GATE_I + h + 1, :]
            b_row = gates_t[GATE_B + h:GATE_B + h + 1, :]
            m_prev = ms_ref[h][:, 0:1]
            cnt = cnt_ref[h]

            xm = jnp.where(causal, x_col, -jnp.inf)
            mu = jnp.maximum(m_prev, jnp.max(xm, axis=0, keepdims=True))
            a = jnp.exp(m_prev - mu)
            s_t = _dot_nt(k, q) * jnp.exp(xm - mu)
            v_ext = jnp.concatenate([v.astype(F32).T.astype(BF16), ones_rows], axis=0)
            nd = a * _dot_nt(cnt.astype(BF16), q) + _dot(v_ext, s_t.astype(BF16))
            den = nd[M_DV:M_DV + 1, :]
            hh = nd[:M_DV, :] * (1.0 / jnp.maximum(jnp.abs(den), jnp.exp(-(b_row + mu))))
            hn = hh * lax.rsqrt(jnp.mean(hh * hh, axis=0, keepdims=True) + RMS_EPS)
            um_ref[rows, lanes] = (hn.T * gain_ref[:, lanes] * o.astype(F32)).astype(BF16)

            b_last = b_row[:, chunk - 1:chunk]
            g = b_last - b_row + ig_row
            m_new = jnp.maximum(b_last + m_prev, jnp.max(g, axis=-1, keepdims=True))
            decay = jnp.exp(b_last + m_prev - m_new)
            wv = (jnp.exp(g - m_new) * v_ext.astype(F32)).astype(BF16)
            cnt_ref[h] = decay * cnt + _dot(wv, k)
            ms_ref[h] = jnp.broadcast_to(m_new, (1, LANES))

    return [functools.partial(run_chunk, c) for c in range(zm_ref.shape[0] // chunk)]


def _alibi_slope(head):
    return 2.0 ** (-8.0 * (head + 1) / A_HEADS)


def _swa_blocks(sinks_ref, qa_ref, kvc_ref, kvp_ref, ua_ref, mask_ref, first):
    hw = A_KV_HEADS * A_HD
    n_blocks = qa_ref.shape[0] // WINDOW
    qi = lax.broadcasted_iota(jnp.int32, (WINDOW, 2 * WINDOW), 0)
    kj = lax.broadcasted_iota(jnp.int32, (WINDOW, 2 * WINDOW), 1)
    dist = WINDOW + qi - kj
    band = (dist >= 0) & (dist < WINDOW)
    mask_ref[0] = jnp.where(band, 0.0, -jnp.inf)
    mask_ref[1] = jnp.where(band & (kj >= WINDOW), 0.0, -jnp.inf)
    lane = lax.broadcasted_iota(jnp.int32, (WINDOW, LANES), 1)
    low = lane < A_HD

    assert 2 * WINDOW <= 256, "positions must stay exactly representable in bf16"
    q_pos = (WINDOW + lax.broadcasted_iota(jnp.int32, (WINDOW, LANES), 0)).astype(F32)
    k_lane = lax.broadcasted_iota(jnp.int32, (2 * WINDOW, LANES), 1)
    k_pos = lax.broadcasted_iota(jnp.int32, (2 * WINDOW, LANES), 0).astype(F32)
    spare = (A_HD, 0)
    k_bias = [jnp.where(k_lane == spare[kvh], 1.0, jnp.where(k_lane == spare[kvh] + 1, k_pos, 0.0)).astype(BF16)
              for kvh in range(A_KV_HEADS)]
    q_bias = {}
    for head in range(A_HEADS):
        slope = _alibi_slope(head)
        at = spare[head // A_GROUP]
        q_bias[head] = jnp.where(lane == at, -slope * q_pos, jnp.where(lane == at + 1, slope, 0.0)).astype(BF16)

    def run_block(j):
        rows = slice(j * WINDOW, (j + 1) * WINDOW)
        kvc = kvc_ref[rows, :]
        kvp = kvp_ref[...] if j == 0 else kvc_ref[(j - 1) * WINDOW:j * WINDOW, :]
        mask = mask_ref[jnp.where(first, 1, 0)] if j == 0 else mask_ref[0]
        kk = jnp.concatenate([kvp[:, :hw], kvc[:, :hw]], axis=0).astype(BF16)
        kk_biased = [jnp.where(k_lane < A_HD, kk, k_bias[0]), jnp.where(k_lane < A_HD, k_bias[1], kk)]
        vv = jnp.concatenate([kvp[:, hw:], kvc[:, hw:]], axis=0).astype(BF16)
        for p in range(A_GROUP):
            qp = qa_ref[rows, p * LANES:(p + 1) * LANES]
            outs = []
            for kvh in range(A_KV_HEADS):
                head = kvh * A_GROUP + p
                keep = low if kvh == 0 else jnp.logical_not(low)
                s = _dot_nt(jnp.where(keep, qp, q_bias[head]), kk_biased[kvh])
                s = s + mask
                sink = sinks_ref[head]
                mx = jnp.maximum(jnp.max(s, axis=-1, keepdims=True), sink)
                e = jnp.exp(s - mx)
                den = jnp.sum(e, axis=-1, keepdims=True) + jnp.exp(sink - mx)
                outs.append(_dot(e.astype(BF16), vv) / den)
            ua_ref[rows, p * LANES:(p + 1) * LANES] = jnp.where(low, outs[0], outs[1]).astype(BF16)

    return [functools.partial(run_block, j) for j in range(n_blocks)]


def _prompt_mixers_kernel(sinks_ref, zm_ref, gates_ref, gain_ref, qa_ref, kvc_ref, kvp_ref,
                          um_ref, ua_ref, c_ref, n_ref, m_ref, cnt_ref, ms_ref, mask_ref, *, chunk):
    step = pl.program_id(1)

    @pl.when(step == 0)
    def _():
        cnt_ref[...] = jnp.zeros_like(cnt_ref)
        ms_ref[...] = jnp.zeros_like(ms_ref)

    chunks = _mlstm_chunks(zm_ref, gates_ref, gain_ref, um_ref, cnt_ref, ms_ref, chunk)
    blocks = _swa_blocks(sinks_ref, qa_ref, kvc_ref, kvp_ref, ua_ref, mask_ref, step == 0)
    per_chunk = len(blocks) // len(chunks)
    for c, run_chunk in enumerate(chunks):
        run_chunk()
        for run_block in blocks[c * per_chunk:(c + 1) * per_chunk]:
            run_block()

    @pl.when(step == pl.num_programs(1) - 1)
    def _():
        for h in range(M_HEADS):
            c_ref[0, h] = cnt_ref[h][:M_DV, :].T
            n_ref[0, h] = cnt_ref[h][M_DV:M_DV + 1, :]
            m_ref[0, h] = ms_ref[h]


def _prompt_mixers(sinks, zm, gates, gain, qa, kva, batch, seq, chunk):
    n = batch * seq
    per_step = min(MLSTM_CHUNKS_PER_STEP, seq // chunk)
    rows = per_step * chunk
    assert seq % rows == 0 and chunk % WINDOW == 0
    ns = seq // rows

    def step_rows(width):
        return pl.BlockSpec((rows, width), lambda b, i: (b * ns + i, 0))

    def prev_block(b, i):
        return (jnp.maximum((b * ns + i) * (rows // WINDOW) - 1, 0), 0)

    def state(*dims):
        return pl.BlockSpec((1, M_HEADS) + dims, lambda b, i: (b, 0, 0, 0))

    return pl.pallas_call(
        functools.partial(_prompt_mixers_kernel, chunk=chunk),
        grid=(batch, ns),
        in_specs=[pl.BlockSpec(memory_space=pltpu.SMEM), step_rows(ZM_WIDTH), step_rows(LANES),
                  pl.BlockSpec((1, M_WIDTH), lambda b, i: (0, 0)),
                  step_rows(A_WIDTH), step_rows(KV_WIDTH), pl.BlockSpec((WINDOW, KV_WIDTH), prev_block)],
        out_specs=[step_rows(M_WIDTH), step_rows(A_WIDTH), state(M_DK, M_DV), state(1, M_DK), state(1, LANES)],
        out_shape=[jax.ShapeDtypeStruct((n, M_WIDTH), BF16),
                   jax.ShapeDtypeStruct((n, A_WIDTH), BF16),
                   jax.ShapeDtypeStruct((batch, M_HEADS, M_DK, M_DV), F32),
                   jax.ShapeDtypeStruct((batch, M_HEADS, 1, M_DK), F32),
                   jax.ShapeDtypeStruct((batch, M_HEADS, 1, LANES), F32)],
        scratch_shapes=[pltpu.VMEM((M_HEADS, 2 * LANES, M_DK), F32), pltpu.VMEM((M_HEADS, 1, LANES), F32),
                        pltpu.VMEM((2, WINDOW, 2 * WINDOW), F32)],
        compiler_params=pltpu.CompilerParams(dimension_semantics=("parallel", "arbitrary"),
                                             vmem_limit_bytes=VMEM_LIMIT),
        name="prompt_mixers",
    )(sinks, zm, gates, gain, qa, kva, kva)


SAMPLE_ROWS = 128


def _mlstm_sample_kernel(zm_ref, gates_ref, gain_ref, c_ref, n_ref, m_ref,
                         um_ref, co_ref, no_ref, mo_ref,
                         qf_ref, qc_ref, nexp_ref, wkk_ref, dec_ref, *, steps):
    rows = SAMPLE_ROWS
    nb = rows // steps
    row = lax.broadcasted_iota(jnp.int32, (rows, rows), 0)
    col = lax.broadcasted_iota(jnp.int32, (rows, rows), 1)
    same = (row // steps) == (col // steps)
    causal = same & (col <= row)
    eye = row == col
    last = same & ((col % steps) == steps - 1)
    sub = lax.broadcasted_iota(jnp.int32, (2 * steps, M_DV), 0)
    tok_batch = lax.broadcasted_iota(jnp.int32, (M_DK, rows), 1) // steps

    def to_row(x_col):
        return jnp.sum(jnp.where(eye, x_col, 0.0), axis=0, keepdims=True)

    def pick(mask, x_row, fill, reduce):
        return reduce(jnp.where(mask, x_row, fill), axis=-1, keepdims=True)

    gates = gates_ref[...]
    for head in range(M_HEADS):
        lanes = slice(head * M_DK, (head + 1) * M_DK)
        ig_col = gates[:, GATE_I + head:GATE_I + head + 1]
        fg_col = gates[:, GATE_F + head:GATE_F + head + 1]
        lf_row = to_row(_log_sigmoid(fg_col))
        b_col = pick(causal, lf_row, 0.0, jnp.sum)
        b_row = to_row(b_col)
        ig_row = to_row(ig_col)
        m_prev = m_ref[head]

        d = jnp.where(causal, b_col - b_row + ig_row, -jnp.inf)
        inter = b_col + m_prev
        m_t = jnp.maximum(inter, jnp.max(d, axis=-1, keepdims=True))
        w = jnp.exp(d - m_t)
        a = jnp.exp(inter - m_t)

        q = zm_ref[:, lanes]
        k = zm_ref[:, M_WIDTH + head * M_DK:M_WIDTH + (head + 1) * M_DK]
        v = zm_ref[:, 2 * M_WIDTH + head * M_DV:2 * M_WIDTH + (head + 1) * M_DV]
        o = zm_ref[:, 3 * M_WIDTH + head * M_DV:3 * M_WIDTH + (head + 1) * M_DV]
        qf_ref[head] = q.astype(F32)
        for t in range(steps):
            nexp_ref[head, pl.ds(t, nb, stride=steps), :] = n_ref[:, lanes]

        for j in range(nb // 2):
            pair = pl.ds(j * 2 * steps, 2 * steps)
            q8 = qf_ref[head, pair, :].astype(BF16)
            r0 = _dot(q8, c_ref[2 * j, head].astype(BF16))
            r1 = _dot(q8, c_ref[2 * j + 1, head].astype(BF16))
            qc_ref[head, pair, :] = jnp.where(sub < steps, r0, r1)

        s = _dot_nt(q, k) * w
        qn = jnp.sum(qf_ref[head] * nexp_ref[head], axis=-1, keepdims=True)
        num = a * qc_ref[head] + _dot(s.astype(BF16), v)
        den = a * qn + jnp.sum(s, axis=-1, keepdims=True)
        hh = num / jnp.maximum(jnp.abs(den), jnp.exp(-m_t))
        um_ref[:, lanes] = _head_norm_gate(hh, o, gain_ref[:, lanes])

        b_last = pick(last, b_row, 0.0, jnp.sum)
        g_col = b_last - b_col + ig_col
        g_max = pick(same, to_row(g_col), -jnp.inf, jnp.max)
        m_new = jnp.maximum(b_last + m_prev, g_max)
        decay = jnp.exp(b_last + m_prev - m_new)
        wk = jnp.exp(g_col - m_new)
        mo_ref[head] = m_new

        kf = k.astype(F32)
        wkk_ref[head] = wk * kf
        dec_ref[head] = jnp.broadcast_to(decay, (rows, M_DV))
        n_sum = wkk_ref[head, pl.ds(0, nb, stride=steps), :]
        for t in range(1, steps):
            n_sum = n_sum + wkk_ref[head, pl.ds(t, nb, stride=steps), :]
        no_ref[:, lanes] = dec_ref[head, pl.ds(0, nb, stride=steps), :] * n_ref[:, lanes] + n_sum

        k_t = kf.T
        wv = (wk * v.astype(F32)).astype(BF16)
        for b in range(nb):
            upd = _dot(jnp.where(tok_batch == b, k_t, 0.0).astype(BF16), wv)
            co_ref[b, head] = dec_ref[head, pl.ds(b * steps, 1), :] * c_ref[b, head] + upd


def _mlstm_sample(zm, gates, gain, c, n2, m_rows, steps):
    rows = zm.shape[0]
    nb = SAMPLE_ROWS // steps

    def tokens(width):
        return pl.BlockSpec((SAMPLE_ROWS, width), lambda g: (g, 0))

    c_spec = pl.BlockSpec((nb, M_HEADS, M_DK, M_DV), lambda g: (g, 0, 0, 0))
    n_spec = pl.BlockSpec((nb, M_HEADS * M_DK), lambda g: (g, 0))
    m_spec = pl.BlockSpec((M_HEADS, SAMPLE_ROWS, 1), lambda g: (0, g, 0))
    per_head = pltpu.VMEM((M_HEADS, SAMPLE_ROWS, M_DK), F32)
    return pl.pallas_call(
        functools.partial(_mlstm_sample_kernel, steps=steps),
        grid=(rows // SAMPLE_ROWS,),
        in_specs=[tokens(ZM_WIDTH), tokens(LANES), pl.BlockSpec((1, M_WIDTH), lambda g: (0, 0)),
                  c_spec, n_spec, m_spec],
        out_specs=[tokens(M_WIDTH), c_spec, n_spec, m_spec],
        out_shape=[jax.ShapeDtypeStruct((rows, M_WIDTH), BF16),
                   jax.ShapeDtypeStruct(c.shape, F32),
                   jax.ShapeDtypeStruct(n2.shape, F32),
                   jax.ShapeDtypeStruct(m_rows.shape, F32)],
        scratch_shapes=[per_head,
                        per_head,
                        per_head,
                        per_head,
                        per_head],
        compiler_params=pltpu.CompilerParams(dimension_semantics=("parallel",),
                                             vmem_limit_bytes=VMEM_LIMIT),
        name="mlstm_sample",
    )(zm, gates, gain, c, n2, m_rows)


def _swa_sample_kernel(sinks_ref, qa_ref, kva_ref, ckt_ref, cvt_ref, ua_ref, nkt_ref, nvt_ref,
                       s_ref, e_ref, pv_ref, *, steps):
    rows = SAMPLE_ROWS
    nb = rows // steps
    pair_rows = 2 * steps
    hw = A_KV_HEADS * A_HD
    tok = lax.broadcasted_iota(jnp.int32, (rows, LANES), 0)
    lane = lax.broadcasted_iota(jnp.int32, (rows, LANES), 1)
    low = lane < A_HD

    qf = qa_ref[...].astype(F32)
    q_blocks = []
    for p in range(A_GROUP):
        qp = qf[:, p * LANES:(p + 1) * LANES]
        q_blocks.append(jnp.where(low, qp, 0.0))
        q_blocks.append(jnp.where(low, 0.0, qp))
    n_blk = len(q_blocks)

    kv_new = kva_ref[...]
    kv_new_t = kv_new.T
    k_new_t = kv_new_t[:hw]
    v_new_t = kv_new_t[hw:]
    v_new = kv_new[:, hw:].astype(BF16)
    s_new = _dot(jnp.concatenate(q_blocks, axis=0).astype(BF16), k_new_t.astype(BF16))

    def pair_rows_of(blocks, j):
        return jnp.concatenate([blk[j * pair_rows:(j + 1) * pair_rows] for blk in blocks], axis=0)

    first = (lax.broadcasted_iota(jnp.int32, (n_blk * pair_rows, LANES), 0) % pair_rows) < steps

    for j in range(nb // 2):
        q_pair = pair_rows_of(q_blocks, j).astype(BF16)
        s_pair = jnp.where(first, _dot(q_pair, ckt_ref[2 * j].astype(BF16)),
                           _dot(q_pair, ckt_ref[2 * j + 1].astype(BF16)))
        for blk in range(n_blk):
            s_ref[pl.ds(blk * rows + j * pair_rows, pair_rows), :] = s_pair[blk * pair_rows:(blk + 1) * pair_rows]

    t_q = tok % steps
    dist_c = WINDOW + t_q - lane
    valid_c = dist_c < WINDOW
    dist_n = t_q - lane % steps
    valid_n = ((tok // steps) == (lane // steps)) & (dist_n >= 0)
    dist_cf = dist_c.astype(F32)
    dist_nf = dist_n.astype(F32)
    pv_new = []
    inv_den = []
    for blk in range(n_blk):
        head = (blk % A_KV_HEADS) * A_GROUP + blk // A_KV_HEADS
        slope = _alibi_slope(head)
        sink = sinks_ref[head]
        r = pl.ds(blk * rows, rows)
        s_c = jnp.where(valid_c, s_ref[r, :] - slope * dist_cf, -jnp.inf)
        s_n = jnp.where(valid_n, s_new[blk * rows:(blk + 1) * rows] - slope * dist_nf, -jnp.inf)
        mx = jnp.maximum(jnp.maximum(jnp.max(s_c, axis=-1, keepdims=True),
                                     jnp.max(s_n, axis=-1, keepdims=True)), sink)
        e_c = jnp.exp(s_c - mx)
        e_n = jnp.exp(s_n - mx)
        den = jnp.sum(e_c, axis=-1, keepdims=True) + jnp.sum(e_n, axis=-1, keepdims=True) + jnp.exp(sink - mx)
        e_ref[r, :] = e_c
        pv_new.append(_dot(e_n.astype(BF16), v_new))
        inv_den.append(1.0 / den)

    for j in range(nb // 2):
        e_pair = jnp.concatenate([e_ref[pl.ds(blk * rows + j * pair_rows, pair_rows), :] for blk in range(n_blk)],
                                 axis=0).astype(BF16)
        pv_pair = jnp.where(first, _dot_nt(e_pair, cvt_ref[2 * j].astype(BF16)),
                            _dot_nt(e_pair, cvt_ref[2 * j + 1].astype(BF16)))
        for blk in range(n_blk):
            pv_ref[pl.ds(blk * rows + j * pair_rows, pair_rows), :] = pv_pair[blk * pair_rows:(blk + 1) * pair_rows]

    for p in range(A_GROUP):
        halves = []
        for kvh in range(A_KV_HEADS):
            blk = p * A_KV_HEADS + kvh
            halves.append((pv_ref[pl.ds(blk * rows, rows), :] + pv_new[blk]) * inv_den[blk])
        ua_ref[:, p * LANES:(p + 1) * LANES] = jnp.where(low, halves[0], halves[1]).astype(BF16)

    tail = lane >= WINDOW - steps
    for b in range(nb):
        shift = (WINDOW - steps - b * steps) % rows
        nkt_ref[b] = jnp.where(tail, pltpu.roll(k_new_t, shift, 1), pltpu.roll(ckt_ref[b], WINDOW - steps, 1))
        nvt_ref[b] = jnp.where(tail, pltpu.roll(v_new_t, shift, 1), pltpu.roll(cvt_ref[b], WINDOW - steps, 1))


def _swa_sample(sinks, qa, kva, ckt, cvt, steps):
    batch = ckt.shape[0]
    assert ckt.shape[1:] == (A_KV_HEADS * A_HD, WINDOW) and SAMPLE_ROWS == WINDOW == LANES
    nb = SAMPLE_ROWS // steps
    cache = pl.BlockSpec((nb, LANES, WINDOW), lambda g: (g, 0, 0))

    def rows(width):
        return pl.BlockSpec((SAMPLE_ROWS, width), lambda g: (g, 0))

    stacked = (A_HEADS * SAMPLE_ROWS, LANES)
    return pl.pallas_call(
        functools.partial(_swa_sample_kernel, steps=steps),
        grid=(batch // nb,),
        in_specs=[pl.BlockSpec(memory_space=pltpu.SMEM), rows(A_WIDTH), rows(KV_WIDTH), cache, cache],
        out_specs=[rows(A_WIDTH), cache, cache],
        out_shape=[jax.ShapeDtypeStruct((batch * steps, A_WIDTH), BF16),
                   jax.ShapeDtypeStruct(ckt.shape, F32),
                   jax.ShapeDtypeStruct(cvt.shape, F32)],
        scratch_shapes=[pltpu.VMEM(stacked, F32), pltpu.VMEM(stacked, F32), pltpu.VMEM(stacked, F32)],
        compiler_params=pltpu.CompilerParams(dimension_semantics=("parallel",),
                                             vmem_limit_bytes=VMEM_LIMIT),
        name="swa_sample",
    )(sinks, qa, kva, ckt, cvt)


def _prep_proj(w_in, b_gate):
    m_end = 4 * M_WIDTH
    g_end = m_end + 2 * M_HEADS
    wt = w_in.T
    qa = wt[g_end:g_end + A_WIDTH].reshape(A_KV_HEADS, A_GROUP, A_HD, D_MODEL).transpose(1, 0, 2, 3)
    gates = jnp.pad(wt[m_end:g_end], ((0, LANES - 2 * M_HEADS), (0, 0)))
    win = jnp.concatenate([wt[:m_end], qa.reshape(A_WIDTH, D_MODEL), wt[g_end + A_WIDTH:], gates],
                          axis=0).astype(BF16)
    bias = jnp.pad(b_gate, (0, LANES - 2 * M_HEADS)).reshape(1, LANES)
    return win, bias


def _prep_out(w_out):
    wa = w_out[M_WIDTH:].reshape(A_KV_HEADS, A_GROUP, A_HD, D_MODEL).transpose(1, 0, 2, 3)
    return jnp.concatenate([w_out[:M_WIDTH], wa.reshape(A_WIDTH, D_MODEL)], axis=0).astype(BF16)


def kernel(x_prompt, x_sample, cache_swa_k, cache_swa_v, state_mlstm_C, state_mlstm_n, state_mlstm_m,
           norm_gains, ffn_w_gate, ffn_w_up, ffn_w_down, w_in, b_gate, mlstm_norm_gain, attn_sinks, w_out):
    assert norm_gains.shape[0] == 1, "single layer"
    batch, seq, _ = x_prompt.shape
    dec_batch, steps, _ = x_sample.shape
    gains = norm_gains[0]
    ffn = (ffn_w_gate, ffn_w_up, ffn_w_down)
    win, bias = _prep_proj(w_in[0], b_gate[0])
    wo = _prep_out(w_out[0])
    gain = mlstm_norm_gain[0].reshape(1, M_WIDTH)
    sinks = attn_sinks[0]
    kv_lanes = A_KV_HEADS * A_HD

    chunk = min(MLSTM_CHUNK, seq)
    assert seq % chunk == 0
    xp, zm, qa, kva, gates = _ffn_proj(x_prompt.reshape(batch * seq, D_MODEL), gains, *ffn, win, bias,
                                       0, 0, 1, 2, chunk)
    um, ua, p_c, p_n, p_m = _prompt_mixers(sinks, zm, gates, gain, qa, kva, batch, seq, chunk)
    yp = _mix_ffn(xp, um, ua, wo, gains, *ffn, 1, 3, 4, 5).reshape(batch, seq, D_MODEL)
    keep = min(WINDOW, seq)
    kv_tail = kva.reshape(batch, seq, KV_WIDTH)[:, seq - keep:]
    p_k = kv_tail[..., :kv_lanes].reshape(1, batch, keep, A_KV_HEADS, A_HD)
    p_v = kv_tail[..., kv_lanes:].reshape(1, batch, keep, A_KV_HEADS, A_HD)
    p_c = p_c[None]
    p_n = p_n[None, :, :, 0, :]
    p_m = p_m[None, :, :, 0, 0]

    xs, zm_s, qa_s, kva_s, gates_s = _ffn_proj(x_sample.reshape(dec_batch * steps, D_MODEL), gains, *ffn,
                                               win, bias, 0, 0, 1, 2, min(chunk, dec_batch * steps))
    m_rows = jnp.repeat(state_mlstm_m[0].T, steps, axis=1)[..., None]
    um_s, s_c, s_n2, s_m_rows = _mlstm_sample(zm_s, gates_s, gain, state_mlstm_C[0],
                                              state_mlstm_n[0].reshape(dec_batch, M_HEADS * M_DK),
                                              m_rows, steps)
    wc = cache_swa_k.shape[2]
    ua_s, s_kt, s_vt = _swa_sample(sinks, qa_s, kva_s,
                                   cache_swa_k[0].reshape(dec_batch, wc, kv_lanes).transpose(0, 2, 1),
                                   cache_swa_v[0].reshape(dec_batch, wc, kv_lanes).transpose(0, 2, 1), steps)
    ys = _mix_ffn(xs, um_s, ua_s, wo, gains, *ffn, 1, 3, 4, 5).reshape(dec_batch, steps, D_MODEL)
    s_n = s_n2.reshape(1, dec_batch, M_HEADS, M_DK)
    s_m = s_m_rows[:, ::steps, 0].T[None]
    s_k = s_kt.transpose(0, 2, 1).reshape(1, dec_batch, wc, A_KV_HEADS, A_HD)
    s_v = s_vt.transpose(0, 2, 1).reshape(1, dec_batch, wc, A_KV_HEADS, A_HD)
    return (yp, ys, p_k, p_v, p_c, p_n, p_m, s_k, s_v, s_c[None], s_n, s_m)
```

```python
import functools

import jax
import jax.numpy as jnp
from jax import lax
from jax.experimental import pallas as pl
from jax.experimental.pallas import tpu as pltpu

F32 = jnp.float32
BF16 = jnp.bfloat16

D_MODEL = 1024
D_FF = 2816
M_HEADS = 4
M_DK = 128
M_DV = 128
M_WIDTH = M_HEADS * M_DV
A_HEADS = 8
A_KV_HEADS = 2
A_GROUP = A_HEADS // A_KV_HEADS
A_HD = 64
A_WIDTH = A_HEADS * A_HD
WINDOW = 128
RMS_EPS = 1e-6

LANES = 128
FFN_CHUNK = 256
N_FFN_CHUNKS = D_FF // FFN_CHUNK
FFN_STAGE_SLOTS = 4
TOKEN_TILE = 512
MLSTM_CHUNK = 256
ZM_WIDTH = 4 * M_WIDTH
KV_WIDTH = 2 * A_KV_HEADS * A_HD
VMEM_LIMIT = 56 * 1024 * 1024
MLSTM_CHUNKS_PER_STEP = 8
GATE_I, GATE_F, GATE_B, GATE_X = 0, M_HEADS, 2 * M_HEADS, 3 * M_HEADS

NT_DIMS = (((1,), (1,)), ((), ()))


def _rms(x, g):
    ms = jnp.mean(x * x, axis=-1, keepdims=True)
    return x * lax.rsqrt(ms + RMS_EPS) * g


def _log_sigmoid(x):
    return jnp.minimum(x, 0.0) - jnp.log1p(jnp.exp(-jnp.abs(x)))


def _dot(a, b):
    return jnp.dot(a, b, preferred_element_type=F32)


def _dot_nt(a, b):
    return lax.dot_general(a, b, NT_DIMS, preferred_element_type=F32)


def _resident(shape):
    nd = len(shape)
    return pl.BlockSpec(shape, lambda *_: (0,) * nd, pipeline_mode=pl.Buffered(1))


def _ffn_weight_copies(w_hbm, j, f, slot, stage_refs, sems):
    wg_hbm, wu_hbm, wd_hbm = w_hbm
    cols = pl.ds(f * FFN_CHUNK, FFN_CHUNK)
    sources = (wg_hbm.at[0, j, :, cols], wu_hbm.at[0, j, :, cols], wd_hbm.at[0, j, cols, :])
    return [pltpu.make_async_copy(src, stage.at[slot], sems.at[slot, k])
            for k, (src, stage) in enumerate(zip(sources, stage_refs))]


def _swiglu(x, gpre, w_refs, h_ref, acc_ref, fetch=None):
    wg_ref, wu_ref, wd_ref = w_refs
    if fetch is not None:
        for f in range(min(FFN_STAGE_SLOTS, N_FFN_CHUNKS)):
            for copy in _ffn_weight_copies(*fetch[:2], f, f % FFN_STAGE_SLOTS, *fetch[2:]):
                copy.start()
    h_ref[...] = _rms(x, gpre).astype(BF16)
    for f in range(N_FFN_CHUNKS):
        cols = slice(f * FFN_CHUNK, (f + 1) * FFN_CHUNK)
        if fetch is not None:
            slot = f % FFN_STAGE_SLOTS
            stage_g, stage_u, stage_d = fetch[2]
            for copy in _ffn_weight_copies(*fetch[:2], f, slot, *fetch[2:]):
                copy.wait()
            wg_ref[:, cols] = stage_g[slot].astype(BF16)
            wu_ref[:, cols] = stage_u[slot].astype(BF16)
            wd_ref[cols, :] = stage_d[slot].astype(BF16)
            if f + FFN_STAGE_SLOTS < N_FFN_CHUNKS:
                for copy in _ffn_weight_copies(*fetch[:2], f + FFN_STAGE_SLOTS, slot, *fetch[2:]):
                    copy.start()
        h = h_ref[...]
        g = _dot(h, wg_ref[:, cols])
        u = _dot(h, wu_ref[:, cols])
        part = _dot((g * jax.nn.sigmoid(g) * u).astype(BF16), wd_ref[cols, :])
        if f == 0:
            acc_ref[...] = part
        else:
            acc_ref[...] += part


def _first_step_fetches(body, fetch):
    first = pl.program_id(0) == 0

    @pl.when(first)
    def _():
        body(fetch)

    @pl.when(jnp.logical_not(first))
    def _():
        body(None)


def _half_step(x, acc_ref, gpost):
    return x + _rms(acc_ref[...], 0.5 * gpost)


def _project(x, gain, win_ref, bias_ref, h_ref, zm_ref, qa_ref, kva_ref, gates_ref, chunk):
    h_ref[...] = _rms(x, gain).astype(BF16)

    def cols(lo, width):
        return _dot_nt(h_ref[...], win_ref[lo:lo + width, :])

    gates = cols(ZM_WIDTH + A_WIDTH + KV_WIDTH, LANES) + bias_ref[...]
    lane = lax.broadcasted_iota(jnp.int32, (chunk, LANES), 1)
    pos = lax.broadcasted_iota(jnp.int32, (chunk, LANES), 0)
    for c0 in range(0, gates.shape[0], chunk):
        g = gates[c0:c0 + chunk]
        b = jnp.where((lane >= GATE_F) & (lane < GATE_B), _log_sigmoid(g), 0.0)
        shift = 1
        while shift < chunk:
            b = b + jnp.where(pos >= shift, pltpu.roll(b, shift, 0), 0.0)
            shift *= 2
        x_gate = pltpu.roll(g, GATE_X - GATE_I, 1) - pltpu.roll(b, GATE_X - GATE_F, 1)
        out = jnp.where(lane < GATE_B, g, jnp.where(lane < GATE_X, pltpu.roll(b, GATE_B - GATE_F, 1), x_gate))
        gates_ref[c0:c0 + chunk, :] = jnp.where(lane < GATE_X + M_HEADS, out, 0.0)

    zm_ref[:, 0:M_WIDTH] = cols(0, M_WIDTH).astype(BF16)
    zm_ref[:, M_WIDTH:2 * M_WIDTH] = (cols(M_WIDTH, M_WIDTH) * (M_DK ** -0.5)).astype(BF16)
    zm_ref[:, 2 * M_WIDTH:3 * M_WIDTH] = cols(2 * M_WIDTH, M_WIDTH).astype(BF16)
    zm_ref[:, 3 * M_WIDTH:] = jax.nn.sigmoid(cols(3 * M_WIDTH, M_WIDTH)).astype(BF16)
    qa_ref[...] = (cols(ZM_WIDTH, A_WIDTH) * (A_HD ** -0.5)).astype(BF16)
    kva_ref[...] = cols(ZM_WIDTH + A_WIDTH, KV_WIDTH)


def _ffn_proj_kernel(x_ref, gains_ref, wg_hbm, wu_hbm, wd_hbm, win_ref, bias_ref,
                     x1_ref, zm_ref, qa_ref, kva_ref, gates_ref,
                     h_ref, acc_ref, wg_ref, wu_ref, wd_ref, stage_g, stage_u, stage_d, sems,
                     *, j, pre, post, mixer, chunk):
    def body(fetch):
        x = x_ref[...]
        _swiglu(x, gains_ref[pre:pre + 1, :], (wg_ref, wu_ref, wd_ref), h_ref, acc_ref, fetch)
        x1 = _half_step(x, acc_ref, gains_ref[post:post + 1, :])
        x1_ref[...] = x1
        _project(x1, gains_ref[mixer:mixer + 1, :], win_ref, bias_ref, h_ref, zm_ref, qa_ref, kva_ref,
                 gates_ref, chunk)

    _first_step_fetches(body, ((wg_hbm, wu_hbm, wd_hbm), j, (stage_g, stage_u, stage_d), sems))


def _mix_ffn_kernel(x_ref, um_ref, ua_ref, wo_ref, gains_ref, wg_hbm, wu_hbm, wd_hbm, o_ref,
                    h_ref, acc_ref, wg_ref, wu_ref, wd_ref, stage_g, stage_u, stage_d, sems,
                    *, j, mix, pre, post):
    def body(fetch):
        u = jnp.concatenate([um_ref[...], ua_ref[...]], axis=-1)
        x = x_ref[...] + _rms(_dot(u, wo_ref[...]), gains_ref[mix:mix + 1, :])
        _swiglu(x, gains_ref[pre:pre + 1, :], (wg_ref, wu_ref, wd_ref), h_ref, acc_ref, fetch)
        o_ref[...] = _half_step(x, acc_ref, gains_ref[post:post + 1, :])

    _first_step_fetches(body, ((wg_hbm, wu_hbm, wd_hbm), j, (stage_g, stage_u, stage_d), sems))


def _ffn_scratch(tm):
    return [pltpu.VMEM((tm, D_MODEL), BF16),
            pltpu.VMEM((tm, D_MODEL), F32),
            pltpu.VMEM((D_MODEL, D_FF), BF16),
            pltpu.VMEM((D_MODEL, D_FF), BF16),
            pltpu.VMEM((D_FF, D_MODEL), BF16),
            pltpu.VMEM((FFN_STAGE_SLOTS, D_MODEL, FFN_CHUNK), F32),
            pltpu.VMEM((FFN_STAGE_SLOTS, D_MODEL, FFN_CHUNK), F32),
            pltpu.VMEM((FFN_STAGE_SLOTS, FFN_CHUNK, D_MODEL), F32),
            pltpu.SemaphoreType.DMA((FFN_STAGE_SLOTS, 3))]


def _dense_params():
    return pltpu.CompilerParams(dimension_semantics=("arbitrary",), vmem_limit_bytes=VMEM_LIMIT)


_HBM = pl.BlockSpec(memory_space=pl.ANY)


def _ffn_proj(x, gains, wg, wu, wd, win, bias, j, pre, post, mixer, chunk):
    n = x.shape[0]
    tm = min(TOKEN_TILE, n)
    assert tm % chunk == 0

    def rows(width):
        return pl.BlockSpec((tm, width), lambda i: (i, 0))

    return pl.pallas_call(
        functools.partial(_ffn_proj_kernel, j=j, pre=pre, post=post, mixer=mixer, chunk=chunk),
        grid=(n // tm,),
        in_specs=[rows(D_MODEL), _resident(gains.shape), _HBM, _HBM, _HBM,
                  _resident(win.shape), _resident(bias.shape)],
        out_specs=[rows(D_MODEL), rows(ZM_WIDTH), rows(A_WIDTH), rows(KV_WIDTH), rows(LANES)],
        out_shape=[jax.ShapeDtypeStruct((n, D_MODEL), F32),
                   jax.ShapeDtypeStruct((n, ZM_WIDTH), BF16),
                   jax.ShapeDtypeStruct((n, A_WIDTH), BF16),
                   jax.ShapeDtypeStruct((n, KV_WIDTH), F32),
                   jax.ShapeDtypeStruct((n, LANES), F32)],
        scratch_shapes=_ffn_scratch(tm),
        compiler_params=_dense_params(),
        name="ffn_half_step_input_projection",
    )(x, gains, wg, wu, wd, win, bias)


def _mix_ffn(x, um, ua, wo, gains, wg, wu, wd, j, mix, pre, post):
    n = x.shape[0]
    tm = min(TOKEN_TILE, n)
    row = pl.BlockSpec((tm, D_MODEL), lambda i: (i, 0))
    half = pl.BlockSpec((tm, M_WIDTH), lambda i: (i, 0))
    return pl.pallas_call(
        functools.partial(_mix_ffn_kernel, j=j, mix=mix, pre=pre, post=post),
        grid=(n // tm,),
        in_specs=[row, half, half, _resident(wo.shape), _resident(gains.shape), _HBM, _HBM, _HBM],
        out_specs=row,
        out_shape=jax.ShapeDtypeStruct((n, D_MODEL), F32),
        scratch_shapes=_ffn_scratch(tm),
        compiler_params=_dense_params(),
        name="out_proj_ffn_half_step",
    )(x, um, ua, wo, gains, wg, wu, wd)


def _head_norm_gate(hh, o, gain):
    hn = hh * lax.rsqrt(jnp.mean(hh * hh, axis=-1, keepdims=True) + RMS_EPS) * gain
    return (o.astype(F32) * hn).astype(BF16)


def _mlstm_chunks(zm_ref, gates_ref, gain_ref, um_ref, cnt_ref, ms_ref, chunk):
    si = lax.broadcasted_iota(jnp.int32, (chunk, chunk), 0)
    ti = lax.broadcasted_iota(jnp.int32, (chunk, chunk), 1)
    causal = si <= ti
    ones_rows = jnp.where(lax.broadcasted_iota(jnp.int32, (LANES, chunk), 0) == 0, 1.0, 0.0).astype(BF16)

    def run_chunk(c):
        rows = slice(c * chunk, (c + 1) * chunk)
        gates = gates_ref[rows, :]
        gates_t = gates.T
        for h in range(M_HEADS):
            lanes = slice(h * M_DK, (h + 1) * M_DK)
            q = zm_ref[rows, lanes]
            k = zm_ref[rows, M_WIDTH + h * M_DK:M_WIDTH + (h + 1) * M_DK]
            v = zm_ref[rows, 2 * M_WIDTH + h * M_DV:2 * M_WIDTH + (h + 1) * M_DV]
            o = zm_ref[rows, 3 * M_WIDTH + h * M_DV:3 * M_WIDTH + (h + 1) * M_DV]
            x_col = gates[:, GATE_X + h:GATE_X + h + 1]
            ig_row = gates_t[GATE_I + h:GATE_I + h + 1, :]
            b_row = gates_t[GATE_B + h:GATE_B + h + 1, :]
            m_prev = ms_ref[h][:, 0:1]
            cnt = cnt_ref[h]

            xm = jnp.where(causal, x_col, -jnp.inf)
            mu = jnp.maximum(m_prev, jnp.max(xm, axis=0, keepdims=True))
            a = jnp.exp(m_prev - mu)
            s_t = _dot_nt(k, q) * jnp.exp(xm - mu)
            v_ext = jnp.concatenate([v.astype(F32).T.astype(BF16), ones_rows], axis=0)
            nd = a * _dot_nt(cnt.astype(BF16), q) + _dot(v_ext, s_t.astype(BF16))
            den = nd[M_DV:M_DV + 1, :]
            hh = nd[:M_DV, :] * (1.0 / jnp.maximum(jnp.abs(den), jnp.exp(-(b_row + mu))))
            hn = hh * lax.rsqrt(jnp.mean(hh * hh, axis=0, keepdims=True) + RMS_EPS)
            um_ref[rows, lanes] = (hn.T * gain_ref[:, lanes] * o.astype(F32)).astype(BF16)

            b_last = b_row[:, chunk - 1:chunk]
            g = b_last - b_row + ig_row
            m_new = jnp.maximum(b_last + m_prev, jnp.max(g, axis=-1, keepdims=True))
            decay = jnp.exp(b_last + m_prev - m_new)
            wv = (jnp.exp(g - m_new) * v_ext.astype(F32)).astype(BF16)
            cnt_ref[h] = decay * cnt + _dot(wv, k)
            ms_ref[h] = jnp.broadcast_to(m_new, (1, LANES))

    return [functools.partial(run_chunk, c) for c in range(zm_ref.shape[0] // chunk)]


def _alibi_slope(head):
    return 2.0 ** (-8.0 * (head + 1) / A_HEADS)


def _swa_blocks(sinks_ref, qa_ref, kvc_ref, kvp_ref, ua_ref, mask_ref, first):
    hw = A_KV_HEADS * A_HD
    n_blocks = qa_ref.shape[0] // WINDOW
    qi = lax.broadcasted_iota(jnp.int32, (WINDOW, 2 * WINDOW), 0)
    kj = lax.broadcasted_iota(jnp.int32, (WINDOW, 2 * WINDOW), 1)
    dist = WINDOW + qi - kj
    band = (dist >= 0) & (dist < WINDOW)
    mask_ref[0] = jnp.where(band, 0.0, -jnp.inf)
    mask_ref[1] = jnp.where(band & (kj >= WINDOW), 0.0, -jnp.inf)
    lane = lax.broadcasted_iota(jnp.int32, (WINDOW, LANES), 1)
    low = lane < A_HD

    assert 2 * WINDOW <= 256, "positions must stay exactly representable in bf16"
    q_pos = (WINDOW + lax.broadcasted_iota(jnp.int32, (WINDOW, LANES), 0)).astype(F32)
    k_lane = lax.broadcasted_iota(jnp.int32, (2 * WINDOW, LANES), 1)
    k_pos = lax.broadcasted_iota(jnp.int32, (2 * WINDOW, LANES), 0).astype(F32)
    spare = (A_HD, 0)
    k_bias = [jnp.where(k_lane == spare[kvh], 1.0, jnp.where(k_lane == spare[kvh] + 1, k_pos, 0.0)).astype(BF16)
              for kvh in range(A_KV_HEADS)]
    q_bias = {}
    for head in range(A_HEADS):
        slope = _alibi_slope(head)
        at = spare[head // A_GROUP]
        q_bias[head] = jnp.where(lane == at, -slope * q_pos, jnp.where(lane == at + 1, slope, 0.0)).astype(BF16)

    def run_block(j):
        rows = slice(j * WINDOW, (j + 1) * WINDOW)
        kvc = kvc_ref[rows, :]
        kvp = kvp_ref[...] if j == 0 else kvc_ref[(j - 1) * WINDOW:j * WINDOW, :]
        mask = mask_ref[jnp.where(first, 1, 0)] if j == 0 else mask_ref[0]
        kk = jnp.concatenate([kvp[:, :hw], kvc[:, :hw]], axis=0).astype(BF16)
        kk_biased = [jnp.where(k_lane < A_HD, kk, k_bias[0]), jnp.where(k_lane < A_HD, k_bias[1], kk)]
        vv = jnp.concatenate([kvp[:, hw:], kvc[:, hw:]], axis=0).astype(BF16)
        for p in range(A_GROUP):
            qp = qa_ref[rows, p * LANES:(p + 1) * LANES]
            outs = []
            for kvh in range(A_KV_HEADS):
                head = kvh * A_GROUP + p
                keep = low if kvh == 0 else jnp.logical_not(low)
                s = _dot_nt(jnp.where(keep, qp, q_bias[head]), kk_biased[kvh])
                s = s + mask
                sink = sinks_ref[head]
                mx = jnp.maximum(jnp.max(s, axis=-1, keepdims=True), sink)
                e = jnp.exp(s - mx)
                den = jnp.sum(e, axis=-1, keepdims=True) + jnp.exp(sink - mx)
                outs.append(_dot(e.astype(BF16), vv) / den)
            ua_ref[rows, p * LANES:(p + 1) * LANES] = jnp.where(low, outs[0], outs[1]).astype(BF16)

    return [functools.partial(run_block, j) for j in range(n_blocks)]


def _prompt_mixers_kernel(sinks_ref, zm_ref, gates_ref, gain_ref, qa_ref, kvc_ref, kvp_ref,
                          um_ref, ua_ref, c_ref, n_ref, m_ref, cnt_ref, ms_ref, mask_ref, *, chunk):
    step = pl.program_id(1)

    @pl.when(step == 0)
    def _():
        cnt_ref[...] = jnp.zeros_like(cnt_ref)
        ms_ref[...] = jnp.zeros_like(ms_ref)

    chunks = _mlstm_chunks(zm_ref, gates_ref, gain_ref, um_ref, cnt_ref, ms_ref, chunk)
    blocks = _swa_blocks(sinks_ref, qa_ref, kvc_ref, kvp_ref, ua_ref, mask_ref, step == 0)
    per_chunk = len(blocks) // len(chunks)
    for c, run_chunk in enumerate(chunks):
        run_chunk()
        for run_block in blocks[c * per_chunk:(c + 1) * per_chunk]:
            run_block()

    @pl.when(step == pl.num_programs(1) - 1)
    def _():
        for h in range(M_HEADS):
            c_ref[0, h] = cnt_ref[h][:M_DV, :].T
            n_ref[0, h] = cnt_ref[h][M_DV:M_DV + 1, :]
            m_ref[0, h] = ms_ref[h]


def _prompt_mixers(sinks, zm, gates, gain, qa, kva, batch, seq, chunk):
    n = batch * seq
    per_step = min(MLSTM_CHUNKS_PER_STEP, seq // chunk)
    rows = per_step * chunk
    assert seq % rows == 0 and chunk % WINDOW == 0
    ns = seq // rows

    def step_rows(width):
        return pl.BlockSpec((rows, width), lambda b, i: (b * ns + i, 0))

    def prev_block(b, i):
        return (jnp.maximum((b * ns + i) * (rows // WINDOW) - 1, 0), 0)

    def state(*dims):
        return pl.BlockSpec((1, M_HEADS) + dims, lambda b, i: (b, 0, 0, 0))

    return pl.pallas_call(
        functools.partial(_prompt_mixers_kernel, chunk=chunk),
        grid=(batch, ns),
        in_specs=[pl.BlockSpec(memory_space=pltpu.SMEM), step_rows(ZM_WIDTH), step_rows(LANES),
                  pl.BlockSpec((1, M_WIDTH), lambda b, i: (0, 0)),
                  step_rows(A_WIDTH), step_rows(KV_WIDTH), pl.BlockSpec((WINDOW, KV_WIDTH), prev_block)],
        out_specs=[step_rows(M_WIDTH), step_rows(A_WIDTH), state(M_DK, M_DV), state(1, M_DK), state(1, LANES)],
        out_shape=[jax.ShapeDtypeStruct((n, M_WIDTH), BF16),
                   jax.ShapeDtypeStruct((n, A_WIDTH), BF16),
                   jax.ShapeDtypeStruct((batch, M_HEADS, M_DK, M_DV), F32),
                   jax.ShapeDtypeStruct((batch, M_HEADS, 1, M_DK), F32),
                   jax.ShapeDtypeStruct((batch, M_HEADS, 1, LANES), F32)],
        scratch_shapes=[pltpu.VMEM((M_HEADS, 2 * LANES, M_DK), F32), pltpu.VMEM((M_HEADS, 1, LANES), F32),
                        pltpu.VMEM((2, WINDOW, 2 * WINDOW), F32)],
        compiler_params=pltpu.CompilerParams(dimension_semantics=("parallel", "arbitrary"),
                                             vmem_limit_bytes=VMEM_LIMIT),
        name="prompt_mixers",
    )(sinks, zm, gates, gain, qa, kva, kva)


SAMPLE_ROWS = 128


def _mlstm_sample_kernel(zm_ref, gates_ref, gain_ref, c_ref, n_ref, m_ref,
                         um_ref, co_ref, no_ref, mo_ref,
                         qf_ref, qc_ref, nexp_ref, wkk_ref, dec_ref, *, steps):
    rows = SAMPLE_ROWS
    nb = rows // steps
    row = lax.broadcasted_iota(jnp.int32, (rows, rows), 0)
    col = lax.broadcasted_iota(jnp.int32, (rows, rows), 1)
    same = (row // steps) == (col // steps)
    causal = same & (col <= row)
    eye = row == col
    last = same & ((col % steps) == steps - 1)
    sub = lax.broadcasted_iota(jnp.int32, (2 * steps, M_DV), 0)
    tok_batch = lax.broadcasted_iota(jnp.int32, (M_DK, rows), 1) // steps

    def to_row(x_col):
        return jnp.sum(jnp.where(eye, x_col, 0.0), axis=0, keepdims=True)

    def pick(mask, x_row, fill, reduce):
        return reduce(jnp.where(mask, x_row, fill), axis=-1, keepdims=True)

    gates = gates_ref[...]
    for head in range(M_HEADS):
        lanes = slice(head * M_DK, (head + 1) * M_DK)
        ig_col = gates[:, GATE_I + head:GATE_I + head + 1]
        fg_col = gates[:, GATE_F + head:GATE_F + head + 1]
        lf_row = to_row(_log_sigmoid(fg_col))
        b_col = pick(causal, lf_row, 0.0, jnp.sum)
        b_row = to_row(b_col)
        ig_row = to_row(ig_col)
        m_prev = m_ref[head]

        d = jnp.where(causal, b_col - b_row + ig_row, -jnp.inf)
        inter = b_col + m_prev
        m_t = jnp.maximum(inter, jnp.max(d, axis=-1, keepdims=True))
        w = jnp.exp(d - m_t)
        a = jnp.exp(inter - m_t)

        q = zm_ref[:, lanes]
        k = zm_ref[:, M_WIDTH + head * M_DK:M_WIDTH + (head + 1) * M_DK]
        v = zm_ref[:, 2 * M_WIDTH + head * M_DV:2 * M_WIDTH + (head + 1) * M_DV]
        o = zm_ref[:, 3 * M_WIDTH + head * M_DV:3 * M_WIDTH + (head + 1) * M_DV]
        qf_ref[head] = q.astype(F32)
        for t in range(steps):
            nexp_ref[head, pl.ds(t, nb, stride=steps), :] = n_ref[:, lanes]

        for j in range(nb // 2):
            pair = pl.ds(j * 2 * steps, 2 * steps)
            q8 = qf_ref[head, pair, :].astype(BF16)
            r0 = _dot(q8, c_ref[2 * j, head].astype(BF16))
            r1 = _dot(q8, c_ref[2 * j + 1, head].astype(BF16))
            qc_ref[head, pair, :] = jnp.where(sub < steps, r0, r1)

        s = _dot_nt(q, k) * w
        qn = jnp.sum(qf_ref[head] * nexp_ref[head], axis=-1, keepdims=True)
        num = a * qc_ref[head] + _dot(s.astype(BF16), v)
        den = a * qn + jnp.sum(s, axis=-1, keepdims=True)
        hh = num / jnp.maximum(jnp.abs(den), jnp.exp(-m_t))
        um_ref[:, lanes] = _head_norm_gate(hh, o, gain_ref[:, lanes])

        b_last = pick(last, b_row, 0.0, jnp.sum)
        g_col = b_last - b_col + ig_col
        g_max = pick(same, to_row(g_col), -jnp.inf, jnp.max)
        m_new = jnp.maximum(b_last + m_prev, g_max)
        decay = jnp.exp(b_last + m_prev - m_new)
        wk = jnp.exp(g_col - m_new)
        mo_ref[head] = m_new

        kf = k.astype(F32)
        wkk_ref[head] = wk * kf
        dec_ref[head] = jnp.broadcast_to(decay, (rows, M_DV))
        n_sum = wkk_ref[head, pl.ds(0, nb, stride=steps), :]
        for t in range(1, steps):
            n_sum = n_sum + wkk_ref[head, pl.ds(t, nb, stride=steps), :]
        no_ref[:, lanes] = dec_ref[head, pl.ds(0, nb, stride=steps), :] * n_ref[:, lanes] + n_sum

        k_t = kf.T
        wv = (wk * v.astype(F32)).astype(BF16)
        for b in range(nb):
            upd = _dot(jnp.where(tok_batch == b, k_t, 0.0).astype(BF16), wv)
            co_ref[b, head] = dec_ref[head, pl.ds(b * steps, 1), :] * c_ref[b, head] + upd


def _swa_sample_kernel(sinks_ref, qa_ref, kva_ref, ckt_ref, cvt_ref, ua_ref, nkt_ref, nvt_ref,
                       s_ref, e_ref, pv_ref, *, steps):
    rows = SAMPLE_ROWS
    nb = rows // steps
    pair_rows = 2 * steps
    hw = A_KV_HEADS * A_HD
    tok = lax.broadcasted_iota(jnp.int32, (rows, LANES), 0)
    lane = lax.broadcasted_iota(jnp.int32, (rows, LANES), 1)
    low = lane < A_HD

    qf = qa_ref[...].astype(F32)
    q_blocks = []
    for p in range(A_GROUP):
        qp = qf[:, p * LANES:(p + 1) * LANES]
        q_blocks.append(jnp.where(low, qp, 0.0))
        q_blocks.append(jnp.where(low, 0.0, qp))
    n_blk = len(q_blocks)

    kv_new = kva_ref[...]
    kv_new_t = kv_new.T
    k_new_t = kv_new_t[:hw]
    v_new_t = kv_new_t[hw:]
    v_new = kv_new[:, hw:].astype(BF16)
    s_new = _dot(jnp.concatenate(q_blocks, axis=0).astype(BF16), k_new_t.astype(BF16))

    def pair_rows_of(blocks, j):
        return jnp.concatenate([blk[j * pair_rows:(j + 1) * pair_rows] for blk in blocks], axis=0)

    first = (lax.broadcasted_iota(jnp.int32, (n_blk * pair_rows, LANES), 0) % pair_rows) < steps

    for j in range(nb // 2):
        q_pair = pair_rows_of(q_blocks, j).astype(BF16)
        s_pair = jnp.where(first, _dot(q_pair, ckt_ref[2 * j].astype(BF16)),
                           _dot(q_pair, ckt_ref[2 * j + 1].astype(BF16)))
        for blk in range(n_blk):
            s_ref[pl.ds(blk * rows + j * pair_rows, pair_rows), :] = s_pair[blk * pair_rows:(blk + 1) * pair_rows]

    t_q = tok % steps
    dist_c = WINDOW + t_q - lane
    valid_c = dist_c < WINDOW
    dist_n = t_q - lane % steps
    valid_n = ((tok // steps) == (lane // steps)) & (dist_n >= 0)
    dist_cf = dist_c.astype(F32)
    dist_nf = dist_n.astype(F32)
    pv_new = []
    inv_den = []
    for blk in range(n_blk):
        head = (blk % A_KV_HEADS) * A_GROUP + blk // A_KV_HEADS
        slope = _alibi_slope(head)
        sink = sinks_ref[head]
        r = pl.ds(blk * rows, rows)
        s_c = jnp.where(valid_c, s_ref[r, :] - slope * dist_cf, -jnp.inf)
        s_n = jnp.where(valid_n, s_new[blk * rows:(blk + 1) * rows] - slope * dist_nf, -jnp.inf)
        mx = jnp.maximum(jnp.maximum(jnp.max(s_c, axis=-1, keepdims=True),
                                     jnp.max(s_n, axis=-1, keepdims=True)), sink)
        e_c = jnp.exp(s_c - mx)
        e_n = jnp.exp(s_n - mx)
        den = jnp.sum(e_c, axis=-1, keepdims=True) + jnp.sum(e_n, axis=-1, keepdims=True) + jnp.exp(sink - mx)
        e_ref[r, :] = e_c
        pv_new.append(_dot(e_n.astype(BF16), v_new))
        inv_den.append(1.0 / den)

    for j in range(nb // 2):
        e_pair = jnp.concatenate([e_ref[pl.ds(blk * rows + j * pair_rows, pair_rows), :] for blk in range(n_blk)],
                                 axis=0).astype(BF16)
        pv_pair = jnp.where(first, _dot_nt(e_pair, cvt_ref[2 * j].astype(BF16)),
                            _dot_nt(e_pair, cvt_ref[2 * j + 1].astype(BF16)))
        for blk in range(n_blk):
            pv_ref[pl.ds(blk * rows + j * pair_rows, pair_rows), :] = pv_pair[blk * pair_rows:(blk + 1) * pair_rows]

    for p in range(A_GROUP):
        halves = []
        for kvh in range(A_KV_HEADS):
            blk = p * A_KV_HEADS + kvh
            halves.append((pv_ref[pl.ds(blk * rows, rows), :] + pv_new[blk]) * inv_den[blk])
        ua_ref[:, p * LANES:(p + 1) * LANES] = jnp.where(low, halves[0], halves[1]).astype(BF16)

    tail = lane >= WINDOW - steps
    for b in range(nb):
        shift = (WINDOW - steps - b * steps) % rows
        nkt_ref[b] = jnp.where(tail, pltpu.roll(k_new_t, shift, 1), pltpu.roll(ckt_ref[b], WINDOW - steps, 1))
        nvt_ref[b] = jnp.where(tail, pltpu.roll(v_new_t, shift, 1), pltpu.roll(cvt_ref[b], WINDOW - steps, 1))


def _sample_mixers_kernel(sinks_ref, zm_ref, gates_ref, gain_ref, c_ref, n_ref, m_ref, qa_ref, kva_ref,
                          ckt_ref, cvt_ref, um_ref, co_ref, no_ref, mo_ref, ua_ref, nkt_ref, nvt_ref,
                          qf_ref, qc_ref, nexp_ref, wkk_ref, dec_ref, s_ref, e_ref, pv_ref, *, steps):
    _mlstm_sample_kernel(zm_ref, gates_ref, gain_ref, c_ref, n_ref, m_ref, um_ref, co_ref, no_ref, mo_ref,
                         qf_ref, qc_ref, nexp_ref, wkk_ref, dec_ref, steps=steps)
    _swa_sample_kernel(sinks_ref, qa_ref, kva_ref, ckt_ref, cvt_ref, ua_ref, nkt_ref, nvt_ref,
                       s_ref, e_ref, pv_ref, steps=steps)


def _sample_mixers(sinks, zm, gates, gain, c, n2, m_rows, qa, kva, ckt, cvt, steps):
    rows = zm.shape[0]
    assert ckt.shape[1:] == (A_KV_HEADS * A_HD, WINDOW) and SAMPLE_ROWS == WINDOW == LANES
    nb = SAMPLE_ROWS // steps

    def tokens(width):
        return pl.BlockSpec((SAMPLE_ROWS, width), lambda g: (g, 0))

    c_spec = pl.BlockSpec((nb, M_HEADS, M_DK, M_DV), lambda g: (g, 0, 0, 0))
    n_spec = pl.BlockSpec((nb, M_HEADS * M_DK), lambda g: (g, 0))
    m_spec = pl.BlockSpec((M_HEADS, SAMPLE_ROWS, 1), lambda g: (0, g, 0))
    cache = pl.BlockSpec((nb, LANES, WINDOW), lambda g: (g, 0, 0))
    per_head = pltpu.VMEM((M_HEADS, SAMPLE_ROWS, M_DK), F32)
    stacked = pltpu.VMEM((A_HEADS * SAMPLE_ROWS, LANES), F32)
    return pl.pallas_call(
        functools.partial(_sample_mixers_kernel, steps=steps),
        grid=(rows // SAMPLE_ROWS,),
        in_specs=[pl.BlockSpec(memory_space=pltpu.SMEM), tokens(ZM_WIDTH), tokens(LANES),
                  pl.BlockSpec((1, M_WIDTH), lambda g: (0, 0)), c_spec, n_spec, m_spec,
                  tokens(A_WIDTH), tokens(KV_WIDTH), cache, cache],
        out_specs=[tokens(M_WIDTH), c_spec, n_spec, m_spec, tokens(A_WIDTH), cache, cache],
        out_shape=[jax.ShapeDtypeStruct((rows, M_WIDTH), BF16),
                   jax.ShapeDtypeStruct(c.shape, F32),
                   jax.ShapeDtypeStruct(n2.shape, F32),
                   jax.ShapeDtypeStruct(m_rows.shape, F32),
                   jax.ShapeDtypeStruct((rows, A_WIDTH), BF16),
                   jax.ShapeDtypeStruct(ckt.shape, F32),
                   jax.ShapeDtypeStruct(cvt.shape, F32)],
        scratch_shapes=[per_head,
                        per_head,
                        per_head,
                        per_head,
                        per_head,
                        stacked,
                        stacked,
                        stacked],
        compiler_params=pltpu.CompilerParams(dimension_semantics=("parallel",),
                                             vmem_limit_bytes=VMEM_LIMIT),
        name="sample_mixers",
    )(sinks, zm, gates, gain, c, n2, m_rows, qa, kva, ckt, cvt)


def _prep_proj(w_in, b_gate):
    m_end = 4 * M_WIDTH
    g_end = m_end + 2 * M_HEADS
    wt = w_in.T
    qa = wt[g_end:g_end + A_WIDTH].reshape(A_KV_HEADS, A_GROUP, A_HD, D_MODEL).transpose(1, 0, 2, 3)
    gates = jnp.pad(wt[m_end:g_end], ((0, LANES - 2 * M_HEADS), (0, 0)))
    win = jnp.concatenate([wt[:m_end], qa.reshape(A_WIDTH, D_MODEL), wt[g_end + A_WIDTH:], gates],
                          axis=0).astype(BF16)
    bias = jnp.pad(b_gate, (0, LANES - 2 * M_HEADS)).reshape(1, LANES)
    return win, bias


def _prep_out(w_out):
    wa = w_out[M_WIDTH:].reshape(A_KV_HEADS, A_GROUP, A_HD, D_MODEL).transpose(1, 0, 2, 3)
    return jnp.concatenate([w_out[:M_WIDTH], wa.reshape(A_WIDTH, D_MODEL)], axis=0).astype(BF16)


def kernel(x_prompt, x_sample, cache_swa_k, cache_swa_v, state_mlstm_C, state_mlstm_n, state_mlstm_m,
           norm_gains, ffn_w_gate, ffn_w_up, ffn_w_down, w_in, b_gate, mlstm_norm_gain, attn_sinks, w_out):
    assert norm_gains.shape[0] == 1, "single layer"
    batch, seq, _ = x_prompt.shape
    dec_batch, steps, _ = x_sample.shape
    gains = norm_gains[0]
    ffn = (ffn_w_gate, ffn_w_up, ffn_w_down)
    win, bias = _prep_proj(w_in[0], b_gate[0])
    wo = _prep_out(w_out[0])
    gain = mlstm_norm_gain[0].reshape(1, M_WIDTH)
    sinks = attn_sinks[0]
    kv_lanes = A_KV_HEADS * A_HD

    chunk = min(MLSTM_CHUNK, seq)
    assert seq % chunk == 0
    xp, zm, qa, kva, gates = _ffn_proj(x_prompt.reshape(batch * seq, D_MODEL), gains, *ffn, win, bias,
                                       0, 0, 1, 2, chunk)
    um, ua, p_c, p_n, p_m = _prompt_mixers(sinks, zm, gates, gain, qa, kva, batch, seq, chunk)
    yp = _mix_ffn(xp, um, ua, wo, gains, *ffn, 1, 3, 4, 5).reshape(batch, seq, D_MODEL)
    keep = min(WINDOW, seq)
    kv_tail = kva.reshape(batch, seq, KV_WIDTH)[:, seq - keep:]
    p_k = kv_tail[..., :kv_lanes].reshape(1, batch, keep, A_KV_HEADS, A_HD)
    p_v = kv_tail[..., kv_lanes:].reshape(1, batch, keep, A_KV_HEADS, A_HD)
    p_c = p_c[None]
    p_n = p_n[None, :, :, 0, :]
    p_m = p_m[None, :, :, 0, 0]

    xs, zm_s, qa_s, kva_s, gates_s = _ffn_proj(x_sample.reshape(dec_batch * steps, D_MODEL), gains, *ffn,
                                               win, bias, 0, 0, 1, 2, min(chunk, dec_batch * steps))
    m_rows = jnp.repeat(state_mlstm_m[0].T, steps, axis=1)[..., None]
    wc = cache_swa_k.shape[2]
    um_s, s_c, s_n2, s_m_rows, ua_s, s_kt, s_vt = _sample_mixers(
        sinks, zm_s, gates_s, gain, state_mlstm_C[0], state_mlstm_n[0].reshape(dec_batch, M_HEADS * M_DK), m_rows,
        qa_s, kva_s, cache_swa_k[0].reshape(dec_batch, wc, kv_lanes).transpose(0, 2, 1),
        cache_swa_v[0].reshape(dec_batch, wc, kv_lanes).transpose(0, 2, 1), steps)
    ys = _mix_ffn(xs, um_s, ua_s, wo, gains, *ffn, 1, 3, 4, 5).reshape(dec_batch, steps, D_MODEL)
    s_n = s_n2.reshape(1, dec_batch, M_HEADS, M_DK)
    s_m = s_m_rows[:, ::steps, 0].T[None]
    s_k = s_kt.transpose(0, 2, 1).reshape(1, dec_batch, wc, A_KV_HEADS, A_HD)
    s_v = s_vt.transpose(0, 2, 1).reshape(1, dec_batch, wc, A_KV_HEADS, A_HD)
    return (yp, ys, p_k, p_v, p_c, p_n, p_m, s_k, s_v, s_c[None], s_n, s_m)
```

```python
import functools

import jax
import jax.numpy as jnp
from jax import lax
from jax.experimental import pallas as pl
from jax.experimental.pallas import tpu as pltpu

F32 = jnp.float32
BF16 = jnp.bfloat16

D_MODEL = 1024
D_FF = 2816
M_HEADS = 4
M_DK = 128
M_DV = 128
M_WIDTH = M_HEADS * M_DV
A_HEADS = 8
A_KV_HEADS = 2
A_GROUP = A_HEADS // A_KV_HEADS
A_HD = 64
A_WIDTH = A_HEADS * A_HD
WINDOW = 128
RMS_EPS = 1e-6

LANES = 128
FFN_CHUNK = 256
N_FFN_CHUNKS = D_FF // FFN_CHUNK
FFN_STAGE_SLOTS = 4
TOKEN_TILE = 512
MLSTM_CHUNK = 256
ZM_WIDTH = 4 * M_WIDTH
KV_WIDTH = 2 * A_KV_HEADS * A_HD
VMEM_LIMIT = 56 * 1024 * 1024
MLSTM_CHUNKS_PER_STEP = 8
GATE_I, GATE_F, GATE_B, GATE_X = 0, M_HEADS, 2 * M_HEADS, 3 * M_HEADS

NT_DIMS = (((1,), (1,)), ((), ()))


def _rms(x, g):
    ms = jnp.mean(x * x, axis=-1, keepdims=True)
    return x * lax.rsqrt(ms + RMS_EPS) * g


def _log_sigmoid(x):
    return jnp.minimum(x, 0.0) - jnp.log1p(jnp.exp(-jnp.abs(x)))


def _dot(a, b):
    return jnp.dot(a, b, preferred_element_type=F32)


def _dot_nt(a, b):
    return lax.dot_general(a, b, NT_DIMS, preferred_element_type=F32)


def _resident(shape):
    nd = len(shape)
    return pl.BlockSpec(shape, lambda *_: (0,) * nd, pipeline_mode=pl.Buffered(1))


def _ffn_weight_copies(w_hbm, j, f, slot, stage_refs, sems):
    wg_hbm, wu_hbm, wd_hbm = w_hbm
    cols = pl.ds(f * FFN_CHUNK, FFN_CHUNK)
    sources = (wg_hbm.at[0, j, :, cols], wu_hbm.at[0, j, :, cols], wd_hbm.at[0, j, cols, :])
    return [pltpu.make_async_copy(src, stage.at[slot], sems.at[slot, k])
            for k, (src, stage) in enumerate(zip(sources, stage_refs))]


def _swiglu(x, gpre, w_refs, h_ref, acc_ref, fetch=None):
    wg_ref, wu_ref, wd_ref = w_refs
    if fetch is not None:
        for f in range(min(FFN_STAGE_SLOTS, N_FFN_CHUNKS)):
            for copy in _ffn_weight_copies(*fetch[:2], f, f % FFN_STAGE_SLOTS, *fetch[2:]):
                copy.start()
    h_ref[...] = _rms(x, gpre).astype(BF16)
    for f in range(N_FFN_CHUNKS):
        cols = slice(f * FFN_CHUNK, (f + 1) * FFN_CHUNK)
        if fetch is not None:
            slot = f % FFN_STAGE_SLOTS
            stage_g, stage_u, stage_d = fetch[2]
            for copy in _ffn_weight_copies(*fetch[:2], f, slot, *fetch[2:]):
                copy.wait()
            wg_ref[:, cols] = stage_g[slot].astype(BF16)
            wu_ref[:, cols] = stage_u[slot].astype(BF16)
            wd_ref[cols, :] = stage_d[slot].astype(BF16)
            if f + FFN_STAGE_SLOTS < N_FFN_CHUNKS:
                for copy in _ffn_weight_copies(*fetch[:2], f + FFN_STAGE_SLOTS, slot, *fetch[2:]):
                    copy.start()
        h = h_ref[...]
        g = _dot(h, wg_ref[:, cols])
        u = _dot(h, wu_ref[:, cols])
        part = _dot((g * jax.nn.sigmoid(g) * u).astype(BF16), wd_ref[cols, :])
        if f == 0:
            acc_ref[...] = part
        else:
            acc_ref[...] += part


def _first_step_fetches(body, fetch):
    first = pl.program_id(0) == 0

    @pl.when(first)
    def _():
        body(fetch)

    @pl.when(jnp.logical_not(first))
    def _():
        body(None)


def _half_step(x, acc_ref, gpost):
    return x + _rms(acc_ref[...], 0.5 * gpost)


def _project(x, gain, win_refs, bias_ref, h_ref, zm_ref, qa_ref, kva_ref, gates_ref, chunk):
    h_ref[...] = _rms(x, gain).astype(BF16)

    wm_ref, wqa_ref, wkv_ref, wg_ref = win_refs

    def cols(lo, width):
        return _dot_nt(h_ref[...], wm_ref[lo:lo + width, :])

    gates = _dot_nt(h_ref[...], wg_ref[...]) + bias_ref[...]
    lane = lax.broadcasted_iota(jnp.int32, (chunk, LANES), 1)
    pos = lax.broadcasted_iota(jnp.int32, (chunk, LANES), 0)
    for c0 in range(0, gates.shape[0], chunk):
        g = gates[c0:c0 + chunk]
        b = jnp.where((lane >= GATE_F) & (lane < GATE_B), _log_sigmoid(g), 0.0)
        shift = 1
        while shift < chunk:
            b = b + jnp.where(pos >= shift, pltpu.roll(b, shift, 0), 0.0)
            shift *= 2
        x_gate = pltpu.roll(g, GATE_X - GATE_I, 1) - pltpu.roll(b, GATE_X - GATE_F, 1)
        out = jnp.where(lane < GATE_B, g, jnp.where(lane < GATE_X, pltpu.roll(b, GATE_B - GATE_F, 1), x_gate))
        gates_ref[c0:c0 + chunk, :] = jnp.where(lane < GATE_X + M_HEADS, out, 0.0)

    zm_ref[:, 0:M_WIDTH] = cols(0, M_WIDTH).astype(BF16)
    zm_ref[:, M_WIDTH:2 * M_WIDTH] = (cols(M_WIDTH, M_WIDTH) * (M_DK ** -0.5)).astype(BF16)
    zm_ref[:, 2 * M_WIDTH:3 * M_WIDTH] = cols(2 * M_WIDTH, M_WIDTH).astype(BF16)
    zm_ref[:, 3 * M_WIDTH:] = jax.nn.sigmoid(cols(3 * M_WIDTH, M_WIDTH)).astype(BF16)
    qa_ref[...] = (_dot_nt(h_ref[...], wqa_ref[...]) * (A_HD ** -0.5)).astype(BF16)
    kva_ref[...] = _dot_nt(h_ref[...], wkv_ref[...])


def _ffn_proj_kernel(x_ref, gains_ref, wg_hbm, wu_hbm, wd_hbm, wm_ref, wqa_ref, wkv_ref, wgate_ref, bias_ref,
                     x1_ref, zm_ref, qa_ref, kva_ref, gates_ref,
                     h_ref, acc_ref, wg_ref, wu_ref, wd_ref, stage_g, stage_u, stage_d, sems,
                     *, j, pre, post, mixer, chunk):
    def body(fetch):
        x = x_ref[...]
        _swiglu(x, gains_ref[pre:pre + 1, :], (wg_ref, wu_ref, wd_ref), h_ref, acc_ref, fetch)
        x1 = _half_step(x, acc_ref, gains_ref[post:post + 1, :])
        x1_ref[...] = x1
        _project(x1, gains_ref[mixer:mixer + 1, :], (wm_ref, wqa_ref, wkv_ref, wgate_ref), bias_ref, h_ref,
                 zm_ref, qa_ref, kva_ref, gates_ref, chunk)

    _first_step_fetches(body, ((wg_hbm, wu_hbm, wd_hbm), j, (stage_g, stage_u, stage_d), sems))


def _mix_ffn_kernel(x_ref, um_ref, ua_ref, wo_ref, gains_ref, wg_hbm, wu_hbm, wd_hbm, o_ref,
                    h_ref, acc_ref, wg_ref, wu_ref, wd_ref, stage_g, stage_u, stage_d, sems,
                    *, j, mix, pre, post):
    def body(fetch):
        u = jnp.concatenate([um_ref[...], ua_ref[...]], axis=-1)
        x = x_ref[...] + _rms(_dot(u, wo_ref[...]), gains_ref[mix:mix + 1, :])
        _swiglu(x, gains_ref[pre:pre + 1, :], (wg_ref, wu_ref, wd_ref), h_ref, acc_ref, fetch)
        o_ref[...] = _half_step(x, acc_ref, gains_ref[post:post + 1, :])

    _first_step_fetches(body, ((wg_hbm, wu_hbm, wd_hbm), j, (stage_g, stage_u, stage_d), sems))


def _ffn_scratch(tm):
    return [pltpu.VMEM((tm, D_MODEL), BF16),
            pltpu.VMEM((tm, D_MODEL), F32),
            pltpu.VMEM((D_MODEL, D_FF), BF16),
            pltpu.VMEM((D_MODEL, D_FF), BF16),
            pltpu.VMEM((D_FF, D_MODEL), BF16),
            pltpu.VMEM((FFN_STAGE_SLOTS, D_MODEL, FFN_CHUNK), F32),
            pltpu.VMEM((FFN_STAGE_SLOTS, D_MODEL, FFN_CHUNK), F32),
            pltpu.VMEM((FFN_STAGE_SLOTS, FFN_CHUNK, D_MODEL), F32),
            pltpu.SemaphoreType.DMA((FFN_STAGE_SLOTS, 3))]


def _dense_params():
    return pltpu.CompilerParams(dimension_semantics=("arbitrary",), vmem_limit_bytes=VMEM_LIMIT)


_HBM = pl.BlockSpec(memory_space=pl.ANY)


def _ffn_proj(x, gains, wg, wu, wd, win, bias, j, pre, post, mixer, chunk):
    n = x.shape[0]
    tm = min(TOKEN_TILE, n)
    assert tm % chunk == 0

    def rows(width):
        return pl.BlockSpec((tm, width), lambda i: (i, 0))

    return pl.pallas_call(
        functools.partial(_ffn_proj_kernel, j=j, pre=pre, post=post, mixer=mixer, chunk=chunk),
        grid=(n // tm,),
        in_specs=[rows(D_MODEL), _resident(gains.shape), _HBM, _HBM, _HBM,
                  *[_resident(w.shape) for w in win], _resident(bias.shape)],
        out_specs=[rows(D_MODEL), rows(ZM_WIDTH), rows(A_WIDTH), rows(KV_WIDTH), rows(LANES)],
        out_shape=[jax.ShapeDtypeStruct((n, D_MODEL), F32),
                   jax.ShapeDtypeStruct((n, ZM_WIDTH), BF16),
                   jax.ShapeDtypeStruct((n, A_WIDTH), BF16),
                   jax.ShapeDtypeStruct((n, KV_WIDTH), F32),
                   jax.ShapeDtypeStruct((n, LANES), F32)],
        scratch_shapes=_ffn_scratch(tm),
        compiler_params=_dense_params(),
        name="ffn_half_step_input_projection",
    )(x, gains, wg, wu, wd, *win, bias)


def _mix_ffn(x, um, ua, wo, gains, wg, wu, wd, j, mix, pre, post):
    n = x.shape[0]
    tm = min(TOKEN_TILE, n)
    row = pl.BlockSpec((tm, D_MODEL), lambda i: (i, 0))
    half = pl.BlockSpec((tm, M_WIDTH), lambda i: (i, 0))
    return pl.pallas_call(
        functools.partial(_mix_ffn_kernel, j=j, mix=mix, pre=pre, post=post),
        grid=(n // tm,),
        in_specs=[row, half, half, _resident(wo.shape), _resident(gains.shape), _HBM, _HBM, _HBM],
        out_specs=row,
        out_shape=jax.ShapeDtypeStruct((n, D_MODEL), F32),
        scratch_shapes=_ffn_scratch(tm),
        compiler_params=_dense_params(),
        name="out_proj_ffn_half_step",
    )(x, um, ua, wo, gains, wg, wu, wd)


def _head_norm_gate(hh, o, gain):
    hn = hh * lax.rsqrt(jnp.mean(hh * hh, axis=-1, keepdims=True) + RMS_EPS) * gain
    return (o.astype(F32) * hn).astype(BF16)


def _mlstm_chunks(zm_ref, gates_ref, gain_ref, um_ref, cnt_ref, ms_ref, chunk):
    si = lax.broadcasted_iota(jnp.int32, (chunk, chunk), 0)
    ti = lax.broadcasted_iota(jnp.int32, (chunk, chunk), 1)
    causal = si <= ti
    ones_rows = jnp.where(lax.broadcasted_iota(jnp.int32, (LANES, chunk), 0) == 0, 1.0, 0.0).astype(BF16)

    def run_chunk(c):
        rows = slice(c * chunk, (c + 1) * chunk)
        gates = gates_ref[rows, :]
        gates_t = gates.T
        for h in range(M_HEADS):
            lanes = slice(h * M_DK, (h + 1) * M_DK)
            q = zm_ref[rows, lanes]
            k = zm_ref[rows, M_WIDTH + h * M_DK:M_WIDTH + (h + 1) * M_DK]
            v = zm_ref[rows, 2 * M_WIDTH + h * M_DV:2 * M_WIDTH + (h + 1) * M_DV]
            o = zm_ref[rows, 3 * M_WIDTH + h * M_DV:3 * M_WIDTH + (h + 1) * M_DV]
            x_col = gates[:, GATE_X + h:GATE_X + h + 1]
            ig_row = gates_t[GATE_I + h:GATE_I + h + 1, :]
            b_row = gates_t[GATE_B + h:GATE_B + h + 1, :]
            m_prev = ms_ref[h][:, 0:1]
            cnt = cnt_ref[h]

            xm = jnp.where(causal, x_col, -jnp.inf)
            mu = jnp.maximum(m_prev, jnp.max(xm, axis=0, keepdims=True))
            a = jnp.exp(m_prev - mu)
            s_t = _dot_nt(k, q) * jnp.exp(xm - mu)
            v_ext = jnp.concatenate([v.astype(F32).T.astype(BF16), ones_rows], axis=0)
            nd = a * _dot_nt(cnt.astype(BF16), q) + _dot(v_ext, s_t.astype(BF16))
            den = nd[M_DV:M_DV + 1, :]
            hh = nd[:M_DV, :] * (1.0 / jnp.maximum(jnp.abs(den), jnp.exp(-(b_row + mu))))
            hn = hh * lax.rsqrt(jnp.mean(hh * hh, axis=0, keepdims=True) + RMS_EPS)
            um_ref[rows, lanes] = (hn.T * gain_ref[:, lanes] * o.astype(F32)).astype(BF16)

            b_last = b_row[:, chunk - 1:chunk]
            g = b_last - b_row + ig_row
            m_new = jnp.maximum(b_last + m_prev, jnp.max(g, axis=-1, keepdims=True))
            decay = jnp.exp(b_last + m_prev - m_new)
            wv = (jnp.exp(g - m_new) * v_ext.astype(F32)).astype(BF16)
            cnt_ref[h] = decay * cnt + _dot(wv, k)
            ms_ref[h] = jnp.broadcast_to(m_new, (1, LANES))

    return [functools.partial(run_chunk, c) for c in range(zm_ref.shape[0] // chunk)]


def _alibi_slope(head):
    return 2.0 ** (-8.0 * (head + 1) / A_HEADS)


def _swa_blocks(sinks_ref, qa_ref, kvc_ref, kvp_ref, ua_ref, mask_ref, first):
    hw = A_KV_HEADS * A_HD
    n_blocks = qa_ref.shape[0] // WINDOW
    qi = lax.broadcasted_iota(jnp.int32, (WINDOW, 2 * WINDOW), 0)
    kj = lax.broadcasted_iota(jnp.int32, (WINDOW, 2 * WINDOW), 1)
    dist = WINDOW + qi - kj
    band = (dist >= 0) & (dist < WINDOW)
    mask_ref[0] = jnp.where(band, 0.0, -jnp.inf)
    mask_ref[1] = jnp.where(band & (kj >= WINDOW), 0.0, -jnp.inf)
    lane = lax.broadcasted_iota(jnp.int32, (WINDOW, LANES), 1)
    low = lane < A_HD

    assert 2 * WINDOW <= 256, "positions must stay exactly representable in bf16"
    q_pos = (WINDOW + lax.broadcasted_iota(jnp.int32, (WINDOW, LANES), 0)).astype(F32)
    k_lane = lax.broadcasted_iota(jnp.int32, (2 * WINDOW, LANES), 1)
    k_pos = lax.broadcasted_iota(jnp.int32, (2 * WINDOW, LANES), 0).astype(F32)
    spare = (A_HD, 0)
    k_bias = [jnp.where(k_lane == spare[kvh], 1.0, jnp.where(k_lane == spare[kvh] + 1, k_pos, 0.0)).astype(BF16)
              for kvh in range(A_KV_HEADS)]
    q_bias = {}
    for head in range(A_HEADS):
        slope = _alibi_slope(head)
        at = spare[head // A_GROUP]
        q_bias[head] = jnp.where(lane == at, -slope * q_pos, jnp.where(lane == at + 1, slope, 0.0)).astype(BF16)

    def run_block(j):
        rows = slice(j * WINDOW, (j + 1) * WINDOW)
        kvc = kvc_ref[rows, :]
        kvp = kvp_ref[...] if j == 0 else kvc_ref[(j - 1) * WINDOW:j * WINDOW, :]
        mask = mask_ref[jnp.where(first, 1, 0)] if j == 0 else mask_ref[0]
        kk = jnp.concatenate([kvp[:, :hw], kvc[:, :hw]], axis=0).astype(BF16)
        kk_biased = [jnp.where(k_lane < A_HD, kk, k_bias[0]), jnp.where(k_lane < A_HD, k_bias[1], kk)]
        vv = jnp.concatenate([kvp[:, hw:], kvc[:, hw:]], axis=0).astype(BF16)
        for p in range(A_GROUP):
            qp = qa_ref[rows, p * LANES:(p + 1) * LANES]
            outs = []
            for kvh in range(A_KV_HEADS):
                head = kvh * A_GROUP + p
                keep = low if kvh == 0 else jnp.logical_not(low)
                s = _dot_nt(jnp.where(keep, qp, q_bias[head]), kk_biased[kvh])
                s = s + mask
                sink = sinks_ref[head]
                mx = jnp.maximum(jnp.max(s, axis=-1, keepdims=True), sink)
                e = jnp.exp(s - mx)
                den = jnp.sum(e, axis=-1, keepdims=True) + jnp.exp(sink - mx)
                outs.append(_dot(e.astype(BF16), vv) / den)
            ua_ref[rows, p * LANES:(p + 1) * LANES] = jnp.where(low, outs[0], outs[1]).astype(BF16)

    return [functools.partial(run_block, j) for j in range(n_blocks)]


def _prompt_mixers_kernel(sinks_ref, zm_ref, gates_ref, gain_ref, qa_ref, kvc_ref, kvp_ref,
                          um_ref, ua_ref, c_ref, n_ref, m_ref, cnt_ref, ms_ref, mask_ref, *, chunk):
    step = pl.program_id(1)

    @pl.when(step == 0)
    def _():
        cnt_ref[...] = jnp.zeros_like(cnt_ref)
        ms_ref[...] = jnp.zeros_like(ms_ref)

    chunks = _mlstm_chunks(zm_ref, gates_ref, gain_ref, um_ref, cnt_ref, ms_ref, chunk)
    blocks = _swa_blocks(sinks_ref, qa_ref, kvc_ref, kvp_ref, ua_ref, mask_ref, step == 0)
    per_chunk = len(blocks) // len(chunks)
    for c, run_chunk in enumerate(chunks):
        run_chunk()
        for run_block in blocks[c * per_chunk:(c + 1) * per_chunk]:
            run_block()

    @pl.when(step == pl.num_programs(1) - 1)
    def _():
        for h in range(M_HEADS):
            c_ref[0, h] = cnt_ref[h][:M_DV, :].T
            n_ref[0, h] = cnt_ref[h][M_DV:M_DV + 1, :]
            m_ref[0, h] = ms_ref[h]


def _prompt_mixers(sinks, zm, gates, gain, qa, kva, batch, seq, chunk):
    n = batch * seq
    per_step = min(MLSTM_CHUNKS_PER_STEP, seq // chunk)
    rows = per_step * chunk
    assert seq % rows == 0 and chunk % WINDOW == 0
    ns = seq // rows

    def step_rows(width):
        return pl.BlockSpec((rows, width), lambda b, i: (b * ns + i, 0))

    def prev_block(b, i):
        return (jnp.maximum((b * ns + i) * (rows // WINDOW) - 1, 0), 0)

    def state(*dims):
        return pl.BlockSpec((1, M_HEADS) + dims, lambda b, i: (b, 0, 0, 0))

    return pl.pallas_call(
        functools.partial(_prompt_mixers_kernel, chunk=chunk),
        grid=(batch, ns),
        in_specs=[pl.BlockSpec(memory_space=pltpu.SMEM), step_rows(ZM_WIDTH), step_rows(LANES),
                  pl.BlockSpec((1, M_WIDTH), lambda b, i: (0, 0)),
                  step_rows(A_WIDTH), step_rows(KV_WIDTH), pl.BlockSpec((WINDOW, KV_WIDTH), prev_block)],
        out_specs=[step_rows(M_WIDTH), step_rows(A_WIDTH), state(M_DK, M_DV), state(1, M_DK), state(1, LANES)],
        out_shape=[jax.ShapeDtypeStruct((n, M_WIDTH), BF16),
                   jax.ShapeDtypeStruct((n, A_WIDTH), BF16),
                   jax.ShapeDtypeStruct((batch, M_HEADS, M_DK, M_DV), F32),
                   jax.ShapeDtypeStruct((batch, M_HEADS, 1, M_DK), F32),
                   jax.ShapeDtypeStruct((batch, M_HEADS, 1, LANES), F32)],
        scratch_shapes=[pltpu.VMEM((M_HEADS, 2 * LANES, M_DK), F32), pltpu.VMEM((M_HEADS, 1, LANES), F32),
                        pltpu.VMEM((2, WINDOW, 2 * WINDOW), F32)],
        compiler_params=pltpu.CompilerParams(dimension_semantics=("parallel", "arbitrary"),
                                             vmem_limit_bytes=VMEM_LIMIT),
        name="prompt_mixers",
    )(sinks, zm, gates, gain, qa, kva, kva)


SAMPLE_ROWS = 128


def _mlstm_sample_kernel(zm_ref, gates_ref, gain_ref, c_ref, n_ref, m_ref,
                         um_ref, co_ref, no_ref, mo_ref,
                         qf_ref, qc_ref, nexp_ref, wkk_ref, dec_ref, *, steps):
    rows = SAMPLE_ROWS
    nb = rows // steps
    row = lax.broadcasted_iota(jnp.int32, (rows, rows), 0)
    col = lax.broadcasted_iota(jnp.int32, (rows, rows), 1)
    same = (row // steps) == (col // steps)
    causal = same & (col <= row)
    eye = row == col
    last = same & ((col % steps) == steps - 1)
    sub = lax.broadcasted_iota(jnp.int32, (2 * steps, M_DV), 0)
    tok_batch = lax.broadcasted_iota(jnp.int32, (M_DK, rows), 1) // steps

    def to_row(x_col):
        return jnp.sum(jnp.where(eye, x_col, 0.0), axis=0, keepdims=True)

    def pick(mask, x_row, fill, reduce):
        return reduce(jnp.where(mask, x_row, fill), axis=-1, keepdims=True)

    gates = gates_ref[...]
    for head in range(M_HEADS):
        lanes = slice(head * M_DK, (head + 1) * M_DK)
        ig_col = gates[:, GATE_I + head:GATE_I + head + 1]
        fg_col = gates[:, GATE_F + head:GATE_F + head + 1]
        lf_row = to_row(_log_sigmoid(fg_col))
        b_col = pick(causal, lf_row, 0.0, jnp.sum)
        b_row = to_row(b_col)
        ig_row = to_row(ig_col)
        m_prev = m_ref[head]

        d = jnp.where(causal, b_col - b_row + ig_row, -jnp.inf)
        inter = b_col + m_prev
        m_t = jnp.maximum(inter, jnp.max(d, axis=-1, keepdims=True))
        w = jnp.exp(d - m_t)
        a = jnp.exp(inter - m_t)

        q = zm_ref[:, lanes]
        k = zm_ref[:, M_WIDTH + head * M_DK:M_WIDTH + (head + 1) * M_DK]
        v = zm_ref[:, 2 * M_WIDTH + head * M_DV:2 * M_WIDTH + (head + 1) * M_DV]
        o = zm_ref[:, 3 * M_WIDTH + head * M_DV:3 * M_WIDTH + (head + 1) * M_DV]
        qf_ref[head] = q.astype(F32)
        for t in range(steps):
            nexp_ref[head, pl.ds(t, nb, stride=steps), :] = n_ref[:, lanes]

        for j in range(nb // 2):
            pair = pl.ds(j * 2 * steps, 2 * steps)
            q8 = qf_ref[head, pair, :].astype(BF16)
            r0 = _dot(q8, c_ref[2 * j, head].astype(BF16))
            r1 = _dot(q8, c_ref[2 * j + 1, head].astype(BF16))
            qc_ref[head, pair, :] = jnp.where(sub < steps, r0, r1)

        s = _dot_nt(q, k) * w
        qn = jnp.sum(qf_ref[head] * nexp_ref[head], axis=-1, keepdims=True)
        num = a * qc_ref[head] + _dot(s.astype(BF16), v)
        den = a * qn + jnp.sum(s, axis=-1, keepdims=True)
        hh = num / jnp.maximum(jnp.abs(den), jnp.exp(-m_t))
        um_ref[:, lanes] = _head_norm_gate(hh, o, gain_ref[:, lanes])

        b_last = pick(last, b_row, 0.0, jnp.sum)
        g_col = b_last - b_col + ig_col
        g_max = pick(same, to_row(g_col), -jnp.inf, jnp.max)
        m_new = jnp.maximum(b_last + m_prev, g_max)
        decay = jnp.exp(b_last + m_prev - m_new)
        wk = jnp.exp(g_col - m_new)
        mo_ref[head] = m_new

        kf = k.astype(F32)
        wkk_ref[head] = wk * kf
        dec_ref[head] = jnp.broadcast_to(decay, (rows, M_DV))
        n_sum = wkk_ref[head, pl.ds(0, nb, stride=steps), :]
        for t in range(1, steps):
            n_sum = n_sum + wkk_ref[head, pl.ds(t, nb, stride=steps), :]
        no_ref[:, lanes] = dec_ref[head, pl.ds(0, nb, stride=steps), :] * n_ref[:, lanes] + n_sum

        k_t = kf.T
        wv = (wk * v.astype(F32)).astype(BF16)
        for b in range(nb):
            upd = _dot(jnp.where(tok_batch == b, k_t, 0.0).astype(BF16), wv)
            co_ref[b, head] = dec_ref[head, pl.ds(b * steps, 1), :] * c_ref[b, head] + upd


def _swa_sample_kernel(sinks_ref, qa_ref, kva_ref, ckt_ref, cvt_ref, ua_ref, nkt_ref, nvt_ref,
                       s_ref, e_ref, pv_ref, *, steps):
    rows = SAMPLE_ROWS
    nb = rows // steps
    pair_rows = 2 * steps
    hw = A_KV_HEADS * A_HD
    tok = lax.broadcasted_iota(jnp.int32, (rows, LANES), 0)
    lane = lax.broadcasted_iota(jnp.int32, (rows, LANES), 1)
    low = lane < A_HD

    qf = qa_ref[...].astype(F32)
    q_blocks = []
    for p in range(A_GROUP):
        qp = qf[:, p * LANES:(p + 1) * LANES]
        q_blocks.append(jnp.where(low, qp, 0.0))
        q_blocks.append(jnp.where(low, 0.0, qp))
    n_blk = len(q_blocks)

    kv_new = kva_ref[...]
    kv_new_t = kv_new.T
    k_new_t = kv_new_t[:hw]
    v_new_t = kv_new_t[hw:]
    v_new = kv_new[:, hw:].astype(BF16)
    s_new = _dot(jnp.concatenate(q_blocks, axis=0).astype(BF16), k_new_t.astype(BF16))

    def pair_rows_of(blocks, j):
        return jnp.concatenate([blk[j * pair_rows:(j + 1) * pair_rows] for blk in blocks], axis=0)

    first = (lax.broadcasted_iota(jnp.int32, (n_blk * pair_rows, LANES), 0) % pair_rows) < steps

    for j in range(nb // 2):
        q_pair = pair_rows_of(q_blocks, j).astype(BF16)
        s_pair = jnp.where(first, _dot(q_pair, ckt_ref[2 * j].astype(BF16)),
                           _dot(q_pair, ckt_ref[2 * j + 1].astype(BF16)))
        for blk in range(n_blk):
            s_ref[pl.ds(blk * rows + j * pair_rows, pair_rows), :] = s_pair[blk * pair_rows:(blk + 1) * pair_rows]

    t_q = tok % steps
    dist_c = WINDOW + t_q - lane
    valid_c = dist_c < WINDOW
    dist_n = t_q - lane % steps
    valid_n = ((tok // steps) == (lane // steps)) & (dist_n >= 0)
    dist_cf = dist_c.astype(F32)
    dist_nf = dist_n.astype(F32)
    pv_new = []
    inv_den = []
    for blk in range(n_blk):
        head = (blk % A_KV_HEADS) * A_GROUP + blk // A_KV_HEADS
        slope = _alibi_slope(head)
        sink = sinks_ref[head]
        r = pl.ds(blk * rows, rows)
        s_c = jnp.where(valid_c, s_ref[r, :] - slope * dist_cf, -jnp.inf)
        s_n = jnp.where(valid_n, s_new[blk * rows:(blk + 1) * rows] - slope * dist_nf, -jnp.inf)
        mx = jnp.maximum(jnp.maximum(jnp.max(s_c, axis=-1, keepdims=True),
                                     jnp.max(s_n, axis=-1, keepdims=True)), sink)
        e_c = jnp.exp(s_c - mx)
        e_n = jnp.exp(s_n - mx)
        den = jnp.sum(e_c, axis=-1, keepdims=True) + jnp.sum(e_n, axis=-1, keepdims=True) + jnp.exp(sink - mx)
        e_ref[r, :] = e_c
        pv_new.append(_dot(e_n.astype(BF16), v_new))
        inv_den.append(1.0 / den)

    for j in range(nb // 2):
        e_pair = jnp.concatenate([e_ref[pl.ds(blk * rows + j * pair_rows, pair_rows), :] for blk in range(n_blk)],
                                 axis=0).astype(BF16)
        pv_pair = jnp.where(first, _dot_nt(e_pair, cvt_ref[2 * j].astype(BF16)),
                            _dot_nt(e_pair, cvt_ref[2 * j + 1].astype(BF16)))
        for blk in range(n_blk):
            pv_ref[pl.ds(blk * rows + j * pair_rows, pair_rows), :] = pv_pair[blk * pair_rows:(blk + 1) * pair_rows]

    for p in range(A_GROUP):
        halves = []
        for kvh in range(A_KV_HEADS):
            blk = p * A_KV_HEADS + kvh
            halves.append((pv_ref[pl.ds(blk * rows, rows), :] + pv_new[blk]) * inv_den[blk])
        ua_ref[:, p * LANES:(p + 1) * LANES] = jnp.where(low, halves[0], halves[1]).astype(BF16)

    tail = lane >= WINDOW - steps
    for b in range(nb):
        shift = (WINDOW - steps - b * steps) % rows
        nkt_ref[b] = jnp.where(tail, pltpu.roll(k_new_t, shift, 1), pltpu.roll(ckt_ref[b], WINDOW - steps, 1))
        nvt_ref[b] = jnp.where(tail, pltpu.roll(v_new_t, shift, 1), pltpu.roll(cvt_ref[b], WINDOW - steps, 1))


def _sample_mixers_kernel(sinks_ref, zm_ref, gates_ref, gain_ref, c_ref, n_ref, m_ref, qa_ref, kva_ref,
                          ckt_ref, cvt_ref, um_ref, co_ref, no_ref, mo_ref, ua_ref, nkt_ref, nvt_ref,
                          qf_ref, qc_ref, nexp_ref, wkk_ref, dec_ref, s_ref, e_ref, pv_ref, *, steps):
    _mlstm_sample_kernel(zm_ref, gates_ref, gain_ref, c_ref, n_ref, m_ref, um_ref, co_ref, no_ref, mo_ref,
                         qf_ref, qc_ref, nexp_ref, wkk_ref, dec_ref, steps=steps)
    _swa_sample_kernel(sinks_ref, qa_ref, kva_ref, ckt_ref, cvt_ref, ua_ref, nkt_ref, nvt_ref,
                       s_ref, e_ref, pv_ref, steps=steps)


def _sample_mixers(sinks, zm, gates, gain, c, n2, m_rows, qa, kva, ckt, cvt, steps):
    rows = zm.shape[0]
    assert ckt.shape[1:] == (A_KV_HEADS * A_HD, WINDOW) and SAMPLE_ROWS == WINDOW == LANES
    nb = SAMPLE_ROWS // steps

    def tokens(width):
        return pl.BlockSpec((SAMPLE_ROWS, width), lambda g: (g, 0))

    c_spec = pl.BlockSpec((nb, M_HEADS, M_DK, M_DV), lambda g: (g, 0, 0, 0))
    n_spec = pl.BlockSpec((nb, M_HEADS * M_DK), lambda g: (g, 0))
    m_spec = pl.BlockSpec((M_HEADS, SAMPLE_ROWS, 1), lambda g: (0, g, 0))
    cache = pl.BlockSpec((nb, LANES, WINDOW), lambda g: (g, 0, 0))
    per_head = pltpu.VMEM((M_HEADS, SAMPLE_ROWS, M_DK), F32)
    stacked = pltpu.VMEM((A_HEADS * SAMPLE_ROWS, LANES), F32)
    return pl.pallas_call(
        functools.partial(_sample_mixers_kernel, steps=steps),
        grid=(rows // SAMPLE_ROWS,),
        in_specs=[pl.BlockSpec(memory_space=pltpu.SMEM), tokens(ZM_WIDTH), tokens(LANES),
                  pl.BlockSpec((1, M_WIDTH), lambda g: (0, 0)), c_spec, n_spec, m_spec,
                  tokens(A_WIDTH), tokens(KV_WIDTH), cache, cache],
        out_specs=[tokens(M_WIDTH), c_spec, n_spec, m_spec, tokens(A_WIDTH), cache, cache],
        out_shape=[jax.ShapeDtypeStruct((rows, M_WIDTH), BF16),
                   jax.ShapeDtypeStruct(c.shape, F32),
                   jax.ShapeDtypeStruct(n2.shape, F32),
                   jax.ShapeDtypeStruct(m_rows.shape, F32),
                   jax.ShapeDtypeStruct((rows, A_WIDTH), BF16),
                   jax.ShapeDtypeStruct(ckt.shape, F32),
                   jax.ShapeDtypeStruct(cvt.shape, F32)],
        scratch_shapes=[per_head,
                        per_head,
                        per_head,
                        per_head,
                        per_head,
                        stacked,
                        stacked,
                        stacked],
        compiler_params=pltpu.CompilerParams(dimension_semantics=("parallel",),
                                             vmem_limit_bytes=VMEM_LIMIT),
        name="sample_mixers",
    )(sinks, zm, gates, gain, c, n2, m_rows, qa, kva, ckt, cvt)


def _prep_proj(w_in, b_gate):
    m_end = 4 * M_WIDTH
    g_end = m_end + 2 * M_HEADS
    wt = w_in.T
    qa = wt[g_end:g_end + A_WIDTH].reshape(A_KV_HEADS, A_GROUP, A_HD, D_MODEL).transpose(1, 0, 2, 3)
    gates = jnp.pad(wt[m_end:g_end], ((0, LANES - 2 * M_HEADS), (0, 0)))
    win = tuple(w.astype(BF16) for w in (wt[:m_end], qa.reshape(A_WIDTH, D_MODEL), wt[g_end + A_WIDTH:], gates))
    bias = jnp.pad(b_gate, (0, LANES - 2 * M_HEADS)).reshape(1, LANES)
    return win, bias


def _prep_out(w_out):
    wa = w_out[M_WIDTH:].reshape(A_KV_HEADS, A_GROUP, A_HD, D_MODEL).transpose(1, 0, 2, 3)
    return jnp.concatenate([w_out[:M_WIDTH], wa.reshape(A_WIDTH, D_MODEL)], axis=0).astype(BF16)


def kernel(x_prompt, x_sample, cache_swa_k, cache_swa_v, state_mlstm_C, state_mlstm_n, state_mlstm_m,
           norm_gains, ffn_w_gate, ffn_w_up, ffn_w_down, w_in, b_gate, mlstm_norm_gain, attn_sinks, w_out):
    assert norm_gains.shape[0] == 1, "single layer"
    batch, seq, _ = x_prompt.shape
    dec_batch, steps, _ = x_sample.shape
    gains = norm_gains[0]
    ffn = (ffn_w_gate, ffn_w_up, ffn_w_down)
    win, bias = _prep_proj(w_in[0], b_gate[0])
    wo = _prep_out(w_out[0])
    gain = mlstm_norm_gain[0].reshape(1, M_WIDTH)
    sinks = attn_sinks[0]
    kv_lanes = A_KV_HEADS * A_HD

    chunk = min(MLSTM_CHUNK, seq)
    assert seq % chunk == 0
    xp, zm, qa, kva, gates = _ffn_proj(x_prompt.reshape(batch * seq, D_MODEL), gains, *ffn, win, bias,
                                       0, 0, 1, 2, chunk)
    um, ua, p_c, p_n, p_m = _prompt_mixers(sinks, zm, gates, gain, qa, kva, batch, seq, chunk)
    yp = _mix_ffn(xp, um, ua, wo, gains, *ffn, 1, 3, 4, 5).reshape(batch, seq, D_MODEL)
    keep = min(WINDOW, seq)
    kv_tail = kva.reshape(batch, seq, KV_WIDTH)[:, seq - keep:]
    p_k = kv_tail[..., :kv_lanes].reshape(1, batch, keep, A_KV_HEADS, A_HD)
    p_v = kv_tail[..., kv_lanes:].reshape(1, batch, keep, A_KV_HEADS, A_HD)
    p_c = p_c[None]
    p_n = p_n[None, :, :, 0, :]
    p_m = p_m[None, :, :, 0, 0]

    xs, zm_s, qa_s, kva_s, gates_s = _ffn_proj(x_sample.reshape(dec_batch * steps, D_MODEL), gains, *ffn,
                                               win, bias, 0, 0, 1, 2, min(chunk, dec_batch * steps))
    m_rows = jnp.repeat(state_mlstm_m[0].T, steps, axis=1)[..., None]
    wc = cache_swa_k.shape[2]
    um_s, s_c, s_n2, s_m_rows, ua_s, s_kt, s_vt = _sample_mixers(
        sinks, zm_s, gates_s, gain, state_mlstm_C[0], state_mlstm_n[0].reshape(dec_batch, M_HEADS * M_DK), m_rows,
        qa_s, kva_s, cache_swa_k[0].reshape(dec_batch, wc, kv_lanes).transpose(0, 2, 1),
        cache_swa_v[0].reshape(dec_batch, wc, kv_lanes).transpose(0, 2, 1), steps)
    ys = _mix_ffn(xs, um_s, ua_s, wo, gains, *ffn, 1, 3, 4, 5).reshape(dec_batch, steps, D_MODEL)
    s_n = s_n2.reshape(1, dec_batch, M_HEADS, M_DK)
    s_m = s_m_rows[:, ::steps, 0].T[None]
    s_k = s_kt.transpose(0, 2, 1).reshape(1, dec_batch, wc, A_KV_HEADS, A_HD)
    s_v = s_vt.transpose(0, 2, 1).reshape(1, dec_batch, wc, A_KV_HEADS, A_HD)
    return (yp, ys, p_k, p_v, p_c, p_n, p_m, s_k, s_v, s_c[None], s_n, s_m)
```

```python
import functools

import jax
import jax.numpy as jnp
from jax import lax
from jax.experimental import pallas as pl
from jax.experimental.pallas import tpu as pltpu

F32 = jnp.float32
BF16 = jnp.bfloat16

D_MODEL = 1024
D_FF = 2816
M_HEADS = 4
M_DK = 128
M_DV = 128
M_WIDTH = M_HEADS * M_DV
A_HEADS = 8
A_KV_HEADS = 2
A_GROUP = A_HEADS // A_KV_HEADS
A_HD = 64
A_WIDTH = A_HEADS * A_HD
WINDOW = 128
RMS_EPS = 1e-6

LANES = 128
FFN_CHUNK = 256
N_FFN_CHUNKS = D_FF // FFN_CHUNK
FFN_STAGE_SLOTS = 4
TOKEN_TILE = 512
MLSTM_CHUNK = 256
ZM_WIDTH = 4 * M_WIDTH
KV_WIDTH = 2 * A_KV_HEADS * A_HD
VMEM_LIMIT = 56 * 1024 * 1024
MLSTM_CHUNKS_PER_STEP = 8
MLSTM_AUX_ROWS = 16
GATE_I, GATE_F, GATE_B, GATE_X =0, M_HEADS, 2 * M_HEADS, 3 * M_HEADS

NT_DIMS = (((1,), (1,)), ((), ()))


def _rms(x, g):
    ms = jnp.mean(x * x, axis=-1, keepdims=True)
    return x * lax.rsqrt(ms + RMS_EPS) * g


def _log_sigmoid(x):
    return jnp.minimum(x, 0.0) - jnp.log1p(jnp.exp(-jnp.abs(x)))


def _dot(a, b):
    return jnp.dot(a, b, preferred_element_type=F32)


def _dot_nt(a, b):
    return lax.dot_general(a, b, NT_DIMS, preferred_element_type=F32)


def _resident(shape):
    nd = len(shape)
    return pl.BlockSpec(shape, lambda *_: (0,) * nd, pipeline_mode=pl.Buffered(1))


def _ffn_weight_copies(w_hbm, j, f, slot, stage_refs, sems):
    wg_hbm, wu_hbm, wd_hbm = w_hbm
    cols = pl.ds(f * FFN_CHUNK, FFN_CHUNK)
    sources = (wg_hbm.at[0, j, :, cols], wu_hbm.at[0, j, :, cols], wd_hbm.at[0, j, cols, :])
    return [pltpu.make_async_copy(src, stage.at[slot], sems.at[slot, k])
            for k, (src, stage) in enumerate(zip(sources, stage_refs))]


def _swiglu(x, gpre, w_refs, h_ref, acc_ref, fetch=None):
    wg_ref, wu_ref, wd_ref = w_refs
    if fetch is not None:
        for f in range(min(FFN_STAGE_SLOTS, N_FFN_CHUNKS)):
            for copy in _ffn_weight_copies(*fetch[:2], f, f % FFN_STAGE_SLOTS, *fetch[2:]):
                copy.start()
    h_ref[...] = _rms(x, gpre).astype(BF16)
    for f in range(N_FFN_CHUNKS):
        cols = slice(f * FFN_CHUNK, (f + 1) * FFN_CHUNK)
        if fetch is not None:
            slot = f % FFN_STAGE_SLOTS
            stage_g, stage_u, stage_d = fetch[2]
            for copy in _ffn_weight_copies(*fetch[:2], f, slot, *fetch[2:]):
                copy.wait()
            wg_ref[:, cols] = stage_g[slot].astype(BF16)
            wu_ref[:, cols] = stage_u[slot].astype(BF16)
            wd_ref[cols, :] = stage_d[slot].astype(BF16)
            if f + FFN_STAGE_SLOTS < N_FFN_CHUNKS:
                for copy in _ffn_weight_copies(*fetch[:2], f + FFN_STAGE_SLOTS, slot, *fetch[2:]):
                    copy.start()
        h = h_ref[...]
        g = _dot(h, wg_ref[:, cols])
        u = _dot(h, wu_ref[:, cols])
        part = _dot((g * jax.nn.sigmoid(g) * u).astype(BF16), wd_ref[cols, :])
        if f == 0:
            acc_ref[...] = part
        else:
            acc_ref[...] += part


def _first_step_fetches(body, fetch):
    first = pl.program_id(0) == 0

    @pl.when(first)
    def _():
        body(fetch)

    @pl.when(jnp.logical_not(first))
    def _():
        body(None)


def _half_step(x, acc_ref, gpost):
    return x + _rms(acc_ref[...], 0.5 * gpost)


def _project(x, gain, win_refs, bias_ref, h_ref, zm_ref, qa_ref, kva_ref, gates_ref, chunk):
    h_ref[...] = _rms(x, gain).astype(BF16)

    wm_ref, wqa_ref, wkv_ref, wg_ref = win_refs

    def cols(lo, width):
        return _dot_nt(h_ref[...], wm_ref[lo:lo + width, :])

    gates = _dot_nt(h_ref[...], wg_ref[...]) + bias_ref[...]
    lane = lax.broadcasted_iota(jnp.int32, (chunk, LANES), 1)
    pos = lax.broadcasted_iota(jnp.int32, (chunk, LANES), 0)
    for c0 in range(0, gates.shape[0], chunk):
        g = gates[c0:c0 + chunk]
        b = jnp.where((lane >= GATE_F) & (lane < GATE_B), _log_sigmoid(g), 0.0)
        shift = 1
        while shift < chunk:
            b = b + jnp.where(pos >= shift, pltpu.roll(b, shift, 0), 0.0)
            shift *= 2
        x_gate = pltpu.roll(g, GATE_X - GATE_I, 1) - pltpu.roll(b, GATE_X - GATE_F, 1)
        out = jnp.where(lane < GATE_B, g, jnp.where(lane < GATE_X, pltpu.roll(b, GATE_B - GATE_F, 1), x_gate))
        gates_ref[c0:c0 + chunk, :] = jnp.where(lane < GATE_X + M_HEADS, out, 0.0)

    zm_ref[:, 0:M_WIDTH] = cols(0, M_WIDTH).astype(BF16)
    zm_ref[:, M_WIDTH:2 * M_WIDTH] = (cols(M_WIDTH, M_WIDTH) * (M_DK ** -0.5)).astype(BF16)
    zm_ref[:, 2 * M_WIDTH:3 * M_WIDTH] = cols(2 * M_WIDTH, M_WIDTH).astype(BF16)
    zm_ref[:, 3 * M_WIDTH:] = jax.nn.sigmoid(cols(3 * M_WIDTH, M_WIDTH)).astype(BF16)
    qa_ref[...] = (_dot_nt(h_ref[...], wqa_ref[...]) * (A_HD ** -0.5)).astype(BF16)
    kva_ref[...] = _dot_nt(h_ref[...], wkv_ref[...])


def _ffn_proj_kernel(x_ref, gains_ref, wg_hbm, wu_hbm, wd_hbm, wm_ref, wqa_ref, wkv_ref, wgate_ref, bias_ref,
                     x1_ref, zm_ref, qa_ref, kva_ref, gates_ref,
                     h_ref, acc_ref, wg_ref, wu_ref, wd_ref, stage_g, stage_u, stage_d, sems,
                     *, j, pre, post, mixer, chunk):
    def body(fetch):
        x = x_ref[...]
        _swiglu(x, gains_ref[pre:pre + 1, :], (wg_ref, wu_ref, wd_ref), h_ref, acc_ref, fetch)
        x1 = _half_step(x, acc_ref, gains_ref[post:post + 1, :])
        x1_ref[...] = x1
        _project(x1, gains_ref[mixer:mixer + 1, :], (wm_ref, wqa_ref, wkv_ref, wgate_ref), bias_ref, h_ref,
                 zm_ref, qa_ref, kva_ref, gates_ref, chunk)

    _first_step_fetches(body, ((wg_hbm, wu_hbm, wd_hbm), j, (stage_g, stage_u, stage_d), sems))


def _mix_ffn_kernel(x_ref, um_ref, ua_ref, wo_ref, gains_ref, wg_hbm, wu_hbm, wd_hbm, o_ref,
                    h_ref, acc_ref, wg_ref, wu_ref, wd_ref, stage_g, stage_u, stage_d, sems,
                    *, j, mix, pre, post):
    def body(fetch):
        u = jnp.concatenate([um_ref[...], ua_ref[...]], axis=-1)
        x = x_ref[...] + _rms(_dot(u, wo_ref[...]), gains_ref[mix:mix + 1, :])
        _swiglu(x, gains_ref[pre:pre + 1, :], (wg_ref, wu_ref, wd_ref), h_ref, acc_ref, fetch)
        o_ref[...] = _half_step(x, acc_ref, gains_ref[post:post + 1, :])

    _first_step_fetches(body, ((wg_hbm, wu_hbm, wd_hbm), j, (stage_g, stage_u, stage_d), sems))


def _ffn_scratch(tm):
    return [pltpu.VMEM((tm, D_MODEL), BF16),
            pltpu.VMEM((tm, D_MODEL), F32),
            pltpu.VMEM((D_MODEL, D_FF), BF16),
            pltpu.VMEM((D_MODEL, D_FF), BF16),
            pltpu.VMEM((D_FF, D_MODEL), BF16),
            pltpu.VMEM((FFN_STAGE_SLOTS, D_MODEL, FFN_CHUNK), F32),
            pltpu.VMEM((FFN_STAGE_SLOTS, D_MODEL, FFN_CHUNK), F32),
            pltpu.VMEM((FFN_STAGE_SLOTS, FFN_CHUNK, D_MODEL), F32),
            pltpu.SemaphoreType.DMA((FFN_STAGE_SLOTS, 3))]


def _dense_params():
    return pltpu.CompilerParams(dimension_semantics=("arbitrary",), vmem_limit_bytes=VMEM_LIMIT)


_HBM = pl.BlockSpec(memory_space=pl.ANY)


def _ffn_proj(x, gains, wg, wu, wd, win, bias, j, pre, post, mixer, chunk):
    n = x.shape[0]
    tm = min(TOKEN_TILE, n)
    assert tm % chunk == 0

    def rows(width):
        return pl.BlockSpec((tm, width), lambda i: (i, 0))

    return pl.pallas_call(
        functools.partial(_ffn_proj_kernel, j=j, pre=pre, post=post, mixer=mixer, chunk=chunk),
        grid=(n // tm,),
        in_specs=[rows(D_MODEL), _resident(gains.shape), _HBM, _HBM, _HBM,
                  *[_resident(w.shape) for w in win], _resident(bias.shape)],
        out_specs=[rows(D_MODEL), rows(ZM_WIDTH), rows(A_WIDTH), rows(KV_WIDTH), rows(LANES)],
        out_shape=[jax.ShapeDtypeStruct((n, D_MODEL), F32),
                   jax.ShapeDtypeStruct((n, ZM_WIDTH), BF16),
                   jax.ShapeDtypeStruct((n, A_WIDTH), BF16),
                   jax.ShapeDtypeStruct((n, KV_WIDTH), F32),
                   jax.ShapeDtypeStruct((n, LANES), F32)],
        scratch_shapes=_ffn_scratch(tm),
        compiler_params=_dense_params(),
        name="ffn_half_step_input_projection",
    )(x, gains, wg, wu, wd, *win, bias)


def _mix_ffn(x, um, ua, wo, gains, wg, wu, wd, j, mix, pre, post):
    n = x.shape[0]
    tm = min(TOKEN_TILE, n)
    row = pl.BlockSpec((tm, D_MODEL), lambda i: (i, 0))
    half = pl.BlockSpec((tm, M_WIDTH), lambda i: (i, 0))
    return pl.pallas_call(
        functools.partial(_mix_ffn_kernel, j=j, mix=mix, pre=pre, post=post),
        grid=(n // tm,),
        in_specs=[row, half, half, _resident(wo.shape), _resident(gains.shape), _HBM, _HBM, _HBM],
        out_specs=row,
        out_shape=jax.ShapeDtypeStruct((n, D_MODEL), F32),
        scratch_shapes=_ffn_scratch(tm),
        compiler_params=_dense_params(),
        name="out_proj_ffn_half_step",
    )(x, um, ua, wo, gains, wg, wu, wd)


def _head_norm_gate(hh, o, gain):
    hn = hh * lax.rsqrt(jnp.mean(hh * hh, axis=-1, keepdims=True) + RMS_EPS) * gain
    return (o.astype(F32) * hn).astype(BF16)


def _mlstm_chunks(zm_ref, gates_ref, gain_ref, um_ref, cnt_ref, ms_ref, chunk):
    si = lax.broadcasted_iota(jnp.int32, (chunk, chunk), 0)
    ti = lax.broadcasted_iota(jnp.int32, (chunk, chunk), 1)
    causal = si <= ti
    ones_rows = jnp.where(lax.broadcasted_iota(jnp.int32, (MLSTM_AUX_ROWS, chunk), 0) == 0, 1.0, 0.0).astype(BF16)

    def run_chunk(c):
        rows = slice(c * chunk, (c + 1) * chunk)
        gates = gates_ref[rows, :]
        gates_t = gates.T
        for h in range(M_HEADS):
            lanes = slice(h * M_DK, (h + 1) * M_DK)
            q = zm_ref[rows, lanes]
            k = zm_ref[rows, M_WIDTH + h * M_DK:M_WIDTH + (h + 1) * M_DK]
            v = zm_ref[rows, 2 * M_WIDTH + h * M_DV:2 * M_WIDTH + (h + 1) * M_DV]
            o = zm_ref[rows, 3 * M_WIDTH + h * M_DV:3 * M_WIDTH + (h + 1) * M_DV]
            x_col = gates[:, GATE_X + h:GATE_X + h + 1]
            ig_row = gates_t[GATE_I + h:GATE_I + h + 1, :]
            b_row = gates_t[GATE_B + h:GATE_B + h + 1, :]
            m_prev = ms_ref[h][:, 0:1]
            cnt = cnt_ref[h]

            xm = jnp.where(causal, x_col, -jnp.inf)
            mu = jnp.maximum(m_prev, jnp.max(xm, axis=0, keepdims=True))
            a = jnp.exp(m_prev - mu)
            s_t = _dot_nt(k, q) * jnp.exp(xm - mu)
            v_ext = jnp.concatenate([v.astype(F32).T.astype(BF16), ones_rows], axis=0)
            nd = a * _dot_nt(cnt.astype(BF16), q) + _dot(v_ext, s_t.astype(BF16))
            den = nd[M_DV:M_DV + 1, :]
            hh = nd[:M_DV, :] * (1.0 / jnp.maximum(jnp.abs(den), jnp.exp(-(b_row + mu))))
            hn = hh * lax.rsqrt(jnp.mean(hh * hh, axis=0, keepdims=True) + RMS_EPS)
            um_ref[rows, lanes] = (hn.T * gain_ref[:, lanes] * o.astype(F32)).astype(BF16)

            b_last = b_row[:, chunk - 1:chunk]
            g = b_last - b_row + ig_row
            m_new = jnp.maximum(b_last + m_prev, jnp.max(g, axis=-1, keepdims=True))
            decay = jnp.exp(b_last + m_prev - m_new)
            wv = (jnp.exp(g - m_new) * v_ext.astype(F32)).astype(BF16)
            cnt_ref[h] = decay * cnt + _dot(wv, k)
            ms_ref[h] = jnp.broadcast_to(m_new, (1, LANES))

    return [functools.partial(run_chunk, c) for c in range(zm_ref.shape[0] // chunk)]


def _alibi_slope(head):
    return 2.0 ** (-8.0 * (head + 1) / A_HEADS)


def _swa_blocks(sinks_ref, qa_ref, kvc_ref, kvp_ref, ua_ref, mask_ref, first):
    hw = A_KV_HEADS * A_HD
    n_blocks = qa_ref.shape[0] // WINDOW
    qi = lax.broadcasted_iota(jnp.int32, (WINDOW, 2 * WINDOW), 0)
    kj = lax.broadcasted_iota(jnp.int32, (WINDOW, 2 * WINDOW), 1)
    dist = WINDOW + qi - kj
    band = (dist >= 0) & (dist < WINDOW)
    mask_ref[0] = jnp.where(band, 0.0, -jnp.inf)
    mask_ref[1] = jnp.where(band & (kj >= WINDOW), 0.0, -jnp.inf)
    lane = lax.broadcasted_iota(jnp.int32, (WINDOW, LANES), 1)
    low = lane < A_HD

    assert 2 * WINDOW <= 256, "positions must stay exactly representable in bf16"
    q_pos = (WINDOW + lax.broadcasted_iota(jnp.int32, (WINDOW, LANES), 0)).astype(F32)
    k_lane = lax.broadcasted_iota(jnp.int32, (2 * WINDOW, LANES), 1)
    k_pos = lax.broadcasted_iota(jnp.int32, (2 * WINDOW, LANES), 0).astype(F32)
    spare = (A_HD, 0)
    k_bias = [jnp.where(k_lane == spare[kvh], 1.0, jnp.where(k_lane == spare[kvh] + 1, k_pos, 0.0)).astype(BF16)
              for kvh in range(A_KV_HEADS)]
    q_bias = {}
    for head in range(A_HEADS):
        slope = _alibi_slope(head)
        at = spare[head // A_GROUP]
        q_bias[head] = jnp.where(lane == at, -slope * q_pos, jnp.where(lane == at + 1, slope, 0.0)).astype(BF16)

    def run_block(j):
        rows = slice(j * WINDOW, (j + 1) * WINDOW)
        kvc = kvc_ref[rows, :]
        kvp = kvp_ref[...] if j == 0 else kvc_ref[(j - 1) * WINDOW:j * WINDOW, :]
        mask = mask_ref[jnp.where(first, 1, 0)] if j == 0 else mask_ref[0]
        kk = jnp.concatenate([kvp[:, :hw], kvc[:, :hw]], axis=0).astype(BF16)
        kk_biased = [jnp.where(k_lane < A_HD, kk, k_bias[0]), jnp.where(k_lane < A_HD, k_bias[1], kk)]
        vv = jnp.concatenate([kvp[:, hw:], kvc[:, hw:]], axis=0).astype(BF16)
        for p in range(A_GROUP):
            qp = qa_ref[rows, p * LANES:(p + 1) * LANES]
            outs = []
            for kvh in range(A_KV_HEADS):
                head = kvh * A_GROUP + p
                keep = low if kvh == 0 else jnp.logical_not(low)
                s = _dot_nt(jnp.where(keep, qp, q_bias[head]), kk_biased[kvh])
                s = s + mask
                sink = sinks_ref[head]
                mx = jnp.maximum(jnp.max(s, axis=-1, keepdims=True), sink)
                e = jnp.exp(s - mx)
                den = jnp.sum(e, axis=-1, keepdims=True) + jnp.exp(sink - mx)
                outs.append(_dot(e.astype(BF16), vv) / den)
            ua_ref[rows, p * LANES:(p + 1) * LANES] = jnp.where(low, outs[0], outs[1]).astype(BF16)

    return [functools.partial(run_block, j) for j in range(n_blocks)]


def _prompt_mixers_kernel(sinks_ref, zm_ref, gates_ref, gain_ref, qa_ref, kvc_ref, kvp_ref,
                          um_ref, ua_ref, c_ref, n_ref, m_ref, cnt_ref, ms_ref, mask_ref, *, chunk):
    step = pl.program_id(1)

    @pl.when(step == 0)
    def _():
        cnt_ref[...] = jnp.zeros_like(cnt_ref)
        ms_ref[...] = jnp.zeros_like(ms_ref)

    chunks = _mlstm_chunks(zm_ref, gates_ref, gain_ref, um_ref, cnt_ref, ms_ref, chunk)
    blocks = _swa_blocks(sinks_ref, qa_ref, kvc_ref, kvp_ref, ua_ref, mask_ref, step == 0)
    per_chunk = len(blocks) // len(chunks)
    for c, run_chunk in enumerate(chunks):
        run_chunk()
        for run_block in blocks[c * per_chunk:(c + 1) * per_chunk]:
            run_block()

    @pl.when(step == pl.num_programs(1) - 1)
    def _():
        for h in range(M_HEADS):
            c_ref[0, h] = cnt_ref[h][:M_DV, :].T
            n_ref[0, h] = cnt_ref[h][M_DV:M_DV + 1, :]
            m_ref[0, h] = ms_ref[h]


def _prompt_mixers(sinks, zm, gates, gain, qa, kva, batch, seq, chunk):
    n = batch * seq
    per_step = min(MLSTM_CHUNKS_PER_STEP, seq // chunk)
    rows = per_step * chunk
    assert seq % rows == 0 and chunk % WINDOW == 0
    ns = seq // rows

    def step_rows(width):
        return pl.BlockSpec((rows, width), lambda b, i: (b * ns + i, 0))

    def prev_block(b, i):
        return (jnp.maximum((b * ns + i) * (rows // WINDOW) - 1, 0), 0)

    def state(*dims):
        return pl.BlockSpec((1, M_HEADS) + dims, lambda b, i: (b, 0, 0, 0))

    return pl.pallas_call(
        functools.partial(_prompt_mixers_kernel, chunk=chunk),
        grid=(batch, ns),
        in_specs=[pl.BlockSpec(memory_space=pltpu.SMEM), step_rows(ZM_WIDTH), step_rows(LANES),
                  pl.BlockSpec((1, M_WIDTH), lambda b, i: (0, 0)),
                  step_rows(A_WIDTH), step_rows(KV_WIDTH), pl.BlockSpec((WINDOW, KV_WIDTH), prev_block)],
        out_specs=[step_rows(M_WIDTH), step_rows(A_WIDTH), state(M_DK, M_DV), state(1, M_DK), state(1, LANES)],
        out_shape=[jax.ShapeDtypeStruct((n, M_WIDTH), BF16),
                   jax.ShapeDtypeStruct((n, A_WIDTH), BF16),
                   jax.ShapeDtypeStruct((batch, M_HEADS, M_DK, M_DV), F32),
                   jax.ShapeDtypeStruct((batch, M_HEADS, 1, M_DK), F32),
                   jax.ShapeDtypeStruct((batch, M_HEADS, 1, LANES), F32)],
        scratch_shapes=[pltpu.VMEM((M_HEADS, M_DV + MLSTM_AUX_ROWS, M_DK), F32),
                        pltpu.VMEM((M_HEADS, 1, LANES), F32),
                        pltpu.VMEM((2, WINDOW, 2 * WINDOW), F32)],
        compiler_params=pltpu.CompilerParams(dimension_semantics=("parallel", "arbitrary"),
                                             vmem_limit_bytes=VMEM_LIMIT),
        name="prompt_mixers",
    )(sinks, zm, gates, gain, qa, kva, kva)


SAMPLE_ROWS = 128


def _mlstm_sample_kernel(zm_ref, gates_ref, gain_ref, c_ref, n_ref, m_ref,
                         um_ref, co_ref, no_ref, mo_ref,
                         qf_ref, qc_ref, nexp_ref, wkk_ref, dec_ref, *, steps):
    rows = SAMPLE_ROWS
    nb = rows // steps
    row = lax.broadcasted_iota(jnp.int32, (rows, rows), 0)
    col = lax.broadcasted_iota(jnp.int32, (rows, rows), 1)
    same = (row // steps) == (col // steps)
    causal = same & (col <= row)
    eye = row == col
    last = same & ((col % steps) == steps - 1)
    sub = lax.broadcasted_iota(jnp.int32, (2 * steps, M_DV), 0)
    tok_batch = lax.broadcasted_iota(jnp.int32, (M_DK, rows), 1) // steps

    def to_row(x_col):
        return jnp.sum(jnp.where(eye, x_col, 0.0), axis=0, keepdims=True)

    def pick(mask, x_row, fill, reduce):
        return reduce(jnp.where(mask, x_row, fill), axis=-1, keepdims=True)

    gates = gates_ref[...]
    for head in range(M_HEADS):
        lanes = slice(head * M_DK, (head + 1) * M_DK)
        ig_col = gates[:, GATE_I + head:GATE_I + head + 1]
        fg_col = gates[:, GATE_F + head:GATE_F + head + 1]
        lf_row = to_row(_log_sigmoid(fg_col))
        b_col = pick(causal, lf_row, 0.0, jnp.sum)
        b_row = to_row(b_col)
        ig_row = to_row(ig_col)
        m_prev = m_ref[head]

        d = jnp.where(causal, b_col - b_row + ig_row, -jnp.inf)
        inter = b_col + m_prev
        m_t = jnp.maximum(inter, jnp.max(d, axis=-1, keepdims=True))
        w = jnp.exp(d - m_t)
        a = jnp.exp(inter - m_t)

        q = zm_ref[:, lanes]
        k = zm_ref[:, M_WIDTH + head * M_DK:M_WIDTH + (head + 1) * M_DK]
        v = zm_ref[:, 2 * M_WIDTH + head * M_DV:2 * M_WIDTH + (head + 1) * M_DV]
        o = zm_ref[:, 3 * M_WIDTH + head * M_DV:3 * M_WIDTH + (head + 1) * M_DV]
        qf_ref[head] = q.astype(F32)
        for t in range(steps):
            nexp_ref[head, pl.ds(t, nb, stride=steps), :] = n_ref[:, lanes]

        for j in range(nb // 2):
            pair = pl.ds(j * 2 * steps, 2 * steps)
            q8 = qf_ref[head, pair, :].astype(BF16)
            r0 = _dot(q8, c_ref[2 * j, head].astype(BF16))
            r1 = _dot(q8, c_ref[2 * j + 1, head].astype(BF16))
            qc_ref[head, pair, :] = jnp.where(sub < steps, r0, r1)

        s = _dot_nt(q, k) * w
        qn = jnp.sum(qf_ref[head] * nexp_ref[head], axis=-1, keepdims=True)
        num = a * qc_ref[head] + _dot(s.astype(BF16), v)
        den = a * qn + jnp.sum(s, axis=-1, keepdims=True)
        hh = num / jnp.maximum(jnp.abs(den), jnp.exp(-m_t))
        um_ref[:, lanes] = _head_norm_gate(hh, o, gain_ref[:, lanes])

        b_last = pick(last, b_row, 0.0, jnp.sum)
        g_col = b_last - b_col + ig_col
        g_max = pick(same, to_row(g_col), -jnp.inf, jnp.max)
        m_new = jnp.maximum(b_last + m_prev, g_max)
        decay = jnp.exp(b_last + m_prev - m_new)
        wk = jnp.exp(g_col - m_new)
        mo_ref[head] = m_new

        kf = k.astype(F32)
        wkk_ref[head] = wk * kf
        dec_ref[head] = jnp.broadcast_to(decay, (rows, M_DV))
        n_sum = wkk_ref[head, pl.ds(0, nb, stride=steps), :]
        for t in range(1, steps):
            n_sum = n_sum + wkk_ref[head, pl.ds(t, nb, stride=steps), :]
        no_ref[:, lanes] = dec_ref[head, pl.ds(0, nb, stride=steps), :] * n_ref[:, lanes] + n_sum

        k_t = kf.T
        wv = (wk * v.astype(F32)).astype(BF16)
        for b in range(nb):
            upd = _dot(jnp.where(tok_batch == b, k_t, 0.0).astype(BF16), wv)
            co_ref[b, head] = dec_ref[head, pl.ds(b * steps, 1), :] * c_ref[b, head] + upd


def _swa_sample_kernel(sinks_ref, qa_ref, kva_ref, ckt_ref, cvt_ref, ua_ref, nkt_ref, nvt_ref,
                       s_ref, e_ref, pv_ref, *, steps):
    rows = SAMPLE_ROWS
    nb = rows // steps
    pair_rows = 2 * steps
    hw = A_KV_HEADS * A_HD
    tok = lax.broadcasted_iota(jnp.int32, (rows, LANES), 0)
    lane = lax.broadcasted_iota(jnp.int32, (rows, LANES), 1)
    low = lane < A_HD

    qf = qa_ref[...].astype(F32)
    q_blocks = []
    for p in range(A_GROUP):
        qp = qf[:, p * LANES:(p + 1) * LANES]
        q_blocks.append(jnp.where(low, qp, 0.0))
        q_blocks.append(jnp.where(low, 0.0, qp))
    n_blk = len(q_blocks)

    kv_new = kva_ref[...]
    kv_new_t = kv_new.T
    k_new_t = kv_new_t[:hw]
    v_new_t = kv_new_t[hw:]
    v_new = kv_new[:, hw:].astype(BF16)
    s_new = _dot(jnp.concatenate(q_blocks, axis=0).astype(BF16), k_new_t.astype(BF16))

    def pair_rows_of(blocks, j):
        return jnp.concatenate([blk[j * pair_rows:(j + 1) * pair_rows] for blk in blocks], axis=0)

    first = (lax.broadcasted_iota(jnp.int32, (n_blk * pair_rows, LANES), 0) % pair_rows) < steps

    for j in range(nb // 2):
        q_pair = pair_rows_of(q_blocks, j).astype(BF16)
        s_pair = jnp.where(first, _dot(q_pair, ckt_ref[2 * j].astype(BF16)),
                           _dot(q_pair, ckt_ref[2 * j + 1].astype(BF16)))
        for blk in range(n_blk):
            s_ref[pl.ds(blk * rows + j * pair_rows, pair_rows), :] = s_pair[blk * pair_rows:(blk + 1) * pair_rows]

    t_q = tok % steps
    dist_c = WINDOW + t_q - lane
    valid_c = dist_c < WINDOW
    dist_n = t_q - lane % steps
    valid_n = ((tok // steps) == (lane // steps)) & (dist_n >= 0)
    dist_cf = dist_c.astype(F32)
    dist_nf = dist_n.astype(F32)
    pv_new = []
    inv_den = []
    for blk in range(n_blk):
        head = (blk % A_KV_HEADS) * A_GROUP + blk // A_KV_HEADS
        slope = _alibi_slope(head)
        sink = sinks_ref[head]
        r = pl.ds(blk * rows, rows)
        s_c = jnp.where(valid_c, s_ref[r, :] - slope * dist_cf, -jnp.inf)
        s_n = jnp.where(valid_n, s_new[blk * rows:(blk + 1) * rows] - slope * dist_nf, -jnp.inf)
        mx = jnp.maximum(jnp.maximum(jnp.max(s_c, axis=-1, keepdims=True),
                                     jnp.max(s_n, axis=-1, keepdims=True)), sink)
        e_c = jnp.exp(s_c - mx)
        e_n = jnp.exp(s_n - mx)
        den = jnp.sum(e_c, axis=-1, keepdims=True) + jnp.sum(e_n, axis=-1, keepdims=True) + jnp.exp(sink - mx)
        e_ref[r, :] = e_c
        pv_new.append(_dot(e_n.astype(BF16), v_new))
        inv_den.append(1.0 / den)

    for j in range(nb // 2):
        e_pair = jnp.concatenate([e_ref[pl.ds(blk * rows + j * pair_rows, pair_rows), :] for blk in range(n_blk)],
                                 axis=0).astype(BF16)
        pv_pair = jnp.where(first, _dot_nt(e_pair, cvt_ref[2 * j].astype(BF16)),
                            _dot_nt(e_pair, cvt_ref[2 * j + 1].astype(BF16)))
        for blk in range(n_blk):
            pv_ref[pl.ds(blk * rows + j * pair_rows, pair_rows), :] = pv_pair[blk * pair_rows:(blk + 1) * pair_rows]

    for p in range(A_GROUP):
        halves = []
        for kvh in range(A_KV_HEADS):
            blk = p * A_KV_HEADS + kvh
            halves.append((pv_ref[pl.ds(blk * rows, rows), :] + pv_new[blk]) * inv_den[blk])
        ua_ref[:, p * LANES:(p + 1) * LANES] = jnp.where(low, halves[0], halves[1]).astype(BF16)

    tail = lane >= WINDOW - steps
    for b in range(nb):
        shift = (WINDOW - steps - b * steps) % rows
        nkt_ref[b] = jnp.where(tail, pltpu.roll(k_new_t, shift, 1), pltpu.roll(ckt_ref[b], WINDOW - steps, 1))
        nvt_ref[b] = jnp.where(tail, pltpu.roll(v_new_t, shift, 1), pltpu.roll(cvt_ref[b], WINDOW - steps, 1))


def _sample_mixers_kernel(sinks_ref, zm_ref, gates_ref, gain_ref, c_ref, n_ref, m_ref, qa_ref, kva_ref,
                          ckt_ref, cvt_ref, um_ref, co_ref, no_ref, mo_ref, ua_ref, nkt_ref, nvt_ref,
                          qf_ref, qc_ref, nexp_ref, wkk_ref, dec_ref, s_ref, e_ref, pv_ref, *, steps):
    _mlstm_sample_kernel(zm_ref, gates_ref, gain_ref, c_ref, n_ref, m_ref, um_ref, co_ref, no_ref, mo_ref,
                         qf_ref, qc_ref, nexp_ref, wkk_ref, dec_ref, steps=steps)
    _swa_sample_kernel(sinks_ref, qa_ref, kva_ref, ckt_ref, cvt_ref, ua_ref, nkt_ref, nvt_ref,
                       s_ref, e_ref, pv_ref, steps=steps)


def _sample_mixers(sinks, zm, gates, gain, c, n2, m_rows, qa, kva, ckt, cvt, steps):
    rows = zm.shape[0]
    assert ckt.shape[1:] == (A_KV_HEADS * A_HD, WINDOW) and SAMPLE_ROWS == WINDOW == LANES
    nb = SAMPLE_ROWS // steps

    def tokens(width):
        return pl.BlockSpec((SAMPLE_ROWS, width), lambda g: (g, 0))

    c_spec = pl.BlockSpec((nb, M_HEADS, M_DK, M_DV), lambda g: (g, 0, 0, 0))
    n_spec = pl.BlockSpec((nb, M_HEADS * M_DK), lambda g: (g, 0))
    m_spec = pl.BlockSpec((M_HEADS, SAMPLE_ROWS, 1), lambda g: (0, g, 0))
    cache = pl.BlockSpec((nb, LANES, WINDOW), lambda g: (g, 0, 0))
    per_head = pltpu.VMEM((M_HEADS, SAMPLE_ROWS, M_DK), F32)
    stacked = pltpu.VMEM((A_HEADS * SAMPLE_ROWS, LANES), F32)
    return pl.pallas_call(
        functools.partial(_sample_mixers_kernel, steps=steps),
        grid=(rows // SAMPLE_ROWS,),
        in_specs=[pl.BlockSpec(memory_space=pltpu.SMEM), tokens(ZM_WIDTH), tokens(LANES),
                  pl.BlockSpec((1, M_WIDTH), lambda g: (0, 0)), c_spec, n_spec, m_spec,
                  tokens(A_WIDTH), tokens(KV_WIDTH), cache, cache],
        out_specs=[tokens(M_WIDTH), c_spec, n_spec, m_spec, tokens(A_WIDTH), cache, cache],
        out_shape=[jax.ShapeDtypeStruct((rows, M_WIDTH), BF16),
                   jax.ShapeDtypeStruct(c.shape, F32),
                   jax.ShapeDtypeStruct(n2.shape, F32),
                   jax.ShapeDtypeStruct(m_rows.shape, F32),
                   jax.ShapeDtypeStruct((rows, A_WIDTH), BF16),
                   jax.ShapeDtypeStruct(ckt.shape, F32),
                   jax.ShapeDtypeStruct(cvt.shape, F32)],
        scratch_shapes=[per_head,
                        per_head,
                        per_head,
                        per_head,
                        per_head,
                        stacked,
                        stacked,
                        stacked],
        compiler_params=pltpu.CompilerParams(dimension_semantics=("parallel",),
                                             vmem_limit_bytes=VMEM_LIMIT),
        name="sample_mixers",
    )(sinks, zm, gates, gain, c, n2, m_rows, qa, kva, ckt, cvt)


def _prep_proj(w_in, b_gate):
    m_end = 4 * M_WIDTH
    g_end = m_end + 2 * M_HEADS
    wt = w_in.T
    qa = wt[g_end:g_end + A_WIDTH].reshape(A_KV_HEADS, A_GROUP, A_HD, D_MODEL).transpose(1, 0, 2, 3)
    gates = jnp.pad(wt[m_end:g_end], ((0, LANES - 2 * M_HEADS), (0, 0)))
    win = tuple(w.astype(BF16) for w in (wt[:m_end], qa.reshape(A_WIDTH, D_MODEL), wt[g_end + A_WIDTH:], gates))
    bias = jnp.pad(b_gate, (0, LANES - 2 * M_HEADS)).reshape(1, LANES)
    return win, bias


def _prep_out(w_out):
    wa = w_out[M_WIDTH:].reshape(A_KV_HEADS, A_GROUP, A_HD, D_MODEL).transpose(1, 0, 2, 3)
    return jnp.concatenate([w_out[:M_WIDTH], wa.reshape(A_WIDTH, D_MODEL)], axis=0).astype(BF16)


def kernel(x_prompt, x_sample, cache_swa_k, cache_swa_v, state_mlstm_C, state_mlstm_n, state_mlstm_m,
           norm_gains, ffn_w_gate, ffn_w_up, ffn_w_down, w_in, b_gate, mlstm_norm_gain, attn_sinks, w_out):
    assert norm_gains.shape[0] == 1, "single layer"
    batch, seq, _ = x_prompt.shape
    dec_batch, steps, _ = x_sample.shape
    gains = norm_gains[0]
    ffn = (ffn_w_gate, ffn_w_up, ffn_w_down)
    win, bias = _prep_proj(w_in[0], b_gate[0])
    wo = _prep_out(w_out[0])
    gain = mlstm_norm_gain[0].reshape(1, M_WIDTH)
    sinks = attn_sinks[0]
    kv_lanes = A_KV_HEADS * A_HD

    chunk = min(MLSTM_CHUNK, seq)
    assert seq % chunk == 0
    xp, zm, qa, kva, gates = _ffn_proj(x_prompt.reshape(batch * seq, D_MODEL), gains, *ffn, win, bias,
                                       0, 0, 1, 2, chunk)
    um, ua, p_c, p_n, p_m = _prompt_mixers(sinks, zm, gates, gain, qa, kva, batch, seq, chunk)
    yp = _mix_ffn(xp, um, ua, wo, gains, *ffn, 1, 3, 4, 5).reshape(batch, seq, D_MODEL)
    keep = min(WINDOW, seq)
    kv_tail = kva.reshape(batch, seq, KV_WIDTH)[:, seq - keep:]
    p_k = kv_tail[..., :kv_lanes].reshape(1, batch, keep, A_KV_HEADS, A_HD)
    p_v = kv_tail[..., kv_lanes:].reshape(1, batch, keep, A_KV_HEADS, A_HD)
    p_c = p_c[None]
    p_n = p_n[None, :, :, 0, :]
    p_m = p_m[None, :, :, 0, 0]

    xs, zm_s, qa_s, kva_s, gates_s = _ffn_proj(x_sample.reshape(dec_batch * steps, D_MODEL), gains, *ffn,
                                               win, bias, 0, 0, 1, 2, min(chunk, dec_batch * steps))
    m_rows = jnp.repeat(state_mlstm_m[0].T, steps, axis=1)[..., None]
    wc = cache_swa_k.shape[2]
    um_s, s_c, s_n2, s_m_rows, ua_s, s_kt, s_vt = _sample_mixers(
        sinks, zm_s, gates_s, gain, state_mlstm_C[0], state_mlstm_n[0].reshape(dec_batch, M_HEADS * M_DK), m_rows,
        qa_s, kva_s, cache_swa_k[0].reshape(dec_batch, wc, kv_lanes).transpose(0, 2, 1),
        cache_swa_v[0].reshape(dec_batch, wc, kv_lanes).transpose(0, 2, 1), steps)
    ys = _mix_ffn(xs, um_s, ua_s, wo, gains, *ffn, 1, 3, 4, 5).reshape(dec_batch, steps, D_MODEL)
    s_n = s_n2.reshape(1, dec_batch, M_HEADS, M_DK)
    s_m = s_m_rows[:, ::steps, 0].T[None]
    s_k = s_kt.transpose(0, 2, 1).reshape(1, dec_batch, wc, A_KV_HEADS, A_HD)
    s_v = s_vt.transpose(0, 2, 1).reshape(1, dec_batch, wc, A_KV_HEADS, A_HD)
    return (yp, ys, p_k, p_v, p_c, p_n, p_m, s_k, s_v, s_c[None], s_n, s_m)
```

```python
import functools

import jax
import jax.numpy as jnp
from jax import lax
from jax.experimental import pallas as pl
from jax.experimental.pallas import tpu as pltpu

F32 = jnp.float32
BF16 = jnp.bfloat16

D_MODEL = 1024
D_FF = 2816
M_HEADS = 4
M_DK = 128
M_DV = 128
M_WIDTH = M_HEADS * M_DV
A_HEADS = 8
A_KV_HEADS = 2
A_GROUP = A_HEADS // A_KV_HEADS
A_HD = 64
A_WIDTH = A_HEADS * A_HD
WINDOW = 128
RMS_EPS = 1e-6

LANES = 128
FFN_CHUNK = 256
N_FFN_CHUNKS = D_FF // FFN_CHUNK
FFN_STAGE_SLOTS = 4
TOKEN_TILE = 512
MLSTM_CHUNK = 256
ZM_WIDTH = 4 * M_WIDTH
KV_WIDTH = 2 * A_KV_HEADS * A_HD
VMEM_LIMIT = 56 * 1024 * 1024
MLSTM_CHUNKS_PER_STEP = 8
MLSTM_AUX_ROWS = 16
GATE_I, GATE_F, GATE_B, GATE_X =0, M_HEADS, 2 * M_HEADS, 3 * M_HEADS

NT_DIMS = (((1,), (1,)), ((), ()))


def _rms(x, g):
    ms = jnp.mean(x * x, axis=-1, keepdims=True)
    return x * lax.rsqrt(ms + RMS_EPS) * g


def _log_sigmoid(x):
    return jnp.minimum(x, 0.0) - jnp.log1p(jnp.exp(-jnp.abs(x)))


def _dot(a, b):
    return jnp.dot(a, b, preferred_element_type=F32)


def _dot_nt(a, b):
    return lax.dot_general(a, b, NT_DIMS, preferred_element_type=F32)


def _resident(shape):
    nd = len(shape)
    return pl.BlockSpec(shape, lambda *_: (0,) * nd, pipeline_mode=pl.Buffered(1))


def _ffn_weight_copies(w_hbm, j, f, slot, stage_refs, sems):
    wg_hbm, wu_hbm, wd_hbm = w_hbm
    cols = pl.ds(f * FFN_CHUNK, FFN_CHUNK)
    sources = (wg_hbm.at[0, j, :, cols], wu_hbm.at[0, j, :, cols], wd_hbm.at[0, j, cols, :])
    return [pltpu.make_async_copy(src, stage.at[slot], sems.at[slot, k])
            for k, (src, stage) in enumerate(zip(sources, stage_refs))]


def _swiglu(x, gpre, w_refs, h_ref, acc_ref, fetch=None):
    wg_ref, wu_ref, wd_ref = w_refs
    if fetch is not None:
        for f in range(min(FFN_STAGE_SLOTS, N_FFN_CHUNKS)):
            for copy in _ffn_weight_copies(*fetch[:2], f, f % FFN_STAGE_SLOTS, *fetch[2:]):
                copy.start()
    h_ref[...] = _rms(x, gpre).astype(BF16)
    for f in range(N_FFN_CHUNKS):
        cols = slice(f * FFN_CHUNK, (f + 1) * FFN_CHUNK)
        if fetch is not None:
            slot = f % FFN_STAGE_SLOTS
            stage_g, stage_u, stage_d = fetch[2]
            for copy in _ffn_weight_copies(*fetch[:2], f, slot, *fetch[2:]):
                copy.wait()
            wg_ref[:, cols] = stage_g[slot].astype(BF16)
            wu_ref[:, cols] = stage_u[slot].astype(BF16)
            wd_ref[cols, :] = stage_d[slot].astype(BF16)
            if f + FFN_STAGE_SLOTS < N_FFN_CHUNKS:
                for copy in _ffn_weight_copies(*fetch[:2], f + FFN_STAGE_SLOTS, slot, *fetch[2:]):
                    copy.start()
        h = h_ref[...]
        g = _dot(h, wg_ref[:, cols])
        u = _dot(h, wu_ref[:, cols])
        part = _dot((g * jax.nn.sigmoid(g) * u).astype(BF16), wd_ref[cols, :])
        if f == 0:
            acc_ref[...] = part
        else:
            acc_ref[...] += part


def _first_step_fetches(body, fetch):
    first = pl.program_id(0) == 0

    @pl.when(first)
    def _():
        body(fetch)

    @pl.when(jnp.logical_not(first))
    def _():
        body(None)


def _half_step(x, acc_ref, gpost):
    return x + _rms(acc_ref[...], 0.5 * gpost)


def _project(x, gain, win_refs, bias_ref, h_ref, zm_ref, qa_ref, kva_ref, gates_ref, chunk):
    h_ref[...] = _rms(x, gain).astype(BF16)

    wm_ref, wqa_ref, wkv_ref, wg_ref = win_refs

    def cols(lo, width):
        return _dot_nt(h_ref[...], wm_ref[lo:lo + width, :])

    gates = _dot_nt(h_ref[...], wg_ref[...]) + bias_ref[...]
    lane = lax.broadcasted_iota(jnp.int32, (chunk, LANES), 1)
    pos = lax.broadcasted_iota(jnp.int32, (chunk, LANES), 0)
    for c0 in range(0, gates.shape[0], chunk):
        g = gates[c0:c0 + chunk]
        b = jnp.where((lane >= GATE_F) & (lane < GATE_B), _log_sigmoid(g), 0.0)
        shift = 1
        while shift < chunk:
            b = b + jnp.where(pos >= shift, pltpu.roll(b, shift, 0), 0.0)
            shift *= 2
        x_gate = pltpu.roll(g, GATE_X - GATE_I, 1) - pltpu.roll(b, GATE_X - GATE_F, 1)
        out = jnp.where(lane < GATE_B, g, jnp.where(lane < GATE_X, pltpu.roll(b, GATE_B - GATE_F, 1), x_gate))
        gates_ref[c0:c0 + chunk, :] = jnp.where(lane < GATE_X + M_HEADS, out, 0.0)

    zm_ref[:, 0:M_WIDTH] = cols(0, M_WIDTH).astype(BF16)
    zm_ref[:, M_WIDTH:2 * M_WIDTH] = (cols(M_WIDTH, M_WIDTH) * (M_DK ** -0.5)).astype(BF16)
    zm_ref[:, 2 * M_WIDTH:3 * M_WIDTH] = cols(2 * M_WIDTH, M_WIDTH).astype(BF16)
    zm_ref[:, 3 * M_WIDTH:] = jax.nn.sigmoid(cols(3 * M_WIDTH, M_WIDTH)).astype(BF16)
    qa_ref[...] = (_dot_nt(h_ref[...], wqa_ref[...]) * (A_HD ** -0.5)).astype(BF16)
    kva_ref[...] = _dot_nt(h_ref[...], wkv_ref[...])


def _ffn_proj_kernel(x_ref, gains_ref, wg_hbm, wu_hbm, wd_hbm, wm_ref, wqa_ref, wkv_ref, wgate_ref, bias_ref,
                     x1_ref, zm_ref, qa_ref, kva_ref, gates_ref,
                     h_ref, acc_ref, wg_ref, wu_ref, wd_ref, stage_g, stage_u, stage_d, sems,
                     *, j, pre, post, mixer, chunk):
    def body(fetch):
        x = x_ref[...]
        _swiglu(x, gains_ref[pre:pre + 1, :], (wg_ref, wu_ref, wd_ref), h_ref, acc_ref, fetch)
        x1 = _half_step(x, acc_ref, gains_ref[post:post + 1, :])
        x1_ref[...] = x1
        _project(x1, gains_ref[mixer:mixer + 1, :], (wm_ref, wqa_ref, wkv_ref, wgate_ref), bias_ref, h_ref,
                 zm_ref, qa_ref, kva_ref, gates_ref, chunk)

    _first_step_fetches(body, ((wg_hbm, wu_hbm, wd_hbm), j, (stage_g, stage_u, stage_d), sems))


def _mix_ffn_kernel(x_ref, um_ref, ua_ref, wo_ref, gains_ref, wg_hbm, wu_hbm, wd_hbm, o_ref,
                    h_ref, acc_ref, wg_ref, wu_ref, wd_ref, stage_g, stage_u, stage_d, sems,
                    *, j, mix, pre, post):
    def body(fetch):
        u = jnp.concatenate([um_ref[...], ua_ref[...]], axis=-1)
        x = x_ref[...] + _rms(_dot(u, wo_ref[...]), gains_ref[mix:mix + 1, :])
        _swiglu(x, gains_ref[pre:pre + 1, :], (wg_ref, wu_ref, wd_ref), h_ref, acc_ref, fetch)
        o_ref[...] = _half_step(x, acc_ref, gains_ref[post:post + 1, :])

    _first_step_fetches(body, ((wg_hbm, wu_hbm, wd_hbm), j, (stage_g, stage_u, stage_d), sems))


def _ffn_scratch(tm):
    return [pltpu.VMEM((tm, D_MODEL), BF16),
            pltpu.VMEM((tm, D_MODEL), F32),
            pltpu.VMEM((D_MODEL, D_FF), BF16),
            pltpu.VMEM((D_MODEL, D_FF), BF16),
            pltpu.VMEM((D_FF, D_MODEL), BF16),
            pltpu.VMEM((FFN_STAGE_SLOTS, D_MODEL, FFN_CHUNK), F32),
            pltpu.VMEM((FFN_STAGE_SLOTS, D_MODEL, FFN_CHUNK), F32),
            pltpu.VMEM((FFN_STAGE_SLOTS, FFN_CHUNK, D_MODEL), F32),
            pltpu.SemaphoreType.DMA((FFN_STAGE_SLOTS, 3))]


def _dense_params():
    return pltpu.CompilerParams(dimension_semantics=("arbitrary",), vmem_limit_bytes=VMEM_LIMIT)


_HBM = pl.BlockSpec(memory_space=pl.ANY)


def _ffn_proj(x, gains, wg, wu, wd, win, bias, j, pre, post, mixer, chunk):
    n = x.shape[0]
    tm = min(TOKEN_TILE, n)
    assert tm % chunk == 0

    def rows(width):
        return pl.BlockSpec((tm, width), lambda i: (i, 0))

    return pl.pallas_call(
        functools.partial(_ffn_proj_kernel, j=j, pre=pre, post=post, mixer=mixer, chunk=chunk),
        grid=(n // tm,),
        in_specs=[rows(D_MODEL), _resident(gains.shape), _HBM, _HBM, _HBM,
                  *[_resident(w.shape) for w in win], _resident(bias.shape)],
        out_specs=[rows(D_MODEL), rows(ZM_WIDTH), rows(A_WIDTH), rows(KV_WIDTH), rows(LANES)],
        out_shape=[jax.ShapeDtypeStruct((n, D_MODEL), F32),
                   jax.ShapeDtypeStruct((n, ZM_WIDTH), BF16),
                   jax.ShapeDtypeStruct((n, A_WIDTH), BF16),
                   jax.ShapeDtypeStruct((n, KV_WIDTH), F32),
                   jax.ShapeDtypeStruct((n, LANES), F32)],
        scratch_shapes=_ffn_scratch(tm),
        compiler_params=_dense_params(),
        name="ffn_half_step_input_projection",
    )(x, gains, wg, wu, wd, *win, bias)


def _mix_ffn(x, um, ua, wo, gains, wg, wu, wd, j, mix, pre, post):
    n = x.shape[0]
    tm = min(TOKEN_TILE, n)
    row = pl.BlockSpec((tm, D_MODEL), lambda i: (i, 0))
    half = pl.BlockSpec((tm, M_WIDTH), lambda i: (i, 0))
    return pl.pallas_call(
        functools.partial(_mix_ffn_kernel, j=j, mix=mix, pre=pre, post=post),
        grid=(n // tm,),
        in_specs=[row, half, half, _resident(wo.shape), _resident(gains.shape), _HBM, _HBM, _HBM],
        out_specs=row,
        out_shape=jax.ShapeDtypeStruct((n, D_MODEL), F32),
        scratch_shapes=_ffn_scratch(tm),
        compiler_params=_dense_params(),
        name="out_proj_ffn_half_step",
    )(x, um, ua, wo, gains, wg, wu, wd)


def _head_norm_gate(hh, o, gain):
    hn = hh * lax.rsqrt(jnp.mean(hh * hh, axis=-1, keepdims=True) + RMS_EPS) * gain
    return (o.astype(F32) * hn).astype(BF16)


def _mlstm_chunks(zm_ref, gates_ref, gain_ref, um_ref, cnt_ref, ms_ref, chunk):
    si = lax.broadcasted_iota(jnp.int32, (chunk, chunk), 0)
    ti = lax.broadcasted_iota(jnp.int32, (chunk, chunk), 1)
    causal = si <= ti
    ones_rows = jnp.where(lax.broadcasted_iota(jnp.int32, (MLSTM_AUX_ROWS, chunk), 0) == 0, 1.0, 0.0).astype(BF16)

    def run_chunk(c):
        rows = slice(c * chunk, (c + 1) * chunk)
        gates = gates_ref[rows, :]
        gates_t = gates.T
        for h in range(M_HEADS):
            lanes = slice(h * M_DK, (h + 1) * M_DK)
            q = zm_ref[rows, lanes]
            k = zm_ref[rows, M_WIDTH + h * M_DK:M_WIDTH + (h + 1) * M_DK]
            v = zm_ref[rows, 2 * M_WIDTH + h * M_DV:2 * M_WIDTH + (h + 1) * M_DV]
            o = zm_ref[rows, 3 * M_WIDTH + h * M_DV:3 * M_WIDTH + (h + 1) * M_DV]
            x_col = gates[:, GATE_X + h:GATE_X + h + 1]
            ig_row = gates_t[GATE_I + h:GATE_I + h + 1, :]
            b_row = gates_t[GATE_B + h:GATE_B + h + 1, :]
            m_prev = ms_ref[h][:, 0:1]
            cnt = cnt_ref[h]

            xm = jnp.where(causal, x_col, -jnp.inf)
            mu = jnp.maximum(m_prev, jnp.max(xm, axis=0, keepdims=True))
            a = jnp.exp(m_prev - mu)
            s_t = _dot_nt(k, q) * jnp.exp(xm - mu)
            v_ext = jnp.concatenate([v.astype(F32).T.astype(BF16), ones_rows], axis=0)
            nd = a * _dot_nt(cnt.astype(BF16), q) + _dot(v_ext, s_t.astype(BF16))
            den = nd[M_DV:M_DV + 1, :]
            hh = nd[:M_DV, :] * (1.0 / jnp.maximum(jnp.abs(den), jnp.exp(-(b_row + mu))))
            hn = hh * lax.rsqrt(jnp.mean(hh * hh, axis=0, keepdims=True) + RMS_EPS)
            um_ref[rows, lanes] = (hn.T * gain_ref[:, lanes] * o.astype(F32)).astype(BF16)

            b_last = b_row[:, chunk - 1:chunk]
            g = b_last - b_row + ig_row
            m_new = jnp.maximum(b_last + m_prev, jnp.max(g, axis=-1, keepdims=True))
            decay = jnp.exp(b_last + m_prev - m_new)
            wv = (jnp.exp(g - m_new) * v_ext.astype(F32)).astype(BF16)
            cnt_ref[h] = decay * cnt + _dot(wv, k)
            ms_ref[h] = jnp.broadcast_to(m_new, (1, LANES))

    return [functools.partial(run_chunk, c) for c in range(zm_ref.shape[0] // chunk)]


def _alibi_slope(head):
    return 2.0 ** (-8.0 * (head + 1) / A_HEADS)


def _swa_blocks(sinks_ref, qa_ref, kvc_ref, kvp_ref, ua_ref, mask_ref, first):
    hw = A_KV_HEADS * A_HD
    n_blocks = qa_ref.shape[0] // WINDOW
    qi = lax.broadcasted_iota(jnp.int32, (WINDOW, 2 * WINDOW), 0)
    kj = lax.broadcasted_iota(jnp.int32, (WINDOW, 2 * WINDOW), 1)
    dist = WINDOW + qi - kj
    band = (dist >= 0) & (dist < WINDOW)
    mask_ref[0] = jnp.where(band, 0.0, -jnp.inf)
    mask_ref[1] = jnp.where(band & (kj >= WINDOW), 0.0, -jnp.inf)
    lane = lax.broadcasted_iota(jnp.int32, (WINDOW, LANES), 1)
    low = lane < A_HD

    assert 2 * WINDOW <= 256, "positions must stay exactly representable in bf16"
    q_pos = (WINDOW + lax.broadcasted_iota(jnp.int32, (WINDOW, LANES), 0)).astype(F32)
    k_lane = lax.broadcasted_iota(jnp.int32, (2 * WINDOW, LANES), 1)
    k_pos = lax.broadcasted_iota(jnp.int32, (2 * WINDOW, LANES), 0).astype(F32)
    spare = (A_HD, 0)
    k_bias = [jnp.where(k_lane == spare[kvh], 1.0, jnp.where(k_lane == spare[kvh] + 1, k_pos, 0.0)).astype(BF16)
              for kvh in range(A_KV_HEADS)]
    q_bias = {}
    for head in range(A_HEADS):
        slope = _alibi_slope(head)
        at = spare[head // A_GROUP]
        q_bias[head] = jnp.where(lane == at, -slope * q_pos, jnp.where(lane == at + 1, slope, 0.0)).astype(BF16)

    def run_block(j):
        rows = slice(j * WINDOW, (j + 1) * WINDOW)
        kvc = kvc_ref[rows, :]
        kvp = kvp_ref[...] if j == 0 else kvc_ref[(j - 1) * WINDOW:j * WINDOW, :]
        mask = mask_ref[jnp.where(first, 1, 0)] if j == 0 else mask_ref[0]
        kk = jnp.concatenate([kvp[:, :hw], kvc[:, :hw]], axis=0).astype(BF16)
        kk_biased = [jnp.where(k_lane < A_HD, kk, k_bias[0]), jnp.where(k_lane < A_HD, k_bias[1], kk)]
        vv = jnp.concatenate([kvp[:, hw:], kvc[:, hw:]], axis=0).astype(BF16)
        for p in range(A_GROUP):
            qp = qa_ref[rows, p * LANES:(p + 1) * LANES]
            outs = []
            for kvh in range(A_KV_HEADS):
                head = kvh * A_GROUP + p
                keep = low if kvh == 0 else jnp.logical_not(low)
                s = _dot_nt(jnp.where(keep, qp, q_bias[head]), kk_biased[kvh])
                s = s + mask
                sink = sinks_ref[head]
                mx = jnp.maximum(jnp.max(s, axis=-1, keepdims=True), sink)
                e = jnp.exp(s - mx)
                den = jnp.sum(e, axis=-1, keepdims=True) + jnp.exp(sink - mx)
                outs.append(_dot(e.astype(BF16), vv) * (1.0 / den))
            ua_ref[rows, p * LANES:(p + 1) * LANES] = jnp.where(low, outs[0], outs[1]).astype(BF16)

    return [functools.partial(run_block, j) for j in range(n_blocks)]


def _prompt_mixers_kernel(sinks_ref, zm_ref, gates_ref, gain_ref, qa_ref, kvc_ref, kvp_ref,
                          um_ref, ua_ref, c_ref, n_ref, m_ref, cnt_ref, ms_ref, mask_ref, *, chunk):
    step = pl.program_id(1)

    @pl.when(step == 0)
    def _():
        cnt_ref[...] = jnp.zeros_like(cnt_ref)
        ms_ref[...] = jnp.zeros_like(ms_ref)

    chunks = _mlstm_chunks(zm_ref, gates_ref, gain_ref, um_ref, cnt_ref, ms_ref, chunk)
    blocks = _swa_blocks(sinks_ref, qa_ref, kvc_ref, kvp_ref, ua_ref, mask_ref, step == 0)
    per_chunk = len(blocks) // len(chunks)
    for c, run_chunk in enumerate(chunks):
        run_chunk()
        for run_block in blocks[c * per_chunk:(c + 1) * per_chunk]:
            run_block()

    @pl.when(step == pl.num_programs(1) - 1)
    def _():
        for h in range(M_HEADS):
            c_ref[0, h] = cnt_ref[h][:M_DV, :].T
            n_ref[0, h] = cnt_ref[h][M_DV:M_DV + 1, :]
            m_ref[0, h] = ms_ref[h]


def _prompt_mixers(sinks, zm, gates, gain, qa, kva, batch, seq, chunk):
    n = batch * seq
    per_step = min(MLSTM_CHUNKS_PER_STEP, seq // chunk)
    rows = per_step * chunk
    assert seq % rows == 0 and chunk % WINDOW == 0
    ns = seq // rows

    def step_rows(width):
        return pl.BlockSpec((rows, width), lambda b, i: (b * ns + i, 0))

    def prev_block(b, i):
        return (jnp.maximum((b * ns + i) * (rows // WINDOW) - 1, 0), 0)

    def state(*dims):
        return pl.BlockSpec((1, M_HEADS) + dims, lambda b, i: (b, 0, 0, 0))

    return pl.pallas_call(
        functools.partial(_prompt_mixers_kernel, chunk=chunk),
        grid=(batch, ns),
        in_specs=[pl.BlockSpec(memory_space=pltpu.SMEM), step_rows(ZM_WIDTH), step_rows(LANES),
                  pl.BlockSpec((1, M_WIDTH), lambda b, i: (0, 0)),
                  step_rows(A_WIDTH), step_rows(KV_WIDTH), pl.BlockSpec((WINDOW, KV_WIDTH), prev_block)],
        out_specs=[step_rows(M_WIDTH), step_rows(A_WIDTH), state(M_DK, M_DV), state(1, M_DK), state(1, LANES)],
        out_shape=[jax.ShapeDtypeStruct((n, M_WIDTH), BF16),
                   jax.ShapeDtypeStruct((n, A_WIDTH), BF16),
                   jax.ShapeDtypeStruct((batch, M_HEADS, M_DK, M_DV), F32),
                   jax.ShapeDtypeStruct((batch, M_HEADS, 1, M_DK), F32),
                   jax.ShapeDtypeStruct((batch, M_HEADS, 1, LANES), F32)],
        scratch_shapes=[pltpu.VMEM((M_HEADS, M_DV + MLSTM_AUX_ROWS, M_DK), F32),
                        pltpu.VMEM((M_HEADS, 1, LANES), F32),
                        pltpu.VMEM((2, WINDOW, 2 * WINDOW), F32)],
        compiler_params=pltpu.CompilerParams(dimension_semantics=("parallel", "arbitrary"),
                                             vmem_limit_bytes=VMEM_LIMIT),
        name="prompt_mixers",
    )(sinks, zm, gates, gain, qa, kva, kva)


SAMPLE_ROWS = 128


def _mlstm_sample_kernel(zm_ref, gates_ref, gain_ref, c_ref, n_ref, m_ref,
                         um_ref, co_ref, no_ref, mo_ref,
                         qf_ref, qc_ref, nexp_ref, wkk_ref, dec_ref, *, steps):
    rows = SAMPLE_ROWS
    nb = rows // steps
    row = lax.broadcasted_iota(jnp.int32, (rows, rows), 0)
    col = lax.broadcasted_iota(jnp.int32, (rows, rows), 1)
    same = (row // steps) == (col // steps)
    causal = same & (col <= row)
    eye = row == col
    last = same & ((col % steps) == steps - 1)
    sub = lax.broadcasted_iota(jnp.int32, (2 * steps, M_DV), 0)
    tok_batch = lax.broadcasted_iota(jnp.int32, (M_DK, rows), 1) // steps

    def to_row(x_col):
        return jnp.sum(jnp.where(eye, x_col, 0.0), axis=0, keepdims=True)

    def pick(mask, x_row, fill, reduce):
        return reduce(jnp.where(mask, x_row, fill), axis=-1, keepdims=True)

    gates = gates_ref[...]
    for head in range(M_HEADS):
        lanes = slice(head * M_DK, (head + 1) * M_DK)
        ig_col = gates[:, GATE_I + head:GATE_I + head + 1]
        fg_col = gates[:, GATE_F + head:GATE_F + head + 1]
        lf_row = to_row(_log_sigmoid(fg_col))
        b_col = pick(causal, lf_row, 0.0, jnp.sum)
        b_row = to_row(b_col)
        ig_row = to_row(ig_col)
        m_prev = m_ref[head]

        d = jnp.where(causal, b_col - b_row + ig_row, -jnp.inf)
        inter = b_col + m_prev
        m_t = jnp.maximum(inter, jnp.max(d, axis=-1, keepdims=True))
        w = jnp.exp(d - m_t)
        a = jnp.exp(inter - m_t)

        q = zm_ref[:, lanes]
        k = zm_ref[:, M_WIDTH + head * M_DK:M_WIDTH + (head + 1) * M_DK]
        v = zm_ref[:, 2 * M_WIDTH + head * M_DV:2 * M_WIDTH + (head + 1) * M_DV]
        o = zm_ref[:, 3 * M_WIDTH + head * M_DV:3 * M_WIDTH + (head + 1) * M_DV]
        qf_ref[head] = q.astype(F32)
        for t in range(steps):
            nexp_ref[head, pl.ds(t, nb, stride=steps), :] = n_ref[:, lanes]

        for j in range(nb // 2):
            pair = pl.ds(j * 2 * steps, 2 * steps)
            q8 = qf_ref[head, pair, :].astype(BF16)
            r0 = _dot(q8, c_ref[2 * j, head].astype(BF16))
            r1 = _dot(q8, c_ref[2 * j + 1, head].astype(BF16))
            qc_ref[head, pair, :] = jnp.where(sub < steps, r0, r1)

        s = _dot_nt(q, k) * w
        qn = jnp.sum(qf_ref[head] * nexp_ref[head], axis=-1, keepdims=True)
        num = a * qc_ref[head] + _dot(s.astype(BF16), v)
        den = a * qn + jnp.sum(s, axis=-1, keepdims=True)
        hh = num * (1.0 / jnp.maximum(jnp.abs(den), jnp.exp(-m_t)))
        um_ref[:, lanes] = _head_norm_gate(hh, o, gain_ref[:, lanes])

        b_last = pick(last, b_row, 0.0, jnp.sum)
        g_col = b_last - b_col + ig_col
        g_max = pick(same, to_row(g_col), -jnp.inf, jnp.max)
        m_new = jnp.maximum(b_last + m_prev, g_max)
        decay = jnp.exp(b_last + m_prev - m_new)
        wk = jnp.exp(g_col - m_new)
        mo_ref[head] = m_new

        kf = k.astype(F32)
        wkk_ref[head] = wk * kf
        dec_ref[head] = jnp.broadcast_to(decay, (rows, M_DV))
        n_sum = wkk_ref[head, pl.ds(0, nb, stride=steps), :]
        for t in range(1, steps):
            n_sum = n_sum + wkk_ref[head, pl.ds(t, nb, stride=steps), :]
        no_ref[:, lanes] = dec_ref[head, pl.ds(0, nb, stride=steps), :] * n_ref[:, lanes] + n_sum

        k_t = kf.T
        wv = (wk * v.astype(F32)).astype(BF16)
        for b in range(nb):
            upd = _dot(jnp.where(tok_batch == b, k_t, 0.0).astype(BF16), wv)
            co_ref[b, head] = dec_ref[head, pl.ds(b * steps, 1), :] * c_ref[b, head] + upd


def _swa_sample_kernel(sinks_ref, qa_ref, kva_ref, ckt_ref, cvt_ref, ua_ref, nkt_ref, nvt_ref,
                       s_ref, e_ref, pv_ref, *, steps):
    rows = SAMPLE_ROWS
    nb = rows // steps
    pair_rows = 2 * steps
    hw = A_KV_HEADS * A_HD
    tok = lax.broadcasted_iota(jnp.int32, (rows, LANES), 0)
    lane = lax.broadcasted_iota(jnp.int32, (rows, LANES), 1)
    low = lane < A_HD

    qf = qa_ref[...].astype(F32)
    q_blocks = []
    for p in range(A_GROUP):
        qp = qf[:, p * LANES:(p + 1) * LANES]
        q_blocks.append(jnp.where(low, qp, 0.0))
        q_blocks.append(jnp.where(low, 0.0, qp))
    n_blk = len(q_blocks)

    kv_new = kva_ref[...]
    kv_new_t = kv_new.T
    k_new_t = kv_new_t[:hw]
    v_new_t = kv_new_t[hw:]
    v_new = kv_new[:, hw:].astype(BF16)
    s_new = _dot(jnp.concatenate(q_blocks, axis=0).astype(BF16), k_new_t.astype(BF16))

    def pair_rows_of(blocks, j):
        return jnp.concatenate([blk[j * pair_rows:(j + 1) * pair_rows] for blk in blocks], axis=0)

    first = (lax.broadcasted_iota(jnp.int32, (n_blk * pair_rows, LANES), 0) % pair_rows) < steps

    for j in range(nb // 2):
        q_pair = pair_rows_of(q_blocks, j).astype(BF16)
        s_pair = jnp.where(first, _dot(q_pair, ckt_ref[2 * j].astype(BF16)),
                           _dot(q_pair, ckt_ref[2 * j + 1].astype(BF16)))
        for blk in range(n_blk):
            s_ref[pl.ds(blk * rows + j * pair_rows, pair_rows), :] = s_pair[blk * pair_rows:(blk + 1) * pair_rows]

    t_q = tok % steps
    dist_c = WINDOW + t_q - lane
    valid_c = dist_c < WINDOW
    dist_n = t_q - lane % steps
    valid_n = ((tok // steps) == (lane // steps)) & (dist_n >= 0)
    dist_cf = dist_c.astype(F32)
    dist_nf = dist_n.astype(F32)
    pv_new = []
    inv_den = []
    for blk in range(n_blk):
        head = (blk % A_KV_HEADS) * A_GROUP + blk // A_KV_HEADS
        slope = _alibi_slope(head)
        sink = sinks_ref[head]
        r = pl.ds(blk * rows, rows)
        s_c = jnp.where(valid_c, s_ref[r, :] - slope * dist_cf, -jnp.inf)
        s_n = jnp.where(valid_n, s_new[blk * rows:(blk + 1) * rows] - slope * dist_nf, -jnp.inf)
        mx = jnp.maximum(jnp.maximum(jnp.max(s_c, axis=-1, keepdims=True),
                                     jnp.max(s_n, axis=-1, keepdims=True)), sink)
        e_c = jnp.exp(s_c - mx)
        e_n = jnp.exp(s_n - mx)
        den = jnp.sum(e_c, axis=-1, keepdims=True) + jnp.sum(e_n, axis=-1, keepdims=True) + jnp.exp(sink - mx)
        e_ref[r, :] = e_c
        pv_new.append(_dot(e_n.astype(BF16), v_new))
        inv_den.append(1.0 / den)

    for j in range(nb // 2):
        e_pair = jnp.concatenate([e_ref[pl.ds(blk * rows + j * pair_rows, pair_rows), :] for blk in range(n_blk)],
                                 axis=0).astype(BF16)
        pv_pair = jnp.where(first, _dot_nt(e_pair, cvt_ref[2 * j].astype(BF16)),
                            _dot_nt(e_pair, cvt_ref[2 * j + 1].astype(BF16)))
        for blk in range(n_blk):
            pv_ref[pl.ds(blk * rows + j * pair_rows, pair_rows), :] = pv_pair[blk * pair_rows:(blk + 1) * pair_rows]

    for p in range(A_GROUP):
        halves = []
        for kvh in range(A_KV_HEADS):
            blk = p * A_KV_HEADS + kvh
            halves.append((pv_ref[pl.ds(blk * rows, rows), :] + pv_new[blk]) * inv_den[blk])
        ua_ref[:, p * LANES:(p + 1) * LANES] = jnp.where(low, halves[0], halves[1]).astype(BF16)

    tail = lane >= WINDOW - steps
    for b in range(nb):
        shift = (WINDOW - steps - b * steps) % rows
        nkt_ref[b] = jnp.where(tail, pltpu.roll(k_new_t, shift, 1), pltpu.roll(ckt_ref[b], WINDOW - steps, 1))
        nvt_ref[b] = jnp.where(tail, pltpu.roll(v_new_t, shift, 1), pltpu.roll(cvt_ref[b], WINDOW - steps, 1))


def _sample_mixers_kernel(sinks_ref, zm_ref, gates_ref, gain_ref, c_ref, n_ref, m_ref, qa_ref, kva_ref,
                          ckt_ref, cvt_ref, um_ref, co_ref, no_ref, mo_ref, ua_ref, nkt_ref, nvt_ref,
                          qf_ref, qc_ref, nexp_ref, wkk_ref, dec_ref, s_ref, e_ref, pv_ref, *, steps):
    _mlstm_sample_kernel(zm_ref, gates_ref, gain_ref, c_ref, n_ref, m_ref, um_ref, co_ref, no_ref, mo_ref,
                         qf_ref, qc_ref, nexp_ref, wkk_ref, dec_ref, steps=steps)
    _swa_sample_kernel(sinks_ref, qa_ref, kva_ref, ckt_ref, cvt_ref, ua_ref, nkt_ref, nvt_ref,
                       s_ref, e_ref, pv_ref, steps=steps)


def _sample_mixers(sinks, zm, gates, gain, c, n2, m_rows, qa, kva, ckt, cvt, steps):
    rows = zm.shape[0]
    assert ckt.shape[1:] == (A_KV_HEADS * A_HD, WINDOW) and SAMPLE_ROWS == WINDOW == LANES
    nb = SAMPLE_ROWS // steps

    def tokens(width):
        return pl.BlockSpec((SAMPLE_ROWS, width), lambda g: (g, 0))

    c_spec = pl.BlockSpec((nb, M_HEADS, M_DK, M_DV), lambda g: (g, 0, 0, 0))
    n_spec = pl.BlockSpec((nb, M_HEADS * M_DK), lambda g: (g, 0))
    m_spec = pl.BlockSpec((M_HEADS, SAMPLE_ROWS, 1), lambda g: (0, g, 0))
    cache = pl.BlockSpec((nb, LANES, WINDOW), lambda g: (g, 0, 0))
    per_head = pltpu.VMEM((M_HEADS, SAMPLE_ROWS, M_DK), F32)
    stacked = pltpu.VMEM((A_HEADS * SAMPLE_ROWS, LANES), F32)
    return pl.pallas_call(
        functools.partial(_sample_mixers_kernel, steps=steps),
        grid=(rows // SAMPLE_ROWS,),
        in_specs=[pl.BlockSpec(memory_space=pltpu.SMEM), tokens(ZM_WIDTH), tokens(LANES),
                  pl.BlockSpec((1, M_WIDTH), lambda g: (0, 0)), c_spec, n_spec, m_spec,
                  tokens(A_WIDTH), tokens(KV_WIDTH), cache, cache],
        out_specs=[tokens(M_WIDTH), c_spec, n_spec, m_spec, tokens(A_WIDTH), cache, cache],
        out_shape=[jax.ShapeDtypeStruct((rows, M_WIDTH), BF16),
                   jax.ShapeDtypeStruct(c.shape, F32),
                   jax.ShapeDtypeStruct(n2.shape, F32),
                   jax.ShapeDtypeStruct(m_rows.shape, F32),
                   jax.ShapeDtypeStruct((rows, A_WIDTH), BF16),
                   jax.ShapeDtypeStruct(ckt.shape, F32),
                   jax.ShapeDtypeStruct(cvt.shape, F32)],
        scratch_shapes=[per_head,
                        per_head,
                        per_head,
                        per_head,
                        per_head,
                        stacked,
                        stacked,
                        stacked],
        compiler_params=pltpu.CompilerParams(dimension_semantics=("parallel",),
                                             vmem_limit_bytes=VMEM_LIMIT),
        name="sample_mixers",
    )(sinks, zm, gates, gain, c, n2, m_rows, qa, kva, ckt, cvt)


def _prep_proj(w_in, b_gate):
    m_end = 4 * M_WIDTH
    g_end = m_end + 2 * M_HEADS
    wt = w_in.T
    qa = wt[g_end:g_end + A_WIDTH].reshape(A_KV_HEADS, A_GROUP, A_HD, D_MODEL).transpose(1, 0, 2, 3)
    gates = jnp.pad(wt[m_end:g_end], ((0, LANES - 2 * M_HEADS), (0, 0)))
    win = tuple(w.astype(BF16) for w in (wt[:m_end], qa.reshape(A_WIDTH, D_MODEL), wt[g_end + A_WIDTH:], gates))
    bias = jnp.pad(b_gate, (0, LANES - 2 * M_HEADS)).reshape(1, LANES)
    return win, bias


def _prep_out(w_out):
    wa = w_out[M_WIDTH:].reshape(A_KV_HEADS, A_GROUP, A_HD, D_MODEL).transpose(1, 0, 2, 3)
    return jnp.concatenate([w_out[:M_WIDTH], wa.reshape(A_WIDTH, D_MODEL)], axis=0).astype(BF16)


def kernel(x_prompt, x_sample, cache_swa_k, cache_swa_v, state_mlstm_C, state_mlstm_n, state_mlstm_m,
           norm_gains, ffn_w_gate, ffn_w_up, ffn_w_down, w_in, b_gate, mlstm_norm_gain, attn_sinks, w_out):
    assert norm_gains.shape[0] == 1, "single layer"
    batch, seq, _ = x_prompt.shape
    dec_batch, steps, _ = x_sample.shape
    gains = norm_gains[0]
    ffn = (ffn_w_gate, ffn_w_up, ffn_w_down)
    win, bias = _prep_proj(w_in[0], b_gate[0])
    wo = _prep_out(w_out[0])
    gain = mlstm_norm_gain[0].reshape(1, M_WIDTH)
    sinks = attn_sinks[0]
    kv_lanes = A_KV_HEADS * A_HD

    chunk = min(MLSTM_CHUNK, seq)
    assert seq % chunk == 0
    xp, zm, qa, kva, gates = _ffn_proj(x_prompt.reshape(batch * seq, D_MODEL), gains, *ffn, win, bias,
                                       0, 0, 1, 2, chunk)
    um, ua, p_c, p_n, p_m = _prompt_mixers(sinks, zm, gates, gain, qa, kva, batch, seq, chunk)
    yp = _mix_ffn(xp, um, ua, wo, gains, *ffn, 1, 3, 4, 5).reshape(batch, seq, D_MODEL)
    keep = min(WINDOW, seq)
    kv_tail = kva.reshape(batch, seq, KV_WIDTH)[:, seq - keep:]
    p_k = kv_tail[..., :kv_lanes].reshape(1, batch, keep, A_KV_HEADS, A_HD)
    p_v = kv_tail[..., kv_lanes:].reshape(1, batch, keep, A_KV_HEADS, A_HD)
    p_c = p_c[None]
    p_n = p_n[None, :, :, 0, :]
    p_m = p_m[None, :, :, 0, 0]

    xs, zm_s, qa_s, kva_s, gates_s = _ffn_proj(x_sample.reshape(dec_batch * steps, D_MODEL), gains, *ffn,
                                               win, bias, 0, 0, 1, 2, min(chunk, dec_batch * steps))
    m_rows = jnp.repeat(state_mlstm_m[0].T, steps, axis=1)[..., None]
    wc = cache_swa_k.shape[2]
    um_s, s_c, s_n2, s_m_rows, ua_s, s_kt, s_vt = _sample_mixers(
        sinks, zm_s, gates_s, gain, state_mlstm_C[0], state_mlstm_n[0].reshape(dec_batch, M_HEADS * M_DK), m_rows,
        qa_s, kva_s, cache_swa_k[0].reshape(dec_batch, wc, kv_lanes).transpose(0, 2, 1),
        cache_swa_v[0].reshape(dec_batch, wc, kv_lanes).transpose(0, 2, 1), steps)
    ys = _mix_ffn(xs, um_s, ua_s, wo, gains, *ffn, 1, 3, 4, 5).reshape(dec_batch, steps, D_MODEL)
    s_n = s_n2.reshape(1, dec_batch, M_HEADS, M_DK)
    s_m = s_m_rows[:, ::steps, 0].T[None]
    s_k = s_kt.transpose(0, 2, 1).reshape(1, dec_batch, wc, A_KV_HEADS, A_HD)
    s_v = s_vt.transpose(0, 2, 1).reshape(1, dec_batch, wc, A_KV_HEADS, A_HD)
    return (yp, ys, p_k, p_v, p_c, p_n, p_m, s_k, s_v, s_c[None], s_n, s_m)
```

```python
import functools

import jax
import jax.numpy as jnp
from jax import lax
from jax.experimental import pallas as pl
from jax.experimental.pallas import tpu as pltpu

F32 = jnp.float32
BF16 = jnp.bfloat16

D_MODEL = 1024
D_FF = 2816
M_HEADS = 4
M_DK = 128
M_DV = 128
M_WIDTH = M_HEADS * M_DV
A_HEADS = 8
A_KV_HEADS = 2
A_GROUP = A_HEADS // A_KV_HEADS
A_HD = 64
A_WIDTH = A_HEADS * A_HD
WINDOW = 128
RMS_EPS = 1e-6

LANES = 128
FFN_CHUNK = 256
N_FFN_CHUNKS = D_FF // FFN_CHUNK
FFN_STAGE_SLOTS = 4
TOKEN_TILE = 512
MLSTM_CHUNK = 256
ZM_WIDTH = 4 * M_WIDTH
KV_WIDTH = 2 * A_KV_HEADS * A_HD
VMEM_LIMIT = 56 * 1024 * 1024
MLSTM_CHUNKS_PER_STEP = 8
MLSTM_AUX_ROWS = 16
GATE_I, GATE_F, GATE_B, GATE_X =0, M_HEADS, 2 * M_HEADS, 3 * M_HEADS

NT_DIMS = (((1,), (1,)), ((), ()))


def _rms(x, g):
    ms = jnp.mean(x * x, axis=-1, keepdims=True)
    return x * lax.rsqrt(ms + RMS_EPS) * g


def _log_sigmoid(x):
    return jnp.minimum(x, 0.0) - jnp.log1p(jnp.exp(-jnp.abs(x)))


def _dot(a, b):
    return jnp.dot(a, b, preferred_element_type=F32)


def _dot_nt(a, b):
    return lax.dot_general(a, b, NT_DIMS, preferred_element_type=F32)


def _resident(shape):
    nd = len(shape)
    return pl.BlockSpec(shape, lambda *_: (0,) * nd, pipeline_mode=pl.Buffered(1))


def _ffn_weight_copies(w_hbm, j, f, slot, stage_refs, sems):
    wg_hbm, wu_hbm, wd_hbm = w_hbm
    cols = pl.ds(f * FFN_CHUNK, FFN_CHUNK)
    sources = (wg_hbm.at[0, j, :, cols], wu_hbm.at[0, j, :, cols], wd_hbm.at[0, j, cols, :])
    return [pltpu.make_async_copy(src, stage.at[slot], sems.at[slot, k])
            for k, (src, stage) in enumerate(zip(sources, stage_refs))]


def _swiglu_chunk(h_ref, acc_ref, wg, wu, wd, first):
    h = h_ref[...]
    g = _dot(h, wg)
    u = _dot(h, wu)
    part = _dot((g * jax.nn.sigmoid(g) * u).astype(BF16), wd)
    if first:
        acc_ref[...] = part
    else:
        acc_ref[...] += part


def _swiglu(x, gpre, w_refs, h_ref, acc_ref, fetch=None):
    wg_ref, wu_ref, wd_ref = w_refs
    h_ref[...] = _rms(x, gpre).astype(BF16)
    if fetch is None:
        for f in range(N_FFN_CHUNKS):
            _swiglu_chunk(h_ref, acc_ref, wg_ref[f], wu_ref[f], wd_ref[f], first=f == 0)
        return

    stage_g, stage_u, stage_d = fetch[2]
    for f in range(min(FFN_STAGE_SLOTS, N_FFN_CHUNKS)):
        for copy in _ffn_weight_copies(*fetch[:2], f, f, *fetch[2:]):
            copy.start()
    acc_ref[...] = jnp.zeros_like(acc_ref)

    def step(f, carry):
        slot = f % FFN_STAGE_SLOTS
        for copy in _ffn_weight_copies(*fetch[:2], f, slot, *fetch[2:]):
            copy.wait()
        wg_ref[f] = stage_g[slot].astype(BF16)
        wu_ref[f] = stage_u[slot].astype(BF16)
        wd_ref[f] = stage_d[slot].astype(BF16)

        @pl.when(f + FFN_STAGE_SLOTS < N_FFN_CHUNKS)
        def _():
            for copy in _ffn_weight_copies(*fetch[:2], f + FFN_STAGE_SLOTS, slot, *fetch[2:]):
                copy.start()

        _swiglu_chunk(h_ref, acc_ref, wg_ref[f], wu_ref[f], wd_ref[f], first=False)
        return carry

    lax.fori_loop(0, N_FFN_CHUNKS, step, 0)


def _first_step_fetches(body, fetch):
    first = pl.program_id(0) == 0

    @pl.when(first)
    def _():
        body(fetch)

    @pl.when(jnp.logical_not(first))
    def _():
        body(None)


def _half_step(x, acc_ref, gpost):
    return x + _rms(acc_ref[...], 0.5 * gpost)


def _project(x, gain, win_refs, bias_ref, h_ref, zm_ref, qa_ref, kva_ref, gates_ref, chunk):
    h_ref[...] = _rms(x, gain).astype(BF16)

    wm_ref, wqa_ref, wkv_ref, wg_ref = win_refs

    def cols(lo, width):
        return _dot_nt(h_ref[...], wm_ref[lo:lo + width, :])

    gates = _dot_nt(h_ref[...], wg_ref[...]) + bias_ref[...]
    lane = lax.broadcasted_iota(jnp.int32, (chunk, LANES), 1)
    pos = lax.broadcasted_iota(jnp.int32, (chunk, LANES), 0)
    for c0 in range(0, gates.shape[0], chunk):
        g = gates[c0:c0 + chunk]
        b = jnp.where((lane >= GATE_F) & (lane < GATE_B), _log_sigmoid(g), 0.0)
        shift = 1
        while shift < chunk:
            b = b + jnp.where(pos >= shift, pltpu.roll(b, shift, 0), 0.0)
            shift *= 2
        x_gate = pltpu.roll(g, GATE_X - GATE_I, 1) - pltpu.roll(b, GATE_X - GATE_F, 1)
        out = jnp.where(lane < GATE_B, g, jnp.where(lane < GATE_X, pltpu.roll(b, GATE_B - GATE_F, 1), x_gate))
        gates_ref[c0:c0 + chunk, :] = jnp.where(lane < GATE_X + M_HEADS, out, 0.0)

    zm_ref[:, 0:M_WIDTH] = cols(0, M_WIDTH).astype(BF16)
    zm_ref[:, M_WIDTH:2 * M_WIDTH] = (cols(M_WIDTH, M_WIDTH) * (M_DK ** -0.5)).astype(BF16)
    zm_ref[:, 2 * M_WIDTH:3 * M_WIDTH] = cols(2 * M_WIDTH, M_WIDTH).astype(BF16)
    zm_ref[:, 3 * M_WIDTH:] = jax.nn.sigmoid(cols(3 * M_WIDTH, M_WIDTH)).astype(BF16)
    qa_ref[...] = (_dot_nt(h_ref[...], wqa_ref[...]) * (A_HD ** -0.5)).astype(BF16)
    kva_ref[...] = _dot_nt(h_ref[...], wkv_ref[...])


def _ffn_proj_kernel(x_ref, gains_ref, wg_hbm, wu_hbm, wd_hbm, wm_ref, wqa_ref, wkv_ref, wgate_ref, bias_ref,
                     x1_ref, zm_ref, qa_ref, kva_ref, gates_ref,
                     h_ref, acc_ref, wg_ref, wu_ref, wd_ref, stage_g, stage_u, stage_d, sems,
                     *, j, pre, post, mixer, chunk):
    def body(fetch):
        x = x_ref[...]
        _swiglu(x, gains_ref[pre:pre + 1, :], (wg_ref, wu_ref, wd_ref), h_ref, acc_ref, fetch)
        x1 = _half_step(x, acc_ref, gains_ref[post:post + 1, :])
        x1_ref[...] = x1
        _project(x1, gains_ref[mixer:mixer + 1, :], (wm_ref, wqa_ref, wkv_ref, wgate_ref), bias_ref, h_ref,
                 zm_ref, qa_ref, kva_ref, gates_ref, chunk)

    _first_step_fetches(body, ((wg_hbm, wu_hbm, wd_hbm), j, (stage_g, stage_u, stage_d), sems))


def _mix_ffn_kernel(x_ref, um_ref, ua_ref, wo_ref, gains_ref, wg_hbm, wu_hbm, wd_hbm, o_ref,
                    h_ref, acc_ref, wg_ref, wu_ref, wd_ref, stage_g, stage_u, stage_d, sems,
                    *, j, mix, pre, post):
    def body(fetch):
        u = jnp.concatenate([um_ref[...], ua_ref[...]], axis=-1)
        x = x_ref[...] + _rms(_dot(u, wo_ref[...]), gains_ref[mix:mix + 1, :])
        _swiglu(x, gains_ref[pre:pre + 1, :], (wg_ref, wu_ref, wd_ref), h_ref, acc_ref, fetch)
        o_ref[...] = _half_step(x, acc_ref, gains_ref[post:post + 1, :])

    _first_step_fetches(body, ((wg_hbm, wu_hbm, wd_hbm), j, (stage_g, stage_u, stage_d), sems))


def _ffn_scratch(tm):
    return [pltpu.VMEM((tm, D_MODEL), BF16),
            pltpu.VMEM((tm, D_MODEL), F32),
            pltpu.VMEM((N_FFN_CHUNKS, D_MODEL, FFN_CHUNK), BF16),
            pltpu.VMEM((N_FFN_CHUNKS, D_MODEL, FFN_CHUNK), BF16),
            pltpu.VMEM((N_FFN_CHUNKS, FFN_CHUNK, D_MODEL), BF16),
            pltpu.VMEM((FFN_STAGE_SLOTS, D_MODEL, FFN_CHUNK), F32),
            pltpu.VMEM((FFN_STAGE_SLOTS, D_MODEL, FFN_CHUNK), F32),
            pltpu.VMEM((FFN_STAGE_SLOTS, FFN_CHUNK, D_MODEL), F32),
            pltpu.SemaphoreType.DMA((FFN_STAGE_SLOTS, 3))]


def _dense_params():
    return pltpu.CompilerParams(dimension_semantics=("arbitrary",), vmem_limit_bytes=VMEM_LIMIT)


_HBM = pl.BlockSpec(memory_space=pl.ANY)


def _ffn_proj(x, gains, wg, wu, wd, win, bias, j, pre, post, mixer, chunk):
    n = x.shape[0]
    tm = min(TOKEN_TILE, n)
    assert tm % chunk == 0

    def rows(width):
        return pl.BlockSpec((tm, width), lambda i: (i, 0))

    return pl.pallas_call(
        functools.partial(_ffn_proj_kernel, j=j, pre=pre, post=post, mixer=mixer, chunk=chunk),
        grid=(n // tm,),
        in_specs=[rows(D_MODEL), _resident(gains.shape), _HBM, _HBM, _HBM,
                  *[_resident(w.shape) for w in win], _resident(bias.shape)],
        out_specs=[rows(D_MODEL), rows(ZM_WIDTH), rows(A_WIDTH), rows(KV_WIDTH), rows(LANES)],
        out_shape=[jax.ShapeDtypeStruct((n, D_MODEL), F32),
                   jax.ShapeDtypeStruct((n, ZM_WIDTH), BF16),
                   jax.ShapeDtypeStruct((n, A_WIDTH), BF16),
                   jax.ShapeDtypeStruct((n, KV_WIDTH), F32),
                   jax.ShapeDtypeStruct((n, LANES), F32)],
        scratch_shapes=_ffn_scratch(tm),
        compiler_params=_dense_params(),
        name="ffn_half_step_input_projection",
    )(x, gains, wg, wu, wd, *win, bias)


def _mix_ffn(x, um, ua, wo, gains, wg, wu, wd, j, mix, pre, post):
    n = x.shape[0]
    tm = min(TOKEN_TILE, n)
    row = pl.BlockSpec((tm, D_MODEL), lambda i: (i, 0))
    half = pl.BlockSpec((tm, M_WIDTH), lambda i: (i, 0))
    return pl.pallas_call(
        functools.partial(_mix_ffn_kernel, j=j, mix=mix, pre=pre, post=post),
        grid=(n // tm,),
        in_specs=[row, half, half, _resident(wo.shape), _resident(gains.shape), _HBM, _HBM, _HBM],
        out_specs=row,
        out_shape=jax.ShapeDtypeStruct((n, D_MODEL), F32),
        scratch_shapes=_ffn_scratch(tm),
        compiler_params=_dense_params(),
        name="out_proj_ffn_half_step",
    )(x, um, ua, wo, gains, wg, wu, wd)


def _head_norm_gate(hh, o, gain):
    hn = hh * lax.rsqrt(jnp.mean(hh * hh, axis=-1, keepdims=True) + RMS_EPS) * gain
    return (o.astype(F32) * hn).astype(BF16)


def _mlstm_chunks(zm_ref, gates_ref, gain_ref, um_ref, cnt_ref, ms_ref, chunk):
    si = lax.broadcasted_iota(jnp.int32, (chunk, chunk), 0)
    ti = lax.broadcasted_iota(jnp.int32, (chunk, chunk), 1)
    causal = si <= ti
    ones_rows = jnp.where(lax.broadcasted_iota(jnp.int32, (MLSTM_AUX_ROWS, chunk), 0) == 0, 1.0, 0.0).astype(BF16)

    def run_chunk(c):
        rows = slice(c * chunk, (c + 1) * chunk)
        gates = gates_ref[rows, :]
        gates_t = gates.T
        for h in range(M_HEADS):
            lanes = slice(h * M_DK, (h + 1) * M_DK)
            q = zm_ref[rows, lanes]
            k = zm_ref[rows, M_WIDTH + h * M_DK:M_WIDTH + (h + 1) * M_DK]
            v = zm_ref[rows, 2 * M_WIDTH + h * M_DV:2 * M_WIDTH + (h + 1) * M_DV]
            o = zm_ref[rows, 3 * M_WIDTH + h * M_DV:3 * M_WIDTH + (h + 1) * M_DV]
            x_col = gates[:, GATE_X + h:GATE_X + h + 1]
            ig_row = gates_t[GATE_I + h:GATE_I + h + 1, :]
            b_row = gates_t[GATE_B + h:GATE_B + h + 1, :]
            m_prev = ms_ref[h][:, 0:1]
            cnt = cnt_ref[h]

            xm = jnp.where(causal, x_col, -jnp.inf)
            mu = jnp.maximum(m_prev, jnp.max(xm, axis=0, keepdims=True))
            a = jnp.exp(m_prev - mu)
            s_t = _dot_nt(k, q) * jnp.exp(xm - mu)
            v_ext = jnp.concatenate([v.astype(F32).T.astype(BF16), ones_rows], axis=0)
            nd = a * _dot_nt(cnt.astype(BF16), q) + _dot(v_ext, s_t.astype(BF16))
            den = nd[M_DV:M_DV + 1, :]
            hh = nd[:M_DV, :] * (1.0 / jnp.maximum(jnp.abs(den), jnp.exp(-(b_row + mu))))
            hn = hh * lax.rsqrt(jnp.mean(hh * hh, axis=0, keepdims=True) + RMS_EPS)
            um_ref[rows, lanes] = (hn.T * gain_ref[:, lanes] * o.astype(F32)).astype(BF16)

            b_last = b_row[:, chunk - 1:chunk]
            g = b_last - b_row + ig_row
            m_new = jnp.maximum(b_last + m_prev, jnp.max(g, axis=-1, keepdims=True))
            decay = jnp.exp(b_last + m_prev - m_new)
            wv = (jnp.exp(g - m_new) * v_ext.astype(F32)).astype(BF16)
            cnt_ref[h] = decay * cnt + _dot(wv, k)
            ms_ref[h] = jnp.broadcast_to(m_new, (1, LANES))

    return [functools.partial(run_chunk, c) for c in range(zm_ref.shape[0] // chunk)]


def _alibi_slope(head):
    return 2.0 ** (-8.0 * (head + 1) / A_HEADS)


def _swa_blocks(sinks_ref, qa_ref, kvc_ref, kvp_ref, ua_ref, mask_ref, first):
    hw = A_KV_HEADS * A_HD
    n_blocks = qa_ref.shape[0] // WINDOW
    qi = lax.broadcasted_iota(jnp.int32, (WINDOW, 2 * WINDOW), 0)
    kj = lax.broadcasted_iota(jnp.int32, (WINDOW, 2 * WINDOW), 1)
    dist = WINDOW + qi - kj
    band = (dist >= 0) & (dist < WINDOW)
    mask_ref[0] = jnp.where(band, 0.0, -jnp.inf)
    mask_ref[1] = jnp.where(band & (kj >= WINDOW), 0.0, -jnp.inf)
    lane = lax.broadcasted_iota(jnp.int32, (WINDOW, LANES), 1)
    low = lane < A_HD

    assert 2 * WINDOW <= 256, "positions must stay exactly representable in bf16"
    q_pos = (WINDOW + lax.broadcasted_iota(jnp.int32, (WINDOW, LANES), 0)).astype(F32)
    k_lane = lax.broadcasted_iota(jnp.int32, (2 * WINDOW, LANES), 1)
    k_pos = lax.broadcasted_iota(jnp.int32, (2 * WINDOW, LANES), 0).astype(F32)
    spare = (A_HD, 0)
    k_bias = [jnp.where(k_lane == spare[kvh], 1.0, jnp.where(k_lane == spare[kvh] + 1, k_pos, 0.0)).astype(BF16)
              for kvh in range(A_KV_HEADS)]
    q_bias = {}
    for head in range(A_HEADS):
        slope = _alibi_slope(head)
        at = spare[head // A_GROUP]
        q_bias[head] = jnp.where(lane == at, -slope * q_pos, jnp.where(lane == at + 1, slope, 0.0)).astype(BF16)

    def run_block(j):
        rows = slice(j * WINDOW, (j + 1) * WINDOW)
        kvc = kvc_ref[rows, :]
        kvp = kvp_ref[...] if j == 0 else kvc_ref[(j - 1) * WINDOW:j * WINDOW, :]
        mask = mask_ref[jnp.where(first, 1, 0)] if j == 0 else mask_ref[0]
        kk = jnp.concatenate([kvp[:, :hw], kvc[:, :hw]], axis=0).astype(BF16)
        kk_biased = [jnp.where(k_lane < A_HD, kk, k_bias[0]), jnp.where(k_lane < A_HD, k_bias[1], kk)]
        vv = jnp.concatenate([kvp[:, hw:], kvc[:, hw:]], axis=0).astype(BF16)
        for p in range(A_GROUP):
            qp = qa_ref[rows, p * LANES:(p + 1) * LANES]
            outs = []
            for kvh in range(A_KV_HEADS):
                head = kvh * A_GROUP + p
                keep = low if kvh == 0 else jnp.logical_not(low)
                s = _dot_nt(jnp.where(keep, qp, q_bias[head]), kk_biased[kvh])
                s = s + mask
                sink = sinks_ref[head]
                mx = jnp.maximum(jnp.max(s, axis=-1, keepdims=True), sink)
                e = jnp.exp(s - mx)
                den = jnp.sum(e, axis=-1, keepdims=True) + jnp.exp(sink - mx)
                outs.append(_dot(e.astype(BF16), vv) / den)
            ua_ref[rows, p * LANES:(p + 1) * LANES] = jnp.where(low, outs[0], outs[1]).astype(BF16)

    return [functools.partial(run_block, j) for j in range(n_blocks)]


def _prompt_mixers_kernel(sinks_ref, zm_ref, gates_ref, gain_ref, qa_ref, kvc_ref, kvp_ref,
                          um_ref, ua_ref, c_ref, n_ref, m_ref, cnt_ref, ms_ref, mask_ref, *, chunk):
    step = pl.program_id(1)

    @pl.when(step == 0)
    def _():
        cnt_ref[...] = jnp.zeros_like(cnt_ref)
        ms_ref[...] = jnp.zeros_like(ms_ref)

    chunks = _mlstm_chunks(zm_ref, gates_ref, gain_ref, um_ref, cnt_ref, ms_ref, chunk)
    blocks = _swa_blocks(sinks_ref, qa_ref, kvc_ref, kvp_ref, ua_ref, mask_ref, step == 0)
    per_chunk = len(blocks) // len(chunks)
    for c, run_chunk in enumerate(chunks):
        run_chunk()
        for run_block in blocks[c * per_chunk:(c + 1) * per_chunk]:
            run_block()

    @pl.when(step == pl.num_programs(1) - 1)
    def _():
        for h in range(M_HEADS):
            c_ref[0, h] = cnt_ref[h][:M_DV, :].T
            n_ref[0, h] = cnt_ref[h][M_DV:M_DV + 1, :]
            m_ref[0, h] = ms_ref[h]


def _prompt_mixers(sinks, zm, gates, gain, qa, kva, batch, seq, chunk):
    n = batch * seq
    per_step = min(MLSTM_CHUNKS_PER_STEP, seq // chunk)
    rows = per_step * chunk
    assert seq % rows == 0 and chunk % WINDOW == 0
    ns = seq // rows

    def step_rows(width):
        return pl.BlockSpec((rows, width), lambda b, i: (b * ns + i, 0))

    def prev_block(b, i):
        return (jnp.maximum((b * ns + i) * (rows // WINDOW) - 1, 0), 0)

    def state(*dims):
        return pl.BlockSpec((1, M_HEADS) + dims, lambda b, i: (b, 0, 0, 0))

    return pl.pallas_call(
        functools.partial(_prompt_mixers_kernel, chunk=chunk),
        grid=(batch, ns),
        in_specs=[pl.BlockSpec(memory_space=pltpu.SMEM), step_rows(ZM_WIDTH), step_rows(LANES),
                  pl.BlockSpec((1, M_WIDTH), lambda b, i: (0, 0)),
                  step_rows(A_WIDTH), step_rows(KV_WIDTH), pl.BlockSpec((WINDOW, KV_WIDTH), prev_block)],
        out_specs=[step_rows(M_WIDTH), step_rows(A_WIDTH), state(M_DK, M_DV), state(1, M_DK), state(1, LANES)],
        out_shape=[jax.ShapeDtypeStruct((n, M_WIDTH), BF16),
                   jax.ShapeDtypeStruct((n, A_WIDTH), BF16),
                   jax.ShapeDtypeStruct((batch, M_HEADS, M_DK, M_DV), F32),
                   jax.ShapeDtypeStruct((batch, M_HEADS, 1, M_DK), F32),
                   jax.ShapeDtypeStruct((batch, M_HEADS, 1, LANES), F32)],
        scratch_shapes=[pltpu.VMEM((M_HEADS, M_DV + MLSTM_AUX_ROWS, M_DK), F32),
                        pltpu.VMEM((M_HEADS, 1, LANES), F32),
                        pltpu.VMEM((2, WINDOW, 2 * WINDOW), F32)],
        compiler_params=pltpu.CompilerParams(dimension_semantics=("parallel", "arbitrary"),
                                             vmem_limit_bytes=VMEM_LIMIT),
        name="prompt_mixers",
    )(sinks, zm, gates, gain, qa, kva, kva)


SAMPLE_ROWS = 128


def _mlstm_sample_kernel(zm_ref, gates_ref, gain_ref, c_ref, n_ref, m_ref,
                         um_ref, co_ref, no_ref, mo_ref,
                         qf_ref, qc_ref, nexp_ref, wkk_ref, dec_ref, *, steps):
    rows = SAMPLE_ROWS
    nb = rows // steps
    row = lax.broadcasted_iota(jnp.int32, (rows, rows), 0)
    col = lax.broadcasted_iota(jnp.int32, (rows, rows), 1)
    same = (row // steps) == (col // steps)
    causal = same & (col <= row)
    eye = row == col
    last = same & ((col % steps) == steps - 1)
    sub = lax.broadcasted_iota(jnp.int32, (2 * steps, M_DV), 0)
    tok_batch = lax.broadcasted_iota(jnp.int32, (M_DK, rows), 1) // steps

    def to_row(x_col):
        return jnp.sum(jnp.where(eye, x_col, 0.0), axis=0, keepdims=True)

    def pick(mask, x_row, fill, reduce):
        return reduce(jnp.where(mask, x_row, fill), axis=-1, keepdims=True)

    gates = gates_ref[...]
    for head in range(M_HEADS):
        lanes = slice(head * M_DK, (head + 1) * M_DK)
        ig_col = gates[:, GATE_I + head:GATE_I + head + 1]
        fg_col = gates[:, GATE_F + head:GATE_F + head + 1]
        lf_row = to_row(_log_sigmoid(fg_col))
        b_col = pick(causal, lf_row, 0.0, jnp.sum)
        b_row = to_row(b_col)
        ig_row = to_row(ig_col)
        m_prev = m_ref[head]

        d = jnp.where(causal, b_col - b_row + ig_row, -jnp.inf)
        inter = b_col + m_prev
        m_t = jnp.maximum(inter, jnp.max(d, axis=-1, keepdims=True))
        w = jnp.exp(d - m_t)
        a = jnp.exp(inter - m_t)

        q = zm_ref[:, lanes]
        k = zm_ref[:, M_WIDTH + head * M_DK:M_WIDTH + (head + 1) * M_DK]
        v = zm_ref[:, 2 * M_WIDTH + head * M_DV:2 * M_WIDTH + (head + 1) * M_DV]
        o = zm_ref[:, 3 * M_WIDTH + head * M_DV:3 * M_WIDTH + (head + 1) * M_DV]
        qf_ref[head] = q.astype(F32)
        for t in range(steps):
            nexp_ref[head, pl.ds(t, nb, stride=steps), :] = n_ref[:, lanes]

        for j in range(nb // 2):
            pair = pl.ds(j * 2 * steps, 2 * steps)
            q8 = qf_ref[head, pair, :].astype(BF16)
            r0 = _dot(q8, c_ref[2 * j, head].astype(BF16))
            r1 = _dot(q8, c_ref[2 * j + 1, head].astype(BF16))
            qc_ref[head, pair, :] = jnp.where(sub < steps, r0, r1)

        s = _dot_nt(q, k) * w
        qn = jnp.sum(qf_ref[head] * nexp_ref[head], axis=-1, keepdims=True)
        num = a * qc_ref[head] + _dot(s.astype(BF16), v)
        den = a * qn + jnp.sum(s, axis=-1, keepdims=True)
        hh = num / jnp.maximum(jnp.abs(den), jnp.exp(-m_t))
        um_ref[:, lanes] = _head_norm_gate(hh, o, gain_ref[:, lanes])

        b_last = pick(last, b_row, 0.0, jnp.sum)
        g_col = b_last - b_col + ig_col
        g_max = pick(same, to_row(g_col), -jnp.inf, jnp.max)
        m_new = jnp.maximum(b_last + m_prev, g_max)
        decay = jnp.exp(b_last + m_prev - m_new)
        wk = jnp.exp(g_col - m_new)
        mo_ref[head] = m_new

        kf = k.astype(F32)
        wkk_ref[head] = wk * kf
        dec_ref[head] = jnp.broadcast_to(decay, (rows, M_DV))
        n_sum = wkk_ref[head, pl.ds(0, nb, stride=steps), :]
        for t in range(1, steps):
            n_sum = n_sum + wkk_ref[head, pl.ds(t, nb, stride=steps), :]
        no_ref[:, lanes] = dec_ref[head, pl.ds(0, nb, stride=steps), :] * n_ref[:, lanes] + n_sum

        k_t = kf.T
        wv = (wk * v.astype(F32)).astype(BF16)
        for b in range(nb):
            upd = _dot(jnp.where(tok_batch == b, k_t, 0.0).astype(BF16), wv)
            co_ref[b, head] = dec_ref[head, pl.ds(b * steps, 1), :] * c_ref[b, head] + upd


def _swa_sample_kernel(sinks_ref, qa_ref, kva_ref, ckt_ref, cvt_ref, ua_ref, nkt_ref, nvt_ref,
                       s_ref, e_ref, pv_ref, *, steps):
    rows = SAMPLE_ROWS
    nb = rows // steps
    pair_rows = 2 * steps
    hw = A_KV_HEADS * A_HD
    tok = lax.broadcasted_iota(jnp.int32, (rows, LANES), 0)
    lane = lax.broadcasted_iota(jnp.int32, (rows, LANES), 1)
    low = lane < A_HD

    qf = qa_ref[...].astype(F32)
    q_blocks = []
    for p in range(A_GROUP):
        qp = qf[:, p * LANES:(p + 1) * LANES]
        q_blocks.append(jnp.where(low, qp, 0.0))
        q_blocks.append(jnp.where(low, 0.0, qp))
    n_blk = len(q_blocks)

    kv_new = kva_ref[...]
    kv_new_t = kv_new.T
    k_new_t = kv_new_t[:hw]
    v_new_t = kv_new_t[hw:]
    v_new = kv_new[:, hw:].astype(BF16)
    s_new = _dot(jnp.concatenate(q_blocks, axis=0).astype(BF16), k_new_t.astype(BF16))

    def pair_rows_of(blocks, j):
        return jnp.concatenate([blk[j * pair_rows:(j + 1) * pair_rows] for blk in blocks], axis=0)

    first = (lax.broadcasted_iota(jnp.int32, (n_blk * pair_rows, LANES), 0) % pair_rows) < steps

    for j in range(nb // 2):
        q_pair = pair_rows_of(q_blocks, j).astype(BF16)
        s_pair = jnp.where(first, _dot(q_pair, ckt_ref[2 * j].astype(BF16)),
                           _dot(q_pair, ckt_ref[2 * j + 1].astype(BF16)))
        for blk in range(n_blk):
            s_ref[pl.ds(blk * rows + j * pair_rows, pair_rows), :] = s_pair[blk * pair_rows:(blk + 1) * pair_rows]

    t_q = tok % steps
    dist_c = WINDOW + t_q - lane
    valid_c = dist_c < WINDOW
    dist_n = t_q - lane % steps
    valid_n = ((tok // steps) == (lane // steps)) & (dist_n >= 0)
    dist_cf = dist_c.astype(F32)
    dist_nf = dist_n.astype(F32)
    pv_new = []
    inv_den = []
    for blk in range(n_blk):
        head = (blk % A_KV_HEADS) * A_GROUP + blk // A_KV_HEADS
        slope = _alibi_slope(head)
        sink = sinks_ref[head]
        r = pl.ds(blk * rows, rows)
        s_c = jnp.where(valid_c, s_ref[r, :] - slope * dist_cf, -jnp.inf)
        s_n = jnp.where(valid_n, s_new[blk * rows:(blk + 1) * rows] - slope * dist_nf, -jnp.inf)
        mx = jnp.maximum(jnp.maximum(jnp.max(s_c, axis=-1, keepdims=True),
                                     jnp.max(s_n, axis=-1, keepdims=True)), sink)
        e_c = jnp.exp(s_c - mx)
        e_n = jnp.exp(s_n - mx)
        den = jnp.sum(e_c, axis=-1, keepdims=True) + jnp.sum(e_n, axis=-1, keepdims=True) + jnp.exp(sink - mx)
        e_ref[r, :] = e_c
        pv_new.append(_dot(e_n.astype(BF16), v_new))
        inv_den.append(1.0 / den)

    for j in range(nb // 2):
        e_pair = jnp.concatenate([e_ref[pl.ds(blk * rows + j * pair_rows, pair_rows), :] for blk in range(n_blk)],
                                 axis=0).astype(BF16)
        pv_pair = jnp.where(first, _dot_nt(e_pair, cvt_ref[2 * j].astype(BF16)),
                            _dot_nt(e_pair, cvt_ref[2 * j + 1].astype(BF16)))
        for blk in range(n_blk):
            pv_ref[pl.ds(blk * rows + j * pair_rows, pair_rows), :] = pv_pair[blk * pair_rows:(blk + 1) * pair_rows]

    for p in range(A_GROUP):
        halves = []
        for kvh in range(A_KV_HEADS):
            blk = p * A_KV_HEADS + kvh
            halves.append((pv_ref[pl.ds(blk * rows, rows), :] + pv_new[blk]) * inv_den[blk])
        ua_ref[:, p * LANES:(p + 1) * LANES] = jnp.where(low, halves[0], halves[1]).astype(BF16)

    tail = lane >= WINDOW - steps
    for b in range(nb):
        shift = (WINDOW - steps - b * steps) % rows
        nkt_ref[b] = jnp.where(tail, pltpu.roll(k_new_t, shift, 1), pltpu.roll(ckt_ref[b], WINDOW - steps, 1))
        nvt_ref[b] = jnp.where(tail, pltpu.roll(v_new_t, shift, 1), pltpu.roll(cvt_ref[b], WINDOW - steps, 1))


def _sample_mixers_kernel(sinks_ref, zm_ref, gates_ref, gain_ref, c_ref, n_ref, m_ref, qa_ref, kva_ref,
                          ckt_ref, cvt_ref, um_ref, co_ref, no_ref, mo_ref, ua_ref, nkt_ref, nvt_ref,
                          qf_ref, qc_ref, nexp_ref, wkk_ref, dec_ref, s_ref, e_ref, pv_ref, *, steps):
    _mlstm_sample_kernel(zm_ref, gates_ref, gain_ref, c_ref, n_ref, m_ref, um_ref, co_ref, no_ref, mo_ref,
                         qf_ref, qc_ref, nexp_ref, wkk_ref, dec_ref, steps=steps)
    _swa_sample_kernel(sinks_ref, qa_ref, kva_ref, ckt_ref, cvt_ref, ua_ref, nkt_ref, nvt_ref,
                       s_ref, e_ref, pv_ref, steps=steps)


def _sample_mixers(sinks, zm, gates, gain, c, n2, m_rows, qa, kva, ckt, cvt, steps):
    rows = zm.shape[0]
    assert ckt.shape[1:] == (A_KV_HEADS * A_HD, WINDOW) and SAMPLE_ROWS == WINDOW == LANES
    nb = SAMPLE_ROWS // steps

    def tokens(width):
        return pl.BlockSpec((SAMPLE_ROWS, width), lambda g: (g, 0))

    c_spec = pl.BlockSpec((nb, M_HEADS, M_DK, M_DV), lambda g: (g, 0, 0, 0))
    n_spec = pl.BlockSpec((nb, M_HEADS * M_DK), lambda g: (g, 0))
    m_spec = pl.BlockSpec((M_HEADS, SAMPLE_ROWS, 1), lambda g: (0, g, 0))
    cache = pl.BlockSpec((nb, LANES, WINDOW), lambda g: (g, 0, 0))
    per_head = pltpu.VMEM((M_HEADS, SAMPLE_ROWS, M_DK), F32)
    stacked = pltpu.VMEM((A_HEADS * SAMPLE_ROWS, LANES), F32)
    return pl.pallas_call(
        functools.partial(_sample_mixers_kernel, steps=steps),
        grid=(rows // SAMPLE_ROWS,),
        in_specs=[pl.BlockSpec(memory_space=pltpu.SMEM), tokens(ZM_WIDTH), tokens(LANES),
                  pl.BlockSpec((1, M_WIDTH), lambda g: (0, 0)), c_spec, n_spec, m_spec,
                  tokens(A_WIDTH), tokens(KV_WIDTH), cache, cache],
        out_specs=[tokens(M_WIDTH), c_spec, n_spec, m_spec, tokens(A_WIDTH), cache, cache],
        out_shape=[jax.ShapeDtypeStruct((rows, M_WIDTH), BF16),
                   jax.ShapeDtypeStruct(c.shape, F32),
                   jax.ShapeDtypeStruct(n2.shape, F32),
                   jax.ShapeDtypeStruct(m_rows.shape, F32),
                   jax.ShapeDtypeStruct((rows, A_WIDTH), BF16),
                   jax.ShapeDtypeStruct(ckt.shape, F32),
                   jax.ShapeDtypeStruct(cvt.shape, F32)],
        scratch_shapes=[per_head,
                        per_head,
                        per_head,
                        per_head,
                        per_head,
                        stacked,
                        stacked,
                        stacked],
        compiler_params=pltpu.CompilerParams(dimension_semantics=("parallel",),
                                             vmem_limit_bytes=VMEM_LIMIT),
        name="sample_mixers",
    )(sinks, zm, gates, gain, c, n2, m_rows, qa, kva, ckt, cvt)


def _prep_proj(w_in, b_gate):
    m_end = 4 * M_WIDTH
    g_end = m_end + 2 * M_HEADS
    wt = w_in.T
    qa = wt[g_end:g_end + A_WIDTH].reshape(A_KV_HEADS, A_GROUP, A_HD, D_MODEL).transpose(1, 0, 2, 3)
    gates = jnp.pad(wt[m_end:g_end], ((0, LANES - 2 * M_HEADS), (0, 0)))
    win = tuple(w.astype(BF16) for w in (wt[:m_end], qa.reshape(A_WIDTH, D_MODEL), wt[g_end + A_WIDTH:], gates))
    bias = jnp.pad(b_gate, (0, LANES - 2 * M_HEADS)).reshape(1, LANES)
    return win, bias


def _prep_out(w_out):
    wa = w_out[M_WIDTH:].reshape(A_KV_HEADS, A_GROUP, A_HD, D_MODEL).transpose(1, 0, 2, 3)
    return jnp.concatenate([w_out[:M_WIDTH], wa.reshape(A_WIDTH, D_MODEL)], axis=0).astype(BF16)


def kernel(x_prompt, x_sample, cache_swa_k, cache_swa_v, state_mlstm_C, state_mlstm_n, state_mlstm_m,
           norm_gains, ffn_w_gate, ffn_w_up, ffn_w_down, w_in, b_gate, mlstm_norm_gain, attn_sinks, w_out):
    assert norm_gains.shape[0] == 1, "single layer"
    batch, seq, _ = x_prompt.shape
    dec_batch, steps, _ = x_sample.shape
    gains = norm_gains[0]
    ffn = (ffn_w_gate, ffn_w_up, ffn_w_down)
    win, bias = _prep_proj(w_in[0], b_gate[0])
    wo = _prep_out(w_out[0])
    gain = mlstm_norm_gain[0].reshape(1, M_WIDTH)
    sinks = attn_sinks[0]
    kv_lanes = A_KV_HEADS * A_HD

    chunk = min(MLSTM_CHUNK, seq)
    assert seq % chunk == 0
    xp, zm, qa, kva, gates = _ffn_proj(x_prompt.reshape(batch * seq, D_MODEL), gains, *ffn, win, bias,
                                       0, 0, 1, 2, chunk)
    um, ua, p_c, p_n, p_m = _prompt_mixers(sinks, zm, gates, gain, qa, kva, batch, seq, chunk)
    yp = _mix_ffn(xp, um, ua, wo, gains, *ffn, 1, 3, 4, 5).reshape(batch, seq, D_MODEL)
    keep = min(WINDOW, seq)
    kv_tail = kva.reshape(batch, seq, KV_WIDTH)[:, seq - keep:]
    p_k = kv_tail[..., :kv_lanes].reshape(1, batch, keep, A_KV_HEADS, A_HD)
    p_v = kv_tail[..., kv_lanes:].reshape(1, batch, keep, A_KV_HEADS, A_HD)
    p_c = p_c[None]
    p_n = p_n[None, :, :, 0, :]
    p_m = p_m[None, :, :, 0, 0]

    xs, zm_s, qa_s, kva_s, gates_s = _ffn_proj(x_sample.reshape(dec_batch * steps, D_MODEL), gains, *ffn,
                                               win, bias, 0, 0, 1, 2, min(chunk, dec_batch * steps))
    m_rows = jnp.repeat(state_mlstm_m[0].T, steps, axis=1)[..., None]
    wc = cache_swa_k.shape[2]
    um_s, s_c, s_n2, s_m_rows, ua_s, s_kt, s_vt = _sample_mixers(
        sinks, zm_s, gates_s, gain, state_mlstm_C[0], state_mlstm_n[0].reshape(dec_batch, M_HEADS * M_DK), m_rows,
        qa_s, kva_s, cache_swa_k[0].reshape(dec_batch, wc, kv_lanes).transpose(0, 2, 1),
        cache_swa_v[0].reshape(dec_batch, wc, kv_lanes).transpose(0, 2, 1), steps)
    ys = _mix_ffn(xs, um_s, ua_s, wo, gains, *ffn, 1, 3, 4, 5).reshape(dec_batch, steps, D_MODEL)
    s_n = s_n2.reshape(1, dec_batch, M_HEADS, M_DK)
    s_m = s_m_rows[:, ::steps, 0].T[None]
    s_k = s_kt.transpose(0, 2, 1).reshape(1, dec_batch, wc, A_KV_HEADS, A_HD)
    s_v = s_vt.transpose(0, 2, 1).reshape(1, dec_batch, wc, A_KV_HEADS, A_HD)
    return (yp, ys, p_k, p_v, p_c, p_n, p_m, s_k, s_v, s_c[None], s_n, s_m)
```

```python
import functools

import jax
import jax.numpy as jnp
from jax import lax
from jax.experimental import pallas as pl
from jax.experimental.pallas import tpu as pltpu

F32 = jnp.float32
BF16 = jnp.bfloat16

D_MODEL = 1024
D_FF = 2816
M_HEADS = 4
M_DK = 128
M_DV = 128
M_WIDTH = M_HEADS * M_DV
A_HEADS = 8
A_KV_HEADS = 2
A_GROUP = A_HEADS // A_KV_HEADS
A_HD = 64
A_WIDTH = A_HEADS * A_HD
WINDOW = 128
RMS_EPS = 1e-6

LANES = 128
FFN_CHUNK = 256
N_FFN_CHUNKS = D_FF // FFN_CHUNK
FFN_STAGE_SLOTS = 4
TOKEN_TILE = 512
MLSTM_CHUNK = 256
ZM_WIDTH = 4 * M_WIDTH
KV_WIDTH = 2 * A_KV_HEADS * A_HD
VMEM_LIMIT = 56 * 1024 * 1024
MLSTM_CHUNKS_PER_STEP = 8
MLSTM_AUX_ROWS = 16
GATE_I, GATE_F, GATE_B, GATE_X =0, M_HEADS, 2 * M_HEADS, 3 * M_HEADS

NT_DIMS = (((1,), (1,)), ((), ()))


def _rms(x, g):
    ms = jnp.mean(x * x, axis=-1, keepdims=True)
    return x * lax.rsqrt(ms + RMS_EPS) * g


def _log_sigmoid(x):
    return jnp.minimum(x, 0.0) - jnp.log1p(jnp.exp(-jnp.abs(x)))


def _dot(a, b):
    return jnp.dot(a, b, preferred_element_type=F32)


def _dot_nt(a, b):
    return lax.dot_general(a, b, NT_DIMS, preferred_element_type=F32)


def _resident(shape):
    nd = len(shape)
    return pl.BlockSpec(shape, lambda *_: (0,) * nd, pipeline_mode=pl.Buffered(1))


def _ffn_weight_copies(w_hbm, j, f, slot, stage_refs, sems):
    wg_hbm, wu_hbm, wd_hbm = w_hbm
    cols = pl.ds(f * FFN_CHUNK, FFN_CHUNK)
    sources = (wg_hbm.at[0, j, :, cols], wu_hbm.at[0, j, :, cols], wd_hbm.at[0, j, cols, :])
    return [pltpu.make_async_copy(src, stage.at[slot], sems.at[slot, k])
            for k, (src, stage) in enumerate(zip(sources, stage_refs))]


def _swiglu_chunk(h_ref, acc_ref, wg, wu, wd, first):
    h = h_ref[...]
    g = _dot(h, wg)
    u = _dot(h, wu)
    part = _dot((g * jax.nn.sigmoid(g) * u).astype(BF16), wd)
    if first:
        acc_ref[...] = part
    else:
        acc_ref[...] += part


def _swiglu(x, gpre, w_refs, h_ref, acc_ref, fetch=None):
    wg_ref, wu_ref, wd_ref = w_refs
    h_ref[...] = _rms(x, gpre).astype(BF16)
    if fetch is None:
        for f in range(N_FFN_CHUNKS):
            _swiglu_chunk(h_ref, acc_ref, wg_ref[f], wu_ref[f], wd_ref[f], first=f == 0)
        return

    stage_g, stage_u, stage_d = fetch[2]
    for f in range(min(FFN_STAGE_SLOTS, N_FFN_CHUNKS)):
        for copy in _ffn_weight_copies(*fetch[:2], f, f, *fetch[2:]):
            copy.start()
    acc_ref[...] = jnp.zeros_like(acc_ref)

    def step(f, carry):
        slot = f % FFN_STAGE_SLOTS
        for copy in _ffn_weight_copies(*fetch[:2], f, slot, *fetch[2:]):
            copy.wait()
        wg_ref[f] = stage_g[slot].astype(BF16)
        wu_ref[f] = stage_u[slot].astype(BF16)
        wd_ref[f] = stage_d[slot].astype(BF16)

        @pl.when(f + FFN_STAGE_SLOTS < N_FFN_CHUNKS)
        def _():
            for copy in _ffn_weight_copies(*fetch[:2], f + FFN_STAGE_SLOTS, slot, *fetch[2:]):
                copy.start()

        _swiglu_chunk(h_ref, acc_ref, wg_ref[f], wu_ref[f], wd_ref[f], first=False)
        return carry

    lax.fori_loop(0, N_FFN_CHUNKS, step, 0)


def _first_step_fetches(body, fetch):
    first = pl.program_id(0) == 0

    @pl.when(first)
    def _():
        body(fetch)

    @pl.when(jnp.logical_not(first))
    def _():
        body(None)


def _half_step(x, acc_ref, gpost):
    return x + _rms(acc_ref[...], 0.5 * gpost)


def _project(x, gain, win_refs, bias_ref, h_ref, zm_ref, qa_ref, kva_ref, gates_ref, chunk):
    h_ref[...] = _rms(x, gain).astype(BF16)

    wm_ref, wqa_ref, wkv_ref, wg_ref = win_refs

    def cols(lo, width):
        return _dot_nt(h_ref[...], wm_ref[lo:lo + width, :])

    gates = _dot_nt(h_ref[...], wg_ref[...]) + bias_ref[...]
    lane = lax.broadcasted_iota(jnp.int32, (chunk, LANES), 1)
    pos = lax.broadcasted_iota(jnp.int32, (chunk, LANES), 0)
    for c0 in range(0, gates.shape[0], chunk):
        g = gates[c0:c0 + chunk]
        b = jnp.where((lane >= GATE_F) & (lane < GATE_B), _log_sigmoid(g), 0.0)
        shift = 1
        while shift < chunk:
            b = b + jnp.where(pos >= shift, pltpu.roll(b, shift, 0), 0.0)
            shift *= 2
        x_gate = pltpu.roll(g, GATE_X - GATE_I, 1) - pltpu.roll(b, GATE_X - GATE_F, 1)
        out = jnp.where(lane < GATE_B, g, jnp.where(lane < GATE_X, pltpu.roll(b, GATE_B - GATE_F, 1), x_gate))
        gates_ref[c0:c0 + chunk, :] = jnp.where(lane < GATE_X + M_HEADS, out, 0.0)

    zm_ref[:, 0:M_WIDTH] = cols(0, M_WIDTH).astype(BF16)
    zm_ref[:, M_WIDTH:2 * M_WIDTH] = (cols(M_WIDTH, M_WIDTH) * (M_DK ** -0.5)).astype(BF16)
    zm_ref[:, 2 * M_WIDTH:3 * M_WIDTH] = cols(2 * M_WIDTH, M_WIDTH).astype(BF16)
    zm_ref[:, 3 * M_WIDTH:] = jax.nn.sigmoid(cols(3 * M_WIDTH, M_WIDTH)).astype(BF16)
    qa_ref[...] = (_dot_nt(h_ref[...], wqa_ref[...]) * (A_HD ** -0.5)).astype(BF16)
    kva_ref[...] = _dot_nt(h_ref[...], wkv_ref[...])


def _ffn_proj_kernel(x_ref, gains_ref, wg_hbm, wu_hbm, wd_hbm, wm_ref, wqa_ref, wkv_ref, wgate_ref, bias_ref,
                     x1_ref, zm_ref, qa_ref, kva_ref, gates_ref,
                     h_ref, acc_ref, wg_ref, wu_ref, wd_ref, stage_g, stage_u, stage_d, sems,
                     *, j, pre, post, mixer, chunk):
    def body(fetch):
        x = x_ref[...]
        _swiglu(x, gains_ref[pre:pre + 1, :], (wg_ref, wu_ref, wd_ref), h_ref, acc_ref, fetch)
        x1 = _half_step(x, acc_ref, gains_ref[post:post + 1, :])
        x1_ref[...] = x1
        _project(x1, gains_ref[mixer:mixer + 1, :], (wm_ref, wqa_ref, wkv_ref, wgate_ref), bias_ref, h_ref,
                 zm_ref, qa_ref, kva_ref, gates_ref, chunk)

    _first_step_fetches(body, ((wg_hbm, wu_hbm, wd_hbm), j, (stage_g, stage_u, stage_d), sems))


def _mix_ffn_kernel(x_ref, um_ref, ua_ref, wom_ref, woa_ref, gains_ref, wg_hbm, wu_hbm, wd_hbm, o_ref,
                    h_ref, acc_ref, wg_ref, wu_ref, wd_ref, stage_g, stage_u, stage_d, sems,
                    *, j, mix, pre, post):
    def body(fetch):
        mixed = _dot(um_ref[...], wom_ref[...]) + _dot(ua_ref[...], woa_ref[...])
        x = x_ref[...] + _rms(mixed, gains_ref[mix:mix + 1, :])
        _swiglu(x, gains_ref[pre:pre + 1, :], (wg_ref, wu_ref, wd_ref), h_ref, acc_ref, fetch)
        o_ref[...] = _half_step(x, acc_ref, gains_ref[post:post + 1, :])

    _first_step_fetches(body, ((wg_hbm, wu_hbm, wd_hbm), j, (stage_g, stage_u, stage_d), sems))


def _ffn_scratch(tm):
    return [pltpu.VMEM((tm, D_MODEL), BF16),
            pltpu.VMEM((tm, D_MODEL), F32),
            pltpu.VMEM((N_FFN_CHUNKS, D_MODEL, FFN_CHUNK), BF16),
            pltpu.VMEM((N_FFN_CHUNKS, D_MODEL, FFN_CHUNK), BF16),
            pltpu.VMEM((N_FFN_CHUNKS, FFN_CHUNK, D_MODEL), BF16),
            pltpu.VMEM((FFN_STAGE_SLOTS, D_MODEL, FFN_CHUNK), F32),
            pltpu.VMEM((FFN_STAGE_SLOTS, D_MODEL, FFN_CHUNK), F32),
            pltpu.VMEM((FFN_STAGE_SLOTS, FFN_CHUNK, D_MODEL), F32),
            pltpu.SemaphoreType.DMA((FFN_STAGE_SLOTS, 3))]


def _dense_params():
    return pltpu.CompilerParams(dimension_semantics=("arbitrary",), vmem_limit_bytes=VMEM_LIMIT)


_HBM = pl.BlockSpec(memory_space=pl.ANY)


def _ffn_proj(x, gains, wg, wu, wd, win, bias, j, pre, post, mixer, chunk):
    n = x.shape[0]
    tm = min(TOKEN_TILE, n)
    assert tm % chunk == 0

    def rows(width):
        return pl.BlockSpec((tm, width), lambda i: (i, 0))

    return pl.pallas_call(
        functools.partial(_ffn_proj_kernel, j=j, pre=pre, post=post, mixer=mixer, chunk=chunk),
        grid=(n // tm,),
        in_specs=[rows(D_MODEL), _resident(gains.shape), _HBM, _HBM, _HBM,
                  *[_resident(w.shape) for w in win], _resident(bias.shape)],
        out_specs=[rows(D_MODEL), rows(ZM_WIDTH), rows(A_WIDTH), rows(KV_WIDTH), rows(LANES)],
        out_shape=[jax.ShapeDtypeStruct((n, D_MODEL), F32),
                   jax.ShapeDtypeStruct((n, ZM_WIDTH), BF16),
                   jax.ShapeDtypeStruct((n, A_WIDTH), BF16),
                   jax.ShapeDtypeStruct((n, KV_WIDTH), F32),
                   jax.ShapeDtypeStruct((n, LANES), F32)],
        scratch_shapes=_ffn_scratch(tm),
        compiler_params=_dense_params(),
        name="ffn_half_step_input_projection",
    )(x, gains, wg, wu, wd, *win, bias)


def _mix_ffn(x, um, ua, wo, gains, wg, wu, wd, j, mix, pre, post):
    n = x.shape[0]
    tm = min(TOKEN_TILE, n)
    row = pl.BlockSpec((tm, D_MODEL), lambda i: (i, 0))
    half = pl.BlockSpec((tm, M_WIDTH), lambda i: (i, 0))
    return pl.pallas_call(
        functools.partial(_mix_ffn_kernel, j=j, mix=mix, pre=pre, post=post),
        grid=(n // tm,),
        in_specs=[row, half, half, *[_resident(w.shape) for w in wo], _resident(gains.shape), _HBM, _HBM, _HBM],
        out_specs=row,
        out_shape=jax.ShapeDtypeStruct((n, D_MODEL), F32),
        scratch_shapes=_ffn_scratch(tm),
        compiler_params=_dense_params(),
        name="out_proj_ffn_half_step",
    )(x, um, ua, *wo, gains, wg, wu, wd)


def _head_norm_gate(hh, o, gain):
    hn = hh * lax.rsqrt(jnp.mean(hh * hh, axis=-1, keepdims=True) + RMS_EPS) * gain
    return (o.astype(F32) * hn).astype(BF16)


def _mlstm_chunks(zm_ref, gates_ref, gain_ref, um_ref, cnt_ref, ms_ref, chunk):
    si = lax.broadcasted_iota(jnp.int32, (chunk, chunk), 0)
    ti = lax.broadcasted_iota(jnp.int32, (chunk, chunk), 1)
    causal = si <= ti
    ones_rows = jnp.where(lax.broadcasted_iota(jnp.int32, (MLSTM_AUX_ROWS, chunk), 0) == 0, 1.0, 0.0).astype(BF16)

    def run_chunk(c):
        rows = slice(c * chunk, (c + 1) * chunk)
        gates = gates_ref[rows, :]
        gates_t = gates.T
        for h in range(M_HEADS):
            lanes = slice(h * M_DK, (h + 1) * M_DK)
            q = zm_ref[rows, lanes]
            k = zm_ref[rows, M_WIDTH + h * M_DK:M_WIDTH + (h + 1) * M_DK]
            v = zm_ref[rows, 2 * M_WIDTH + h * M_DV:2 * M_WIDTH + (h + 1) * M_DV]
            o = zm_ref[rows, 3 * M_WIDTH + h * M_DV:3 * M_WIDTH + (h + 1) * M_DV]
            x_col = gates[:, GATE_X + h:GATE_X + h + 1]
            ig_row = gates_t[GATE_I + h:GATE_I + h + 1, :]
            b_row = gates_t[GATE_B + h:GATE_B + h + 1, :]
            m_prev = ms_ref[h][:, 0:1]
            cnt = cnt_ref[h]

            xm = jnp.where(causal, x_col, -jnp.inf)
            mu = jnp.maximum(m_prev, jnp.max(xm, axis=0, keepdims=True))
            a = jnp.exp(m_prev - mu)
            s_t = _dot_nt(k, q) * jnp.exp(xm - mu)
            v_ext = jnp.concatenate([v.astype(F32).T.astype(BF16), ones_rows], axis=0)
            nd = a * _dot_nt(cnt.astype(BF16), q) + _dot(v_ext, s_t.astype(BF16))
            den = nd[M_DV:M_DV + 1, :]
            hh = nd[:M_DV, :] * (1.0 / jnp.maximum(jnp.abs(den), jnp.exp(-(b_row + mu))))
            hn = hh * lax.rsqrt(jnp.mean(hh * hh, axis=0, keepdims=True) + RMS_EPS)
            um_ref[rows, lanes] = (hn.T * gain_ref[:, lanes] * o.astype(F32)).astype(BF16)

            b_last = b_row[:, chunk - 1:chunk]
            g = b_last - b_row + ig_row
            m_new = jnp.maximum(b_last + m_prev, jnp.max(g, axis=-1, keepdims=True))
            decay = jnp.exp(b_last + m_prev - m_new)
            wv = (jnp.exp(g - m_new) * v_ext.astype(F32)).astype(BF16)
            cnt_ref[h] = decay * cnt + _dot(wv, k)
            ms_ref[h] = jnp.broadcast_to(m_new, (1, LANES))

    return [functools.partial(run_chunk, c) for c in range(zm_ref.shape[0] // chunk)]


def _alibi_slope(head):
    return 2.0 ** (-8.0 * (head + 1) / A_HEADS)


def _swa_blocks(sinks_ref, qa_ref, kvc_ref, kvp_ref, ua_ref, mask_ref, first):
    hw = A_KV_HEADS * A_HD
    n_blocks = qa_ref.shape[0] // WINDOW
    qi = lax.broadcasted_iota(jnp.int32, (WINDOW, 2 * WINDOW), 0)
    kj = lax.broadcasted_iota(jnp.int32, (WINDOW, 2 * WINDOW), 1)
    dist = WINDOW + qi - kj
    band = (dist >= 0) & (dist < WINDOW)
    mask_ref[0] = jnp.where(band, 0.0, -jnp.inf)
    mask_ref[1] = jnp.where(band & (kj >= WINDOW), 0.0, -jnp.inf)
    lane = lax.broadcasted_iota(jnp.int32, (WINDOW, LANES), 1)
    low = lane < A_HD

    assert 2 * WINDOW <= 256, "positions must stay exactly representable in bf16"
    q_pos = (WINDOW + lax.broadcasted_iota(jnp.int32, (WINDOW, LANES), 0)).astype(F32)
    k_lane = lax.broadcasted_iota(jnp.int32, (2 * WINDOW, LANES), 1)
    k_pos = lax.broadcasted_iota(jnp.int32, (2 * WINDOW, LANES), 0).astype(F32)
    spare = (A_HD, 0)
    k_bias = [jnp.where(k_lane == spare[kvh], 1.0, jnp.where(k_lane == spare[kvh] + 1, k_pos, 0.0)).astype(BF16)
              for kvh in range(A_KV_HEADS)]
    q_bias = {}
    for head in range(A_HEADS):
        slope = _alibi_slope(head)
        at = spare[head // A_GROUP]
        q_bias[head] = jnp.where(lane == at, -slope * q_pos, jnp.where(lane == at + 1, slope, 0.0)).astype(BF16)

    def run_block(j):
        rows = slice(j * WINDOW, (j + 1) * WINDOW)
        kvc = kvc_ref[rows, :]
        kvp = kvp_ref[...] if j == 0 else kvc_ref[(j - 1) * WINDOW:j * WINDOW, :]
        mask = mask_ref[jnp.where(first, 1, 0)] if j == 0 else mask_ref[0]
        kk = jnp.concatenate([kvp[:, :hw], kvc[:, :hw]], axis=0).astype(BF16)
        kk_biased = [jnp.where(k_lane < A_HD, kk, k_bias[0]), jnp.where(k_lane < A_HD, k_bias[1], kk)]
        vv = jnp.concatenate([kvp[:, hw:], kvc[:, hw:]], axis=0).astype(BF16)
        for p in range(A_GROUP):
            qp = qa_ref[rows, p * LANES:(p + 1) * LANES]
            outs = []
            for kvh in range(A_KV_HEADS):
                head = kvh * A_GROUP + p
                keep = low if kvh == 0 else jnp.logical_not(low)
                s = _dot_nt(jnp.where(keep, qp, q_bias[head]), kk_biased[kvh])
                s = s + mask
                sink = sinks_ref[head]
                mx = jnp.maximum(jnp.max(s, axis=-1, keepdims=True), sink)
                e = jnp.exp(s - mx)
                den = jnp.sum(e, axis=-1, keepdims=True) + jnp.exp(sink - mx)
                outs.append(_dot(e.astype(BF16), vv) / den)
            ua_ref[rows, p * LANES:(p + 1) * LANES] = jnp.where(low, outs[0], outs[1]).astype(BF16)

    return [functools.partial(run_block, j) for j in range(n_blocks)]


def _prompt_mixers_kernel(sinks_ref, zm_ref, gates_ref, gain_ref, qa_ref, kvc_ref, kvp_ref,
                          um_ref, ua_ref, c_ref, n_ref, m_ref, cnt_ref, ms_ref, mask_ref, *, chunk):
    step = pl.program_id(1)

    @pl.when(step == 0)
    def _():
        cnt_ref[...] = jnp.zeros_like(cnt_ref)
        ms_ref[...] = jnp.zeros_like(ms_ref)

    chunks = _mlstm_chunks(zm_ref, gates_ref, gain_ref, um_ref, cnt_ref, ms_ref, chunk)
    blocks = _swa_blocks(sinks_ref, qa_ref, kvc_ref, kvp_ref, ua_ref, mask_ref, step == 0)
    per_chunk = len(blocks) // len(chunks)
    for c, run_chunk in enumerate(chunks):
        run_chunk()
        for run_block in blocks[c * per_chunk:(c + 1) * per_chunk]:
            run_block()

    @pl.when(step == pl.num_programs(1) - 1)
    def _():
        for h in range(M_HEADS):
            c_ref[0, h] = cnt_ref[h][:M_DV, :].T
            n_ref[0, h] = cnt_ref[h][M_DV:M_DV + 1, :]
            m_ref[0, h] = ms_ref[h]


def _prompt_mixers(sinks, zm, gates, gain, qa, kva, batch, seq, chunk):
    n = batch * seq
    per_step = min(MLSTM_CHUNKS_PER_STEP, seq // chunk)
    rows = per_step * chunk
    assert seq % rows == 0 and chunk % WINDOW == 0
    ns = seq // rows

    def step_rows(width):
        return pl.BlockSpec((rows, width), lambda b, i: (b * ns + i, 0))

    def prev_block(b, i):
        return (jnp.maximum((b * ns + i) * (rows // WINDOW) - 1, 0), 0)

    def state(*dims):
        return pl.BlockSpec((1, M_HEADS) + dims, lambda b, i: (b, 0, 0, 0))

    return pl.pallas_call(
        functools.partial(_prompt_mixers_kernel, chunk=chunk),
        grid=(batch, ns),
        in_specs=[pl.BlockSpec(memory_space=pltpu.SMEM), step_rows(ZM_WIDTH), step_rows(LANES),
                  pl.BlockSpec((1, M_WIDTH), lambda b, i: (0, 0)),
                  step_rows(A_WIDTH), step_rows(KV_WIDTH), pl.BlockSpec((WINDOW, KV_WIDTH), prev_block)],
        out_specs=[step_rows(M_WIDTH), step_rows(A_WIDTH), state(M_DK, M_DV), state(1, M_DK), state(1, LANES)],
        out_shape=[jax.ShapeDtypeStruct((n, M_WIDTH), BF16),
                   jax.ShapeDtypeStruct((n, A_WIDTH), BF16),
                   jax.ShapeDtypeStruct((batch, M_HEADS, M_DK, M_DV), F32),
                   jax.ShapeDtypeStruct((batch, M_HEADS, 1, M_DK), F32),
                   jax.ShapeDtypeStruct((batch, M_HEADS, 1, LANES), F32)],
        scratch_shapes=[pltpu.VMEM((M_HEADS, M_DV + MLSTM_AUX_ROWS, M_DK), F32),
                        pltpu.VMEM((M_HEADS, 1, LANES), F32),
                        pltpu.VMEM((2, WINDOW, 2 * WINDOW), F32)],
        compiler_params=pltpu.CompilerParams(dimension_semantics=("parallel", "arbitrary"),
                                             vmem_limit_bytes=VMEM_LIMIT),
        name="prompt_mixers",
    )(sinks, zm, gates, gain, qa, kva, kva)


SAMPLE_ROWS = 128


def _mlstm_sample_kernel(zm_ref, gates_ref, gain_ref, c_ref, n_ref, m_ref,
                         um_ref, co_ref, no_ref, mo_ref,
                         qf_ref, qc_ref, nexp_ref, wkk_ref, dec_ref, *, steps):
    rows = SAMPLE_ROWS
    nb = rows // steps
    row = lax.broadcasted_iota(jnp.int32, (rows, rows), 0)
    col = lax.broadcasted_iota(jnp.int32, (rows, rows), 1)
    same = (row // steps) == (col // steps)
    causal = same & (col <= row)
    eye = row == col
    last = same & ((col % steps) == steps - 1)
    sub = lax.broadcasted_iota(jnp.int32, (2 * steps, M_DV), 0)
    tok_batch = lax.broadcasted_iota(jnp.int32, (M_DK, rows), 1) // steps

    def to_row(x_col):
        return jnp.sum(jnp.where(eye, x_col, 0.0), axis=0, keepdims=True)

    def pick(mask, x_row, fill, reduce):
        return reduce(jnp.where(mask, x_row, fill), axis=-1, keepdims=True)

    gates = gates_ref[...]
    for head in range(M_HEADS):
        lanes = slice(head * M_DK, (head + 1) * M_DK)
        ig_col = gates[:, GATE_I + head:GATE_I + head + 1]
        fg_col = gates[:, GATE_F + head:GATE_F + head + 1]
        lf_row = to_row(_log_sigmoid(fg_col))
        b_col = pick(causal, lf_row, 0.0, jnp.sum)
        b_row = to_row(b_col)
        ig_row = to_row(ig_col)
        m_prev = m_ref[head]

        d = jnp.where(causal, b_col - b_row + ig_row, -jnp.inf)
        inter = b_col + m_prev
        m_t = jnp.maximum(inter, jnp.max(d, axis=-1, keepdims=True))
        w = jnp.exp(d - m_t)
        a = jnp.exp(inter - m_t)

        q = zm_ref[:, lanes]
        k = zm_ref[:, M_WIDTH + head * M_DK:M_WIDTH + (head + 1) * M_DK]
        v = zm_ref[:, 2 * M_WIDTH + head * M_DV:2 * M_WIDTH + (head + 1) * M_DV]
        o = zm_ref[:, 3 * M_WIDTH + head * M_DV:3 * M_WIDTH + (head + 1) * M_DV]
        qf_ref[head] = q.astype(F32)
        for t in range(steps):
            nexp_ref[head, pl.ds(t, nb, stride=steps), :] = n_ref[:, lanes]

        for j in range(nb // 2):
            pair = pl.ds(j * 2 * steps, 2 * steps)
            q8 = qf_ref[head, pair, :].astype(BF16)
            r0 = _dot(q8, c_ref[2 * j, head].astype(BF16))
            r1 = _dot(q8, c_ref[2 * j + 1, head].astype(BF16))
            qc_ref[head, pair, :] = jnp.where(sub < steps, r0, r1)

        s = _dot_nt(q, k) * w
        qn = jnp.sum(qf_ref[head] * nexp_ref[head], axis=-1, keepdims=True)
        num = a * qc_ref[head] + _dot(s.astype(BF16), v)
        den = a * qn + jnp.sum(s, axis=-1, keepdims=True)
        hh = num / jnp.maximum(jnp.abs(den), jnp.exp(-m_t))
        um_ref[:, lanes] = _head_norm_gate(hh, o, gain_ref[:, lanes])

        b_last = pick(last, b_row, 0.0, jnp.sum)
        g_col = b_last - b_col + ig_col
        g_max = pick(same, to_row(g_col), -jnp.inf, jnp.max)
        m_new = jnp.maximum(b_last + m_prev, g_max)
        decay = jnp.exp(b_last + m_prev - m_new)
        wk = jnp.exp(g_col - m_new)
        mo_ref[head] = m_new

        kf = k.astype(F32)
        wkk_ref[head] = wk * kf
        dec_ref[head] = jnp.broadcast_to(decay, (rows, M_DV))
        n_sum = wkk_ref[head, pl.ds(0, nb, stride=steps), :]
        for t in range(1, steps):
            n_sum = n_sum + wkk_ref[head, pl.ds(t, nb, stride=steps), :]
        no_ref[:, lanes] = dec_ref[head, pl.ds(0, nb, stride=steps), :] * n_ref[:, lanes] + n_sum

        k_t = kf.T
        wv = (wk * v.astype(F32)).astype(BF16)
        for b in range(nb):
            upd = _dot(jnp.where(tok_batch == b, k_t, 0.0).astype(BF16), wv)
            co_ref[b, head] = dec_ref[head, pl.ds(b * steps, 1), :] * c_ref[b, head] + upd


def _swa_sample_kernel(sinks_ref, qa_ref, kva_ref, ckt_ref, cvt_ref, ua_ref, nkt_ref, nvt_ref,
                       s_ref, e_ref, pv_ref, *, steps):
    rows = SAMPLE_ROWS
    nb = rows // steps
    pair_rows = 2 * steps
    hw = A_KV_HEADS * A_HD
    tok = lax.broadcasted_iota(jnp.int32, (rows, LANES), 0)
    lane = lax.broadcasted_iota(jnp.int32, (rows, LANES), 1)
    low = lane < A_HD

    qf = qa_ref[...].astype(F32)
    q_blocks = []
    for p in range(A_GROUP):
        qp = qf[:, p * LANES:(p + 1) * LANES]
        q_blocks.append(jnp.where(low, qp, 0.0))
        q_blocks.append(jnp.where(low, 0.0, qp))
    n_blk = len(q_blocks)

    kv_new = kva_ref[...]
    kv_new_t = kv_new.T
    k_new_t = kv_new_t[:hw]
    v_new_t = kv_new_t[hw:]
    v_new = kv_new[:, hw:].astype(BF16)
    s_new = _dot(jnp.concatenate(q_blocks, axis=0).astype(BF16), k_new_t.astype(BF16))

    def pair_rows_of(blocks, j):
        return jnp.concatenate([blk[j * pair_rows:(j + 1) * pair_rows] for blk in blocks], axis=0)

    first = (lax.broadcasted_iota(jnp.int32, (n_blk * pair_rows, LANES), 0) % pair_rows) < steps

    for j in range(nb // 2):
        q_pair = pair_rows_of(q_blocks, j).astype(BF16)
        s_pair = jnp.where(first, _dot(q_pair, ckt_ref[2 * j].astype(BF16)),
                           _dot(q_pair, ckt_ref[2 * j + 1].astype(BF16)))
        for blk in range(n_blk):
            s_ref[pl.ds(blk * rows + j * pair_rows, pair_rows), :] = s_pair[blk * pair_rows:(blk + 1) * pair_rows]

    t_q = tok % steps
    dist_c = WINDOW + t_q - lane
    valid_c = dist_c < WINDOW
    dist_n = t_q - lane % steps
    valid_n = ((tok // steps) == (lane // steps)) & (dist_n >= 0)
    dist_cf = dist_c.astype(F32)
    dist_nf = dist_n.astype(F32)
    pv_new = []
    inv_den = []
    for blk in range(n_blk):
        head = (blk % A_KV_HEADS) * A_GROUP + blk // A_KV_HEADS
        slope = _alibi_slope(head)
        sink = sinks_ref[head]
        r = pl.ds(blk * rows, rows)
        s_c = jnp.where(valid_c, s_ref[r, :] - slope * dist_cf, -jnp.inf)
        s_n = jnp.where(valid_n, s_new[blk * rows:(blk + 1) * rows] - slope * dist_nf, -jnp.inf)
        mx = jnp.maximum(jnp.maximum(jnp.max(s_c, axis=-1, keepdims=True),
                                     jnp.max(s_n, axis=-1, keepdims=True)), sink)
        e_c = jnp.exp(s_c - mx)
        e_n = jnp.exp(s_n - mx)
        den = jnp.sum(e_c, axis=-1, keepdims=True) + jnp.sum(e_n, axis=-1, keepdims=True) + jnp.exp(sink - mx)
        e_ref[r, :] = e_c
        pv_new.append(_dot(e_n.astype(BF16), v_new))
        inv_den.append(1.0 / den)

    for j in range(nb // 2):
        e_pair = jnp.concatenate([e_ref[pl.ds(blk * rows + j * pair_rows, pair_rows), :] for blk in range(n_blk)],
                                 axis=0).astype(BF16)
        pv_pair = jnp.where(first, _dot_nt(e_pair, cvt_ref[2 * j].astype(BF16)),
                            _dot_nt(e_pair, cvt_ref[2 * j + 1].astype(BF16)))
        for blk in range(n_blk):
            pv_ref[pl.ds(blk * rows + j * pair_rows, pair_rows), :] = pv_pair[blk * pair_rows:(blk + 1) * pair_rows]

    for p in range(A_GROUP):
        halves = []
        for kvh in range(A_KV_HEADS):
            blk = p * A_KV_HEADS + kvh
            halves.append((pv_ref[pl.ds(blk * rows, rows), :] + pv_new[blk]) * inv_den[blk])
        ua_ref[:, p * LANES:(p + 1) * LANES] = jnp.where(low, halves[0], halves[1]).astype(BF16)

    tail = lane >= WINDOW - steps
    for b in range(nb):
        shift = (WINDOW - steps - b * steps) % rows
        nkt_ref[b] = jnp.where(tail, pltpu.roll(k_new_t, shift, 1), pltpu.roll(ckt_ref[b], WINDOW - steps, 1))
        nvt_ref[b] = jnp.where(tail, pltpu.roll(v_new_t, shift, 1), pltpu.roll(cvt_ref[b], WINDOW - steps, 1))


def _sample_mixers_kernel(sinks_ref, zm_ref, gates_ref, gain_ref, c_ref, n_ref, m_ref, qa_ref, kva_ref,
                          ckt_ref, cvt_ref, um_ref, co_ref, no_ref, mo_ref, ua_ref, nkt_ref, nvt_ref,
                          qf_ref, qc_ref, nexp_ref, wkk_ref, dec_ref, s_ref, e_ref, pv_ref, *, steps):
    _mlstm_sample_kernel(zm_ref, gates_ref, gain_ref, c_ref, n_ref, m_ref, um_ref, co_ref, no_ref, mo_ref,
                         qf_ref, qc_ref, nexp_ref, wkk_ref, dec_ref, steps=steps)
    _swa_sample_kernel(sinks_ref, qa_ref, kva_ref, ckt_ref, cvt_ref, ua_ref, nkt_ref, nvt_ref,
                       s_ref, e_ref, pv_ref, steps=steps)


def _sample_mixers(sinks, zm, gates, gain, c, n2, m_rows, qa, kva, ckt, cvt, steps):
    rows = zm.shape[0]
    assert ckt.shape[1:] == (A_KV_HEADS * A_HD, WINDOW) and SAMPLE_ROWS == WINDOW == LANES
    nb = SAMPLE_ROWS // steps

    def tokens(width):
        return pl.BlockSpec((SAMPLE_ROWS, width), lambda g: (g, 0))

    c_spec = pl.BlockSpec((nb, M_HEADS, M_DK, M_DV), lambda g: (g, 0, 0, 0))
    n_spec = pl.BlockSpec((nb, M_HEADS * M_DK), lambda g: (g, 0))
    m_spec = pl.BlockSpec((M_HEADS, SAMPLE_ROWS, 1), lambda g: (0, g, 0))
    cache = pl.BlockSpec((nb, LANES, WINDOW), lambda g: (g, 0, 0))
    per_head = pltpu.VMEM((M_HEADS, SAMPLE_ROWS, M_DK), F32)
    stacked = pltpu.VMEM((A_HEADS * SAMPLE_ROWS, LANES), F32)
    return pl.pallas_call(
        functools.partial(_sample_mixers_kernel, steps=steps),
        grid=(rows // SAMPLE_ROWS,),
        in_specs=[pl.BlockSpec(memory_space=pltpu.SMEM), tokens(ZM_WIDTH), tokens(LANES),
                  pl.BlockSpec((1, M_WIDTH), lambda g: (0, 0)), c_spec, n_spec, m_spec,
                  tokens(A_WIDTH), tokens(KV_WIDTH), cache, cache],
        out_specs=[tokens(M_WIDTH), c_spec, n_spec, m_spec, tokens(A_WIDTH), cache, cache],
        out_shape=[jax.ShapeDtypeStruct((rows, M_WIDTH), BF16),
                   jax.ShapeDtypeStruct(c.shape, F32),
                   jax.ShapeDtypeStruct(n2.shape, F32),
                   jax.ShapeDtypeStruct(m_rows.shape, F32),
                   jax.ShapeDtypeStruct((rows, A_WIDTH), BF16),
                   jax.ShapeDtypeStruct(ckt.shape, F32),
                   jax.ShapeDtypeStruct(cvt.shape, F32)],
        scratch_shapes=[per_head,
                        per_head,
                        per_head,
                        per_head,
                        per_head,
                        stacked,
                        stacked,
                        stacked],
        compiler_params=pltpu.CompilerParams(dimension_semantics=("parallel",),
                                             vmem_limit_bytes=VMEM_LIMIT),
        name="sample_mixers",
    )(sinks, zm, gates, gain, c, n2, m_rows, qa, kva, ckt, cvt)


def _prep_proj(w_in, b_gate):
    m_end = 4 * M_WIDTH
    g_end = m_end + 2 * M_HEADS
    wt = w_in.T
    qa = wt[g_end:g_end + A_WIDTH].reshape(A_KV_HEADS, A_GROUP, A_HD, D_MODEL).transpose(1, 0, 2, 3)
    gates = jnp.pad(wt[m_end:g_end], ((0, LANES - 2 * M_HEADS), (0, 0)))
    win = tuple(w.astype(BF16) for w in (wt[:m_end], qa.reshape(A_WIDTH, D_MODEL), wt[g_end + A_WIDTH:], gates))
    bias = jnp.pad(b_gate, (0, LANES - 2 * M_HEADS)).reshape(1, LANES)
    return win, bias


def _prep_out(w_out):
    wa = w_out[M_WIDTH:].reshape(A_KV_HEADS, A_GROUP, A_HD, D_MODEL).transpose(1, 0, 2, 3)
    return w_out[:M_WIDTH].astype(BF16), wa.reshape(A_WIDTH, D_MODEL).astype(BF16)


def kernel(x_prompt, x_sample, cache_swa_k, cache_swa_v, state_mlstm_C, state_mlstm_n, state_mlstm_m,
           norm_gains, ffn_w_gate, ffn_w_up, ffn_w_down, w_in, b_gate, mlstm_norm_gain, attn_sinks, w_out):
    assert norm_gains.shape[0] == 1, "single layer"
    batch, seq, _ = x_prompt.shape
    dec_batch, steps, _ = x_sample.shape
    gains = norm_gains[0]
    ffn = (ffn_w_gate, ffn_w_up, ffn_w_down)
    win, bias = _prep_proj(w_in[0], b_gate[0])
    wo = _prep_out(w_out[0])
    gain = mlstm_norm_gain[0].reshape(1, M_WIDTH)
    sinks = attn_sinks[0]
    kv_lanes = A_KV_HEADS * A_HD

    chunk = min(MLSTM_CHUNK, seq)
    assert seq % chunk == 0
    xp, zm, qa, kva, gates = _ffn_proj(x_prompt.reshape(batch * seq, D_MODEL), gains, *ffn, win, bias,
                                       0, 0, 1, 2, chunk)
    um, ua, p_c, p_n, p_m = _prompt_mixers(sinks, zm, gates, gain, qa, kva, batch, seq, chunk)
    yp = _mix_ffn(xp, um, ua, wo, gains, *ffn, 1, 3, 4, 5).reshape(batch, seq, D_MODEL)
    keep = min(WINDOW, seq)
    kv_tail = kva.reshape(batch, seq, KV_WIDTH)[:, seq - keep:]
    p_k = kv_tail[..., :kv_lanes].reshape(1, batch, keep, A_KV_HEADS, A_HD)
    p_v = kv_tail[..., kv_lanes:].reshape(1, batch, keep, A_KV_HEADS, A_HD)
    p_c = p_c[None]
    p_n = p_n[None, :, :, 0, :]
    p_m = p_m[None, :, :, 0, 0]

    xs, zm_s, qa_s, kva_s, gates_s = _ffn_proj(x_sample.reshape(dec_batch * steps, D_MODEL), gains, *ffn,
                                               win, bias, 0, 0, 1, 2, min(chunk, dec_batch * steps))
    m_rows = jnp.repeat(state_mlstm_m[0].T, steps, axis=1)[..., None]
    wc = cache_swa_k.shape[2]
    um_s, s_c, s_n2, s_m_rows, ua_s, s_kt, s_vt = _sample_mixers(
        sinks, zm_s, gates_s, gain, state_mlstm_C[0], state_mlstm_n[0].reshape(dec_batch, M_HEADS * M_DK), m_rows,
        qa_s, kva_s, cache_swa_k[0].reshape(dec_batch, wc, kv_lanes).transpose(0, 2, 1),
        cache_swa_v[0].reshape(dec_batch, wc, kv_lanes).transpose(0, 2, 1), steps)
    ys = _mix_ffn(xs, um_s, ua_s, wo, gains, *ffn, 1, 3, 4, 5).reshape(dec_batch, steps, D_MODEL)
    s_n = s_n2.reshape(1, dec_batch, M_HEADS, M_DK)
    s_m = s_m_rows[:, ::steps, 0].T[None]
    s_k = s_kt.transpose(0, 2, 1).reshape(1, dec_batch, wc, A_KV_HEADS, A_HD)
    s_v = s_vt.transpose(0, 2, 1).reshape(1, dec_batch, wc, A_KV_HEADS, A_HD)
    return (yp, ys, p_k, p_v, p_c, p_n, p_m, s_k, s_v, s_c[None], s_n, s_m)
```

```python
import functools

import jax
import jax.numpy as jnp
from jax import lax
from jax.experimental import pallas as pl
from jax.experimental.pallas import tpu as pltpu

F32 = jnp.float32
BF16 = jnp.bfloat16

D_MODEL = 1024
D_FF = 2816
M_HEADS = 4
M_DK = 128
M_DV = 128
M_WIDTH = M_HEADS * M_DV
A_HEADS = 8
A_KV_HEADS = 2
A_GROUP = A_HEADS // A_KV_HEADS
A_HD = 64
A_WIDTH = A_HEADS * A_HD
WINDOW = 128
RMS_EPS = 1e-6

LANES = 128
FFN_CHUNK = 256
N_FFN_CHUNKS = D_FF // FFN_CHUNK
FFN_STAGE_SLOTS = 4
TOKEN_TILE = 512
MLSTM_CHUNK = 256
ZM_WIDTH = 4 * M_WIDTH
KV_WIDTH = 2 * A_KV_HEADS * A_HD
VMEM_LIMIT = 56 * 1024 * 1024
MLSTM_CHUNKS_PER_STEP = 4
MLSTM_AUX_ROWS = 16
GATE_I, GATE_F, GATE_B, GATE_X =0, M_HEADS, 2 * M_HEADS, 3 * M_HEADS

NT_DIMS = (((1,), (1,)), ((), ()))


def _rms(x, g):
    ms = jnp.mean(x * x, axis=-1, keepdims=True)
    return x * lax.rsqrt(ms + RMS_EPS) * g


def _log_sigmoid(x):
    return jnp.minimum(x, 0.0) - jnp.log1p(jnp.exp(-jnp.abs(x)))


def _dot(a, b):
    return jnp.dot(a, b, preferred_element_type=F32)


def _dot_nt(a, b):
    return lax.dot_general(a, b, NT_DIMS, preferred_element_type=F32)


def _resident(shape):
    nd = len(shape)
    return pl.BlockSpec(shape, lambda *_: (0,) * nd, pipeline_mode=pl.Buffered(1))


def _ffn_weight_copies(w_hbm, j, f, slot, stage_refs, sems):
    wg_hbm, wu_hbm, wd_hbm = w_hbm
    cols = pl.ds(f * FFN_CHUNK, FFN_CHUNK)
    sources = (wg_hbm.at[0, j, :, cols], wu_hbm.at[0, j, :, cols], wd_hbm.at[0, j, cols, :])
    return [pltpu.make_async_copy(src, stage.at[slot], sems.at[slot, k])
            for k, (src, stage) in enumerate(zip(sources, stage_refs))]


def _swiglu_chunk(h_ref, acc_ref, wg, wu, wd, first):
    h = h_ref[...]
    g = _dot(h, wg)
    u = _dot(h, wu)
    part = _dot((g * jax.nn.sigmoid(g) * u).astype(BF16), wd)
    if first:
        acc_ref[...] = part
    else:
        acc_ref[...] += part


def _swiglu(x, gpre, w_refs, h_ref, acc_ref, fetch=None):
    wg_ref, wu_ref, wd_ref = w_refs
    h_ref[...] = _rms(x, gpre).astype(BF16)
    if fetch is None:
        for f in range(N_FFN_CHUNKS):
            _swiglu_chunk(h_ref, acc_ref, wg_ref[f], wu_ref[f], wd_ref[f], first=f == 0)
        return

    stage_g, stage_u, stage_d = fetch[2]
    for f in range(min(FFN_STAGE_SLOTS, N_FFN_CHUNKS)):
        for copy in _ffn_weight_copies(*fetch[:2], f, f, *fetch[2:]):
            copy.start()
    acc_ref[...] = jnp.zeros_like(acc_ref)

    def step(f, carry):
        slot = f % FFN_STAGE_SLOTS
        for copy in _ffn_weight_copies(*fetch[:2], f, slot, *fetch[2:]):
            copy.wait()
        wg_ref[f] = stage_g[slot].astype(BF16)
        wu_ref[f] = stage_u[slot].astype(BF16)
        wd_ref[f] = stage_d[slot].astype(BF16)

        @pl.when(f + FFN_STAGE_SLOTS < N_FFN_CHUNKS)
        def _():
            for copy in _ffn_weight_copies(*fetch[:2], f + FFN_STAGE_SLOTS, slot, *fetch[2:]):
                copy.start()

        _swiglu_chunk(h_ref, acc_ref, wg_ref[f], wu_ref[f], wd_ref[f], first=False)
        return carry

    lax.fori_loop(0, N_FFN_CHUNKS, step, 0)


def _first_step_fetches(body, fetch):
    first = pl.program_id(0) == 0

    @pl.when(first)
    def _():
        body(fetch)

    @pl.when(jnp.logical_not(first))
    def _():
        body(None)


def _half_step(x, acc_ref, gpost):
    return x + _rms(acc_ref[...], 0.5 * gpost)


def _project(x, gain, win_refs, bias_ref, h_ref, zm_ref, qa_ref, kva_ref, gates_ref, chunk):
    h_ref[...] = _rms(x, gain).astype(BF16)

    wm_ref, wqa_ref, wkv_ref, wg_ref = win_refs

    def cols(lo, width):
        return _dot_nt(h_ref[...], wm_ref[lo:lo + width, :])

    gates = _dot_nt(h_ref[...], wg_ref[...]) + bias_ref[...]
    lane = lax.broadcasted_iota(jnp.int32, (chunk, LANES), 1)
    pos = lax.broadcasted_iota(jnp.int32, (chunk, LANES), 0)
    for c0 in range(0, gates.shape[0], chunk):
        g = gates[c0:c0 + chunk]
        b = jnp.where((lane >= GATE_F) & (lane < GATE_B), _log_sigmoid(g), 0.0)
        shift = 1
        while shift < chunk:
            b = b + jnp.where(pos >= shift, pltpu.roll(b, shift, 0), 0.0)
            shift *= 2
        x_gate = pltpu.roll(g, GATE_X - GATE_I, 1) - pltpu.roll(b, GATE_X - GATE_F, 1)
        out = jnp.where(lane < GATE_B, g, jnp.where(lane < GATE_X, pltpu.roll(b, GATE_B - GATE_F, 1), x_gate))
        gates_ref[c0:c0 + chunk, :] = jnp.where(lane < GATE_X + M_HEADS, out, 0.0)

    zm_ref[:, 0:M_WIDTH] = cols(0, M_WIDTH).astype(BF16)
    zm_ref[:, M_WIDTH:2 * M_WIDTH] = (cols(M_WIDTH, M_WIDTH) * (M_DK ** -0.5)).astype(BF16)
    zm_ref[:, 2 * M_WIDTH:3 * M_WIDTH] = cols(2 * M_WIDTH, M_WIDTH).astype(BF16)
    zm_ref[:, 3 * M_WIDTH:] = jax.nn.sigmoid(cols(3 * M_WIDTH, M_WIDTH)).astype(BF16)
    qa_ref[...] = (_dot_nt(h_ref[...], wqa_ref[...]) * (A_HD ** -0.5)).astype(BF16)
    kva_ref[...] = _dot_nt(h_ref[...], wkv_ref[...])


def _ffn_proj_kernel(x_ref, gains_ref, wg_hbm, wu_hbm, wd_hbm, wm_ref, wqa_ref, wkv_ref, wgate_ref, bias_ref,
                     x1_ref, zm_ref, qa_ref, kva_ref, gates_ref,
                     h_ref, acc_ref, wg_ref, wu_ref, wd_ref, stage_g, stage_u, stage_d, sems,
                     *, j, pre, post, mixer, chunk):
    def body(fetch):
        x = x_ref[...]
        _swiglu(x, gains_ref[pre:pre + 1, :], (wg_ref, wu_ref, wd_ref), h_ref, acc_ref, fetch)
        x1 = _half_step(x, acc_ref, gains_ref[post:post + 1, :])
        x1_ref[...] = x1
        _project(x1, gains_ref[mixer:mixer + 1, :], (wm_ref, wqa_ref, wkv_ref, wgate_ref), bias_ref, h_ref,
                 zm_ref, qa_ref, kva_ref, gates_ref, chunk)

    _first_step_fetches(body, ((wg_hbm, wu_hbm, wd_hbm), j, (stage_g, stage_u, stage_d), sems))


def _mix_ffn_kernel(x_ref, um_ref, ua_ref, wo_ref, gains_ref, wg_hbm, wu_hbm, wd_hbm, o_ref,
                    h_ref, acc_ref, wg_ref, wu_ref, wd_ref, stage_g, stage_u, stage_d, sems,
                    *, j, mix, pre, post):
    def body(fetch):
        u = jnp.concatenate([um_ref[...], ua_ref[...]], axis=-1)
        x = x_ref[...] + _rms(_dot(u, wo_ref[...]), gains_ref[mix:mix + 1, :])
        _swiglu(x, gains_ref[pre:pre + 1, :], (wg_ref, wu_ref, wd_ref), h_ref, acc_ref, fetch)
        o_ref[...] = _half_step(x, acc_ref, gains_ref[post:post + 1, :])

    _first_step_fetches(body, ((wg_hbm, wu_hbm, wd_hbm), j, (stage_g, stage_u, stage_d), sems))


def _ffn_scratch(tm):
    return [pltpu.VMEM((tm, D_MODEL), BF16),
            pltpu.VMEM((tm, D_MODEL), F32),
            pltpu.VMEM((N_FFN_CHUNKS, D_MODEL, FFN_CHUNK), BF16),
            pltpu.VMEM((N_FFN_CHUNKS, D_MODEL, FFN_CHUNK), BF16),
            pltpu.VMEM((N_FFN_CHUNKS, FFN_CHUNK, D_MODEL), BF16),
            pltpu.VMEM((FFN_STAGE_SLOTS, D_MODEL, FFN_CHUNK), F32),
            pltpu.VMEM((FFN_STAGE_SLOTS, D_MODEL, FFN_CHUNK), F32),
            pltpu.VMEM((FFN_STAGE_SLOTS, FFN_CHUNK, D_MODEL), F32),
            pltpu.SemaphoreType.DMA((FFN_STAGE_SLOTS, 3))]


def _dense_params():
    return pltpu.CompilerParams(dimension_semantics=("arbitrary",), vmem_limit_bytes=VMEM_LIMIT)


_HBM = pl.BlockSpec(memory_space=pl.ANY)


def _ffn_proj(x, gains, wg, wu, wd, win, bias, j, pre, post, mixer, chunk):
    n = x.shape[0]
    tm = min(TOKEN_TILE, n)
    assert tm % chunk == 0

    def rows(width):
        return pl.BlockSpec((tm, width), lambda i: (i, 0))

    return pl.pallas_call(
        functools.partial(_ffn_proj_kernel, j=j, pre=pre, post=post, mixer=mixer, chunk=chunk),
        grid=(n // tm,),
        in_specs=[rows(D_MODEL), _resident(gains.shape), _HBM, _HBM, _HBM,
                  *[_resident(w.shape) for w in win], _resident(bias.shape)],
        out_specs=[rows(D_MODEL), rows(ZM_WIDTH), rows(A_WIDTH), rows(KV_WIDTH), rows(LANES)],
        out_shape=[jax.ShapeDtypeStruct((n, D_MODEL), F32),
                   jax.ShapeDtypeStruct((n, ZM_WIDTH), BF16),
                   jax.ShapeDtypeStruct((n, A_WIDTH), BF16),
                   jax.ShapeDtypeStruct((n, KV_WIDTH), F32),
                   jax.ShapeDtypeStruct((n, LANES), F32)],
        scratch_shapes=_ffn_scratch(tm),
        compiler_params=_dense_params(),
        name="ffn_half_step_input_projection",
    )(x, gains, wg, wu, wd, *win, bias)


def _mix_ffn(x, um, ua, wo, gains, wg, wu, wd, j, mix, pre, post):
    n = x.shape[0]
    tm = min(TOKEN_TILE, n)
    row = pl.BlockSpec((tm, D_MODEL), lambda i: (i, 0))
    half = pl.BlockSpec((tm, M_WIDTH), lambda i: (i, 0))
    return pl.pallas_call(
        functools.partial(_mix_ffn_kernel, j=j, mix=mix, pre=pre, post=post),
        grid=(n // tm,),
        in_specs=[row, half, half, _resident(wo.shape), _resident(gains.shape), _HBM, _HBM, _HBM],
        out_specs=row,
        out_shape=jax.ShapeDtypeStruct((n, D_MODEL), F32),
        scratch_shapes=_ffn_scratch(tm),
        compiler_params=_dense_params(),
        name="out_proj_ffn_half_step",
    )(x, um, ua, wo, gains, wg, wu, wd)


def _head_norm_gate(hh, o, gain):
    hn = hh * lax.rsqrt(jnp.mean(hh * hh, axis=-1, keepdims=True) + RMS_EPS) * gain
    return (o.astype(F32) * hn).astype(BF16)


def _mlstm_chunks(zm_ref, gates_ref, gain_ref, um_ref, cnt_ref, ms_ref, chunk):
    si = lax.broadcasted_iota(jnp.int32, (chunk, chunk), 0)
    ti = lax.broadcasted_iota(jnp.int32, (chunk, chunk), 1)
    causal = si <= ti
    ones_rows = jnp.where(lax.broadcasted_iota(jnp.int32, (MLSTM_AUX_ROWS, chunk), 0) == 0, 1.0, 0.0).astype(BF16)

    def run_chunk(c):
        rows = slice(c * chunk, (c + 1) * chunk)
        gates = gates_ref[rows, :]
        gates_t = gates.T
        for h in range(M_HEADS):
            lanes = slice(h * M_DK, (h + 1) * M_DK)
            q = zm_ref[rows, lanes]
            k = zm_ref[rows, M_WIDTH + h * M_DK:M_WIDTH + (h + 1) * M_DK]
            v = zm_ref[rows, 2 * M_WIDTH + h * M_DV:2 * M_WIDTH + (h + 1) * M_DV]
            o = zm_ref[rows, 3 * M_WIDTH + h * M_DV:3 * M_WIDTH + (h + 1) * M_DV]
            x_col = gates[:, GATE_X + h:GATE_X + h + 1]
            ig_row = gates_t[GATE_I + h:GATE_I + h + 1, :]
            b_row = gates_t[GATE_B + h:GATE_B + h + 1, :]
            m_prev = ms_ref[h][:, 0:1]
            cnt = cnt_ref[h]

            xm = jnp.where(causal, x_col, -jnp.inf)
            mu = jnp.maximum(m_prev, jnp.max(xm, axis=0, keepdims=True))
            a = jnp.exp(m_prev - mu)
            s_t = _dot_nt(k, q) * jnp.exp(xm - mu)
            v_ext = jnp.concatenate([v.astype(F32).T.astype(BF16), ones_rows], axis=0)
            nd = a * _dot_nt(cnt.astype(BF16), q) + _dot(v_ext, s_t.astype(BF16))
            den = nd[M_DV:M_DV + 1, :]
            hh = nd[:M_DV, :] * (1.0 / jnp.maximum(jnp.abs(den), jnp.exp(-(b_row + mu))))
            hn = hh * lax.rsqrt(jnp.mean(hh * hh, axis=0, keepdims=True) + RMS_EPS)
            um_ref[rows, lanes] = (hn.T * gain_ref[:, lanes] * o.astype(F32)).astype(BF16)

            b_last = b_row[:, chunk - 1:chunk]
            g = b_last - b_row + ig_row
            m_new = jnp.maximum(b_last + m_prev, jnp.max(g, axis=-1, keepdims=True))
            decay = jnp.exp(b_last + m_prev - m_new)
            wv = (jnp.exp(g - m_new) * v_ext.astype(F32)).astype(BF16)
            cnt_ref[h] = decay * cnt + _dot(wv, k)
            ms_ref[h] = jnp.broadcast_to(m_new, (1, LANES))

    return [functools.partial(run_chunk, c) for c in range(zm_ref.shape[0] // chunk)]


def _alibi_slope(head):
    return 2.0 ** (-8.0 * (head + 1) / A_HEADS)


def _swa_blocks(sinks_ref, qa_ref, kvc_ref, kvp_ref, ua_ref, mask_ref, first):
    hw = A_KV_HEADS * A_HD
    n_blocks = qa_ref.shape[0] // WINDOW
    qi = lax.broadcasted_iota(jnp.int32, (WINDOW, 2 * WINDOW), 0)
    kj = lax.broadcasted_iota(jnp.int32, (WINDOW, 2 * WINDOW), 1)
    dist = WINDOW + qi - kj
    band = (dist >= 0) & (dist < WINDOW)
    mask_ref[0] = jnp.where(band, 0.0, -jnp.inf)
    mask_ref[1] = jnp.where(band & (kj >= WINDOW), 0.0, -jnp.inf)
    lane = lax.broadcasted_iota(jnp.int32, (WINDOW, LANES), 1)
    low = lane < A_HD

    assert 2 * WINDOW <= 256, "positions must stay exactly representable in bf16"
    q_pos = (WINDOW + lax.broadcasted_iota(jnp.int32, (WINDOW, LANES), 0)).astype(F32)
    k_lane = lax.broadcasted_iota(jnp.int32, (2 * WINDOW, LANES), 1)
    k_pos = lax.broadcasted_iota(jnp.int32, (2 * WINDOW, LANES), 0).astype(F32)
    spare = (A_HD, 0)
    k_bias = [jnp.where(k_lane == spare[kvh], 1.0, jnp.where(k_lane == spare[kvh] + 1, k_pos, 0.0)).astype(BF16)
              for kvh in range(A_KV_HEADS)]
    q_bias = {}
    for head in range(A_HEADS):
        slope = _alibi_slope(head)
        at = spare[head // A_GROUP]
        q_bias[head] = jnp.where(lane == at, -slope * q_pos, jnp.where(lane == at + 1, slope, 0.0)).astype(BF16)

    def run_block(j):
        rows = slice(j * WINDOW, (j + 1) * WINDOW)
        kvc = kvc_ref[rows, :]
        kvp = kvp_ref[...] if j == 0 else kvc_ref[(j - 1) * WINDOW:j * WINDOW, :]
        mask = mask_ref[jnp.where(first, 1, 0)] if j == 0 else mask_ref[0]
        kk = jnp.concatenate([kvp[:, :hw], kvc[:, :hw]], axis=0).astype(BF16)
        kk_biased = [jnp.where(k_lane < A_HD, kk, k_bias[0]), jnp.where(k_lane < A_HD, k_bias[1], kk)]
        vv = jnp.concatenate([kvp[:, hw:], kvc[:, hw:]], axis=0).astype(BF16)
        for p in range(A_GROUP):
            qp = qa_ref[rows, p * LANES:(p + 1) * LANES]
            outs = []
            for kvh in range(A_KV_HEADS):
                head = kvh * A_GROUP + p
                keep = low if kvh == 0 else jnp.logical_not(low)
                s = _dot_nt(jnp.where(keep, qp, q_bias[head]), kk_biased[kvh])
                s = s + mask
                sink = sinks_ref[head]
                mx = jnp.maximum(jnp.max(s, axis=-1, keepdims=True), sink)
                e = jnp.exp(s - mx)
                den = jnp.sum(e, axis=-1, keepdims=True) + jnp.exp(sink - mx)
                outs.append(_dot(e.astype(BF16), vv) / den)
            ua_ref[rows, p * LANES:(p + 1) * LANES] = jnp.where(low, outs[0], outs[1]).astype(BF16)

    return [functools.partial(run_block, j) for j in range(n_blocks)]


def _prompt_mixers_kernel(sinks_ref, zm_ref, gates_ref, gain_ref, qa_ref, kvc_ref, kvp_ref,
                          um_ref, ua_ref, c_ref, n_ref, m_ref, cnt_ref, ms_ref, mask_ref, *, chunk):
    step = pl.program_id(1)

    @pl.when(step == 0)
    def _():
        cnt_ref[...] = jnp.zeros_like(cnt_ref)
        ms_ref[...] = jnp.zeros_like(ms_ref)

    chunks = _mlstm_chunks(zm_ref, gates_ref, gain_ref, um_ref, cnt_ref, ms_ref, chunk)
    blocks = _swa_blocks(sinks_ref, qa_ref, kvc_ref, kvp_ref, ua_ref, mask_ref, step == 0)
    per_chunk = len(blocks) // len(chunks)
    for c, run_chunk in enumerate(chunks):
        run_chunk()
        for run_block in blocks[c * per_chunk:(c + 1) * per_chunk]:
            run_block()

    @pl.when(step == pl.num_programs(1) - 1)
    def _():
        for h in range(M_HEADS):
            c_ref[0, h] = cnt_ref[h][:M_DV, :].T
            n_ref[0, h] = cnt_ref[h][M_DV:M_DV + 1, :]
            m_ref[0, h] = ms_ref[h]


def _prompt_mixers(sinks, zm, gates, gain, qa, kva, batch, seq, chunk):
    n = batch * seq
    per_step = min(MLSTM_CHUNKS_PER_STEP, seq // chunk)
    rows = per_step * chunk
    assert seq % rows == 0 and chunk % WINDOW == 0
    ns = seq // rows

    def step_rows(width):
        return pl.BlockSpec((rows, width), lambda b, i: (b * ns + i, 0))

    def prev_block(b, i):
        return (jnp.maximum((b * ns + i) * (rows // WINDOW) - 1, 0), 0)

    def state(*dims):
        return pl.BlockSpec((1, M_HEADS) + dims, lambda b, i: (b, 0, 0, 0))

    return pl.pallas_call(
        functools.partial(_prompt_mixers_kernel, chunk=chunk),
        grid=(batch, ns),
        in_specs=[pl.BlockSpec(memory_space=pltpu.SMEM), step_rows(ZM_WIDTH), step_rows(LANES),
                  pl.BlockSpec((1, M_WIDTH), lambda b, i: (0, 0)),
                  step_rows(A_WIDTH), step_rows(KV_WIDTH), pl.BlockSpec((WINDOW, KV_WIDTH), prev_block)],
        out_specs=[step_rows(M_WIDTH), step_rows(A_WIDTH), state(M_DK, M_DV), state(1, M_DK), state(1, LANES)],
        out_shape=[jax.ShapeDtypeStruct((n, M_WIDTH), BF16),
                   jax.ShapeDtypeStruct((n, A_WIDTH), BF16),
                   jax.ShapeDtypeStruct((batch, M_HEADS, M_DK, M_DV), F32),
                   jax.ShapeDtypeStruct((batch, M_HEADS, 1, M_DK), F32),
                   jax.ShapeDtypeStruct((batch, M_HEADS, 1, LANES), F32)],
        scratch_shapes=[pltpu.VMEM((M_HEADS, M_DV + MLSTM_AUX_ROWS, M_DK), F32),
                        pltpu.VMEM((M_HEADS, 1, LANES), F32),
                        pltpu.VMEM((2, WINDOW, 2 * WINDOW), F32)],
        compiler_params=pltpu.CompilerParams(dimension_semantics=("parallel", "arbitrary"),
                                             vmem_limit_bytes=VMEM_LIMIT),
        name="prompt_mixers",
    )(sinks, zm, gates, gain, qa, kva, kva)


SAMPLE_ROWS = 128


def _mlstm_sample_kernel(zm_ref, gates_ref, gain_ref, c_ref, n_ref, m_ref,
                         um_ref, co_ref, no_ref, mo_ref,
                         qf_ref, qc_ref, nexp_ref, wkk_ref, dec_ref, *, steps):
    rows = SAMPLE_ROWS
    nb = rows // steps
    row = lax.broadcasted_iota(jnp.int32, (rows, rows), 0)
    col = lax.broadcasted_iota(jnp.int32, (rows, rows), 1)
    same = (row // steps) == (col // steps)
    causal = same & (col <= row)
    eye = row == col
    last = same & ((col % steps) == steps - 1)
    sub = lax.broadcasted_iota(jnp.int32, (2 * steps, M_DV), 0)
    tok_batch = lax.broadcasted_iota(jnp.int32, (M_DK, rows), 1) // steps

    def to_row(x_col):
        return jnp.sum(jnp.where(eye, x_col, 0.0), axis=0, keepdims=True)

    def pick(mask, x_row, fill, reduce):
        return reduce(jnp.where(mask, x_row, fill), axis=-1, keepdims=True)

    gates = gates_ref[...]
    for head in range(M_HEADS):
        lanes = slice(head * M_DK, (head + 1) * M_DK)
        ig_col = gates[:, GATE_I + head:GATE_I + head + 1]
        fg_col = gates[:, GATE_F + head:GATE_F + head + 1]
        lf_row = to_row(_log_sigmoid(fg_col))
        b_col = pick(causal, lf_row, 0.0, jnp.sum)
        b_row = to_row(b_col)
        ig_row = to_row(ig_col)
        m_prev = m_ref[head]

        d = jnp.where(causal, b_col - b_row + ig_row, -jnp.inf)
        inter = b_col + m_prev
        m_t = jnp.maximum(inter, jnp.max(d, axis=-1, keepdims=True))
        w = jnp.exp(d - m_t)
        a = jnp.exp(inter - m_t)

        q = zm_ref[:, lanes]
        k = zm_ref[:, M_WIDTH + head * M_DK:M_WIDTH + (head + 1) * M_DK]
        v = zm_ref[:, 2 * M_WIDTH + head * M_DV:2 * M_WIDTH + (head + 1) * M_DV]
        o = zm_ref[:, 3 * M_WIDTH + head * M_DV:3 * M_WIDTH + (head + 1) * M_DV]
        qf_ref[head] = q.astype(F32)
        for t in range(steps):
            nexp_ref[head, pl.ds(t, nb, stride=steps), :] = n_ref[:, lanes]

        for j in range(nb // 2):
            pair = pl.ds(j * 2 * steps, 2 * steps)
            q8 = qf_ref[head, pair, :].astype(BF16)
            r0 = _dot(q8, c_ref[2 * j, head].astype(BF16))
            r1 = _dot(q8, c_ref[2 * j + 1, head].astype(BF16))
            qc_ref[head, pair, :] = jnp.where(sub < steps, r0, r1)

        s = _dot_nt(q, k) * w
        qn = jnp.sum(qf_ref[head] * nexp_ref[head], axis=-1, keepdims=True)
        num = a * qc_ref[head] + _dot(s.astype(BF16), v)
        den = a * qn + jnp.sum(s, axis=-1, keepdims=True)
        hh = num / jnp.maximum(jnp.abs(den), jnp.exp(-m_t))
        um_ref[:, lanes] = _head_norm_gate(hh, o, gain_ref[:, lanes])

        b_last = pick(last, b_row, 0.0, jnp.sum)
        g_col = b_last - b_col + ig_col
        g_max = pick(same, to_row(g_col), -jnp.inf, jnp.max)
        m_new = jnp.maximum(b_last + m_prev, g_max)
        decay = jnp.exp(b_last + m_prev - m_new)
        wk = jnp.exp(g_col - m_new)
        mo_ref[head] = m_new

        kf = k.astype(F32)
        wkk_ref[head] = wk * kf
        dec_ref[head] = jnp.broadcast_to(decay, (rows, M_DV))
        n_sum = wkk_ref[head, pl.ds(0, nb, stride=steps), :]
        for t in range(1, steps):
            n_sum = n_sum + wkk_ref[head, pl.ds(t, nb, stride=steps), :]
        no_ref[:, lanes] = dec_ref[head, pl.ds(0, nb, stride=steps), :] * n_ref[:, lanes] + n_sum

        k_t = kf.T
        wv = (wk * v.astype(F32)).astype(BF16)
        for b in range(nb):
            upd = _dot(jnp.where(tok_batch == b, k_t, 0.0).astype(BF16), wv)
            co_ref[b, head] = dec_ref[head, pl.ds(b * steps, 1), :] * c_ref[b, head] + upd


def _swa_sample_kernel(sinks_ref, qa_ref, kva_ref, ckt_ref, cvt_ref, ua_ref, nkt_ref, nvt_ref,
                       s_ref, e_ref, pv_ref, *, steps):
    rows = SAMPLE_ROWS
    nb = rows // steps
    pair_rows = 2 * steps
    hw = A_KV_HEADS * A_HD
    tok = lax.broadcasted_iota(jnp.int32, (rows, LANES), 0)
    lane = lax.broadcasted_iota(jnp.int32, (rows, LANES), 1)
    low = lane < A_HD

    qf = qa_ref[...].astype(F32)
    q_blocks = []
    for p in range(A_GROUP):
        qp = qf[:, p * LANES:(p + 1) * LANES]
        q_blocks.append(jnp.where(low, qp, 0.0))
        q_blocks.append(jnp.where(low, 0.0, qp))
    n_blk = len(q_blocks)

    kv_new = kva_ref[...]
    kv_new_t = kv_new.T
    k_new_t = kv_new_t[:hw]
    v_new_t = kv_new_t[hw:]
    v_new = kv_new[:, hw:].astype(BF16)
    s_new = _dot(jnp.concatenate(q_blocks, axis=0).astype(BF16), k_new_t.astype(BF16))

    def pair_rows_of(blocks, j):
        return jnp.concatenate([blk[j * pair_rows:(j + 1) * pair_rows] for blk in blocks], axis=0)

    first = (lax.broadcasted_iota(jnp.int32, (n_blk * pair_rows, LANES), 0) % pair_rows) < steps

    for j in range(nb // 2):
        q_pair = pair_rows_of(q_blocks, j).astype(BF16)
        s_pair = jnp.where(first, _dot(q_pair, ckt_ref[2 * j].astype(BF16)),
                           _dot(q_pair, ckt_ref[2 * j + 1].astype(BF16)))
        for blk in range(n_blk):
            s_ref[pl.ds(blk * rows + j * pair_rows, pair_rows), :] = s_pair[blk * pair_rows:(blk + 1) * pair_rows]

    t_q = tok % steps
    dist_c = WINDOW + t_q - lane
    valid_c = dist_c < WINDOW
    dist_n = t_q - lane % steps
    valid_n = ((tok // steps) == (lane // steps)) & (dist_n >= 0)
    dist_cf = dist_c.astype(F32)
    dist_nf = dist_n.astype(F32)
    pv_new = []
    inv_den = []
    for blk in range(n_blk):
        head = (blk % A_KV_HEADS) * A_GROUP + blk // A_KV_HEADS
        slope = _alibi_slope(head)
        sink = sinks_ref[head]
        r = pl.ds(blk * rows, rows)
        s_c = jnp.where(valid_c, s_ref[r, :] - slope * dist_cf, -jnp.inf)
        s_n = jnp.where(valid_n, s_new[blk * rows:(blk + 1) * rows] - slope * dist_nf, -jnp.inf)
        mx = jnp.maximum(jnp.maximum(jnp.max(s_c, axis=-1, keepdims=True),
                                     jnp.max(s_n, axis=-1, keepdims=True)), sink)
        e_c = jnp.exp(s_c - mx)
        e_n = jnp.exp(s_n - mx)
        den = jnp.sum(e_c, axis=-1, keepdims=True) + jnp.sum(e_n, axis=-1, keepdims=True) + jnp.exp(sink - mx)
        e_ref[r, :] = e_c
        pv_new.append(_dot(e_n.astype(BF16), v_new))
        inv_den.append(1.0 / den)

    for j in range(nb // 2):
        e_pair = jnp.concatenate([e_ref[pl.ds(blk * rows + j * pair_rows, pair_rows), :] for blk in range(n_blk)],
                                 axis=0).astype(BF16)
        pv_pair = jnp.where(first, _dot_nt(e_pair, cvt_ref[2 * j].astype(BF16)),
                            _dot_nt(e_pair, cvt_ref[2 * j + 1].astype(BF16)))
        for blk in range(n_blk):
            pv_ref[pl.ds(blk * rows + j * pair_rows, pair_rows), :] = pv_pair[blk * pair_rows:(blk + 1) * pair_rows]

    for p in range(A_GROUP):
        halves = []
        for kvh in range(A_KV_HEADS):
            blk = p * A_KV_HEADS + kvh
            halves.append((pv_ref[pl.ds(blk * rows, rows), :] + pv_new[blk]) * inv_den[blk])
        ua_ref[:, p * LANES:(p + 1) * LANES] = jnp.where(low, halves[0], halves[1]).astype(BF16)

    tail = lane >= WINDOW - steps
    for b in range(nb):
        shift = (WINDOW - steps - b * steps) % rows
        nkt_ref[b] = jnp.where(tail, pltpu.roll(k_new_t, shift, 1), pltpu.roll(ckt_ref[b], WINDOW - steps, 1))
        nvt_ref[b] = jnp.where(tail, pltpu.roll(v_new_t, shift, 1), pltpu.roll(cvt_ref[b], WINDOW - steps, 1))


def _sample_mixers_kernel(sinks_ref, zm_ref, gates_ref, gain_ref, c_ref, n_ref, m_ref, qa_ref, kva_ref,
                          ckt_ref, cvt_ref, um_ref, co_ref, no_ref, mo_ref, ua_ref, nkt_ref, nvt_ref,
                          qf_ref, qc_ref, nexp_ref, wkk_ref, dec_ref, s_ref, e_ref, pv_ref, *, steps):
    _mlstm_sample_kernel(zm_ref, gates_ref, gain_ref, c_ref, n_ref, m_ref, um_ref, co_ref, no_ref, mo_ref,
                         qf_ref, qc_ref, nexp_ref, wkk_ref, dec_ref, steps=steps)
    _swa_sample_kernel(sinks_ref, qa_ref, kva_ref, ckt_ref, cvt_ref, ua_ref, nkt_ref, nvt_ref,
                       s_ref, e_ref, pv_ref, steps=steps)


def _sample_mixers(sinks, zm, gates, gain, c, n2, m_rows, qa, kva, ckt, cvt, steps):
    rows = zm.shape[0]
    assert ckt.shape[1:] == (A_KV_HEADS * A_HD, WINDOW) and SAMPLE_ROWS == WINDOW == LANES
    nb = SAMPLE_ROWS // steps

    def tokens(width):
        return pl.BlockSpec((SAMPLE_ROWS, width), lambda g: (g, 0))

    c_spec = pl.BlockSpec((nb, M_HEADS, M_DK, M_DV), lambda g: (g, 0, 0, 0))
    n_spec = pl.BlockSpec((nb, M_HEADS * M_DK), lambda g: (g, 0))
    m_spec = pl.BlockSpec((M_HEADS, SAMPLE_ROWS, 1), lambda g: (0, g, 0))
    cache = pl.BlockSpec((nb, LANES, WINDOW), lambda g: (g, 0, 0))
    per_head = pltpu.VMEM((M_HEADS, SAMPLE_ROWS, M_DK), F32)
    stacked = pltpu.VMEM((A_HEADS * SAMPLE_ROWS, LANES), F32)
    return pl.pallas_call(
        functools.partial(_sample_mixers_kernel, steps=steps),
        grid=(rows // SAMPLE_ROWS,),
        in_specs=[pl.BlockSpec(memory_space=pltpu.SMEM), tokens(ZM_WIDTH), tokens(LANES),
                  pl.BlockSpec((1, M_WIDTH), lambda g: (0, 0)), c_spec, n_spec, m_spec,
                  tokens(A_WIDTH), tokens(KV_WIDTH), cache, cache],
        out_specs=[tokens(M_WIDTH), c_spec, n_spec, m_spec, tokens(A_WIDTH), cache, cache],
        out_shape=[jax.ShapeDtypeStruct((rows, M_WIDTH), BF16),
                   jax.ShapeDtypeStruct(c.shape, F32),
                   jax.ShapeDtypeStruct(n2.shape, F32),
                   jax.ShapeDtypeStruct(m_rows.shape, F32),
                   jax.ShapeDtypeStruct((rows, A_WIDTH), BF16),
                   jax.ShapeDtypeStruct(ckt.shape, F32),
                   jax.ShapeDtypeStruct(cvt.shape, F32)],
        scratch_shapes=[per_head,
                        per_head,
                        per_head,
                        per_head,
                        per_head,
                        stacked,
                        stacked,
                        stacked],
        compiler_params=pltpu.CompilerParams(dimension_semantics=("parallel",),
                                             vmem_limit_bytes=VMEM_LIMIT),
        name="sample_mixers",
    )(sinks, zm, gates, gain, c, n2, m_rows, qa, kva, ckt, cvt)


def _prep_proj(w_in, b_gate):
    m_end = 4 * M_WIDTH
    g_end = m_end + 2 * M_HEADS
    wt = w_in.T
    qa = wt[g_end:g_end + A_WIDTH].reshape(A_KV_HEADS, A_GROUP, A_HD, D_MODEL).transpose(1, 0, 2, 3)
    gates = jnp.pad(wt[m_end:g_end], ((0, LANES - 2 * M_HEADS), (0, 0)))
    win = tuple(w.astype(BF16) for w in (wt[:m_end], qa.reshape(A_WIDTH, D_MODEL), wt[g_end + A_WIDTH:], gates))
    bias = jnp.pad(b_gate, (0, LANES - 2 * M_HEADS)).reshape(1, LANES)
    return win, bias


def _prep_out(w_out):
    wa = w_out[M_WIDTH:].reshape(A_KV_HEADS, A_GROUP, A_HD, D_MODEL).transpose(1, 0, 2, 3)
    return jnp.concatenate([w_out[:M_WIDTH], wa.reshape(A_WIDTH, D_MODEL)], axis=0).astype(BF16)


def kernel(x_prompt, x_sample, cache_swa_k, cache_swa_v, state_mlstm_C, state_mlstm_n, state_mlstm_m,
           norm_gains, ffn_w_gate, ffn_w_up, ffn_w_down, w_in, b_gate, mlstm_norm_gain, attn_sinks, w_out):
    assert norm_gains.shape[0] == 1, "single layer"
    batch, seq, _ = x_prompt.shape
    dec_batch, steps, _ = x_sample.shape
    gains = norm_gains[0]
    ffn = (ffn_w_gate, ffn_w_up, ffn_w_down)
    win, bias = _prep_proj(w_in[0], b_gate[0])
    wo = _prep_out(w_out[0])
    gain = mlstm_norm_gain[0].reshape(1, M_WIDTH)
    sinks = attn_sinks[0]
    kv_lanes = A_KV_HEADS * A_HD

    chunk = min(MLSTM_CHUNK, seq)
    assert seq % chunk == 0
    xp, zm, qa, kva, gates = _ffn_proj(x_prompt.reshape(batch * seq, D_MODEL), gains, *ffn, win, bias,
                                       0, 0, 1, 2, chunk)
    um, ua, p_c, p_n, p_m = _prompt_mixers(sinks, zm, gates, gain, qa, kva, batch, seq, chunk)
    yp = _mix_ffn(xp, um, ua, wo, gains, *ffn, 1, 3, 4, 5).reshape(batch, seq, D_MODEL)
    keep = min(WINDOW, seq)
    kv_tail = kva.reshape(batch, seq, KV_WIDTH)[:, seq - keep:]
    p_k = kv_tail[..., :kv_lanes].reshape(1, batch, keep, A_KV_HEADS, A_HD)
    p_v = kv_tail[..., kv_lanes:].reshape(1, batch, keep, A_KV_HEADS, A_HD)
    p_c = p_c[None]
    p_n = p_n[None, :, :, 0, :]
    p_m = p_m[None, :, :, 0, 0]

    xs, zm_s, qa_s, kva_s, gates_s = _ffn_proj(x_sample.reshape(dec_batch * steps, D_MODEL), gains, *ffn,
                                               win, bias, 0, 0, 1, 2, min(chunk, dec_batch * steps))
    m_rows = jnp.repeat(state_mlstm_m[0].T, steps, axis=1)[..., None]
    wc = cache_swa_k.shape[2]
    um_s, s_c, s_n2, s_m_rows, ua_s, s_kt, s_vt = _sample_mixers(
        sinks, zm_s, gates_s, gain, state_mlstm_C[0], state_mlstm_n[0].reshape(dec_batch, M_HEADS * M_DK), m_rows,
        qa_s, kva_s, cache_swa_k[0].reshape(dec_batch, wc, kv_lanes).transpose(0, 2, 1),
        cache_swa_v[0].reshape(dec_batch, wc, kv_lanes).transpose(0, 2, 1), steps)
    ys = _mix_ffn(xs, um_s, ua_s, wo, gains, *ffn, 1, 3, 4, 5).reshape(dec_batch, steps, D_MODEL)
    s_n = s_n2.reshape(1, dec_batch, M_HEADS, M_DK)
    s_m = s_m_rows[:, ::steps, 0].T[None]
    s_k = s_kt.transpose(0, 2, 1).reshape(1, dec_batch, wc, A_KV_HEADS, A_HD)
    s_v = s_vt.transpose(0, 2, 1).reshape(1, dec_batch, wc, A_KV_HEADS, A_HD)
    return (yp, ys, p_k, p_v, p_c, p_n, p_m, s_k, s_v, s_c[None], s_n, s_m)
```

```python
import functools

import jax
import jax.numpy as jnp
from jax import lax
from jax.experimental import pallas as pl
from jax.experimental.pallas import tpu as pltpu

F32 = jnp.float32
BF16 = jnp.bfloat16

D_MODEL = 1024
D_FF = 2816
M_HEADS = 4
M_DK = 128
M_DV = 128
M_WIDTH = M_HEADS * M_DV
A_HEADS = 8
A_KV_HEADS = 2
A_GROUP = A_HEADS // A_KV_HEADS
A_HD = 64
A_WIDTH = A_HEADS * A_HD
WINDOW = 128
RMS_EPS = 1e-6

LANES = 128
FFN_CHUNK = 256
N_FFN_CHUNKS = D_FF // FFN_CHUNK
FFN_STAGE_SLOTS = 4
TOKEN_TILE = 512
MLSTM_CHUNK = 256
ZM_WIDTH = 4 * M_WIDTH
KV_WIDTH = 2 * A_KV_HEADS * A_HD
VMEM_LIMIT = 56 * 1024 * 1024
MLSTM_CHUNKS_PER_STEP = 2
MLSTM_AUX_ROWS = 16
GATE_I, GATE_F, GATE_B, GATE_X =0, M_HEADS, 2 * M_HEADS, 3 * M_HEADS

NT_DIMS = (((1,), (1,)), ((), ()))


def _rms(x, g):
    ms = jnp.mean(x * x, axis=-1, keepdims=True)
    return x * lax.rsqrt(ms + RMS_EPS) * g


def _log_sigmoid(x):
    return jnp.minimum(x, 0.0) - jnp.log1p(jnp.exp(-jnp.abs(x)))


def _dot(a, b):
    return jnp.dot(a, b, preferred_element_type=F32)


def _dot_nt(a, b):
    return lax.dot_general(a, b, NT_DIMS, preferred_element_type=F32)


def _resident(shape):
    nd = len(shape)
    return pl.BlockSpec(shape, lambda *_: (0,) * nd, pipeline_mode=pl.Buffered(1))


def _ffn_weight_copies(w_hbm, j, f, slot, stage_refs, sems):
    wg_hbm, wu_hbm, wd_hbm = w_hbm
    cols = pl.ds(f * FFN_CHUNK, FFN_CHUNK)
    sources = (wg_hbm.at[0, j, :, cols], wu_hbm.at[0, j, :, cols], wd_hbm.at[0, j, cols, :])
    return [pltpu.make_async_copy(src, stage.at[slot], sems.at[slot, k])
            for k, (src, stage) in enumerate(zip(sources, stage_refs))]


def _swiglu_chunk(h_ref, acc_ref, wg, wu, wd, first):
    h = h_ref[...]
    g = _dot(h, wg)
    u = _dot(h, wu)
    part = _dot((g * jax.nn.sigmoid(g) * u).astype(BF16), wd)
    if first:
        acc_ref[...] = part
    else:
        acc_ref[...] += part


def _swiglu(x, gpre, w_refs, h_ref, acc_ref, fetch=None):
    wg_ref, wu_ref, wd_ref = w_refs
    h_ref[...] = _rms(x, gpre).astype(BF16)
    if fetch is None:
        for f in range(N_FFN_CHUNKS):
            _swiglu_chunk(h_ref, acc_ref, wg_ref[f], wu_ref[f], wd_ref[f], first=f == 0)
        return

    stage_g, stage_u, stage_d = fetch[2]
    for f in range(min(FFN_STAGE_SLOTS, N_FFN_CHUNKS)):
        for copy in _ffn_weight_copies(*fetch[:2], f, f, *fetch[2:]):
            copy.start()
    acc_ref[...] = jnp.zeros_like(acc_ref)

    def step(f, carry):
        slot = f % FFN_STAGE_SLOTS
        for copy in _ffn_weight_copies(*fetch[:2], f, slot, *fetch[2:]):
            copy.wait()
        wg_ref[f] = stage_g[slot].astype(BF16)
        wu_ref[f] = stage_u[slot].astype(BF16)
        wd_ref[f] = stage_d[slot].astype(BF16)

        @pl.when(f + FFN_STAGE_SLOTS < N_FFN_CHUNKS)
        def _():
            for copy in _ffn_weight_copies(*fetch[:2], f + FFN_STAGE_SLOTS, slot, *fetch[2:]):
                copy.start()

        _swiglu_chunk(h_ref, acc_ref, wg_ref[f], wu_ref[f], wd_ref[f], first=False)
        return carry

    lax.fori_loop(0, N_FFN_CHUNKS, step, 0)


def _first_step_fetches(body, fetch):
    first = pl.program_id(0) == 0

    @pl.when(first)
    def _():
        body(fetch)

    @pl.when(jnp.logical_not(first))
    def _():
        body(None)


def _half_step(x, acc_ref, gpost):
    return x + _rms(acc_ref[...], 0.5 * gpost)


def _project(x, gain, win_refs, bias_ref, h_ref, zm_ref, qa_ref, kva_ref, gates_ref, chunk):
    h_ref[...] = _rms(x, gain).astype(BF16)

    wm_ref, wqa_ref, wkv_ref, wg_ref = win_refs

    def cols(lo, width):
        return _dot_nt(h_ref[...], wm_ref[lo:lo + width, :])

    gates = _dot_nt(h_ref[...], wg_ref[...]) + bias_ref[...]
    lane = lax.broadcasted_iota(jnp.int32, (chunk, LANES), 1)
    pos = lax.broadcasted_iota(jnp.int32, (chunk, LANES), 0)
    for c0 in range(0, gates.shape[0], chunk):
        g = gates[c0:c0 + chunk]
        b = jnp.where((lane >= GATE_F) & (lane < GATE_B), _log_sigmoid(g), 0.0)
        shift = 1
        while shift < chunk:
            b = b + jnp.where(pos >= shift, pltpu.roll(b, shift, 0), 0.0)
            shift *= 2
        x_gate = pltpu.roll(g, GATE_X - GATE_I, 1) - pltpu.roll(b, GATE_X - GATE_F, 1)
        out = jnp.where(lane < GATE_B, g, jnp.where(lane < GATE_X, pltpu.roll(b, GATE_B - GATE_F, 1), x_gate))
        gates_ref[c0:c0 + chunk, :] = jnp.where(lane < GATE_X + M_HEADS, out, 0.0)

    zm_ref[:, 0:M_WIDTH] = cols(0, M_WIDTH).astype(BF16)
    zm_ref[:, M_WIDTH:2 * M_WIDTH] = (cols(M_WIDTH, M_WIDTH) * (M_DK ** -0.5)).astype(BF16)
    zm_ref[:, 2 * M_WIDTH:3 * M_WIDTH] = cols(2 * M_WIDTH, M_WIDTH).astype(BF16)
    zm_ref[:, 3 * M_WIDTH:] = jax.nn.sigmoid(cols(3 * M_WIDTH, M_WIDTH)).astype(BF16)
    qa_ref[...] = (_dot_nt(h_ref[...], wqa_ref[...]) * (A_HD ** -0.5)).astype(BF16)
    kva_ref[...] = _dot_nt(h_ref[...], wkv_ref[...])


def _ffn_proj_kernel(x_ref, gains_ref, wg_hbm, wu_hbm, wd_hbm, wm_ref, wqa_ref, wkv_ref, wgate_ref, bias_ref,
                     x1_ref, zm_ref, qa_ref, kva_ref, gates_ref,
                     h_ref, acc_ref, wg_ref, wu_ref, wd_ref, stage_g, stage_u, stage_d, sems,
                     *, j, pre, post, mixer, chunk):
    def body(fetch):
        x = x_ref[...]
        _swiglu(x, gains_ref[pre:pre + 1, :], (wg_ref, wu_ref, wd_ref), h_ref, acc_ref, fetch)
        x1 = _half_step(x, acc_ref, gains_ref[post:post + 1, :])
        x1_ref[...] = x1
        _project(x1, gains_ref[mixer:mixer + 1, :], (wm_ref, wqa_ref, wkv_ref, wgate_ref), bias_ref, h_ref,
                 zm_ref, qa_ref, kva_ref, gates_ref, chunk)

    _first_step_fetches(body, ((wg_hbm, wu_hbm, wd_hbm), j, (stage_g, stage_u, stage_d), sems))


def _mix_ffn_kernel(x_ref, um_ref, ua_ref, wo_ref, gains_ref, wg_hbm, wu_hbm, wd_hbm, o_ref,
                    h_ref, acc_ref, wg_ref, wu_ref, wd_ref, stage_g, stage_u, stage_d, sems,
                    *, j, mix, pre, post):
    def body(fetch):
        u = jnp.concatenate([um_ref[...], ua_ref[...]], axis=-1)
        x = x_ref[...] + _rms(_dot(u, wo_ref[...]), gains_ref[mix:mix + 1, :])
        _swiglu(x, gains_ref[pre:pre + 1, :], (wg_ref, wu_ref, wd_ref), h_ref, acc_ref, fetch)
        o_ref[...] = _half_step(x, acc_ref, gains_ref[post:post + 1, :])

    _first_step_fetches(body, ((wg_hbm, wu_hbm, wd_hbm), j, (stage_g, stage_u, stage_d), sems))


def _ffn_scratch(tm):
    return [pltpu.VMEM((tm, D_MODEL), BF16),
            pltpu.VMEM((tm, D_MODEL), F32),
            pltpu.VMEM((N_FFN_CHUNKS, D_MODEL, FFN_CHUNK), BF16),
            pltpu.VMEM((N_FFN_CHUNKS, D_MODEL, FFN_CHUNK), BF16),
            pltpu.VMEM((N_FFN_CHUNKS, FFN_CHUNK, D_MODEL), BF16),
            pltpu.VMEM((FFN_STAGE_SLOTS, D_MODEL, FFN_CHUNK), F32),
            pltpu.VMEM((FFN_STAGE_SLOTS, D_MODEL, FFN_CHUNK), F32),
            pltpu.VMEM((FFN_STAGE_SLOTS, FFN_CHUNK, D_MODEL), F32),
            pltpu.SemaphoreType.DMA((FFN_STAGE_SLOTS, 3))]


def _dense_params():
    return pltpu.CompilerParams(dimension_semantics=("arbitrary",), vmem_limit_bytes=VMEM_LIMIT)


_HBM = pl.BlockSpec(memory_space=pl.ANY)


def _ffn_proj(x, gains, wg, wu, wd, win, bias, j, pre, post, mixer, chunk):
    n = x.shape[0]
    tm = min(TOKEN_TILE, n)
    assert tm % chunk == 0

    def rows(width):
        return pl.BlockSpec((tm, width), lambda i: (i, 0))

    return pl.pallas_call(
        functools.partial(_ffn_proj_kernel, j=j, pre=pre, post=post, mixer=mixer, chunk=chunk),
        grid=(n // tm,),
        in_specs=[rows(D_MODEL), _resident(gains.shape), _HBM, _HBM, _HBM,
                  *[_resident(w.shape) for w in win], _resident(bias.shape)],
        out_specs=[rows(D_MODEL), rows(ZM_WIDTH), rows(A_WIDTH), rows(KV_WIDTH), rows(LANES)],
        out_shape=[jax.ShapeDtypeStruct((n, D_MODEL), F32),
                   jax.ShapeDtypeStruct((n, ZM_WIDTH), BF16),
                   jax.ShapeDtypeStruct((n, A_WIDTH), BF16),
                   jax.ShapeDtypeStruct((n, KV_WIDTH), F32),
                   jax.ShapeDtypeStruct((n, LANES), F32)],
        scratch_shapes=_ffn_scratch(tm),
        compiler_params=_dense_params(),
        name="ffn_half_step_input_projection",
    )(x, gains, wg, wu, wd, *win, bias)


def _mix_ffn(x, um, ua, wo, gains, wg, wu, wd, j, mix, pre, post):
    n = x.shape[0]
    tm = min(TOKEN_TILE, n)
    row = pl.BlockSpec((tm, D_MODEL), lambda i: (i, 0))
    half = pl.BlockSpec((tm, M_WIDTH), lambda i: (i, 0))
    return pl.pallas_call(
        functools.partial(_mix_ffn_kernel, j=j, mix=mix, pre=pre, post=post),
        grid=(n // tm,),
        in_specs=[row, half, half, _resident(wo.shape), _resident(gains.shape), _HBM, _HBM, _HBM],
        out_specs=row,
        out_shape=jax.ShapeDtypeStruct((n, D_MODEL), F32),
        scratch_shapes=_ffn_scratch(tm),
        compiler_params=_dense_params(),
        name="out_proj_ffn_half_step",
    )(x, um, ua, wo, gains, wg, wu, wd)


def _head_norm_gate(hh, o, gain):
    hn = hh * lax.rsqrt(jnp.mean(hh * hh, axis=-1, keepdims=True) + RMS_EPS) * gain
    return (o.astype(F32) * hn).astype(BF16)


def _mlstm_chunks(zm_ref, gates_ref, gain_ref, um_ref, cnt_ref, ms_ref, chunk):
    si = lax.broadcasted_iota(jnp.int32, (chunk, chunk), 0)
    ti = lax.broadcasted_iota(jnp.int32, (chunk, chunk), 1)
    causal = si <= ti
    ones_rows = jnp.where(lax.broadcasted_iota(jnp.int32, (MLSTM_AUX_ROWS, chunk), 0) == 0, 1.0, 0.0).astype(BF16)

    def run_chunk(c):
        rows = slice(c * chunk, (c + 1) * chunk)
        gates = gates_ref[rows, :]
        gates_t = gates.T
        for h in range(M_HEADS):
            lanes = slice(h * M_DK, (h + 1) * M_DK)
            q = zm_ref[rows, lanes]
            k = zm_ref[rows, M_WIDTH + h * M_DK:M_WIDTH + (h + 1) * M_DK]
            v = zm_ref[rows, 2 * M_WIDTH + h * M_DV:2 * M_WIDTH + (h + 1) * M_DV]
            o = zm_ref[rows, 3 * M_WIDTH + h * M_DV:3 * M_WIDTH + (h + 1) * M_DV]
            x_col = gates[:, GATE_X + h:GATE_X + h + 1]
            ig_row = gates_t[GATE_I + h:GATE_I + h + 1, :]
            b_row = gates_t[GATE_B + h:GATE_B + h + 1, :]
            m_prev = ms_ref[h][:, 0:1]
            cnt = cnt_ref[h]

            xm = jnp.where(causal, x_col, -jnp.inf)
            mu = jnp.maximum(m_prev, jnp.max(xm, axis=0, keepdims=True))
            a = jnp.exp(m_prev - mu)
            s_t = _dot_nt(k, q) * jnp.exp(xm - mu)
            v_ext = jnp.concatenate([v.astype(F32).T.astype(BF16), ones_rows], axis=0)
            nd = a * _dot_nt(cnt.astype(BF16), q) + _dot(v_ext, s_t.astype(BF16))
            den = nd[M_DV:M_DV + 1, :]
            hh = nd[:M_DV, :] * (1.0 / jnp.maximum(jnp.abs(den), jnp.exp(-(b_row + mu))))
            hn = hh * lax.rsqrt(jnp.mean(hh * hh, axis=0, keepdims=True) + RMS_EPS)
            um_ref[rows, lanes] = (hn.T * gain_ref[:, lanes] * o.astype(F32)).astype(BF16)

            b_last = b_row[:, chunk - 1:chunk]
            g = b_last - b_row + ig_row
            m_new = jnp.maximum(b_last + m_prev, jnp.max(g, axis=-1, keepdims=True))
            decay = jnp.exp(b_last + m_prev - m_new)
            wv = (jnp.exp(g - m_new) * v_ext.astype(F32)).astype(BF16)
            cnt_ref[h] = decay * cnt + _dot(wv, k)
            ms_ref[h] = jnp.broadcast_to(m_new, (1, LANES))

    return [functools.partial(run_chunk, c) for c in range(zm_ref.shape[0] // chunk)]


def _alibi_slope(head):
    return 2.0 ** (-8.0 * (head + 1) / A_HEADS)


def _swa_blocks(sinks_ref, qa_ref, kvc_ref, kvp_ref, ua_ref, mask_ref, first):
    hw = A_KV_HEADS * A_HD
    n_blocks = qa_ref.shape[0] // WINDOW
    qi = lax.broadcasted_iota(jnp.int32, (WINDOW, 2 * WINDOW), 0)
    kj = lax.broadcasted_iota(jnp.int32, (WINDOW, 2 * WINDOW), 1)
    dist = WINDOW + qi - kj
    band = (dist >= 0) & (dist < WINDOW)
    mask_ref[0] = jnp.where(band, 0.0, -jnp.inf)
    mask_ref[1] = jnp.where(band & (kj >= WINDOW), 0.0, -jnp.inf)
    lane = lax.broadcasted_iota(jnp.int32, (WINDOW, LANES), 1)
    low = lane < A_HD

    assert 2 * WINDOW <= 256, "positions must stay exactly representable in bf16"
    q_pos = (WINDOW + lax.broadcasted_iota(jnp.int32, (WINDOW, LANES), 0)).astype(F32)
    k_lane = lax.broadcasted_iota(jnp.int32, (2 * WINDOW, LANES), 1)
    k_pos = lax.broadcasted_iota(jnp.int32, (2 * WINDOW, LANES), 0).astype(F32)
    spare = (A_HD, 0)
    k_bias = [jnp.where(k_lane == spare[kvh], 1.0, jnp.where(k_lane == spare[kvh] + 1, k_pos, 0.0)).astype(BF16)
              for kvh in range(A_KV_HEADS)]
    q_bias = {}
    for head in range(A_HEADS):
        slope = _alibi_slope(head)
        at = spare[head // A_GROUP]
        q_bias[head] = jnp.where(lane == at, -slope * q_pos, jnp.where(lane == at + 1, slope, 0.0)).astype(BF16)

    def run_block(j):
        rows = slice(j * WINDOW, (j + 1) * WINDOW)
        kvc = kvc_ref[rows, :]
        kvp = kvp_ref[...] if j == 0 else kvc_ref[(j - 1) * WINDOW:j * WINDOW, :]
        mask = mask_ref[jnp.where(first, 1, 0)] if j == 0 else mask_ref[0]
        kk = jnp.concatenate([kvp[:, :hw], kvc[:, :hw]], axis=0).astype(BF16)
        kk_biased = [jnp.where(k_lane < A_HD, kk, k_bias[0]), jnp.where(k_lane < A_HD, k_bias[1], kk)]
        vv = jnp.concatenate([kvp[:, hw:], kvc[:, hw:]], axis=0).astype(BF16)
        for p in range(A_GROUP):
            qp = qa_ref[rows, p * LANES:(p + 1) * LANES]
            outs = []
            for kvh in range(A_KV_HEADS):
                head = kvh * A_GROUP + p
                keep = low if kvh == 0 else jnp.logical_not(low)
                s = _dot_nt(jnp.where(keep, qp, q_bias[head]), kk_biased[kvh])
                s = s + mask
                sink = sinks_ref[head]
                mx = jnp.maximum(jnp.max(s, axis=-1, keepdims=True), sink)
                e = jnp.exp(s - mx)
                den = jnp.sum(e, axis=-1, keepdims=True) + jnp.exp(sink - mx)
                outs.append(_dot(e.astype(BF16), vv) / den)
            ua_ref[rows, p * LANES:(p + 1) * LANES] = jnp.where(low, outs[0], outs[1]).astype(BF16)

    return [functools.partial(run_block, j) for j in range(n_blocks)]


def _prompt_mixers_kernel(sinks_ref, zm_ref, gates_ref, gain_ref, qa_ref, kvc_ref, kvp_ref,
                          um_ref, ua_ref, c_ref, n_ref, m_ref, cnt_ref, ms_ref, mask_ref, *, chunk):
    step = pl.program_id(1)

    @pl.when(step == 0)
    def _():
        cnt_ref[...] = jnp.zeros_like(cnt_ref)
        ms_ref[...] = jnp.zeros_like(ms_ref)

    chunks = _mlstm_chunks(zm_ref, gates_ref, gain_ref, um_ref, cnt_ref, ms_ref, chunk)
    blocks = _swa_blocks(sinks_ref, qa_ref, kvc_ref, kvp_ref, ua_ref, mask_ref, step == 0)
    per_chunk = len(blocks) // len(chunks)
    for c, run_chunk in enumerate(chunks):
        run_chunk()
        for run_block in blocks[c * per_chunk:(c + 1) * per_chunk]:
            run_block()

    @pl.when(step == pl.num_programs(1) - 1)
    def _():
        for h in range(M_HEADS):
            c_ref[0, h] = cnt_ref[h][:M_DV, :].T
            n_ref[0, h] = cnt_ref[h][M_DV:M_DV + 1, :]
            m_ref[0, h] = ms_ref[h]


def _prompt_mixers(sinks, zm, gates, gain, qa, kva, batch, seq, chunk):
    n = batch * seq
    per_step = min(MLSTM_CHUNKS_PER_STEP, seq // chunk)
    rows = per_step * chunk
    assert seq % rows == 0 and chunk % WINDOW == 0
    ns = seq // rows

    def step_rows(width):
        return pl.BlockSpec((rows, width), lambda b, i: (b * ns + i, 0))

    def prev_block(b, i):
        return (jnp.maximum((b * ns + i) * (rows // WINDOW) - 1, 0), 0)

    def state(*dims):
        return pl.BlockSpec((1, M_HEADS) + dims, lambda b, i: (b, 0, 0, 0))

    return pl.pallas_call(
        functools.partial(_prompt_mixers_kernel, chunk=chunk),
        grid=(batch, ns),
        in_specs=[pl.BlockSpec(memory_space=pltpu.SMEM), step_rows(ZM_WIDTH), step_rows(LANES),
                  pl.BlockSpec((1, M_WIDTH), lambda b, i: (0, 0)),
                  step_rows(A_WIDTH), step_rows(KV_WIDTH), pl.BlockSpec((WINDOW, KV_WIDTH), prev_block)],
        out_specs=[step_rows(M_WIDTH), step_rows(A_WIDTH), state(M_DK, M_DV), state(1, M_DK), state(1, LANES)],
        out_shape=[jax.ShapeDtypeStruct((n, M_WIDTH), BF16),
                   jax.ShapeDtypeStruct((n, A_WIDTH), BF16),
                   jax.ShapeDtypeStruct((batch, M_HEADS, M_DK, M_DV), F32),
                   jax.ShapeDtypeStruct((batch, M_HEADS, 1, M_DK), F32),
                   jax.ShapeDtypeStruct((batch, M_HEADS, 1, LANES), F32)],
        scratch_shapes=[pltpu.VMEM((M_HEADS, M_DV + MLSTM_AUX_ROWS, M_DK), F32),
                        pltpu.VMEM((M_HEADS, 1, LANES), F32),
                        pltpu.VMEM((2, WINDOW, 2 * WINDOW), F32)],
        compiler_params=pltpu.CompilerParams(dimension_semantics=("parallel", "arbitrary"),
                                             vmem_limit_bytes=VMEM_LIMIT),
        name="prompt_mixers",
    )(sinks, zm, gates, gain, qa, kva, kva)


SAMPLE_ROWS = 128


def _mlstm_sample_kernel(zm_ref, gates_ref, gain_ref, c_ref, n_ref, m_ref,
                         um_ref, co_ref, no_ref, mo_ref,
                         qf_ref, qc_ref, nexp_ref, wkk_ref, dec_ref, *, steps):
    rows = SAMPLE_ROWS
    nb = rows // steps
    row = lax.broadcasted_iota(jnp.int32, (rows, rows), 0)
    col = lax.broadcasted_iota(jnp.int32, (rows, rows), 1)
    same = (row // steps) == (col // steps)
    causal = same & (col <= row)
    eye = row == col
    last = same & ((col % steps) == steps - 1)
    sub = lax.broadcasted_iota(jnp.int32, (2 * steps, M_DV), 0)
    tok_batch = lax.broadcasted_iota(jnp.int32, (M_DK, rows), 1) // steps

    def to_row(x_col):
        return jnp.sum(jnp.where(eye, x_col, 0.0), axis=0, keepdims=True)

    def pick(mask, x_row, fill, reduce):
        return reduce(jnp.where(mask, x_row, fill), axis=-1, keepdims=True)

    gates = gates_ref[...]
    for head in range(M_HEADS):
        lanes = slice(head * M_DK, (head + 1) * M_DK)
        ig_col = gates[:, GATE_I + head:GATE_I + head + 1]
        fg_col = gates[:, GATE_F + head:GATE_F + head + 1]
        lf_row = to_row(_log_sigmoid(fg_col))
        b_col = pick(causal, lf_row, 0.0, jnp.sum)
        b_row = to_row(b_col)
        ig_row = to_row(ig_col)
        m_prev = m_ref[head]

        d = jnp.where(causal, b_col - b_row + ig_row, -jnp.inf)
        inter = b_col + m_prev
        m_t = jnp.maximum(inter, jnp.max(d, axis=-1, keepdims=True))
        w = jnp.exp(d - m_t)
        a = jnp.exp(inter - m_t)

        q = zm_ref[:, lanes]
        k = zm_ref[:, M_WIDTH + head * M_DK:M_WIDTH + (head + 1) * M_DK]
        v = zm_ref[:, 2 * M_WIDTH + head * M_DV:2 * M_WIDTH + (head + 1) * M_DV]
        o = zm_ref[:, 3 * M_WIDTH + head * M_DV:3 * M_WIDTH + (head + 1) * M_DV]
        qf_ref[head] = q.astype(F32)
        for t in range(steps):
            nexp_ref[head, pl.ds(t, nb, stride=steps), :] = n_ref[:, lanes]

        for j in range(nb // 2):
            pair = pl.ds(j * 2 * steps, 2 * steps)
            q8 = qf_ref[head, pair, :].astype(BF16)
            r0 = _dot(q8, c_ref[2 * j, head].astype(BF16))
            r1 = _dot(q8, c_ref[2 * j + 1, head].astype(BF16))
            qc_ref[head, pair, :] = jnp.where(sub < steps, r0, r1)

        s = _dot_nt(q, k) * w
        qn = jnp.sum(qf_ref[head] * nexp_ref[head], axis=-1, keepdims=True)
        num = a * qc_ref[head] + _dot(s.astype(BF16), v)
        den = a * qn + jnp.sum(s, axis=-1, keepdims=True)
        hh = num / jnp.maximum(jnp.abs(den), jnp.exp(-m_t))
        um_ref[:, lanes] = _head_norm_gate(hh, o, gain_ref[:, lanes])

        b_last = pick(last, b_row, 0.0, jnp.sum)
        g_col = b_last - b_col + ig_col
        g_max = pick(same, to_row(g_col), -jnp.inf, jnp.max)
        m_new = jnp.maximum(b_last + m_prev, g_max)
        decay = jnp.exp(b_last + m_prev - m_new)
        wk = jnp.exp(g_col - m_new)
        mo_ref[head] = m_new

        kf = k.astype(F32)
        wkk_ref[head] = wk * kf
        dec_ref[head] = jnp.broadcast_to(decay, (rows, M_DV))
        n_sum = wkk_ref[head, pl.ds(0, nb, stride=steps), :]
        for t in range(1, steps):
            n_sum = n_sum + wkk_ref[head, pl.ds(t, nb, stride=steps), :]
        no_ref[:, lanes] = dec_ref[head, pl.ds(0, nb, stride=steps), :] * n_ref[:, lanes] + n_sum

        k_t = kf.T
        wv = (wk * v.astype(F32)).astype(BF16)
        for b in range(nb):
            upd = _dot(jnp.where(tok_batch == b, k_t, 0.0).astype(BF16), wv)
            co_ref[b, head] = dec_ref[head, pl.ds(b * steps, 1), :] * c_ref[b, head] + upd


def _swa_sample_kernel(sinks_ref, qa_ref, kva_ref, ckt_ref, cvt_ref, ua_ref, nkt_ref, nvt_ref,
                       s_ref, e_ref, pv_ref, *, steps):
    rows = SAMPLE_ROWS
    nb = rows // steps
    pair_rows = 2 * steps
    hw = A_KV_HEADS * A_HD
    tok = lax.broadcasted_iota(jnp.int32, (rows, LANES), 0)
    lane = lax.broadcasted_iota(jnp.int32, (rows, LANES), 1)
    low = lane < A_HD

    qf = qa_ref[...].astype(F32)
    q_blocks = []
    for p in range(A_GROUP):
        qp = qf[:, p * LANES:(p + 1) * LANES]
        q_blocks.append(jnp.where(low, qp, 0.0))
        q_blocks.append(jnp.where(low, 0.0, qp))
    n_blk = len(q_blocks)

    kv_new = kva_ref[...]
    kv_new_t = kv_new.T
    k_new_t = kv_new_t[:hw]
    v_new_t = kv_new_t[hw:]
    v_new = kv_new[:, hw:].astype(BF16)
    s_new = _dot(jnp.concatenate(q_blocks, axis=0).astype(BF16), k_new_t.astype(BF16))

    def pair_rows_of(blocks, j):
        return jnp.concatenate([blk[j * pair_rows:(j + 1) * pair_rows] for blk in blocks], axis=0)

    first = (lax.broadcasted_iota(jnp.int32, (n_blk * pair_rows, LANES), 0) % pair_rows) < steps

    for j in range(nb // 2):
        q_pair = pair_rows_of(q_blocks, j).astype(BF16)
        s_pair = jnp.where(first, _dot(q_pair, ckt_ref[2 * j].astype(BF16)),
                           _dot(q_pair, ckt_ref[2 * j + 1].astype(BF16)))
        for blk in range(n_blk):
            s_ref[pl.ds(blk * rows + j * pair_rows, pair_rows), :] = s_pair[blk * pair_rows:(blk + 1) * pair_rows]

    t_q = tok % steps
    dist_c = WINDOW + t_q - lane
    valid_c = dist_c < WINDOW
    dist_n = t_q - lane % steps
    valid_n = ((tok // steps) == (lane // steps)) & (dist_n >= 0)
    dist_cf = dist_c.astype(F32)
    dist_nf = dist_n.astype(F32)
    pv_new = []
    inv_den = []
    for blk in range(n_blk):
        head = (blk % A_KV_HEADS) * A_GROUP + blk // A_KV_HEADS
        slope = _alibi_slope(head)
        sink = sinks_ref[head]
        r = pl.ds(blk * rows, rows)
        s_c = jnp.where(valid_c, s_ref[r, :] - slope * dist_cf, -jnp.inf)
        s_n = jnp.where(valid_n, s_new[blk * rows:(blk + 1) * rows] - slope * dist_nf, -jnp.inf)
        mx = jnp.maximum(jnp.maximum(jnp.max(s_c, axis=-1, keepdims=True),
                                     jnp.max(s_n, axis=-1, keepdims=True)), sink)
        e_c = jnp.exp(s_c - mx)
        e_n = jnp.exp(s_n - mx)
        den = jnp.sum(e_c, axis=-1, keepdims=True) + jnp.sum(e_n, axis=-1, keepdims=True) + jnp.exp(sink - mx)
        e_ref[r, :] = e_c
        pv_new.append(_dot(e_n.astype(BF16), v_new))
        inv_den.append(1.0 / den)

    for j in range(nb // 2):
        e_pair = jnp.concatenate([e_ref[pl.ds(blk * rows + j * pair_rows, pair_rows), :] for blk in range(n_blk)],
                                 axis=0).astype(BF16)
        pv_pair = jnp.where(first, _dot_nt(e_pair, cvt_ref[2 * j].astype(BF16)),
                            _dot_nt(e_pair, cvt_ref[2 * j + 1].astype(BF16)))
        for blk in range(n_blk):
            pv_ref[pl.ds(blk * rows + j * pair_rows, pair_rows), :] = pv_pair[blk * pair_rows:(blk + 1) * pair_rows]

    for p in range(A_GROUP):
        halves = []
        for kvh in range(A_KV_HEADS):
            blk = p * A_KV_HEADS + kvh
            halves.append((pv_ref[pl.ds(blk * rows, rows), :] + pv_new[blk]) * inv_den[blk])
        ua_ref[:, p * LANES:(p + 1) * LANES] = jnp.where(low, halves[0], halves[1]).astype(BF16)

    tail = lane >= WINDOW - steps
    for b in range(nb):
        shift = (WINDOW - steps - b * steps) % rows
        nkt_ref[b] = jnp.where(tail, pltpu.roll(k_new_t, shift, 1), pltpu.roll(ckt_ref[b], WINDOW - steps, 1))
        nvt_ref[b] = jnp.where(tail, pltpu.roll(v_new_t, shift, 1), pltpu.roll(cvt_ref[b], WINDOW - steps, 1))


def _sample_mixers_kernel(sinks_ref, zm_ref, gates_ref, gain_ref, c_ref, n_ref, m_ref, qa_ref, kva_ref,
                          ckt_ref, cvt_ref, um_ref, co_ref, no_ref, mo_ref, ua_ref, nkt_ref, nvt_ref,
                          qf_ref, qc_ref, nexp_ref, wkk_ref, dec_ref, s_ref, e_ref, pv_ref, *, steps):
    _mlstm_sample_kernel(zm_ref, gates_ref, gain_ref, c_ref, n_ref, m_ref, um_ref, co_ref, no_ref, mo_ref,
                         qf_ref, qc_ref, nexp_ref, wkk_ref, dec_ref, steps=steps)
    _swa_sample_kernel(sinks_ref, qa_ref, kva_ref, ckt_ref, cvt_ref, ua_ref, nkt_ref, nvt_ref,
                       s_ref, e_ref, pv_ref, steps=steps)


def _sample_mixers(sinks, zm, gates, gain, c, n2, m_rows, qa, kva, ckt, cvt, steps):
    rows = zm.shape[0]
    assert ckt.shape[1:] == (A_KV_HEADS * A_HD, WINDOW) and SAMPLE_ROWS == WINDOW == LANES
    nb = SAMPLE_ROWS // steps

    def tokens(width):
        return pl.BlockSpec((SAMPLE_ROWS, width), lambda g: (g, 0))

    c_spec = pl.BlockSpec((nb, M_HEADS, M_DK, M_DV), lambda g: (g, 0, 0, 0))
    n_spec = pl.BlockSpec((nb, M_HEADS * M_DK), lambda g: (g, 0))
    m_spec = pl.BlockSpec((M_HEADS, SAMPLE_ROWS, 1), lambda g: (0, g, 0))
    cache = pl.BlockSpec((nb, LANES, WINDOW), lambda g: (g, 0, 0))
    per_head = pltpu.VMEM((M_HEADS, SAMPLE_ROWS, M_DK), F32)
    stacked = pltpu.VMEM((A_HEADS * SAMPLE_ROWS, LANES), F32)
    return pl.pallas_call(
        functools.partial(_sample_mixers_kernel, steps=steps),
        grid=(rows // SAMPLE_ROWS,),
        in_specs=[pl.BlockSpec(memory_space=pltpu.SMEM), tokens(ZM_WIDTH), tokens(LANES),
                  pl.BlockSpec((1, M_WIDTH), lambda g: (0, 0)), c_spec, n_spec, m_spec,
                  tokens(A_WIDTH), tokens(KV_WIDTH), cache, cache],
        out_specs=[tokens(M_WIDTH), c_spec, n_spec, m_spec, tokens(A_WIDTH), cache, cache],
        out_shape=[jax.ShapeDtypeStruct((rows, M_WIDTH), BF16),
                   jax.ShapeDtypeStruct(c.shape, F32),
                   jax.ShapeDtypeStruct(n2.shape, F32),
                   jax.ShapeDtypeStruct(m_rows.shape, F32),
                   jax.ShapeDtypeStruct((rows, A_WIDTH), BF16),
                   jax.ShapeDtypeStruct(ckt.shape, F32),
                   jax.ShapeDtypeStruct(cvt.shape, F32)],
        scratch_shapes=[per_head,
                        per_head,
                        per_head,
                        per_head,
                        per_head,
                        stacked,
                        stacked,
                        stacked],
        compiler_params=pltpu.CompilerParams(dimension_semantics=("parallel",),
                                             vmem_limit_bytes=VMEM_LIMIT),
        name="sample_mixers",
    )(sinks, zm, gates, gain, c, n2, m_rows, qa, kva, ckt, cvt)


def _prep_proj(w_in, b_gate):
    m_end = 4 * M_WIDTH
    g_end = m_end + 2 * M_HEADS
    wt = w_in.T
    qa = wt[g_end:g_end + A_WIDTH].reshape(A_KV_HEADS, A_GROUP, A_HD, D_MODEL).transpose(1, 0, 2, 3)
    gates = jnp.pad(wt[m_end:g_end], ((0, LANES - 2 * M_HEADS), (0, 0)))
    win = tuple(w.astype(BF16) for w in (wt[:m_end], qa.reshape(A_WIDTH, D_MODEL), wt[g_end + A_WIDTH:], gates))
    bias = jnp.pad(b_gate, (0, LANES - 2 * M_HEADS)).reshape(1, LANES)
    return win, bias


def _prep_out(w_out):
    wa = w_out[M_WIDTH:].reshape(A_KV_HEADS, A_GROUP, A_HD, D_MODEL).transpose(1, 0, 2, 3)
    return jnp.concatenate([w_out[:M_WIDTH], wa.reshape(A_WIDTH, D_MODEL)], axis=0).astype(BF16)


def kernel(x_prompt, x_sample, cache_swa_k, cache_swa_v, state_mlstm_C, state_mlstm_n, state_mlstm_m,
           norm_gains, ffn_w_gate, ffn_w_up, ffn_w_down, w_in, b_gate, mlstm_norm_gain, attn_sinks, w_out):
    assert norm_gains.shape[0] == 1, "single layer"
    batch, seq, _ = x_prompt.shape
    dec_batch, steps, _ = x_sample.shape
    gains = norm_gains[0]
    ffn = (ffn_w_gate, ffn_w_up, ffn_w_down)
    win, bias = _prep_proj(w_in[0], b_gate[0])
    wo = _prep_out(w_out[0])
    gain = mlstm_norm_gain[0].reshape(1, M_WIDTH)
    sinks = attn_sinks[0]
    kv_lanes = A_KV_HEADS * A_HD

    chunk = min(MLSTM_CHUNK, seq)
    assert seq % chunk == 0
    xp, zm, qa, kva, gates = _ffn_proj(x_prompt.reshape(batch * seq, D_MODEL), gains, *ffn, win, bias,
                                       0, 0, 1, 2, chunk)
    um, ua, p_c, p_n, p_m = _prompt_mixers(sinks, zm, gates, gain, qa, kva, batch, seq, chunk)
    yp = _mix_ffn(xp, um, ua, wo, gains, *ffn, 1, 3, 4, 5).reshape(batch, seq, D_MODEL)
    keep = min(WINDOW, seq)
    kv_tail = kva.reshape(batch, seq, KV_WIDTH)[:, seq - keep:]
    p_k = kv_tail[..., :kv_lanes].reshape(1, batch, keep, A_KV_HEADS, A_HD)
    p_v = kv_tail[..., kv_lanes:].reshape(1, batch, keep, A_KV_HEADS, A_HD)
    p_c = p_c[None]
    p_n = p_n[None, :, :, 0, :]
    p_m = p_m[None, :, :, 0, 0]

    xs, zm_s, qa_s, kva_s, gates_s = _ffn_proj(x_sample.reshape(dec_batch * steps, D_MODEL), gains, *ffn,
                                               win, bias, 0, 0, 1, 2, min(chunk, dec_batch * steps))
    m_rows = jnp.repeat(state_mlstm_m[0].T, steps, axis=1)[..., None]
    wc = cache_swa_k.shape[2]
    um_s, s_c, s_n2, s_m_rows, ua_s, s_kt, s_vt = _sample_mixers(
        sinks, zm_s, gates_s, gain, state_mlstm_C[0], state_mlstm_n[0].reshape(dec_batch, M_HEADS * M_DK), m_rows,
        qa_s, kva_s, cache_swa_k[0].reshape(dec_batch, wc, kv_lanes).transpose(0, 2, 1),
        cache_swa_v[0].reshape(dec_batch, wc, kv_lanes).transpose(0, 2, 1), steps)
    ys = _mix_ffn(xs, um_s, ua_s, wo, gains, *ffn, 1, 3, 4, 5).reshape(dec_batch, steps, D_MODEL)
    s_n = s_n2.reshape(1, dec_batch, M_HEADS, M_DK)
    s_m = s_m_rows[:, ::steps, 0].T[None]
    s_k = s_kt.transpose(0, 2, 1).reshape(1, dec_batch, wc, A_KV_HEADS, A_HD)
    s_v = s_vt.transpose(0, 2, 1).reshape(1, dec_batch, wc, A_KV_HEADS, A_HD)
    return (yp, ys, p_k, p_v, p_c, p_n, p_m, s_k, s_v, s_c[None], s_n, s_m)
```
